```python
import jax
import jax.numpy as jnp
from jax import lax
import numpy as np

D_MODEL = 2048
BATCH = 2
SEQ = 4096
DEPTH = 2

N_META = 16
CHUNK = 64
PAD = CHUNK - N_META
D_FF = 5504
FFN_RES = 0.5
EPS = 1e-6
N_BRANCH = 3
BRANCH_WIDTH = 1024

GLA_HEADS = 4
GLA_DK = 128
GLA_DV = 256
GLA_GATE_RANK = 16
GLA_GATE_TAU = 16.0

RET_HEADS = 4
RET_DK = 256
RET_DV = 256
ROPE_BASE = 10000.0

HGRN_HEADS = 8
HGRN_DK = 128
HGRN_DV = 128
FORGET_FLOOR = 1e-20

IN_SPLITS = (
    GLA_HEADS * GLA_DK, GLA_HEADS * GLA_DK, GLA_HEADS * GLA_DV, GLA_HEADS * GLA_DV, GLA_GATE_RANK,
    RET_HEADS * RET_DK, RET_HEADS * RET_DK, RET_HEADS * RET_DV, RET_HEADS * RET_DV,
    HGRN_HEADS * HGRN_DK, HGRN_HEADS * HGRN_DK, HGRN_HEADS * HGRN_DV, HGRN_HEADS * HGRN_DV,
    N_BRANCH * D_MODEL,
)
W_IN = sum(IN_SPLITS)

kernel_name = 'hybrid_gla_retnet_hgrn2_macaron_block'


def rms_norm(x, g):
    xf = x.astype(jnp.float32)
    y = xf * lax.rsqrt(jnp.mean(xf * xf, axis=-1, keepdims=True) + EPS)
    return (y * g.astype(jnp.float32)).astype(x.dtype)


def head_rms_norm(y, n_heads, g):
    b, l, w = y.shape
    yf = y.astype(jnp.float32).reshape(b, l, n_heads, w // n_heads)
    yf = yf * lax.rsqrt(jnp.mean(yf * yf, axis=-1, keepdims=True) + EPS)
    return yf.reshape(b, l, w) * g.astype(jnp.float32)


def head_group_norm(y, n_heads, g):
    b, l, w = y.shape
    yf = y.astype(jnp.float32).reshape(b, l, n_heads, w // n_heads)
    yf = yf - jnp.mean(yf, axis=-1, keepdims=True)
    yf = yf * lax.rsqrt(jnp.mean(yf * yf, axis=-1, keepdims=True) + EPS)
    return yf.reshape(b, l, w) * g.astype(jnp.float32)


def swiglu(h, w_gate, w_up, w_down):
    return (jax.nn.silu(h @ w_gate) * (h @ w_up)) @ w_down


def rotary(t, pos):
    half = t.shape[-1] // 2
    inv_freq = ROPE_BASE ** (-jnp.arange(half, dtype=jnp.float32) / half)
    ang = pos[:, None] * inv_freq[None, :]
    cos = jnp.cos(ang)[None, :, None, :]
    sin = jnp.sin(ang)[None, :, None, :]
    t1, t2 = t[..., :half], t[..., half:]
    return jnp.concatenate([t1 * cos - t2 * sin, t1 * sin + t2 * cos], axis=-1)


def to_chunks(t, n_heads):
    b, l, w = t.shape
    t = jnp.pad(t.astype(jnp.float32), ((0, 0), (PAD, 0), (0, 0)))
    t = t.reshape(b, (l + PAD) // CHUNK, CHUNK, n_heads, w // n_heads)
    return t.transpose(1, 0, 3, 2, 4)


def from_chunks(o):
    nc, b, h, c, dv = o.shape
    return o.transpose(1, 0, 3, 2, 4).reshape(b, nc * c, h * dv)[:, PAD:]


def chunk_gated_linear_attention(q, k, v, log_a):
    causal = jnp.tril(jnp.ones((CHUNK, CHUNK), dtype=bool))[:, :, None]
    nc, b, h, c, dk = q.shape
    dv = v.shape[-1]

    def step(state, inp):
        qc, kc, vc, gc = inp
        cum = jnp.cumsum(gc, axis=-2)
        diff = cum[..., :, None, :] - cum[..., None, :, :]
        decay = jnp.where(causal, jnp.exp(jnp.where(causal, diff, 0.0)), 0.0)
        scores = jnp.einsum('bhid,bhjd,bhijd->bhij', qc, kc, decay)
        o = (jnp.einsum('bhij,bhjv->bhiv', scores, vc)
             + jnp.einsum('bhid,bhdv->bhiv', qc * jnp.exp(cum), state))
        last = cum[..., -1:, :]
        state = (jnp.exp(last[..., 0, :])[..., None] * state
                 + jnp.einsum('bhjd,bhjv->bhdv', kc * jnp.exp(last - cum), vc))
        return state, o

    s0 = jnp.zeros((b, h, dk, dv), jnp.float32)
    _, o = lax.scan(step, s0, (q, k, v, log_a))
    return o


def chunk_retention(q, k, v, log_gamma):
    idx = jnp.arange(CHUNK, dtype=jnp.float32)
    rel = idx[:, None] - idx[None, :]
    causal = (rel >= 0)[None]
    intra = jnp.where(causal, jnp.exp(jnp.where(causal, rel[None], 0.0) * log_gamma[:, None, None]), 0.0)
    q_decay = jnp.exp((idx[None, :] + 1.0) * log_gamma[:, None])[..., None]
    k_decay = jnp.exp((CHUNK - 1.0 - idx[None, :]) * log_gamma[:, None])[..., None]
    chunk_decay = jnp.exp(CHUNK * log_gamma)[:, None, None]
    nc, b, h, c, dk = q.shape
    dv = v.shape[-1]

    def step(state, inp):
        qc, kc, vc = inp
        scores = jnp.einsum('bhid,bhjd->bhij', qc, kc) * intra
        o = (jnp.einsum('bhij,bhjv->bhiv', scores, vc)
             + jnp.einsum('bhid,bhdv->bhiv', qc * q_decay, state))
        state = chunk_decay * state + jnp.einsum('bhjd,bhjv->bhdv', kc * k_decay, vc)
        return state, o

    s0 = jnp.zeros((b, h, dk, dv), jnp.float32)
    _, o = lax.scan(step, s0, (q, k, v))
    return o


def hybrid_mixer(u, w_in, gla_w_gate2, gla_b_gate, gla_norm, ret_norm, hgrn_lb, hgrn_norm, w_branch, w_out):
    b, l, _ = u.shape
    f32 = jnp.float32
    split_at = [int(s) for s in np.cumsum(IN_SPLITS)[:-1]]
    (gq, gk, gv, gg, g_lr, rq, rk, rv, rg, hq, hf, hi, hg, mg) = jnp.split(u @ w_in, split_at, axis=-1)

    g_log = jax.nn.log_sigmoid((g_lr @ gla_w_gate2 + gla_b_gate).astype(f32)) / GLA_GATE_TAU
    o = chunk_gated_linear_attention(
        to_chunks(gq.astype(f32) * GLA_DK ** -0.5, GLA_HEADS), to_chunks(gk, GLA_HEADS),
        to_chunks(gv, GLA_HEADS), to_chunks(g_log, GLA_HEADS))
    y_gla = head_rms_norm(from_chunks(o), GLA_HEADS, gla_norm) * jax.nn.silu(gg.astype(f32))

    pos = jnp.arange(l, dtype=f32)
    rq = rotary(rq.astype(f32).reshape(b, l, RET_HEADS, RET_DK), pos).reshape(b, l, -1) * RET_DK ** -0.5
    rk = rotary(rk.astype(f32).reshape(b, l, RET_HEADS, RET_DK), pos).reshape(b, l, -1)
    log_gamma = jnp.log(1.0 - 2.0 ** (-5.0 - jnp.arange(RET_HEADS, dtype=f32)))
    o = chunk_retention(to_chunks(rq, RET_HEADS), to_chunks(rk, RET_HEADS), to_chunks(rv, RET_HEADS), log_gamma)
    y_ret = head_group_norm(from_chunks(o), RET_HEADS, ret_norm) * jax.nn.silu(rg.astype(f32))

    hf = hf.astype(f32)
    lb = hgrn_lb.astype(f32)
    forget = lb + (1.0 - lb) * jax.nn.sigmoid(hf)
    log_f = jnp.log(jnp.maximum(forget, FORGET_FLOOR))
    k_in = (1.0 - lb) * jax.nn.sigmoid(-hf)
    i_in = jax.nn.silu(hi.astype(f32))
    o = chunk_gated_linear_attention(
        to_chunks(hq, HGRN_HEADS), to_chunks(k_in, HGRN_HEADS),
        to_chunks(i_in, HGRN_HEADS), to_chunks(log_f, HGRN_HEADS))
    y_hgrn = head_rms_norm(from_chunks(o), HGRN_HEADS, hgrn_norm) * jax.nn.silu(hg.astype(f32))

    ys = jnp.stack([y_gla, y_ret, y_hgrn], axis=2).astype(u.dtype)
    gates = jax.nn.sigmoid(mg.astype(f32)).reshape(b, l, N_BRANCH, D_MODEL).astype(u.dtype)
    merged = (jnp.einsum('blnw,nwd->blnd', ys, w_branch) * gates).sum(axis=2)
    return merged @ w_out


def setup_inputs(seed: int = 0) -> dict:
    key = jax.random.key(seed)
    ks = jax.random.split(key, 24)
    f32 = jnp.float32

    def nrm(k, shape, scale):
        return jax.random.normal(k, shape, f32) * scale

    def gain(k, shape):
        return 1.0 + 0.02 * jax.random.normal(k, shape, f32)

    return {
        'x': nrm(ks[0], (BATCH, SEQ, D_MODEL), 1.0),
        'meta_tokens': nrm(ks[1], (N_META, D_MODEL), 1.0),
        'ffn1_norm': gain(ks[2], (DEPTH, D_MODEL)),
        'ffn1_w_gate': nrm(ks[3], (DEPTH, D_MODEL, D_FF), D_MODEL ** -0.5),
        'ffn1_w_up': nrm(ks[4], (DEPTH, D_MODEL, D_FF), D_MODEL ** -0.5),
        'ffn1_w_down': nrm(ks[5], (DEPTH, D_FF, D_MODEL), D_FF ** -0.5),
        'mix_norm': gain(ks[6], (DEPTH, D_MODEL)),
        'w_in': nrm(ks[7], (DEPTH, D_MODEL, W_IN), D_MODEL ** -0.5),
        'gla_w_gate2': nrm(ks[8], (DEPTH, GLA_GATE_RANK, GLA_HEADS * GLA_DK), GLA_GATE_RANK ** -0.5),
        'gla_b_gate': nrm(ks[9], (DEPTH, GLA_HEADS * GLA_DK), 0.1),
        'gla_norm': gain(ks[10], (DEPTH, GLA_HEADS * GLA_DV)),
        'ret_norm': gain(ks[11], (DEPTH, RET_HEADS * RET_DV)),
        'hgrn_lb_logits': nrm(ks[12], (DEPTH, HGRN_HEADS * HGRN_DK), 0.1),
        'hgrn_norm': gain(ks[13], (DEPTH, HGRN_HEADS * HGRN_DV)),
        'w_branch': nrm(ks[14], (DEPTH, N_BRANCH, BRANCH_WIDTH, D_MODEL), BRANCH_WIDTH ** -0.5),
        'w_out': nrm(ks[15], (DEPTH, D_MODEL, D_MODEL), D_MODEL ** -0.5),
        'ffn2_norm': gain(ks[16], (DEPTH, D_MODEL)),
        'ffn2_w_gate': nrm(ks[17], (DEPTH, D_MODEL, D_FF), D_MODEL ** -0.5),
        'ffn2_w_up': nrm(ks[18], (DEPTH, D_MODEL, D_FF), D_MODEL ** -0.5),
        'ffn2_w_down': nrm(ks[19], (DEPTH, D_FF, D_MODEL), D_FF ** -0.5),
        'final_norm': gain(ks[20], (D_MODEL,)),
    }


def reference(x, meta_tokens, ffn1_norm, ffn1_w_gate, ffn1_w_up, ffn1_w_down, mix_norm, w_in,
              gla_w_gate2, gla_b_gate, gla_norm, ret_norm, hgrn_lb_logits, hgrn_norm, w_branch, w_out,
              ffn2_norm, ffn2_w_gate, ffn2_w_up, ffn2_w_down, final_norm):
    b = x.shape[0]
    meta = jnp.broadcast_to(meta_tokens[None].astype(x.dtype), (b, N_META, D_MODEL))
    h = jnp.concatenate([meta, x], axis=1)
    lb_soft = jax.nn.softmax(hgrn_lb_logits.astype(jnp.float32), axis=0)
    lower_bounds = jnp.cumsum(lb_soft, axis=0) - lb_soft[0]
    for layer in range(DEPTH):
        h = h + FFN_RES * swiglu(rms_norm(h, ffn1_norm[layer]),
                                 ffn1_w_gate[layer], ffn1_w_up[layer], ffn1_w_down[layer])
        h = h + hybrid_mixer(rms_norm(h, mix_norm[layer]), w_in[layer], gla_w_gate2[layer], gla_b_gate[layer],
                             gla_norm[layer], ret_norm[layer], lower_bounds[layer], hgrn_norm[layer],
                             w_branch[layer], w_out[layer])
        h = h + FFN_RES * swiglu(rms_norm(h, ffn2_norm[layer]),
                                 ffn2_w_gate[layer], ffn2_w_up[layer], ffn2_w_down[layer])
    return rms_norm(h, final_norm)[:, N_META:]
```

```python
import functools

import numpy as np
import jax
import jax.numpy as jnp
from jax import lax
from jax.experimental import pallas as pl
from jax.experimental.pallas import tpu as pltpu

D_MODEL = 2048
BATCH = 2
SEQ = 4096
DEPTH = 2
N_META = 16
CHUNK = 64
PAD = CHUNK - N_META
D_FF = 5504
FFN_RES = 0.5
EPS = 1e-6
N_BRANCH = 3
BRANCH_WIDTH = 1024

GLA_HEADS, GLA_DK, GLA_DV = 4, 128, 256
GLA_GATE_RANK = 16
GLA_GATE_TAU = 16.0
RET_HEADS, RET_DK, RET_DV = 4, 256, 256
ROPE_BASE = 10000.0
HGRN_HEADS, HGRN_DK, HGRN_DV = 8, 128, 128
FORGET_FLOOR = 1e-20

LANE = 128
L_PAD = PAD + N_META + SEQ
N_CHUNKS = L_PAD // CHUNK
ROWS = BATCH * L_PAD
N_LEVELS = 6
assert 1 << N_LEVELS == CHUNK

D_FF_PAD = 5632
FF_TILE = 512
ROW_TILE = 640
PROJ_TILE = 512
LR_PAD = 512

OFF_MG = 0
OFF_GLA = OFF_MG + N_BRANCH * D_MODEL
OFF_RET = OFF_GLA + 2 * GLA_HEADS * GLA_DK + 2 * GLA_HEADS * GLA_DV
OFF_HGRN = OFF_RET + 2 * RET_HEADS * RET_DK + 2 * RET_HEADS * RET_DV
OFF_LR = OFF_HGRN + 2 * HGRN_HEADS * HGRN_DK + 2 * HGRN_HEADS * HGRN_DV
W_PROJ = OFF_LR + LR_PAD

VMEM_LIMIT = 56 * 1024 * 1024

F32 = jnp.float32
BF16 = jnp.bfloat16


def _params(*sem):
    return pltpu.CompilerParams(dimension_semantics=sem, vmem_limit_bytes=VMEM_LIMIT)


def _rms_rows(x, gain):
    ms = jnp.mean(x * x, axis=-1, keepdims=True)
    return x * lax.rsqrt(ms + EPS) * gain


def _sigmoid(x):
    return 1.0 / (1.0 + jnp.exp(-x))


def _silu(x):
    return x * _sigmoid(x)


def _dot(a, b):
    return jnp.dot(a, b, preferred_element_type=F32)


def _dot_nt(a, b):
    return lax.dot_general(a, b, (((1,), (1,)), ((), ())), preferred_element_type=F32)


def _dot_tn(a, b):
    return lax.dot_general(a, b, (((0,), (0,)), ((), ())), preferred_element_type=F32)


def _ffn_kernel(h_ref, g_ref, wg_ref, wu_ref, wd_ref, fg_ref, o_ref, u_scr, acc_scr, *, final):
    j = pl.program_id(1)

    @pl.when(j == 0)
    def _():
        u_scr[...] = _rms_rows(h_ref[...], g_ref[...]).astype(BF16)
        acc_scr[...] = jnp.zeros_like(acc_scr)

    u = u_scr[...]
    a = _dot(u, wg_ref[...])
    b = _dot(u, wu_ref[...])
    act = (_silu(a) * b).astype(BF16)
    acc_scr[...] += _dot(act, wd_ref[...])

    @pl.when(j == pl.num_programs(1) - 1)
    def _():
        y = h_ref[...] + FFN_RES * acc_scr[...]
        if final:
            y = _rms_rows(y, fg_ref[...])
        o_ref[...] = y


def _ffn(h, gain, wg, wu, wd, final_gain, final):
    grid = (ROWS // ROW_TILE, D_FF_PAD // FF_TILE)
    return pl.pallas_call(
        functools.partial(_ffn_kernel, final=final),
        grid=grid,
        in_specs=[
            pl.BlockSpec((ROW_TILE, D_MODEL), lambda i, j: (i, 0)),
            pl.BlockSpec((1, D_MODEL), lambda i, j: (0, 0)),
            pl.BlockSpec((D_MODEL, FF_TILE), lambda i, j: (0, j)),
            pl.BlockSpec((D_MODEL, FF_TILE), lambda i, j: (0, j)),
            pl.BlockSpec((FF_TILE, D_MODEL), lambda i, j: (j, 0)),
            pl.BlockSpec((1, D_MODEL), lambda i, j: (0, 0)),
        ],
        out_specs=pl.BlockSpec((ROW_TILE, D_MODEL), lambda i, j: (i, 0)),
        out_shape=jax.ShapeDtypeStruct((ROWS, D_MODEL), F32),
        scratch_shapes=[pltpu.VMEM((ROW_TILE, D_MODEL), BF16),
                        pltpu.VMEM((ROW_TILE, D_MODEL), F32)],
        compiler_params=_params("parallel", "arbitrary"),
        name="ffn",
    )(h, gain, wg, wu, wd, final_gain)


def _proj_kernel(h_ref, g_ref, w_ref, o_ref, u_scr):
    @pl.when(pl.program_id(1) == 0)
    def _():
        u_scr[...] = _rms_rows(h_ref[...], g_ref[...]).astype(BF16)

    o_ref[...] = _dot(u_scr[...], w_ref[...])


def _proj(h, gain, w):
    grid = (ROWS // ROW_TILE, W_PROJ // PROJ_TILE)
    return pl.pallas_call(
        _proj_kernel,
        grid=grid,
        in_specs=[
            pl.BlockSpec((ROW_TILE, D_MODEL), lambda i, j: (i, 0)),
            pl.BlockSpec((1, D_MODEL), lambda i, j: (0, 0)),
            pl.BlockSpec((D_MODEL, PROJ_TILE), lambda i, j: (0, j)),
        ],
        out_specs=pl.BlockSpec((ROW_TILE, PROJ_TILE), lambda i, j: (i, j)),
        out_shape=jax.ShapeDtypeStruct((ROWS, W_PROJ), F32),
        scratch_shapes=[pltpu.VMEM((ROW_TILE, D_MODEL), BF16)],
        compiler_params=_params("parallel", "arbitrary"),
        name="in_proj",
    )(h, gain, w)


def _decay_tables():
    c = CHUNK
    blocks = []
    r = np.arange(c)
    for l in range(N_LEVELS):
        b = 1 << l
        m = np.zeros((c, c), np.float32)
        for i in range(c):
            p = i % (2 * b)
            s = i - p + b
            if p >= b:
                m[i, s:i + 1] = 1.0
            else:
                m[i, i + 1:s] = 1.0
        blocks.append(m)
    blocks.append((r[None, :] <= r[:, None]).astype(np.float32))
    blocks.append((r[None, :] > r[:, None]).astype(np.float32))
    blocks.append(np.ones((8, c), np.float32))
    wc = np.concatenate(blocks, axis=0)

    masks = np.zeros((N_LEVELS + 1, c, c), np.float32)
    for l in range(N_LEVELS):
        b = 1 << l
        same = (r[:, None] // (2 * b)) == (r[None, :] // (2 * b))
        up = (r[:, None] & b) != 0
        lo = (r[None, :] & b) == 0
        masks[l] = (same & up & lo).astype(np.float32)
    masks[N_LEVELS] = np.eye(c, dtype=np.float32)
    return wc, masks


def _gla_core(q, k, v, g, wc_ref, mk_ref, s_ref, n_heads, dk, dv):
    c = CHUNK
    g_hi = g.astype(BF16)
    r1 = g - g_hi.astype(F32)
    g_mid = r1.astype(BF16)
    g_lo = (r1 - g_mid.astype(F32)).astype(BF16)
    wc = wc_ref[...]
    ex = jnp.exp(_dot(wc, g_hi) + _dot(wc, g_mid) + _dot(wc, g_lo))

    rows = lax.broadcasted_iota(jnp.int32, (c, 1), 0)
    qb = q.astype(BF16)
    kb = k.astype(BF16)
    vb = v.astype(BF16)
    scores = []
    for h in range(n_heads):
        scores.append(mk_ref[N_LEVELS] * _dot_nt(qb[:, h * dk:(h + 1) * dk], kb[:, h * dk:(h + 1) * dk]))
    for l in range(N_LEVELS):
        upper = (rows & (1 << l)) != 0
        x = (jnp.where(upper, q, k) * ex[l * c:(l + 1) * c]).astype(BF16)
        for h in range(n_heads):
            xh = x[:, h * dk:(h + 1) * dk]
            scores[h] = scores[h] + mk_ref[l] * _dot_nt(xh, xh)

    base = N_LEVELS * c
    qt = (q * ex[base:base + c]).astype(BF16)
    kt = (k * ex[base + c:base + 2 * c]).astype(BF16)
    tot = ex[base + 2 * c:base + 2 * c + 1]
    outs = []
    for h in range(n_heads):
        st = s_ref[h]
        vh = vb[:, h * dv:(h + 1) * dv]
        o = _dot(scores[h].astype(BF16), vh) + _dot_nt(qt[:, h * dk:(h + 1) * dk], st.astype(BF16))
        s_ref[h] = st * tot[:, h * dk:(h + 1) * dk] + _dot_tn(vh, kt[:, h * dk:(h + 1) * dk])
        outs.append(o)
    return outs


def _head_rms(outs, gain, gate):
    dv = outs[0].shape[-1]
    ys = []
    for h, o in enumerate(outs):
        ms = jnp.mean(o * o, axis=-1, keepdims=True)
        ys.append(o * lax.rsqrt(ms + EPS))
    y = jnp.concatenate(ys, axis=-1)
    return y * gain * _silu(gate)


def _log_sigmoid(z):
    return jnp.minimum(z, 0.0) - jnp.log1p(jnp.exp(-jnp.abs(z)))


def _gla_kernel(q_ref, k_ref, v_ref, gate_ref, lr_ref, w2_ref, b_ref, norm_ref, wc_ref, mk_ref,
                o_ref, s_ref):
    @pl.when(pl.program_id(1) == 0)
    def _():
        s_ref[...] = jnp.zeros_like(s_ref)

    z = _dot(lr_ref[...].astype(BF16), w2_ref[...]) + b_ref[...]
    g = _log_sigmoid(z) * (1.0 / GLA_GATE_TAU)
    q = q_ref[...] * (GLA_DK ** -0.5)
    outs = _gla_core(q, k_ref[...], v_ref[...], g, wc_ref, mk_ref, s_ref, GLA_HEADS, GLA_DK, GLA_DV)
    o_ref[...] = _head_rms(outs, norm_ref[...], gate_ref[...]).astype(o_ref.dtype)


def _hgrn_kernel(q_ref, f_ref, i_ref, gate_ref, lb_ref, norm_ref, wc_ref, mk_ref, o_ref, s_ref, *, layer):
    @pl.when(pl.program_id(1) == 0)
    def _():
        s_ref[...] = jnp.zeros_like(s_ref)

    logits = lb_ref[...]
    e = jnp.exp(logits - jnp.max(logits, axis=0, keepdims=True))
    soft = e / jnp.sum(e, axis=0, keepdims=True)
    lb = jnp.zeros((1, logits.shape[1]), F32)
    for l in range(1, layer + 1):
        lb = lb + soft[l:l + 1]

    hf = f_ref[...]
    forget = lb + (1.0 - lb) * _sigmoid(hf)
    g = jnp.log(jnp.maximum(forget, FORGET_FLOOR))
    k = (1.0 - lb) * _sigmoid(-hf)
    v = _silu(i_ref[...])
    outs = _gla_core(q_ref[...], k, v, g, wc_ref, mk_ref, s_ref, HGRN_HEADS, HGRN_DK, HGRN_DV)
    o_ref[...] = _head_rms(outs, norm_ref[...], gate_ref[...]).astype(o_ref.dtype)


def _row_block(b, c):
    return b * N_CHUNKS + c


def _col_spec(width, offset):
    assert offset % width == 0
    blk = offset // width
    return pl.BlockSpec((CHUNK, width), lambda b, c: (_row_block(b, c), blk))


def _const_spec(shape):
    nd = len(shape)
    return pl.BlockSpec(shape, lambda b, c: (0,) * nd)


def _gla(proj, w2, bias, norm, wc, masks):
    qk = GLA_HEADS * GLA_DK
    vw = GLA_HEADS * GLA_DV
    return pl.pallas_call(
        _gla_kernel,
        grid=(BATCH, N_CHUNKS),
        in_specs=[
            _col_spec(qk, OFF_GLA),
            _col_spec(qk, OFF_GLA + qk),
            _col_spec(vw, OFF_GLA + 2 * qk),
            _col_spec(vw, OFF_GLA + 2 * qk + vw),
            _col_spec(LANE, OFF_LR),
            _const_spec(w2.shape),
            _const_spec(bias.shape),
            _const_spec(norm.shape),
            _const_spec(wc.shape),
            _const_spec(masks.shape),
        ],
        out_specs=pl.BlockSpec((CHUNK, vw), lambda b, c: (_row_block(b, c), 0)),
        out_shape=jax.ShapeDtypeStruct((ROWS, vw), BF16),
        scratch_shapes=[pltpu.VMEM((GLA_HEADS, GLA_DV, GLA_DK), F32)],
        compiler_params=_params("parallel", "arbitrary"),
        name="gla",
    )(proj, proj, proj, proj, proj, w2, bias, norm, wc, masks)


def _hgrn(proj, lb_logits, norm, wc, masks, layer):
    w = HGRN_HEADS * HGRN_DK
    return pl.pallas_call(
        functools.partial(_hgrn_kernel, layer=layer),
        grid=(BATCH, N_CHUNKS),
        in_specs=[
            _col_spec(w, OFF_HGRN),
            _col_spec(w, OFF_HGRN + w),
            _col_spec(w, OFF_HGRN + 2 * w),
            _col_spec(w, OFF_HGRN + 3 * w),
            _const_spec(lb_logits.shape),
            _const_spec(norm.shape),
            _const_spec(wc.shape),
            _const_spec(masks.shape),
        ],
        out_specs=pl.BlockSpec((CHUNK, w), lambda b, c: (_row_block(b, c), 0)),
        out_shape=jax.ShapeDtypeStruct((ROWS, w), BF16),
        scratch_shapes=[pltpu.VMEM((HGRN_HEADS, HGRN_DV, HGRN_DK), F32)],
        compiler_params=_params("parallel", "arbitrary"),
        name="hgrn",
    )(proj, proj, proj, proj, lb_logits, norm, wc, masks)


def _ret_kernel(q_ref, k_ref, v_ref, gate_ref, cos_ref, sin_ref, qd_ref, kd_ref, cd_ref, im_ref, norm_ref,
                o_ref, s_ref):
    @pl.when(pl.program_id(1) == 0)
    def _():
        s_ref[...] = jnp.zeros_like(s_ref)

    half = RET_DK // 2
    cos = cos_ref[...]
    sin = sin_ref[...]
    q = q_ref[...]
    k = k_ref[...]
    v = v_ref[...].astype(BF16)
    gate = gate_ref[...]
    ys = []
    for h in range(RET_HEADS):
        lo = h * RET_DK
        q1, q2 = q[:, lo:lo + half], q[:, lo + half:lo + RET_DK]
        k1, k2 = k[:, lo:lo + half], k[:, lo + half:lo + RET_DK]
        qr = jnp.concatenate([q1 * cos - q2 * sin, q1 * sin + q2 * cos], axis=-1) * (RET_DK ** -0.5)
        kr = jnp.concatenate([k1 * cos - k2 * sin, k1 * sin + k2 * cos], axis=-1)
        vh = v[:, h * RET_DV:(h + 1) * RET_DV]
        scores = _dot_nt(qr.astype(BF16), kr.astype(BF16)) * im_ref[h]
        st = s_ref[h]
        o = _dot(scores.astype(BF16), vh) + _dot_nt((qr * qd_ref[h]).astype(BF16), st.astype(BF16))
        s_ref[h] = st * cd_ref[h] + _dot_tn(vh, (kr * kd_ref[h]).astype(BF16))
        o = o - jnp.mean(o, axis=-1, keepdims=True)
        ms = jnp.mean(o * o, axis=-1, keepdims=True)
        ys.append(o * lax.rsqrt(ms + EPS))
    y = jnp.concatenate(ys, axis=-1)
    o_ref[...] = (y * norm_ref[...] * _silu(gate)).astype(o_ref.dtype)


def _ret(proj, cos, sin, qd, kd, cd, im, norm):
    w = RET_HEADS * RET_DK
    half = RET_DK // 2
    return pl.pallas_call(
        _ret_kernel,
        grid=(BATCH, N_CHUNKS),
        in_specs=[
            _col_spec(w, OFF_RET),
            _col_spec(w, OFF_RET + w),
            _col_spec(w, OFF_RET + 2 * w),
            _col_spec(w, OFF_RET + 3 * w),
            pl.BlockSpec((CHUNK, half), lambda b, c: (c, 0)),
            pl.BlockSpec((CHUNK, half), lambda b, c: (c, 0)),
            _const_spec(qd.shape),
            _const_spec(kd.shape),
            _const_spec(cd.shape),
            _const_spec(im.shape),
            _const_spec(norm.shape),
        ],
        out_specs=pl.BlockSpec((CHUNK, w), lambda b, c: (_row_block(b, c), 0)),
        out_shape=jax.ShapeDtypeStruct((ROWS, w), BF16),
        scratch_shapes=[pltpu.VMEM((RET_HEADS, RET_DV, RET_DK), F32)],
        compiler_params=_params("parallel", "arbitrary"),
        name="retention",
    )(proj, proj, proj, proj, cos, sin, qd, kd, cd, im, norm)


def _ret_tables():
    f32 = jnp.float32
    half = RET_DK // 2
    pos = jnp.arange(L_PAD, dtype=f32) - PAD
    inv_freq = ROPE_BASE ** (-jnp.arange(half, dtype=f32) / half)
    ang = pos[:, None] * inv_freq[None, :]
    log_gamma = jnp.log(1.0 - 2.0 ** (-5.0 - jnp.arange(RET_HEADS, dtype=f32)))
    idx = jnp.arange(CHUNK, dtype=f32)
    rel = idx[:, None] - idx[None, :]
    causal = (rel >= 0)[None]
    intra = jnp.where(causal, jnp.exp(jnp.where(causal, rel[None], 0.0) * log_gamma[:, None, None]), 0.0)
    q_decay = jnp.exp((idx[None, :] + 1.0) * log_gamma[:, None])[..., None]
    k_decay = jnp.exp((CHUNK - 1.0 - idx[None, :]) * log_gamma[:, None])[..., None]
    chunk_decay = jnp.exp(CHUNK * log_gamma)[:, None, None]
    return jnp.cos(ang), jnp.sin(ang), q_decay, k_decay, chunk_decay, intra


def _merge_kernel(y_ref, mg_ref, wb_ref, wo_ref, h_ref, o_ref, acc_scr):
    n = pl.program_id(1)
    t = _dot(y_ref[0], wb_ref[0]) * _sigmoid(mg_ref[...])

    @pl.when(n == 0)
    def _():
        acc_scr[...] = t

    @pl.when(n > 0)
    def _():
        acc_scr[...] += t

    @pl.when(n == N_BRANCH - 1)
    def _():
        o_ref[...] = h_ref[...] + _dot(acc_scr[...].astype(BF16), wo_ref[...])


MERGE_TILE = 320


def _merge(ys, proj, wb, wo, h):
    grid = (ROWS // MERGE_TILE, N_BRANCH)
    return pl.pallas_call(
        _merge_kernel,
        grid=grid,
        in_specs=[
            pl.BlockSpec((1, MERGE_TILE, BRANCH_WIDTH), lambda i, n: (n, i, 0)),
            pl.BlockSpec((MERGE_TILE, D_MODEL), lambda i, n: (i, n)),
            pl.BlockSpec((1, BRANCH_WIDTH, D_MODEL), lambda i, n: (n, 0, 0)),
            pl.BlockSpec((D_MODEL, D_MODEL), lambda i, n: (0, 0)),
            pl.BlockSpec((MERGE_TILE, D_MODEL), lambda i, n: (i, 0)),
        ],
        out_specs=pl.BlockSpec((MERGE_TILE, D_MODEL), lambda i, n: (i, 0)),
        out_shape=jax.ShapeDtypeStruct((ROWS, D_MODEL), F32),
        scratch_shapes=[pltpu.VMEM((MERGE_TILE, D_MODEL), F32)],
        compiler_params=_params("parallel", "arbitrary"),
        name="merge",
    )(ys, proj, wb, wo, h)


def _prep_w_in(w):
    s = np.cumsum([
        GLA_HEADS * GLA_DK, GLA_HEADS * GLA_DK, GLA_HEADS * GLA_DV, GLA_HEADS * GLA_DV, GLA_GATE_RANK,
        RET_HEADS * RET_DK, RET_HEADS * RET_DK, RET_HEADS * RET_DV, RET_HEADS * RET_DV,
        HGRN_HEADS * HGRN_DK, HGRN_HEADS * HGRN_DK, HGRN_HEADS * HGRN_DV, HGRN_HEADS * HGRN_DV,
    ])
    gla, lr, rest, mg = w[:, :s[3]], w[:, s[3]:s[4]], w[:, s[4]:s[12]], w[:, s[12]:]
    lr = jnp.pad(lr, ((0, 0), (0, LR_PAD - GLA_GATE_RANK)))
    return jnp.concatenate([mg, gla, rest, lr], axis=1).astype(BF16)


def kernel(x, meta_tokens, ffn1_norm, ffn1_w_gate, ffn1_w_up, ffn1_w_down, mix_norm, w_in, gla_w_gate2, gla_b_gate, gla_norm, ret_norm, hgrn_lb_logits, hgrn_norm, w_branch, w_out, ffn2_norm, ffn2_w_gate, ffn2_w_up, ffn2_w_down, final_norm):
    b = x.shape[0]
    meta = jnp.broadcast_to(meta_tokens[None].astype(x.dtype), (b, N_META, D_MODEL))
    h = jnp.concatenate([jnp.zeros((b, PAD, D_MODEL), x.dtype), meta, x], axis=1).reshape(ROWS, D_MODEL)

    wc_np, masks_np = _decay_tables()
    wc = jnp.asarray(wc_np, BF16)
    masks = jnp.asarray(masks_np, F32)
    cos, sin, qd, kd, cd, im = _ret_tables()
    ff_pad = D_FF_PAD - D_FF

    def ffn_weights(wg, wu, wd):
        return (jnp.pad(wg, ((0, 0), (0, ff_pad))).astype(BF16),
                jnp.pad(wu, ((0, 0), (0, ff_pad))).astype(BF16),
                jnp.pad(wd, ((0, ff_pad), (0, 0))).astype(BF16))

    row = lambda v: v.reshape(1, -1).astype(F32)
    final_gain = row(final_norm)
    for layer in range(DEPTH):
        h = _ffn(h, row(ffn1_norm[layer]),
                 *ffn_weights(ffn1_w_gate[layer], ffn1_w_up[layer], ffn1_w_down[layer]), final_gain, False)
        proj = _proj(h, row(mix_norm[layer]), _prep_w_in(w_in[layer]))
        w2 = jnp.pad(gla_w_gate2[layer], ((0, LANE - GLA_GATE_RANK), (0, 0))).astype(BF16)
        y_gla = _gla(proj, w2, row(gla_b_gate[layer]), row(gla_norm[layer]), wc, masks)
        y_ret = _ret(proj, cos, sin, qd, kd, cd, im, row(ret_norm[layer]))
        y_hgrn = _hgrn(proj, hgrn_lb_logits.astype(F32), row(hgrn_norm[layer]), wc, masks, layer)
        ys = jnp.stack([y_gla, y_ret, y_hgrn], axis=0)
        h = _merge(ys, proj, w_branch[layer].astype(BF16), w_out[layer].astype(BF16), h)
        h = _ffn(h, row(ffn2_norm[layer]),
                 *ffn_weights(ffn2_w_gate[layer], ffn2_w_up[layer], ffn2_w_down[layer]), final_gain,
                 layer == DEPTH - 1)
    return h.reshape(b, L_PAD, D_MODEL)[:, CHUNK:]
```

```python
import functools

import numpy as np
import jax
import jax.numpy as jnp
from jax import lax
from jax.experimental import pallas as pl
from jax.experimental.pallas import tpu as pltpu

D_MODEL = 2048
BATCH = 2
SEQ = 4096
DEPTH = 2
N_META = 16
CHUNK = 64
PAD = CHUNK - N_META
D_FF = 5504
FFN_RES = 0.5
EPS = 1e-6
N_BRANCH = 3
BRANCH_WIDTH = 1024

GLA_HEADS, GLA_DK, GLA_DV = 4, 128, 256
GLA_GATE_RANK = 16
GLA_GATE_TAU = 16.0
RET_HEADS, RET_DK, RET_DV = 4, 256, 256
ROPE_BASE = 10000.0
HGRN_HEADS, HGRN_DK, HGRN_DV = 8, 128, 128
FORGET_FLOOR = 1e-20

LANE = 128
L_PAD = PAD + N_META + SEQ
N_CHUNKS = L_PAD // CHUNK
ROWS = BATCH * L_PAD
N_LEVELS = 6
assert 1 << N_LEVELS == CHUNK

D_FF_PAD = 5632
FF_TILE = 512
ROW_TILE = 640
PROJ_ROW_TILE = 832
PROJ_TILE = 1792
MERGE_TILE = 320
CAST_TILE = 512
LR_PAD = CAST_TILE
W_IN_SHIFT = GLA_GATE_RANK

OFF_MG = 0
OFF_GLA = OFF_MG + N_BRANCH * D_MODEL
OFF_RET = OFF_GLA + 2 * GLA_HEADS * GLA_DK + 2 * GLA_HEADS * GLA_DV
OFF_HGRN = OFF_RET + 2 * RET_HEADS * RET_DK + 2 * RET_HEADS * RET_DV
OFF_LR = OFF_HGRN + 2 * HGRN_HEADS * HGRN_DK + 2 * HGRN_HEADS * HGRN_DV
W_PROJ = OFF_LR + LR_PAD

VMEM_LIMIT = 56 * 1024 * 1024

F32 = jnp.float32
BF16 = jnp.bfloat16


def _params(*sem):
    return pltpu.CompilerParams(dimension_semantics=sem, vmem_limit_bytes=VMEM_LIMIT)


def _rms_rows(x, gain):
    ms = jnp.mean(x * x, axis=-1, keepdims=True)
    return x * lax.rsqrt(ms + EPS) * gain


def _sigmoid(x):
    return 1.0 / (1.0 + jnp.exp(-x))


def _silu(x):
    return x * _sigmoid(x)


def _dot(a, b):
    return jnp.dot(a, b, preferred_element_type=F32)


def _dot_nt(a, b):
    return lax.dot_general(a, b, (((1,), (1,)), ((), ())), preferred_element_type=F32)


def _dot_tn(a, b):
    return lax.dot_general(a, b, (((0,), (0,)), ((), ())), preferred_element_type=F32)


def _cast_kernel(w_ref, o_ref, *, axis, valid):
    x = w_ref[...]
    idx = pl.program_id(0) * x.shape[axis] + lax.broadcasted_iota(jnp.int32, x.shape, axis)
    o_ref[...] = jnp.where(idx < valid, x, 0.0).astype(BF16)


def _cast(w, lead, axis, padded):
    rows, cols = w.shape[-2:]
    nl = len(lead)
    if axis == 0:
        block, out_shape, grid = (CAST_TILE, cols), (padded, cols), padded // CAST_TILE
        imap = lambda j: (*lead, j, 0)
        omap = lambda j: (j, 0)
    else:
        block, out_shape, grid = (rows, CAST_TILE), (rows, padded), padded // CAST_TILE
        imap = lambda j: (*lead, 0, j)
        omap = lambda j: (0, j)
    return pl.pallas_call(
        functools.partial(_cast_kernel, axis=axis, valid=w.shape[-2 + axis]),
        grid=(grid,),
        in_specs=[pl.BlockSpec((None,) * nl + block, imap)],
        out_specs=pl.BlockSpec(block, omap),
        out_shape=jax.ShapeDtypeStruct(out_shape, BF16),
        compiler_params=_params("parallel"),
        name="cast",
    )(w)


_N_MG, _N_GLA, _N_RET, _N_HGRN = 12, 6, 8, 8
_SRC_TILE = np.concatenate([22 + np.arange(_N_MG), np.arange(_N_GLA), 6 + np.arange(_N_RET),
                            14 + np.arange(_N_HGRN), [6]]).astype(np.int32)


def _w_in_kernel(src_ref, a_ref, b_ref, o_ref):
    del src_ref
    j = pl.program_id(0)
    gla_lo, gla_hi, lr = _N_MG, _N_MG + _N_GLA, _N_MG + _N_GLA + _N_RET + _N_HGRN
    step = 256

    @pl.when((j < gla_lo) | ((j >= gla_hi) & (j < lr)))
    def _():
        for r in range(0, D_MODEL, step):
            xx = jnp.concatenate([a_ref[r:r + step, :], b_ref[r:r + step, :]], axis=1)
            o_ref[r:r + step, :] = xx[:, W_IN_SHIFT:W_IN_SHIFT + CAST_TILE].astype(BF16)

    @pl.when((j >= gla_lo) & (j < gla_hi))
    def _():
        o_ref[...] = a_ref[...].astype(BF16)

    @pl.when(j == lr)
    def _():
        col = lax.broadcasted_iota(jnp.int32, a_ref.shape, 1)
        o_ref[...] = jnp.where(col < GLA_GATE_RANK, a_ref[...], 0.0).astype(BF16)


def _cast_w_in(w_in, layer):
    n_tiles = W_PROJ // CAST_TILE
    assert n_tiles == _SRC_TILE.shape[0]
    lanes_per_tile = CAST_TILE // LANE
    return pl.pallas_call(
        _w_in_kernel,
        grid_spec=pltpu.PrefetchScalarGridSpec(
            num_scalar_prefetch=1,
            grid=(n_tiles,),
            in_specs=[
                pl.BlockSpec((None, D_MODEL, CAST_TILE), lambda j, src: (layer, 0, src[j])),
                pl.BlockSpec((None, D_MODEL, LANE), lambda j, src: (layer, 0, (src[j] + 1) * lanes_per_tile)),
            ],
            out_specs=pl.BlockSpec((D_MODEL, CAST_TILE), lambda j, src: (0, j)),
        ),
        out_shape=jax.ShapeDtypeStruct((D_MODEL, W_PROJ), BF16),
        compiler_params=_params("arbitrary"),
        name="cast_w_in",
    )(jnp.asarray(_SRC_TILE), w_in, w_in)


def _ffn_kernel(h_ref, g_ref, wg_ref, wu_ref, wd_ref, fg_ref, o_ref, u_scr, acc_scr, *, final):
    j = pl.program_id(1)

    @pl.when(j == 0)
    def _():
        u_scr[...] = _rms_rows(h_ref[...], g_ref[...]).astype(BF16)
        acc_scr[...] = jnp.zeros_like(acc_scr)

    u = u_scr[...]
    a = _dot(u, wg_ref[...])
    b = _dot(u, wu_ref[...])
    act = (_silu(a) * b).astype(BF16)
    acc_scr[...] += _dot(act, wd_ref[...])

    @pl.when(j == pl.num_programs(1) - 1)
    def _():
        y = h_ref[...] + FFN_RES * acc_scr[...]
        if final:
            y = _rms_rows(y, fg_ref[...])
        o_ref[...] = y


def _ffn(h, gain, wg, wu, wd, final_gain, final):
    grid = (ROWS // ROW_TILE, D_FF_PAD // FF_TILE)
    return pl.pallas_call(
        functools.partial(_ffn_kernel, final=final),
        grid=grid,
        in_specs=[
            pl.BlockSpec((ROW_TILE, D_MODEL), lambda i, j: (i, 0)),
            pl.BlockSpec((1, D_MODEL), lambda i, j: (0, 0)),
            pl.BlockSpec((D_MODEL, FF_TILE), lambda i, j: (0, j)),
            pl.BlockSpec((D_MODEL, FF_TILE), lambda i, j: (0, j)),
            pl.BlockSpec((FF_TILE, D_MODEL), lambda i, j: (j, 0)),
            pl.BlockSpec((1, D_MODEL), lambda i, j: (0, 0)),
        ],
        out_specs=pl.BlockSpec((ROW_TILE, D_MODEL), lambda i, j: (i, 0)),
        out_shape=jax.ShapeDtypeStruct((ROWS, D_MODEL), F32),
        scratch_shapes=[pltpu.VMEM((ROW_TILE, D_MODEL), BF16),
                        pltpu.VMEM((ROW_TILE, D_MODEL), F32)],
        compiler_params=_params("parallel", "arbitrary"),
        name="ffn",
    )(h, gain, wg, wu, wd, final_gain)


def _proj_kernel(h_ref, g_ref, w_ref, o_ref, u_scr):
    @pl.when(pl.program_id(1) == 0)
    def _():
        u_scr[...] = _rms_rows(h_ref[...], g_ref[...]).astype(BF16)

    o_ref[...] = _dot(u_scr[...], w_ref[...]).astype(o_ref.dtype)


def _proj(h, gain, w):
    grid = (ROWS // PROJ_ROW_TILE, W_PROJ // PROJ_TILE)
    return pl.pallas_call(
        _proj_kernel,
        grid=grid,
        in_specs=[
            pl.BlockSpec((PROJ_ROW_TILE, D_MODEL), lambda i, j: (i, 0)),
            pl.BlockSpec((1, D_MODEL), lambda i, j: (0, 0)),
            pl.BlockSpec((D_MODEL, PROJ_TILE), lambda i, j: (0, j)),
        ],
        out_specs=pl.BlockSpec((PROJ_ROW_TILE, PROJ_TILE), lambda i, j: (i, j)),
        out_shape=jax.ShapeDtypeStruct((ROWS, W_PROJ), BF16),
        scratch_shapes=[pltpu.VMEM((PROJ_ROW_TILE, D_MODEL), BF16)],
        compiler_params=_params("parallel", "arbitrary"),
        name="in_proj",
    )(h, gain, w)


def _decay_tables():
    c = CHUNK
    blocks = []
    r = np.arange(c)
    for l in range(N_LEVELS):
        b = 1 << l
        m = np.zeros((c, c), np.float32)
        for i in range(c):
            p = i % (2 * b)
            s = i - p + b
            if p >= b:
                m[i, s:i + 1] = 1.0
            else:
                m[i, i + 1:s] = 1.0
        blocks.append(m)
    blocks.append((r[None, :] <= r[:, None]).astype(np.float32))
    blocks.append((r[None, :] > r[:, None]).astype(np.float32))
    blocks.append(np.ones((8, c), np.float32))
    wc = np.concatenate(blocks, axis=0)

    masks = np.zeros((N_LEVELS + 1, c, c), np.float32)
    for l in range(N_LEVELS):
        b = 1 << l
        same = (r[:, None] // (2 * b)) == (r[None, :] // (2 * b))
        up = (r[:, None] & b) != 0
        lo = (r[None, :] & b) == 0
        masks[l] = (same & up & lo).astype(np.float32)
    masks[N_LEVELS] = np.eye(c, dtype=np.float32)
    return wc, masks


def _gla_core(q, k, v, g, wc_ref, mk_ref, s_ref, n_heads, dk, dv):
    c = CHUNK
    g_hi = g.astype(BF16)
    r1 = g - g_hi.astype(F32)
    g_mid = r1.astype(BF16)
    g_lo = (r1 - g_mid.astype(F32)).astype(BF16)
    wc = wc_ref[...]
    ex = jnp.exp(_dot(wc, g_hi) + _dot(wc, g_mid) + _dot(wc, g_lo))

    rows = lax.broadcasted_iota(jnp.int32, (c, 1), 0)
    qb = q.astype(BF16)
    kb = k.astype(BF16)
    vb = v.astype(BF16)
    scores = []
    for h in range(n_heads):
        scores.append(mk_ref[N_LEVELS] * _dot_nt(qb[:, h * dk:(h + 1) * dk], kb[:, h * dk:(h + 1) * dk]))
    for l in range(N_LEVELS):
        upper = (rows & (1 << l)) != 0
        x = (jnp.where(upper, q, k) * ex[l * c:(l + 1) * c]).astype(BF16)
        for h in range(n_heads):
            xh = x[:, h * dk:(h + 1) * dk]
            scores[h] = scores[h] + mk_ref[l] * _dot_nt(xh, xh)

    base = N_LEVELS * c
    qt = (q * ex[base:base + c]).astype(BF16)
    kt = (k * ex[base + c:base + 2 * c]).astype(BF16)
    tot = ex[base + 2 * c:base + 2 * c + 1]
    outs = []
    for h in range(n_heads):
        st = s_ref[h]
        vh = vb[:, h * dv:(h + 1) * dv]
        o = _dot(scores[h].astype(BF16), vh) + _dot_nt(qt[:, h * dk:(h + 1) * dk], st.astype(BF16))
        s_ref[h] = st * tot[:, h * dk:(h + 1) * dk] + _dot_tn(vh, kt[:, h * dk:(h + 1) * dk])
        outs.append(o)
    return outs


def _head_rms(outs, gain, gate):
    dv = outs[0].shape[-1]
    ys = []
    for h, o in enumerate(outs):
        ms = jnp.mean(o * o, axis=-1, keepdims=True)
        ys.append(o * lax.rsqrt(ms + EPS))
    y = jnp.concatenate(ys, axis=-1)
    return y * gain * _silu(gate)


def _log_sigmoid(z):
    return jnp.minimum(z, 0.0) - jnp.log1p(jnp.exp(-jnp.abs(z)))


def _gla_kernel(q_ref, k_ref, v_ref, gate_ref, lr_ref, w2_ref, b_ref, norm_ref, wc_ref, mk_ref,
                o_ref, s_ref):
    @pl.when(pl.program_id(1) == 0)
    def _():
        s_ref[...] = jnp.zeros_like(s_ref)

    z = _dot(lr_ref[...], w2_ref[...]) + b_ref[...]
    g = _log_sigmoid(z) * (1.0 / GLA_GATE_TAU)
    q = q_ref[...].astype(F32) * (GLA_DK ** -0.5)
    outs = _gla_core(q, k_ref[...].astype(F32), v_ref[...].astype(F32), g, wc_ref, mk_ref, s_ref,
                     GLA_HEADS, GLA_DK, GLA_DV)
    o_ref[...] = _head_rms(outs, norm_ref[...], gate_ref[...].astype(F32)).astype(o_ref.dtype)


def _hgrn_kernel(q_ref, f_ref, i_ref, gate_ref, lb_ref, norm_ref, wc_ref, mk_ref, o_ref, s_ref, *, layer):
    @pl.when(pl.program_id(1) == 0)
    def _():
        s_ref[...] = jnp.zeros_like(s_ref)

    logits = lb_ref[...]
    e = jnp.exp(logits - jnp.max(logits, axis=0, keepdims=True))
    soft = e / jnp.sum(e, axis=0, keepdims=True)
    lb = jnp.zeros((1, logits.shape[1]), F32)
    for l in range(1, layer + 1):
        lb = lb + soft[l:l + 1]

    hf = f_ref[...].astype(F32)
    forget = lb + (1.0 - lb) * _sigmoid(hf)
    g = jnp.log(jnp.maximum(forget, FORGET_FLOOR))
    k = (1.0 - lb) * _sigmoid(-hf)
    v = _silu(i_ref[...].astype(F32))
    outs = _gla_core(q_ref[...].astype(F32), k, v, g, wc_ref, mk_ref, s_ref, HGRN_HEADS, HGRN_DK, HGRN_DV)
    o_ref[...] = _head_rms(outs, norm_ref[...], gate_ref[...].astype(F32)).astype(o_ref.dtype)


def _row_block(b, c):
    return b * N_CHUNKS + c


def _col_spec(width, offset):
    assert offset % width == 0
    blk = offset // width
    return pl.BlockSpec((CHUNK, width), lambda b, c: (_row_block(b, c), blk))


def _const_spec(shape):
    nd = len(shape)
    return pl.BlockSpec(shape, lambda b, c: (0,) * nd)


def _gla(proj, w2, bias, norm, wc, masks):
    qk = GLA_HEADS * GLA_DK
    vw = GLA_HEADS * GLA_DV
    return pl.pallas_call(
        _gla_kernel,
        grid=(BATCH, N_CHUNKS),
        in_specs=[
            _col_spec(qk, OFF_GLA),
            _col_spec(qk, OFF_GLA + qk),
            _col_spec(vw, OFF_GLA + 2 * qk),
            _col_spec(vw, OFF_GLA + 2 * qk + vw),
            _col_spec(LANE, OFF_LR),
            _const_spec(w2.shape),
            _const_spec(bias.shape),
            _const_spec(norm.shape),
            _const_spec(wc.shape),
            _const_spec(masks.shape),
        ],
        out_specs=pl.BlockSpec((CHUNK, vw), lambda b, c: (_row_block(b, c), 0)),
        out_shape=jax.ShapeDtypeStruct((ROWS, vw), BF16),
        scratch_shapes=[pltpu.VMEM((GLA_HEADS, GLA_DV, GLA_DK), F32)],
        compiler_params=_params("parallel", "arbitrary"),
        name="gla",
    )(proj, proj, proj, proj, proj, w2, bias, norm, wc, masks)


def _hgrn(proj, lb_logits, norm, wc, masks, layer):
    w = HGRN_HEADS * HGRN_DK
    return pl.pallas_call(
        functools.partial(_hgrn_kernel, layer=layer),
        grid=(BATCH, N_CHUNKS),
        in_specs=[
            _col_spec(w, OFF_HGRN),
            _col_spec(w, OFF_HGRN + w),
            _col_spec(w, OFF_HGRN + 2 * w),
            _col_spec(w, OFF_HGRN + 3 * w),
            _const_spec(lb_logits.shape),
            _const_spec(norm.shape),
            _const_spec(wc.shape),
            _const_spec(masks.shape),
        ],
        out_specs=pl.BlockSpec((CHUNK, w), lambda b, c: (_row_block(b, c), 0)),
        out_shape=jax.ShapeDtypeStruct((ROWS, w), BF16),
        scratch_shapes=[pltpu.VMEM((HGRN_HEADS, HGRN_DV, HGRN_DK), F32)],
        compiler_params=_params("parallel", "arbitrary"),
        name="hgrn",
    )(proj, proj, proj, proj, lb_logits, norm, wc, masks)


def _ret_kernel(q_ref, k_ref, v_ref, gate_ref, cos_ref, sin_ref, qd_ref, kd_ref, cd_ref, im_ref, norm_ref,
                o_ref, s_ref):
    @pl.when(pl.program_id(1) == 0)
    def _():
        s_ref[...] = jnp.zeros_like(s_ref)

    half = RET_DK // 2
    cos = cos_ref[...]
    sin = sin_ref[...]
    q = q_ref[...].astype(F32)
    k = k_ref[...].astype(F32)
    v = v_ref[...]
    gate = gate_ref[...].astype(F32)
    ys = []
    for h in range(RET_HEADS):
        lo = h * RET_DK
        q1, q2 = q[:, lo:lo + half], q[:, lo + half:lo + RET_DK]
        k1, k2 = k[:, lo:lo + half], k[:, lo + half:lo + RET_DK]
        qr = jnp.concatenate([q1 * cos - q2 * sin, q1 * sin + q2 * cos], axis=-1) * (RET_DK ** -0.5)
        kr = jnp.concatenate([k1 * cos - k2 * sin, k1 * sin + k2 * cos], axis=-1)
        vh = v[:, h * RET_DV:(h + 1) * RET_DV]
        scores = _dot_nt(qr.astype(BF16), kr.astype(BF16)) * im_ref[h]
        st = s_ref[h]
        o = _dot(scores.astype(BF16), vh) + _dot_nt((qr * qd_ref[h]).astype(BF16), st.astype(BF16))
        s_ref[h] = st * cd_ref[h] + _dot_tn(vh, (kr * kd_ref[h]).astype(BF16))
        o = o - jnp.mean(o, axis=-1, keepdims=True)
        ms = jnp.mean(o * o, axis=-1, keepdims=True)
        ys.append(o * lax.rsqrt(ms + EPS))
    y = jnp.concatenate(ys, axis=-1)
    o_ref[...] = (y * norm_ref[...] * _silu(gate)).astype(o_ref.dtype)


def _ret(proj, cos, sin, qd, kd, cd, im, norm):
    w = RET_HEADS * RET_DK
    half = RET_DK // 2
    return pl.pallas_call(
        _ret_kernel,
        grid=(BATCH, N_CHUNKS),
        in_specs=[
            _col_spec(w, OFF_RET),
            _col_spec(w, OFF_RET + w),
            _col_spec(w, OFF_RET + 2 * w),
            _col_spec(w, OFF_RET + 3 * w),
            pl.BlockSpec((CHUNK, half), lambda b, c: (c, 0)),
            pl.BlockSpec((CHUNK, half), lambda b, c: (c, 0)),
            _const_spec(qd.shape),
            _const_spec(kd.shape),
            _const_spec(cd.shape),
            _const_spec(im.shape),
            _const_spec(norm.shape),
        ],
        out_specs=pl.BlockSpec((CHUNK, w), lambda b, c: (_row_block(b, c), 0)),
        out_shape=jax.ShapeDtypeStruct((ROWS, w), BF16),
        scratch_shapes=[pltpu.VMEM((RET_HEADS, RET_DV, RET_DK), F32)],
        compiler_params=_params("parallel", "arbitrary"),
        name="retention",
    )(proj, proj, proj, proj, cos, sin, qd, kd, cd, im, norm)


def _ret_tables():
    f32 = jnp.float32
    half = RET_DK // 2
    pos = jnp.arange(L_PAD, dtype=f32) - PAD
    inv_freq = ROPE_BASE ** (-jnp.arange(half, dtype=f32) / half)
    ang = pos[:, None] * inv_freq[None, :]
    log_gamma = jnp.log(1.0 - 2.0 ** (-5.0 - jnp.arange(RET_HEADS, dtype=f32)))
    idx = jnp.arange(CHUNK, dtype=f32)
    rel = idx[:, None] - idx[None, :]
    causal = (rel >= 0)[None]
    intra = jnp.where(causal, jnp.exp(jnp.where(causal, rel[None], 0.0) * log_gamma[:, None, None]), 0.0)
    q_decay = jnp.exp((idx[None, :] + 1.0) * log_gamma[:, None])[..., None]
    k_decay = jnp.exp((CHUNK - 1.0 - idx[None, :]) * log_gamma[:, None])[..., None]
    chunk_decay = jnp.exp(CHUNK * log_gamma)[:, None, None]
    return jnp.cos(ang), jnp.sin(ang), q_decay, k_decay, chunk_decay, intra


def _merge_kernel(y0_ref, y1_ref, y2_ref, mg_ref, wb0_ref, wb1_ref, wb2_ref, wo_ref, h_ref, o_ref):
    merged = None
    for n, (y_ref, wb_ref) in enumerate(((y0_ref, wb0_ref), (y1_ref, wb1_ref), (y2_ref, wb2_ref))):
        gate = _sigmoid(mg_ref[:, n * D_MODEL:(n + 1) * D_MODEL].astype(F32))
        t = _dot(y_ref[...], wb_ref[...]) * gate
        merged = t if merged is None else merged + t
    o_ref[...] = h_ref[...] + _dot(merged.astype(BF16), wo_ref[...])


def _merge(ys, proj, wbs, wo, h):
    assert OFF_MG == 0
    rows = lambda width: pl.BlockSpec((MERGE_TILE, width), lambda i: (i, 0))
    whole = lambda a: pl.BlockSpec(a.shape, lambda i: (0, 0))
    return pl.pallas_call(
        _merge_kernel,
        grid=(ROWS // MERGE_TILE,),
        in_specs=[rows(BRANCH_WIDTH)] * N_BRANCH + [rows(N_BRANCH * D_MODEL)]
        + [whole(w) for w in wbs] + [whole(wo), rows(D_MODEL)],
        out_specs=rows(D_MODEL),
        out_shape=jax.ShapeDtypeStruct((ROWS, D_MODEL), F32),
        compiler_params=_params("parallel"),
        name="merge",
    )(*ys, proj, *wbs, wo, h)


def kernel(x, meta_tokens, ffn1_norm, ffn1_w_gate, ffn1_w_up, ffn1_w_down, mix_norm, w_in, gla_w_gate2, gla_b_gate, gla_norm, ret_norm, hgrn_lb_logits, hgrn_norm, w_branch, w_out, ffn2_norm, ffn2_w_gate, ffn2_w_up, ffn2_w_down, final_norm):
    b = x.shape[0]
    meta = jnp.broadcast_to(meta_tokens[None].astype(x.dtype), (b, N_META, D_MODEL))
    h = jnp.concatenate([jnp.zeros((b, PAD, D_MODEL), x.dtype), meta, x], axis=1).reshape(ROWS, D_MODEL)

    wc_np, masks_np = _decay_tables()
    wc = jnp.asarray(wc_np, BF16)
    masks = jnp.asarray(masks_np, F32)
    cos, sin, qd, kd, cd, im = _ret_tables()

    def ffn_weights(wg, wu, wd, layer):
        return (_cast(wg, (layer,), 1, D_FF_PAD), _cast(wu, (layer,), 1, D_FF_PAD),
                _cast(wd, (layer,), 0, D_FF_PAD))

    row = lambda v: v.reshape(1, -1).astype(F32)
    final_gain = row(final_norm)
    for layer in range(DEPTH):
        h = _ffn(h, row(ffn1_norm[layer]), *ffn_weights(ffn1_w_gate, ffn1_w_up, ffn1_w_down, layer),
                 final_gain, False)
        proj = _proj(h, row(mix_norm[layer]), _cast_w_in(w_in, layer))
        w2 = jnp.pad(gla_w_gate2[layer], ((0, LANE - GLA_GATE_RANK), (0, 0))).astype(BF16)
        y_gla = _gla(proj, w2, row(gla_b_gate[layer]), row(gla_norm[layer]), wc, masks)
        y_ret = _ret(proj, cos, sin, qd, kd, cd, im, row(ret_norm[layer]))
        y_hgrn = _hgrn(proj, hgrn_lb_logits.astype(F32), row(hgrn_norm[layer]), wc, masks, layer)
        wbs = [_cast(w_branch, (layer, n), 0, BRANCH_WIDTH) for n in range(N_BRANCH)]
        h = _merge((y_gla, y_ret, y_hgrn), proj, wbs, _cast(w_out, (layer,), 0, D_MODEL), h)
        h = _ffn(h, row(ffn2_norm[layer]), *ffn_weights(ffn2_w_gate, ffn2_w_up, ffn2_w_down, layer),
                 final_gain, layer == DEPTH - 1)
    return h.reshape(b, L_PAD, D_MODEL)[:, CHUNK:]
```

```python
import functools

import numpy as np
import jax
import jax.numpy as jnp
from jax import lax
from jax.experimental import pallas as pl
from jax.experimental.pallas import tpu as pltpu

D_MODEL = 2048
BATCH = 2
SEQ = 4096
DEPTH = 2
N_META = 16
CHUNK = 64
PAD = CHUNK - N_META
D_FF = 5504
FFN_RES = 0.5
EPS = 1e-6
N_BRANCH = 3
BRANCH_WIDTH = 1024

GLA_HEADS, GLA_DK, GLA_DV = 4, 128, 256
GLA_GATE_RANK = 16
GLA_GATE_TAU = 16.0
RET_HEADS, RET_DK, RET_DV = 4, 256, 256
ROPE_BASE = 10000.0
HGRN_HEADS, HGRN_DK, HGRN_DV = 8, 128, 128
FORGET_FLOOR = 1e-20

LANE = 128
L_PAD = PAD + N_META + SEQ
N_CHUNKS = L_PAD // CHUNK
ROWS = BATCH * L_PAD
N_LEVELS = 6
assert 1 << N_LEVELS == CHUNK

D_FF_PAD = 5632
FF_TILE = 512
ROW_TILE = 640
PROJ_ROW_TILE = 832
PROJ_TILE = 1792
MERGE_TILE = 320
CAST_TILE = 512
LR_PAD = CAST_TILE
W_IN_SHIFT = GLA_GATE_RANK

OFF_MG = 0
OFF_GLA = OFF_MG + N_BRANCH * D_MODEL
OFF_RET = OFF_GLA + 2 * GLA_HEADS * GLA_DK + 2 * GLA_HEADS * GLA_DV
OFF_HGRN = OFF_RET + 2 * RET_HEADS * RET_DK + 2 * RET_HEADS * RET_DV
OFF_LR = OFF_HGRN + 2 * HGRN_HEADS * HGRN_DK + 2 * HGRN_HEADS * HGRN_DV
W_PROJ = OFF_LR + LR_PAD

VMEM_LIMIT = 56 * 1024 * 1024

F32 = jnp.float32
BF16 = jnp.bfloat16


def _params(*sem):
    return pltpu.CompilerParams(dimension_semantics=sem, vmem_limit_bytes=VMEM_LIMIT)


def _rms_rows(x, gain):
    ms = jnp.mean(x * x, axis=-1, keepdims=True)
    return x * lax.rsqrt(ms + EPS) * gain


def _sigmoid(x):
    return 1.0 / (1.0 + jnp.exp(-x))


def _silu(x):
    return x * _sigmoid(x)


def _dot(a, b):
    return jnp.dot(a, b, preferred_element_type=F32)


def _dot_nt(a, b):
    return lax.dot_general(a, b, (((1,), (1,)), ((), ())), preferred_element_type=F32)


def _dot_tn(a, b):
    return lax.dot_general(a, b, (((0,), (0,)), ((), ())), preferred_element_type=F32)


def _cast_kernel(w_ref, o_ref, *, axis, valid):
    x = w_ref[...]
    idx = pl.program_id(0) * x.shape[axis] + lax.broadcasted_iota(jnp.int32, x.shape, axis)
    o_ref[...] = jnp.where(idx < valid, x, 0.0).astype(BF16)


def _cast(w, lead, axis, padded):
    rows, cols = w.shape[-2:]
    nl = len(lead)
    if axis == 0:
        block, out_shape, grid = (CAST_TILE, cols), (padded, cols), padded // CAST_TILE
        imap = lambda j: (*lead, j, 0)
        omap = lambda j: (j, 0)
    else:
        block, out_shape, grid = (rows, CAST_TILE), (rows, padded), padded // CAST_TILE
        imap = lambda j: (*lead, 0, j)
        omap = lambda j: (0, j)
    return pl.pallas_call(
        functools.partial(_cast_kernel, axis=axis, valid=w.shape[-2 + axis]),
        grid=(grid,),
        in_specs=[pl.BlockSpec((None,) * nl + block, imap)],
        out_specs=pl.BlockSpec(block, omap),
        out_shape=jax.ShapeDtypeStruct(out_shape, BF16),
        compiler_params=_params("parallel"),
        name="cast",
    )(w)


_N_MG, _N_GLA, _N_RET, _N_HGRN = 12, 6, 8, 8
_SRC_TILE = np.concatenate([22 + np.arange(_N_MG), np.arange(_N_GLA), 6 + np.arange(_N_RET),
                            14 + np.arange(_N_HGRN), [6]]).astype(np.int32)


def _w_in_kernel(src_ref, a_ref, b_ref, o_ref):
    del src_ref
    j = pl.program_id(0)
    gla_lo, gla_hi, lr = _N_MG, _N_MG + _N_GLA, _N_MG + _N_GLA + _N_RET + _N_HGRN
    step = CAST_TILE

    def emit(rows_of):
        for c in range(0, D_MODEL, step):
            o_ref[c:c + step, :] = rows_of(c).T.astype(BF16)

    @pl.when((j < gla_lo) | ((j >= gla_hi) & (j < lr)))
    def _():
        emit(lambda c: jnp.concatenate([a_ref[W_IN_SHIFT:, c:c + step], b_ref[:, c:c + step]], axis=0))

    @pl.when((j >= gla_lo) & (j < gla_hi))
    def _():
        emit(lambda c: a_ref[:, c:c + step])

    @pl.when(j == lr)
    def _():
        row = lax.broadcasted_iota(jnp.int32, (CAST_TILE, step), 0)
        emit(lambda c: jnp.where(row < GLA_GATE_RANK, a_ref[:, c:c + step], 0.0))


def _cast_w_in(w_in, layer):
    n_tiles = W_PROJ // CAST_TILE
    assert n_tiles == _SRC_TILE.shape[0]
    shifts_per_tile = CAST_TILE // W_IN_SHIFT
    wt = jnp.swapaxes(w_in, 1, 2)
    return pl.pallas_call(
        _w_in_kernel,
        grid_spec=pltpu.PrefetchScalarGridSpec(
            num_scalar_prefetch=1,
            grid=(n_tiles,),
            in_specs=[
                pl.BlockSpec((None, CAST_TILE, D_MODEL), lambda j, src: (layer, src[j], 0)),
                pl.BlockSpec((None, W_IN_SHIFT, D_MODEL), lambda j, src: (layer, (src[j] + 1) * shifts_per_tile, 0)),
            ],
            out_specs=pl.BlockSpec((D_MODEL, CAST_TILE), lambda j, src: (0, j)),
        ),
        out_shape=jax.ShapeDtypeStruct((D_MODEL, W_PROJ), BF16),
        compiler_params=_params("arbitrary"),
        name="cast_w_in",
    )(jnp.asarray(_SRC_TILE), wt, wt)


def _ffn_kernel(h_ref, g_ref, wg_ref, wu_ref, wd_ref, fg_ref, o_ref, u_scr, acc_scr, *, final):
    j = pl.program_id(1)

    @pl.when(j == 0)
    def _():
        u_scr[...] = _rms_rows(h_ref[...], g_ref[...]).astype(BF16)
        acc_scr[...] = jnp.zeros_like(acc_scr)

    u = u_scr[...]
    a = _dot(u, wg_ref[...])
    b = _dot(u, wu_ref[...])
    act = (_silu(a) * b).astype(BF16)
    acc_scr[...] += _dot(act, wd_ref[...])

    @pl.when(j == pl.num_programs(1) - 1)
    def _():
        y = h_ref[...] + FFN_RES * acc_scr[...]
        if final:
            y = _rms_rows(y, fg_ref[...])
        o_ref[...] = y


def _ffn(h, gain, wg, wu, wd, final_gain, final):
    grid = (ROWS // ROW_TILE, D_FF_PAD // FF_TILE)
    return pl.pallas_call(
        functools.partial(_ffn_kernel, final=final),
        grid=grid,
        in_specs=[
            pl.BlockSpec((ROW_TILE, D_MODEL), lambda i, j: (i, 0)),
            pl.BlockSpec((1, D_MODEL), lambda i, j: (0, 0)),
            pl.BlockSpec((D_MODEL, FF_TILE), lambda i, j: (0, j)),
            pl.BlockSpec((D_MODEL, FF_TILE), lambda i, j: (0, j)),
            pl.BlockSpec((FF_TILE, D_MODEL), lambda i, j: (j, 0)),
            pl.BlockSpec((1, D_MODEL), lambda i, j: (0, 0)),
        ],
        out_specs=pl.BlockSpec((ROW_TILE, D_MODEL), lambda i, j: (i, 0)),
        out_shape=jax.ShapeDtypeStruct((ROWS, D_MODEL), F32),
        scratch_shapes=[pltpu.VMEM((ROW_TILE, D_MODEL), BF16),
                        pltpu.VMEM((ROW_TILE, D_MODEL), F32)],
        compiler_params=_params("parallel", "arbitrary"),
        name="ffn",
    )(h, gain, wg, wu, wd, final_gain)


def _proj_kernel(h_ref, g_ref, w_ref, o_ref, u_scr):
    @pl.when(pl.program_id(1) == 0)
    def _():
        u_scr[...] = _rms_rows(h_ref[...], g_ref[...]).astype(BF16)

    o_ref[...] = _dot(u_scr[...], w_ref[...]).astype(o_ref.dtype)


def _proj(h, gain, w):
    grid = (ROWS // PROJ_ROW_TILE, W_PROJ // PROJ_TILE)
    return pl.pallas_call(
        _proj_kernel,
        grid=grid,
        in_specs=[
            pl.BlockSpec((PROJ_ROW_TILE, D_MODEL), lambda i, j: (i, 0)),
            pl.BlockSpec((1, D_MODEL), lambda i, j: (0, 0)),
            pl.BlockSpec((D_MODEL, PROJ_TILE), lambda i, j: (0, j)),
        ],
        out_specs=pl.BlockSpec((PROJ_ROW_TILE, PROJ_TILE), lambda i, j: (i, j)),
        out_shape=jax.ShapeDtypeStruct((ROWS, W_PROJ), BF16),
        scratch_shapes=[pltpu.VMEM((PROJ_ROW_TILE, D_MODEL), BF16)],
        compiler_params=_params("parallel", "arbitrary"),
        name="in_proj",
    )(h, gain, w)


def _decay_tables():
    c = CHUNK
    blocks = []
    r = np.arange(c)
    for l in range(N_LEVELS):
        b = 1 << l
        m = np.zeros((c, c), np.float32)
        for i in range(c):
            p = i % (2 * b)
            s = i - p + b
            if p >= b:
                m[i, s:i + 1] = 1.0
            else:
                m[i, i + 1:s] = 1.0
        blocks.append(m)
    blocks.append((r[None, :] <= r[:, None]).astype(np.float32))
    blocks.append((r[None, :] > r[:, None]).astype(np.float32))
    blocks.append(np.ones((8, c), np.float32))
    wc = np.concatenate(blocks, axis=0)
    wc = np.concatenate([wc] * 3, axis=1)

    masks = np.zeros((N_LEVELS + 1, c, c), np.float32)
    for l in range(N_LEVELS):
        b = 1 << l
        same = (r[:, None] // (2 * b)) == (r[None, :] // (2 * b))
        up = (r[:, None] & b) != 0
        lo = (r[None, :] & b) == 0
        masks[l] = (same & up & lo).astype(np.float32)
    masks[N_LEVELS] = np.eye(c, dtype=np.float32)
    return wc, masks


def _gla_core(q, k, v, g, wc_ref, mk_ref, s_ref, s_base, n_heads, dk, dv):
    c = CHUNK
    g_hi = g.astype(BF16)
    r1 = g - g_hi.astype(F32)
    g_mid = r1.astype(BF16)
    g_lo = (r1 - g_mid.astype(F32)).astype(BF16)
    ex = jnp.exp(_dot(wc_ref[...], jnp.concatenate([g_hi, g_mid, g_lo], axis=0)))

    rows = lax.broadcasted_iota(jnp.int32, (c, 1), 0)
    qb = q.astype(BF16)
    kb = k.astype(BF16)
    vb = v.astype(BF16)
    scores = []
    for h in range(n_heads):
        scores.append(mk_ref[N_LEVELS] * _dot_nt(qb[:, h * dk:(h + 1) * dk], kb[:, h * dk:(h + 1) * dk]))
    for l in range(N_LEVELS):
        upper = (rows & (1 << l)) != 0
        x = (jnp.where(upper, q, k) * ex[l * c:(l + 1) * c]).astype(BF16)
        for h in range(n_heads):
            xh = x[:, h * dk:(h + 1) * dk]
            scores[h] = scores[h] + mk_ref[l] * _dot_nt(xh, xh)

    base = N_LEVELS * c
    qt = (q * ex[base:base + c]).astype(BF16)
    kt = (k * ex[base + c:base + 2 * c]).astype(BF16)
    tot = ex[base + 2 * c:base + 2 * c + 1]
    outs = []
    for h in range(n_heads):
        st = s_ref[s_base + h]
        vh = vb[:, h * dv:(h + 1) * dv]
        o = _dot(scores[h].astype(BF16), vh) + _dot_nt(qt[:, h * dk:(h + 1) * dk], st.astype(BF16))
        s_ref[s_base + h] = st * tot[:, h * dk:(h + 1) * dk] + _dot_tn(vh, kt[:, h * dk:(h + 1) * dk])
        outs.append(o)
    return outs


def _head_rms(outs, gain, gate):
    dv = outs[0].shape[-1]
    ys = []
    for h, o in enumerate(outs):
        ms = jnp.mean(o * o, axis=-1, keepdims=True)
        ys.append(o * lax.rsqrt(ms + EPS))
    y = jnp.concatenate(ys, axis=-1)
    return y * gain * _silu(gate)


def _log_sigmoid(z):
    return jnp.minimum(z, 0.0) - jnp.log1p(jnp.exp(-jnp.abs(z)))


def _gla_kernel(q_ref, k_ref, v_ref, gate_ref, lr_ref, w2_ref, b_ref, norm_ref, wc_ref, mk_ref,
                o_ref, s_ref):
    @pl.when(pl.program_id(0) == 0)
    def _():
        s_ref[...] = jnp.zeros_like(s_ref)

    for b in range(BATCH):
        z = _dot(lr_ref[b], w2_ref[...]) + b_ref[...]
        g = _log_sigmoid(z) * (1.0 / GLA_GATE_TAU)
        q = q_ref[b].astype(F32) * (GLA_DK ** -0.5)
        outs = _gla_core(q, k_ref[b].astype(F32), v_ref[b].astype(F32), g, wc_ref, mk_ref, s_ref,
                         b * GLA_HEADS, GLA_HEADS, GLA_DK, GLA_DV)
        o_ref[b] = _head_rms(outs, norm_ref[...], gate_ref[b].astype(F32)).astype(o_ref.dtype)


def _hgrn_kernel(q_ref, f_ref, i_ref, gate_ref, lb_ref, norm_ref, wc_ref, mk_ref, o_ref, s_ref, *, layer):
    @pl.when(pl.program_id(0) == 0)
    def _():
        s_ref[...] = jnp.zeros_like(s_ref)

    logits = lb_ref[...]
    e = jnp.exp(logits - jnp.max(logits, axis=0, keepdims=True))
    soft = e / jnp.sum(e, axis=0, keepdims=True)
    lb = jnp.zeros((1, logits.shape[1]), F32)
    for l in range(1, layer + 1):
        lb = lb + soft[l:l + 1]

    for b in range(BATCH):
        hf = f_ref[b].astype(F32)
        forget = lb + (1.0 - lb) * _sigmoid(hf)
        g = jnp.log(jnp.maximum(forget, FORGET_FLOOR))
        k = (1.0 - lb) * _sigmoid(-hf)
        v = _silu(i_ref[b].astype(F32))
        outs = _gla_core(q_ref[b].astype(F32), k, v, g, wc_ref, mk_ref, s_ref,
                         b * HGRN_HEADS, HGRN_HEADS, HGRN_DK, HGRN_DV)
        o_ref[b] = _head_rms(outs, norm_ref[...], gate_ref[b].astype(F32)).astype(o_ref.dtype)


def _col_spec(width, offset):
    assert offset % width == 0
    blk = offset // width
    return pl.BlockSpec((BATCH, CHUNK, width), lambda c: (0, c, blk))


def _const_spec(shape):
    nd = len(shape)
    return pl.BlockSpec(shape, lambda c: (0,) * nd)


def _mixer_out(width):
    return dict(
        out_specs=pl.BlockSpec((BATCH, CHUNK, width), lambda c: (0, c, 0)),
        out_shape=jax.ShapeDtypeStruct((BATCH, L_PAD, width), BF16),
        compiler_params=_params("arbitrary"),
    )


def _gla(proj, w2, bias, norm, wc, masks):
    qk = GLA_HEADS * GLA_DK
    vw = GLA_HEADS * GLA_DV
    return pl.pallas_call(
        _gla_kernel,
        grid=(N_CHUNKS,),
        in_specs=[
            _col_spec(qk, OFF_GLA),
            _col_spec(qk, OFF_GLA + qk),
            _col_spec(vw, OFF_GLA + 2 * qk),
            _col_spec(vw, OFF_GLA + 2 * qk + vw),
            _col_spec(LANE, OFF_LR),
            _const_spec(w2.shape),
            _const_spec(bias.shape),
            _const_spec(norm.shape),
            _const_spec(wc.shape),
            _const_spec(masks.shape),
        ],
        scratch_shapes=[pltpu.VMEM((BATCH * GLA_HEADS, GLA_DV, GLA_DK), F32)],
        name="gla",
        **_mixer_out(vw),
    )(proj, proj, proj, proj, proj, w2, bias, norm, wc, masks)


def _hgrn(proj, lb_logits, norm, wc, masks, layer):
    w = HGRN_HEADS * HGRN_DK
    return pl.pallas_call(
        functools.partial(_hgrn_kernel, layer=layer),
        grid=(N_CHUNKS,),
        in_specs=[
            _col_spec(w, OFF_HGRN),
            _col_spec(w, OFF_HGRN + w),
            _col_spec(w, OFF_HGRN + 2 * w),
            _col_spec(w, OFF_HGRN + 3 * w),
            _const_spec(lb_logits.shape),
            _const_spec(norm.shape),
            _const_spec(wc.shape),
            _const_spec(masks.shape),
        ],
        scratch_shapes=[pltpu.VMEM((BATCH * HGRN_HEADS, HGRN_DV, HGRN_DK), F32)],
        name="hgrn",
        **_mixer_out(w),
    )(proj, proj, proj, proj, lb_logits, norm, wc, masks)


def _ret_kernel(q_ref, k_ref, v_ref, gate_ref, cos_ref, sin_ref, qd_ref, kd_ref, cd_ref, im_ref, norm_ref,
                o_ref, s_ref):
    @pl.when(pl.program_id(0) == 0)
    def _():
        s_ref[...] = jnp.zeros_like(s_ref)

    half = RET_DK // 2
    cos = cos_ref[...]
    sin = sin_ref[...]
    for b in range(BATCH):
        q = q_ref[b].astype(F32)
        k = k_ref[b].astype(F32)
        v = v_ref[b]
        ys = []
        for h in range(RET_HEADS):
            lo = h * RET_DK
            q1, q2 = q[:, lo:lo + half], q[:, lo + half:lo + RET_DK]
            k1, k2 = k[:, lo:lo + half], k[:, lo + half:lo + RET_DK]
            qr = jnp.concatenate([q1 * cos - q2 * sin, q1 * sin + q2 * cos], axis=-1) * (RET_DK ** -0.5)
            kr = jnp.concatenate([k1 * cos - k2 * sin, k1 * sin + k2 * cos], axis=-1)
            vh = v[:, h * RET_DV:(h + 1) * RET_DV]
            scores = _dot_nt(qr.astype(BF16), kr.astype(BF16)) * im_ref[h]
            st = s_ref[b * RET_HEADS + h]
            o = _dot(scores.astype(BF16), vh) + _dot_nt((qr * qd_ref[h]).astype(BF16), st.astype(BF16))
            s_ref[b * RET_HEADS + h] = st * cd_ref[h] + _dot_tn(vh, (kr * kd_ref[h]).astype(BF16))
            o = o - jnp.mean(o, axis=-1, keepdims=True)
            ms = jnp.mean(o * o, axis=-1, keepdims=True)
            ys.append(o * lax.rsqrt(ms + EPS))
        y = jnp.concatenate(ys, axis=-1)
        o_ref[b] = (y * norm_ref[...] * _silu(gate_ref[b].astype(F32))).astype(o_ref.dtype)


def _ret(proj, cos, sin, qd, kd, cd, im, norm):
    w = RET_HEADS * RET_DK
    half = RET_DK // 2
    return pl.pallas_call(
        _ret_kernel,
        grid=(N_CHUNKS,),
        in_specs=[
            _col_spec(w, OFF_RET),
            _col_spec(w, OFF_RET + w),
            _col_spec(w, OFF_RET + 2 * w),
            _col_spec(w, OFF_RET + 3 * w),
            pl.BlockSpec((CHUNK, half), lambda c: (c, 0)),
            pl.BlockSpec((CHUNK, half), lambda c: (c, 0)),
            _const_spec(qd.shape),
            _const_spec(kd.shape),
            _const_spec(cd.shape),
            _const_spec(im.shape),
            _const_spec(norm.shape),
        ],
        scratch_shapes=[pltpu.VMEM((BATCH * RET_HEADS, RET_DV, RET_DK), F32)],
        name="retention",
        **_mixer_out(w),
    )(proj, proj, proj, proj, cos, sin, qd, kd, cd, im, norm)


def _ret_tables():
    f32 = jnp.float32
    half = RET_DK // 2
    pos = jnp.arange(L_PAD, dtype=f32) - PAD
    inv_freq = ROPE_BASE ** (-jnp.arange(half, dtype=f32) / half)
    ang = pos[:, None] * inv_freq[None, :]
    log_gamma = jnp.log(1.0 - 2.0 ** (-5.0 - jnp.arange(RET_HEADS, dtype=f32)))
    idx = jnp.arange(CHUNK, dtype=f32)
    rel = idx[:, None] - idx[None, :]
    causal = (rel >= 0)[None]
    intra = jnp.where(causal, jnp.exp(jnp.where(causal, rel[None], 0.0) * log_gamma[:, None, None]), 0.0)
    q_decay = jnp.exp((idx[None, :] + 1.0) * log_gamma[:, None])[..., None]
    k_decay = jnp.exp((CHUNK - 1.0 - idx[None, :]) * log_gamma[:, None])[..., None]
    chunk_decay = jnp.exp(CHUNK * log_gamma)[:, None, None]
    return jnp.cos(ang), jnp.sin(ang), q_decay, k_decay, chunk_decay, intra


def _merge_kernel(y0_ref, y1_ref, y2_ref, mg_ref, wb0_ref, wb1_ref, wb2_ref, wo_ref, h_ref, o_ref):
    merged = None
    for n, (y_ref, wb_ref) in enumerate(((y0_ref, wb0_ref), (y1_ref, wb1_ref), (y2_ref, wb2_ref))):
        gate = _sigmoid(mg_ref[:, n * D_MODEL:(n + 1) * D_MODEL].astype(F32))
        t = _dot(y_ref[...], wb_ref[...]) * gate
        merged = t if merged is None else merged + t
    o_ref[...] = h_ref[...] + _dot(merged.astype(BF16), wo_ref[...])


def _merge(ys, proj, wbs, wo, h):
    assert OFF_MG == 0
    rows = lambda width: pl.BlockSpec((MERGE_TILE, width), lambda i: (i, 0))
    whole = lambda a: pl.BlockSpec(a.shape, lambda i: (0, 0))
    return pl.pallas_call(
        _merge_kernel,
        grid=(ROWS // MERGE_TILE,),
        in_specs=[rows(BRANCH_WIDTH)] * N_BRANCH + [rows(N_BRANCH * D_MODEL)]
        + [whole(w) for w in wbs] + [whole(wo), rows(D_MODEL)],
        out_specs=rows(D_MODEL),
        out_shape=jax.ShapeDtypeStruct((ROWS, D_MODEL), F32),
        compiler_params=_params("parallel"),
        name="merge",
    )(*ys, proj, *wbs, wo, h)


def kernel(x, meta_tokens, ffn1_norm, ffn1_w_gate, ffn1_w_up, ffn1_w_down, mix_norm, w_in, gla_w_gate2, gla_b_gate, gla_norm, ret_norm, hgrn_lb_logits, hgrn_norm, w_branch, w_out, ffn2_norm, ffn2_w_gate, ffn2_w_up, ffn2_w_down, final_norm):
    b = x.shape[0]
    meta = jnp.broadcast_to(meta_tokens[None].astype(x.dtype), (b, N_META, D_MODEL))
    h = jnp.concatenate([jnp.zeros((b, PAD, D_MODEL), x.dtype), meta, x], axis=1).reshape(ROWS, D_MODEL)

    wc_np, masks_np = _decay_tables()
    wc = jnp.asarray(wc_np, BF16)
    masks = jnp.asarray(masks_np, F32)
    cos, sin, qd, kd, cd, im = _ret_tables()

    def ffn_weights(wg, wu, wd, layer):
        return (_cast(wg, (layer,), 1, D_FF_PAD), _cast(wu, (layer,), 1, D_FF_PAD),
                _cast(wd, (layer,), 0, D_FF_PAD))

    row = lambda v: v.reshape(1, -1).astype(F32)
    final_gain = row(final_norm)
    for layer in range(DEPTH):
        h = _ffn(h, row(ffn1_norm[layer]), *ffn_weights(ffn1_w_gate, ffn1_w_up, ffn1_w_down, layer),
                 final_gain, False)
        proj = _proj(h, row(mix_norm[layer]), _cast_w_in(w_in, layer))
        w2 = jnp.pad(gla_w_gate2[layer], ((0, LANE - GLA_GATE_RANK), (0, 0))).astype(BF16)
        proj_seq = proj.reshape(b, L_PAD, W_PROJ)
        y_gla = _gla(proj_seq, w2, row(gla_b_gate[layer]), row(gla_norm[layer]), wc, masks)
        y_ret = _ret(proj_seq, cos, sin, qd, kd, cd, im, row(ret_norm[layer]))
        y_hgrn = _hgrn(proj_seq, hgrn_lb_logits.astype(F32), row(hgrn_norm[layer]), wc, masks, layer)
        ys = [y.reshape(ROWS, BRANCH_WIDTH) for y in (y_gla, y_ret, y_hgrn)]
        wbs = [_cast(w_branch, (layer, n), 0, BRANCH_WIDTH) for n in range(N_BRANCH)]
        h = _merge(ys, proj, wbs, _cast(w_out, (layer,), 0, D_MODEL), h)
        h = _ffn(h, row(ffn2_norm[layer]), *ffn_weights(ffn2_w_gate, ffn2_w_up, ffn2_w_down, layer),
                 final_gain, layer == DEPTH - 1)
    return h.reshape(b, L_PAD, D_MODEL)[:, CHUNK:]
```

```python
import functools
from typing import NamedTuple

import numpy as np
import jax
import jax.numpy as jnp
from jax import lax
from jax.experimental import pallas as pl
from jax.experimental.pallas import tpu as pltpu

D_MODEL = 2048
BATCH = 2
SEQ = 4096
DEPTH = 2
N_META = 16
CHUNK = 64
PAD = CHUNK - N_META
D_FF = 5504
FFN_RES = 0.5
EPS = 1e-6
N_BRANCH = 3
BRANCH_WIDTH = 1024

GLA_HEADS, GLA_DK, GLA_DV = 4, 128, 256
GLA_GATE_RANK = 16
GLA_GATE_TAU = 16.0
RET_HEADS, RET_DK, RET_DV = 4, 256, 256
ROPE_BASE = 10000.0
HGRN_HEADS, HGRN_DK, HGRN_DV = 8, 128, 128
FORGET_FLOOR = 1e-20

LANE = 128
L_PAD = PAD + N_META + SEQ
N_CHUNKS = L_PAD // CHUNK
ROWS = BATCH * L_PAD
N_LEVELS = 6
assert 1 << N_LEVELS == CHUNK

D_FF_PAD = 5632
FF_TILE = 512
ROW_TILE = 640
PROJ_ROW_TILE = 832
PROJ_TILE = 1792
MERGE_TILE = 320
CAST_TILE = 512
SIDE_BLOCK = LANE
LR_PAD = CAST_TILE
W_IN_SHIFT = GLA_GATE_RANK

OFF_MG = 0
OFF_GLA = OFF_MG + N_BRANCH * D_MODEL
OFF_RET = OFF_GLA + 2 * GLA_HEADS * GLA_DK + 2 * GLA_HEADS * GLA_DV
OFF_HGRN = OFF_RET + 2 * RET_HEADS * RET_DK + 2 * RET_HEADS * RET_DV
OFF_LR = OFF_HGRN + 2 * HGRN_HEADS * HGRN_DK + 2 * HGRN_HEADS * HGRN_DV
W_PROJ = OFF_LR + LR_PAD

VMEM_LIMIT = 56 * 1024 * 1024

F32 = jnp.float32
BF16 = jnp.bfloat16


def _params(*sem):
    return pltpu.CompilerParams(dimension_semantics=sem, vmem_limit_bytes=VMEM_LIMIT)


def _rms_rows(x, gain):
    ms = jnp.mean(x * x, axis=-1, keepdims=True)
    return x * lax.rsqrt(ms + EPS) * gain


def _sigmoid(x):
    return 1.0 / (1.0 + jnp.exp(-x))


def _silu(x):
    return x * _sigmoid(x)


def _dot(a, b):
    return jnp.dot(a, b, preferred_element_type=F32)


def _dot_nt(a, b):
    return lax.dot_general(a, b, (((1,), (1,)), ((), ())), preferred_element_type=F32)


def _dot_tn(a, b):
    return lax.dot_general(a, b, (((0,), (0,)), ((), ())), preferred_element_type=F32)


def _cast_kernel(w_ref, o_ref, *, axis, valid):
    x = w_ref[...]
    idx = pl.program_id(0) * x.shape[axis] + lax.broadcasted_iota(jnp.int32, x.shape, axis)
    o_ref[...] = jnp.where(idx < valid, x, 0.0).astype(BF16)


def _cast(w, lead, axis, padded):
    rows, cols = w.shape[-2:]
    nl = len(lead)
    if axis == 0:
        block, out_shape, grid = (CAST_TILE, cols), (padded, cols), padded // CAST_TILE
        imap = lambda j: (*lead, j, 0)
        omap = lambda j: (j, 0)
    else:
        block, out_shape, grid = (rows, CAST_TILE), (rows, padded), padded // CAST_TILE
        imap = lambda j: (*lead, 0, j)
        omap = lambda j: (0, j)
    return pl.pallas_call(
        functools.partial(_cast_kernel, axis=axis, valid=w.shape[-2 + axis]),
        grid=(grid,),
        in_specs=[pl.BlockSpec((None,) * nl + block, imap)],
        out_specs=pl.BlockSpec(block, omap),
        out_shape=jax.ShapeDtypeStruct(out_shape, BF16),
        compiler_params=_params("parallel"),
        name="cast",
    )(w)


SIDE_IDLE, SIDE_COPY, SIDE_ZERO, SIDE_SHIFTED, SIDE_LOW_RANK = 0, 1, 2, 3, 4


class _SideCast(NamedTuple):
    src: jax.Array
    lead: tuple
    axis: int
    padded: int
    start: int


class _SideWIn(NamedTuple):
    src_t: jax.Array
    layer: int
    start: int


class _SidePlan(NamedTuple):
    table: np.ndarray
    in_specs: list
    operands: list
    out_specs: list
    out_shapes: list
    runners: list
    steps_per_row: int


def _w_in_blocks():
    per = lambda width: width // SIDE_BLOCK
    src = lambda col: col // SIDE_BLOCK
    gla_w = 2 * GLA_HEADS * GLA_DK + 2 * GLA_HEADS * GLA_DV
    ret_w = 2 * RET_HEADS * RET_DK + 2 * RET_HEADS * RET_DV
    hgrn_w = 2 * HGRN_HEADS * HGRN_DK + 2 * HGRN_HEADS * HGRN_DV
    lr_col = gla_w
    blocks = []
    for t in range(per(N_BRANCH * D_MODEL)):
        blocks.append((src(lr_col + ret_w + hgrn_w) + t, SIDE_SHIFTED))
    for t in range(per(gla_w)):
        blocks.append((t, SIDE_COPY))
    for t in range(per(ret_w + hgrn_w)):
        blocks.append((src(lr_col) + t, SIDE_SHIFTED))
    blocks.append((src(lr_col), SIDE_LOW_RANK))
    blocks += [(src(lr_col), SIDE_ZERO)] * (per(LR_PAD) - 1)
    assert len(blocks) == per(W_PROJ)
    return [(s, t, mode) for t, (s, mode) in enumerate(blocks)]


def _side_plan(jobs, n_rows, steps_per_row):
    n_steps = n_rows * steps_per_row
    rows, in_specs, operands, out_specs, out_shapes, runners = [], [], [], [], [], []

    def add_rows(blocks, start):
        n = len(blocks)
        assert start + n <= n_steps
        arr = np.asarray(blocks, np.int32).T
        tab = np.zeros((3, n_steps), np.int32)
        tab[:2, :start] = arr[:2, :1]
        tab[:, start:start + n] = arr
        tab[:2, start + n:] = arr[:2, -1:]
        base = len(rows) * n_steps
        rows.extend(tab)
        return base

    step_of = lambda i, j: i * steps_per_row + j
    for job in jobs:
        if isinstance(job, _SideCast):
            r, c = job.src.shape[-2:]
            valid = job.src.shape[-2 + job.axis]
            assert valid % SIDE_BLOCK == 0 and job.padded % SIDE_BLOCK == 0
            n_valid, n_all = valid // SIDE_BLOCK, job.padded // SIDE_BLOCK
            base = add_rows([(min(t, n_valid - 1), t, SIDE_COPY if t < n_valid else SIDE_ZERO)
                             for t in range(n_all)], job.start)
            lead, nl = job.lead, len(job.lead)
            if job.axis == 0:
                block, out_shape = (SIDE_BLOCK, c), (job.padded, c)
                imap = lambda i, j, tbl, base=base, lead=lead: (*lead, tbl[base + step_of(i, j)], 0)
                omap = lambda i, j, tbl, base=base: (tbl[base + n_steps + step_of(i, j)], 0)
            else:
                block, out_shape = (r, SIDE_BLOCK), (r, job.padded)
                imap = lambda i, j, tbl, base=base, lead=lead: (*lead, 0, tbl[base + step_of(i, j)])
                omap = lambda i, j, tbl, base=base: (0, tbl[base + n_steps + step_of(i, j)])
            in_specs.append(pl.BlockSpec((None,) * nl + block, imap))
            operands.append(job.src)

            def run(tbl_ref, step, ins, out, base=base):
                mode = tbl_ref[base + 2 * n_steps + step]

                @pl.when(mode != SIDE_IDLE)
                def _():
                    out[...] = jnp.where(mode == SIDE_ZERO, 0.0, ins[0][...]).astype(BF16)

            runners.append((1, run))
        else:
            base = add_rows(_w_in_blocks(), job.start)
            layer = job.layer
            shifts_per_block = SIDE_BLOCK // W_IN_SHIFT
            in_specs.append(pl.BlockSpec(
                (None, SIDE_BLOCK, D_MODEL),
                lambda i, j, tbl, base=base, layer=layer: (layer, tbl[base + step_of(i, j)], 0)))
            in_specs.append(pl.BlockSpec(
                (None, W_IN_SHIFT, D_MODEL),
                lambda i, j, tbl, base=base, layer=layer:
                (layer, (tbl[base + step_of(i, j)] + 1) * shifts_per_block, 0)))
            operands += [job.src_t, job.src_t]
            block, out_shape = (D_MODEL, SIDE_BLOCK), (D_MODEL, W_PROJ)
            omap = lambda i, j, tbl, base=base: (0, tbl[base + n_steps + step_of(i, j)])

            def run(tbl_ref, step, ins, out, base=base):
                a_ref, b_ref = ins
                mode = tbl_ref[base + 2 * n_steps + step]

                @pl.when(mode != SIDE_IDLE)
                def _():
                    shifted = jnp.concatenate([a_ref[W_IN_SHIFT:, :], b_ref[...]], axis=0)
                    x = jnp.where(mode == SIDE_SHIFTED, shifted, a_ref[...])
                    keep = jnp.where(mode == SIDE_LOW_RANK, GLA_GATE_RANK,
                                     jnp.where(mode == SIDE_ZERO, 0, SIDE_BLOCK))
                    row = lax.broadcasted_iota(jnp.int32, x.shape, 0)
                    out[...] = jnp.where(row < keep, x, 0.0).T.astype(BF16)

            runners.append((2, run))
        out_specs.append(pl.BlockSpec(block, omap))
        out_shapes.append(jax.ShapeDtypeStruct(out_shape, BF16))

    table = np.concatenate(rows) if rows else np.zeros((1,), np.int32)
    return _SidePlan(table, in_specs, operands, out_specs, out_shapes, runners, steps_per_row)


def _run_sides(plan, tbl_ref, side_ins, side_outs):
    step = pl.program_id(0) * plan.steps_per_row + pl.program_id(1)
    k = 0
    for (n_in, run), out in zip(plan.runners, side_outs):
        run(tbl_ref, step, side_ins[k:k + n_in], out)
        k += n_in


def _hosted_call(kernel, plan, grid, in_specs, out_spec, out_shape, scratch_shapes, name, operands):
    n_side_in = len(plan.in_specs)
    n_main_in = len(in_specs)

    def body(tbl_ref, *refs):
        ins = refs[:n_main_in]
        side_ins = refs[n_main_in:n_main_in + n_side_in]
        out = refs[n_main_in + n_side_in]
        side_outs = refs[n_main_in + n_side_in + 1:n_main_in + n_side_in + 1 + len(plan.out_specs)]
        scratch = refs[n_main_in + n_side_in + 1 + len(plan.out_specs):]
        kernel(*ins, out, *scratch)
        _run_sides(plan, tbl_ref, side_ins, side_outs)

    res = pl.pallas_call(
        body,
        grid_spec=pltpu.PrefetchScalarGridSpec(
            num_scalar_prefetch=1,
            grid=grid,
            in_specs=list(in_specs) + plan.in_specs,
            out_specs=[out_spec] + plan.out_specs,
            scratch_shapes=scratch_shapes,
        ),
        out_shape=[out_shape] + plan.out_shapes,
        compiler_params=_params("arbitrary", "arbitrary"),
        name=name,
    )(jnp.asarray(plan.table), *operands, *plan.operands)
    return res[0], res[1:]


def _ffn_kernel(h_ref, g_ref, wg_ref, wu_ref, wd_ref, fg_ref, o_ref, u_scr, *, final):
    j = pl.program_id(1)

    @pl.when(j == 0)
    def _():
        u_scr[...] = _rms_rows(h_ref[...], g_ref[...]).astype(BF16)
        o_ref[...] = jnp.zeros_like(o_ref)

    u = u_scr[...]
    a = _dot(u, wg_ref[...])
    b = _dot(u, wu_ref[...])
    act = (_silu(a) * b).astype(BF16)
    o_ref[...] += _dot(act, wd_ref[...])

    @pl.when(j == pl.num_programs(1) - 1)
    def _():
        y = h_ref[...] + FFN_RES * o_ref[...]
        if final:
            y = _rms_rows(y, fg_ref[...])
        o_ref[...] = y


def _ffn(h, gain, wg, wu, wd, final_gain, final, sides=()):
    grid = (ROWS // ROW_TILE, D_FF_PAD // FF_TILE)
    plan = _side_plan(sides, *grid)
    return _hosted_call(
        functools.partial(_ffn_kernel, final=final), plan, grid,
        in_specs=[
            pl.BlockSpec((ROW_TILE, D_MODEL), lambda i, j, tbl: (i, 0)),
            pl.BlockSpec((1, D_MODEL), lambda i, j, tbl: (0, 0)),
            pl.BlockSpec((D_MODEL, FF_TILE), lambda i, j, tbl: (0, j)),
            pl.BlockSpec((D_MODEL, FF_TILE), lambda i, j, tbl: (0, j)),
            pl.BlockSpec((FF_TILE, D_MODEL), lambda i, j, tbl: (j, 0)),
            pl.BlockSpec((1, D_MODEL), lambda i, j, tbl: (0, 0)),
        ],
        out_spec=pl.BlockSpec((ROW_TILE, D_MODEL), lambda i, j, tbl: (i, 0)),
        out_shape=jax.ShapeDtypeStruct((ROWS, D_MODEL), F32),
        scratch_shapes=[pltpu.VMEM((ROW_TILE, D_MODEL), BF16)],
        name="ffn",
        operands=(h, gain, wg, wu, wd, final_gain),
    )


def _proj_kernel(h_ref, g_ref, w_ref, o_ref, u_scr):
    @pl.when(pl.program_id(1) == 0)
    def _():
        u_scr[...] = _rms_rows(h_ref[...], g_ref[...]).astype(BF16)

    o_ref[...] = _dot(u_scr[...], w_ref[...]).astype(o_ref.dtype)


def _proj(h, gain, w, sides=()):
    grid = (ROWS // PROJ_ROW_TILE, W_PROJ // PROJ_TILE)
    plan = _side_plan(sides, *grid)
    return _hosted_call(
        _proj_kernel, plan, grid,
        in_specs=[
            pl.BlockSpec((PROJ_ROW_TILE, D_MODEL), lambda i, j, tbl: (i, 0)),
            pl.BlockSpec((1, D_MODEL), lambda i, j, tbl: (0, 0)),
            pl.BlockSpec((D_MODEL, PROJ_TILE), lambda i, j, tbl: (0, j)),
        ],
        out_spec=pl.BlockSpec((PROJ_ROW_TILE, PROJ_TILE), lambda i, j, tbl: (i, j)),
        out_shape=jax.ShapeDtypeStruct((ROWS, W_PROJ), BF16),
        scratch_shapes=[pltpu.VMEM((PROJ_ROW_TILE, D_MODEL), BF16)],
        name="in_proj",
        operands=(h, gain, w),
    )


def _decay_tables():
    c = CHUNK
    blocks = []
    r = np.arange(c)
    for l in range(N_LEVELS):
        b = 1 << l
        m = np.zeros((c, c), np.float32)
        for i in range(c):
            p = i % (2 * b)
            s = i - p + b
            if p >= b:
                m[i, s:i + 1] = 1.0
            else:
                m[i, i + 1:s] = 1.0
        blocks.append(m)
    blocks.append((r[None, :] <= r[:, None]).astype(np.float32))
    blocks.append((r[None, :] > r[:, None]).astype(np.float32))
    blocks.append(np.ones((8, c), np.float32))
    wc = np.concatenate(blocks, axis=0)
    wc = np.concatenate([wc] * 3, axis=1)

    masks = np.zeros((N_LEVELS + 1, c, c), np.float32)
    for l in range(N_LEVELS):
        b = 1 << l
        same = (r[:, None] // (2 * b)) == (r[None, :] // (2 * b))
        up = (r[:, None] & b) != 0
        lo = (r[None, :] & b) == 0
        masks[l] = (same & up & lo).astype(np.float32)
    masks[N_LEVELS] = np.eye(c, dtype=np.float32)
    return wc, masks


def _gla_core(q, k, v, g, wc_ref, mk_ref, s_ref, s_base, n_heads, dk, dv):
    c = CHUNK
    g_hi = g.astype(BF16)
    r1 = g - g_hi.astype(F32)
    g_mid = r1.astype(BF16)
    g_lo = (r1 - g_mid.astype(F32)).astype(BF16)
    ex = jnp.exp(_dot(wc_ref[...], jnp.concatenate([g_hi, g_mid, g_lo], axis=0)))

    rows = lax.broadcasted_iota(jnp.int32, (c, 1), 0)
    qb = q.astype(BF16)
    kb = k.astype(BF16)
    vb = v.astype(BF16)
    scores = []
    for h in range(n_heads):
        scores.append(mk_ref[N_LEVELS] * _dot_nt(qb[:, h * dk:(h + 1) * dk], kb[:, h * dk:(h + 1) * dk]))
    for l in range(N_LEVELS):
        upper = (rows & (1 << l)) != 0
        x = (jnp.where(upper, q, k) * ex[l * c:(l + 1) * c]).astype(BF16)
        for h in range(n_heads):
            xh = x[:, h * dk:(h + 1) * dk]
            scores[h] = scores[h] + mk_ref[l] * _dot_nt(xh, xh)

    base = N_LEVELS * c
    qt = (q * ex[base:base + c]).astype(BF16)
    kt = (k * ex[base + c:base + 2 * c]).astype(BF16)
    tot = ex[base + 2 * c:base + 2 * c + 1]
    outs = []
    for h in range(n_heads):
        st = s_ref[s_base + h]
        vh = vb[:, h * dv:(h + 1) * dv]
        o = _dot(scores[h].astype(BF16), vh) + _dot_nt(qt[:, h * dk:(h + 1) * dk], st.astype(BF16))
        s_ref[s_base + h] = st * tot[:, h * dk:(h + 1) * dk] + _dot_tn(vh, kt[:, h * dk:(h + 1) * dk])
        outs.append(o)
    return outs


def _head_rms(outs, gain, gate):
    dv = outs[0].shape[-1]
    ys = []
    for h, o in enumerate(outs):
        ms = jnp.mean(o * o, axis=-1, keepdims=True)
        ys.append(o * lax.rsqrt(ms + EPS))
    y = jnp.concatenate(ys, axis=-1)
    return y * gain * _silu(gate)


def _log_sigmoid(z):
    return jnp.minimum(z, 0.0) - jnp.log1p(jnp.exp(-jnp.abs(z)))


def _gla_kernel(q_ref, k_ref, v_ref, gate_ref, lr_ref, w2_ref, b_ref, norm_ref, wc_ref, mk_ref,
                o_ref, s_ref):
    @pl.when(pl.program_id(0) == 0)
    def _():
        s_ref[...] = jnp.zeros_like(s_ref)

    for b in range(BATCH):
        z = _dot(lr_ref[b], w2_ref[...]) + b_ref[...]
        g = _log_sigmoid(z) * (1.0 / GLA_GATE_TAU)
        q = q_ref[b].astype(F32) * (GLA_DK ** -0.5)
        outs = _gla_core(q, k_ref[b].astype(F32), v_ref[b].astype(F32), g, wc_ref, mk_ref, s_ref,
                         b * GLA_HEADS, GLA_HEADS, GLA_DK, GLA_DV)
        o_ref[b] = _head_rms(outs, norm_ref[...], gate_ref[b].astype(F32)).astype(o_ref.dtype)


def _hgrn_kernel(q_ref, f_ref, i_ref, gate_ref, lb_ref, norm_ref, wc_ref, mk_ref, o_ref, s_ref, *, layer):
    @pl.when(pl.program_id(0) == 0)
    def _():
        s_ref[...] = jnp.zeros_like(s_ref)

    logits = lb_ref[...]
    e = jnp.exp(logits - jnp.max(logits, axis=0, keepdims=True))
    soft = e / jnp.sum(e, axis=0, keepdims=True)
    lb = jnp.zeros((1, logits.shape[1]), F32)
    for l in range(1, layer + 1):
        lb = lb + soft[l:l + 1]

    for b in range(BATCH):
        hf = f_ref[b].astype(F32)
        forget = lb + (1.0 - lb) * _sigmoid(hf)
        g = jnp.log(jnp.maximum(forget, FORGET_FLOOR))
        k = (1.0 - lb) * _sigmoid(-hf)
        v = _silu(i_ref[b].astype(F32))
        outs = _gla_core(q_ref[b].astype(F32), k, v, g, wc_ref, mk_ref, s_ref,
                         b * HGRN_HEADS, HGRN_HEADS, HGRN_DK, HGRN_DV)
        o_ref[b] = _head_rms(outs, norm_ref[...], gate_ref[b].astype(F32)).astype(o_ref.dtype)


def _col_spec(width, offset):
    assert offset % width == 0
    blk = offset // width
    return pl.BlockSpec((BATCH, CHUNK, width), lambda c: (0, c, blk))


def _const_spec(shape):
    nd = len(shape)
    return pl.BlockSpec(shape, lambda c: (0,) * nd)


def _mixer_out(width):
    return dict(
        out_specs=pl.BlockSpec((BATCH, CHUNK, width), lambda c: (0, c, 0)),
        out_shape=jax.ShapeDtypeStruct((BATCH, L_PAD, width), BF16),
        compiler_params=_params("arbitrary"),
    )


def _gla(proj, w2, bias, norm, wc, masks):
    qk = GLA_HEADS * GLA_DK
    vw = GLA_HEADS * GLA_DV
    return pl.pallas_call(
        _gla_kernel,
        grid=(N_CHUNKS,),
        in_specs=[
            _col_spec(qk, OFF_GLA),
            _col_spec(qk, OFF_GLA + qk),
            _col_spec(vw, OFF_GLA + 2 * qk),
            _col_spec(vw, OFF_GLA + 2 * qk + vw),
            _col_spec(LANE, OFF_LR),
            _const_spec(w2.shape),
            _const_spec(bias.shape),
            _const_spec(norm.shape),
            _const_spec(wc.shape),
            _const_spec(masks.shape),
        ],
        scratch_shapes=[pltpu.VMEM((BATCH * GLA_HEADS, GLA_DV, GLA_DK), F32)],
        name="gla",
        **_mixer_out(vw),
    )(proj, proj, proj, proj, proj, w2, bias, norm, wc, masks)


def _hgrn(proj, lb_logits, norm, wc, masks, layer):
    w = HGRN_HEADS * HGRN_DK
    return pl.pallas_call(
        functools.partial(_hgrn_kernel, layer=layer),
        grid=(N_CHUNKS,),
        in_specs=[
            _col_spec(w, OFF_HGRN),
            _col_spec(w, OFF_HGRN + w),
            _col_spec(w, OFF_HGRN + 2 * w),
            _col_spec(w, OFF_HGRN + 3 * w),
            _const_spec(lb_logits.shape),
            _const_spec(norm.shape),
            _const_spec(wc.shape),
            _const_spec(masks.shape),
        ],
        scratch_shapes=[pltpu.VMEM((BATCH * HGRN_HEADS, HGRN_DV, HGRN_DK), F32)],
        name="hgrn",
        **_mixer_out(w),
    )(proj, proj, proj, proj, lb_logits, norm, wc, masks)


def _ret_kernel(q_ref, k_ref, v_ref, gate_ref, cos_ref, sin_ref, qd_ref, kd_ref, cd_ref, im_ref, norm_ref,
                o_ref, s_ref):
    @pl.when(pl.program_id(0) == 0)
    def _():
        s_ref[...] = jnp.zeros_like(s_ref)

    half = RET_DK // 2
    cos = cos_ref[...]
    sin = sin_ref[...]
    for b in range(BATCH):
        q = q_ref[b].astype(F32)
        k = k_ref[b].astype(F32)
        v = v_ref[b]
        ys = []
        for h in range(RET_HEADS):
            lo = h * RET_DK
            q1, q2 = q[:, lo:lo + half], q[:, lo + half:lo + RET_DK]
            k1, k2 = k[:, lo:lo + half], k[:, lo + half:lo + RET_DK]
            qr = jnp.concatenate([q1 * cos - q2 * sin, q1 * sin + q2 * cos], axis=-1) * (RET_DK ** -0.5)
            kr = jnp.concatenate([k1 * cos - k2 * sin, k1 * sin + k2 * cos], axis=-1)
            vh = v[:, h * RET_DV:(h + 1) * RET_DV]
            scores = _dot_nt(qr.astype(BF16), kr.astype(BF16)) * im_ref[h]
            st = s_ref[b * RET_HEADS + h]
            o = _dot(scores.astype(BF16), vh) + _dot_nt((qr * qd_ref[h]).astype(BF16), st.astype(BF16))
            s_ref[b * RET_HEADS + h] = st * cd_ref[h] + _dot_tn(vh, (kr * kd_ref[h]).astype(BF16))
            o = o - jnp.mean(o, axis=-1, keepdims=True)
            ms = jnp.mean(o * o, axis=-1, keepdims=True)
            ys.append(o * lax.rsqrt(ms + EPS))
        y = jnp.concatenate(ys, axis=-1)
        o_ref[b] = (y * norm_ref[...] * _silu(gate_ref[b].astype(F32))).astype(o_ref.dtype)


def _ret(proj, cos, sin, qd, kd, cd, im, norm):
    w = RET_HEADS * RET_DK
    half = RET_DK // 2
    return pl.pallas_call(
        _ret_kernel,
        grid=(N_CHUNKS,),
        in_specs=[
            _col_spec(w, OFF_RET),
            _col_spec(w, OFF_RET + w),
            _col_spec(w, OFF_RET + 2 * w),
            _col_spec(w, OFF_RET + 3 * w),
            pl.BlockSpec((CHUNK, half), lambda c: (c, 0)),
            pl.BlockSpec((CHUNK, half), lambda c: (c, 0)),
            _const_spec(qd.shape),
            _const_spec(kd.shape),
            _const_spec(cd.shape),
            _const_spec(im.shape),
            _const_spec(norm.shape),
        ],
        scratch_shapes=[pltpu.VMEM((BATCH * RET_HEADS, RET_DV, RET_DK), F32)],
        name="retention",
        **_mixer_out(w),
    )(proj, proj, proj, proj, cos, sin, qd, kd, cd, im, norm)


def _ret_tables():
    f32 = jnp.float32
    half = RET_DK // 2
    pos = jnp.arange(L_PAD, dtype=f32) - PAD
    inv_freq = ROPE_BASE ** (-jnp.arange(half, dtype=f32) / half)
    ang = pos[:, None] * inv_freq[None, :]
    log_gamma = jnp.log(1.0 - 2.0 ** (-5.0 - jnp.arange(RET_HEADS, dtype=f32)))
    idx = jnp.arange(CHUNK, dtype=f32)
    rel = idx[:, None] - idx[None, :]
    causal = (rel >= 0)[None]
    intra = jnp.where(causal, jnp.exp(jnp.where(causal, rel[None], 0.0) * log_gamma[:, None, None]), 0.0)
    q_decay = jnp.exp((idx[None, :] + 1.0) * log_gamma[:, None])[..., None]
    k_decay = jnp.exp((CHUNK - 1.0 - idx[None, :]) * log_gamma[:, None])[..., None]
    chunk_decay = jnp.exp(CHUNK * log_gamma)[:, None, None]
    return jnp.cos(ang), jnp.sin(ang), q_decay, k_decay, chunk_decay, intra


def _merge_kernel(y0_ref, y1_ref, y2_ref, mg_ref, wb_ref, wo_ref, h_ref, o_ref):
    merged = None
    for n, y_ref in enumerate((y0_ref, y1_ref, y2_ref)):
        gate = _sigmoid(mg_ref[:, n * D_MODEL:(n + 1) * D_MODEL].astype(F32))
        t = _dot(y_ref[...], wb_ref[n * BRANCH_WIDTH:(n + 1) * BRANCH_WIDTH, :]) * gate
        merged = t if merged is None else merged + t
    o_ref[...] = h_ref[...] + _dot(merged.astype(BF16), wo_ref[...])


def _merge(ys, proj, wb, wo, h):
    assert OFF_MG == 0
    rows = lambda width: pl.BlockSpec((MERGE_TILE, width), lambda i: (i, 0))
    whole = lambda a: pl.BlockSpec(a.shape, lambda i: (0, 0))
    return pl.pallas_call(
        _merge_kernel,
        grid=(ROWS // MERGE_TILE,),
        in_specs=[rows(BRANCH_WIDTH)] * N_BRANCH + [rows(N_BRANCH * D_MODEL)]
        + [whole(wb), whole(wo), rows(D_MODEL)],
        out_specs=rows(D_MODEL),
        out_shape=jax.ShapeDtypeStruct((ROWS, D_MODEL), F32),
        compiler_params=_params("parallel"),
        name="merge",
    )(*ys, proj, wb, wo, h)


def kernel(x, meta_tokens, ffn1_norm, ffn1_w_gate, ffn1_w_up, ffn1_w_down, mix_norm, w_in, gla_w_gate2, gla_b_gate, gla_norm, ret_norm, hgrn_lb_logits, hgrn_norm, w_branch, w_out, ffn2_norm, ffn2_w_gate, ffn2_w_up, ffn2_w_down, final_norm):
    b = x.shape[0]
    meta = jnp.broadcast_to(meta_tokens[None].astype(x.dtype), (b, N_META, D_MODEL))
    h = jnp.concatenate([jnp.zeros((b, PAD, D_MODEL), x.dtype), meta, x], axis=1).reshape(ROWS, D_MODEL)

    wc_np, masks_np = _decay_tables()
    wc = jnp.asarray(wc_np, BF16)
    masks = jnp.asarray(masks_np, F32)
    cos, sin, qd, kd, cd, im = _ret_tables()

    def ffn_casts(wg, wu, wd, layer):
        n = D_FF_PAD // SIDE_BLOCK
        return [_SideCast(wg, (layer,), 1, D_FF_PAD, 0), _SideCast(wu, (layer,), 1, D_FF_PAD, n),
                _SideCast(wd, (layer,), 0, D_FF_PAD, 2 * n)]

    row = lambda v: v.reshape(1, -1).astype(F32)
    final_gain = row(final_norm)
    w_in_t = jnp.swapaxes(w_in, 1, 2)
    wb_rows = w_branch.reshape(DEPTH, N_BRANCH * BRANCH_WIDTH, D_MODEL)
    ffn1_w = (_cast(ffn1_w_gate, (0,), 1, D_FF_PAD), _cast(ffn1_w_up, (0,), 1, D_FF_PAD),
              _cast(ffn1_w_down, (0,), 0, D_FF_PAD))
    for layer in range(DEPTH):
        h, (w_proj, *ffn2_w) = _ffn(
            h, row(ffn1_norm[layer]), *ffn1_w, final_gain, False,
            sides=[_SideWIn(w_in_t, layer, 0)] + ffn_casts(ffn2_w_gate, ffn2_w_up, ffn2_w_down, layer))
        proj, (wb, wo) = _proj(
            h, row(mix_norm[layer]), w_proj,
            sides=[_SideCast(wb_rows, (layer,), 0, N_BRANCH * BRANCH_WIDTH, 0),
                   _SideCast(w_out, (layer,), 0, D_MODEL, N_BRANCH * BRANCH_WIDTH // SIDE_BLOCK)])
        w2 = jnp.pad(gla_w_gate2[layer], ((0, LANE - GLA_GATE_RANK), (0, 0))).astype(BF16)
        proj_seq = proj.reshape(b, L_PAD, W_PROJ)
        y_gla = _gla(proj_seq, w2, row(gla_b_gate[layer]), row(gla_norm[layer]), wc, masks)
        y_ret = _ret(proj_seq, cos, sin, qd, kd, cd, im, row(ret_norm[layer]))
        y_hgrn = _hgrn(proj_seq, hgrn_lb_logits.astype(F32), row(hgrn_norm[layer]), wc, masks, layer)
        ys = [y.reshape(ROWS, BRANCH_WIDTH) for y in (y_gla, y_ret, y_hgrn)]
        h = _merge(ys, proj, wb, wo, h)
        last = layer == DEPTH - 1
        h, ffn1_w = _ffn(
            h, row(ffn2_norm[layer]), *ffn2_w, final_gain, last,
            sides=[] if last else ffn_casts(ffn1_w_gate, ffn1_w_up, ffn1_w_down, layer + 1))
    return h.reshape(b, L_PAD, D_MODEL)[:, CHUNK:]
```

```python
import functools
from typing import NamedTuple

import numpy as np
import jax
import jax.numpy as jnp
from jax import lax
from jax.experimental import pallas as pl
from jax.experimental.pallas import tpu as pltpu

D_MODEL = 2048
BATCH = 2
SEQ = 4096
DEPTH = 2
N_META = 16
CHUNK = 64
PAD = CHUNK - N_META
D_FF = 5504
FFN_RES = 0.5
EPS = 1e-6
N_BRANCH = 3
BRANCH_WIDTH = 1024

GLA_HEADS, GLA_DK, GLA_DV = 4, 128, 256
GLA_GATE_RANK = 16
GLA_GATE_TAU = 16.0
RET_HEADS, RET_DK, RET_DV = 4, 256, 256
ROPE_BASE = 10000.0
HGRN_HEADS, HGRN_DK, HGRN_DV = 8, 128, 128
FORGET_FLOOR = 1e-20

LANE = 128
L_PAD = PAD + N_META + SEQ
N_CHUNKS = L_PAD // CHUNK
ROWS = BATCH * L_PAD
N_LEVELS = 6
assert 1 << N_LEVELS == CHUNK

FFN1_TILES = (640, 512)
FFN2_TILES = (640, 512)
FINAL_TILES = (832, 512)
PROJ_ROW_TILE = 832
PROJ_TILE = 1792
MERGE_TILE = 320
CAST_TILE = 512
SIDE_BLOCK = LANE
LR_PAD = CAST_TILE
W_IN_SHIFT = GLA_GATE_RANK

OFF_MG = 0
OFF_GLA = OFF_MG + N_BRANCH * D_MODEL
OFF_RET = OFF_GLA + 2 * GLA_HEADS * GLA_DK + 2 * GLA_HEADS * GLA_DV
OFF_HGRN = OFF_RET + 2 * RET_HEADS * RET_DK + 2 * RET_HEADS * RET_DV
OFF_LR = OFF_HGRN + 2 * HGRN_HEADS * HGRN_DK + 2 * HGRN_HEADS * HGRN_DV
W_PROJ = OFF_LR + LR_PAD

VMEM_LIMIT = 56 * 1024 * 1024

F32 = jnp.float32
BF16 = jnp.bfloat16


def _params(*sem):
    return pltpu.CompilerParams(dimension_semantics=sem, vmem_limit_bytes=VMEM_LIMIT)


def _rms_rows(x, gain):
    ms = jnp.mean(x * x, axis=-1, keepdims=True)
    return x * lax.rsqrt(ms + EPS) * gain


def _sigmoid(x):
    return 1.0 / (1.0 + jnp.exp(-x))


def _silu(x):
    return x * _sigmoid(x)


def _dot(a, b):
    return jnp.dot(a, b, preferred_element_type=F32)


def _dot_nt(a, b):
    return lax.dot_general(a, b, (((1,), (1,)), ((), ())), preferred_element_type=F32)


def _dot_tn(a, b):
    return lax.dot_general(a, b, (((0,), (0,)), ((), ())), preferred_element_type=F32)


def _cast_kernel(w_ref, o_ref, *, axis, valid):
    x = w_ref[...]
    idx = pl.program_id(0) * x.shape[axis] + lax.broadcasted_iota(jnp.int32, x.shape, axis)
    o_ref[...] = jnp.where(idx < valid, x, 0.0).astype(BF16)


def _cast(w, lead, axis, padded):
    rows, cols = w.shape[-2:]
    nl = len(lead)
    if axis == 0:
        block, out_shape, grid = (CAST_TILE, cols), (padded, cols), padded // CAST_TILE
        imap = lambda j: (*lead, j, 0)
        omap = lambda j: (j, 0)
    else:
        block, out_shape, grid = (rows, CAST_TILE), (rows, padded), padded // CAST_TILE
        imap = lambda j: (*lead, 0, j)
        omap = lambda j: (0, j)
    return pl.pallas_call(
        functools.partial(_cast_kernel, axis=axis, valid=w.shape[-2 + axis]),
        grid=(grid,),
        in_specs=[pl.BlockSpec((None,) * nl + block, imap)],
        out_specs=pl.BlockSpec(block, omap),
        out_shape=jax.ShapeDtypeStruct(out_shape, BF16),
        compiler_params=_params("parallel"),
        name="cast",
    )(w)


SIDE_IDLE, SIDE_COPY, SIDE_ZERO, SIDE_SHIFTED, SIDE_LOW_RANK = 0, 1, 2, 3, 4


class _SideCast(NamedTuple):
    src: jax.Array
    lead: tuple
    axis: int
    padded: int
    start: int


class _SideWIn(NamedTuple):
    src_t: jax.Array
    layer: int
    start: int


class _SidePlan(NamedTuple):
    table: np.ndarray
    in_specs: list
    operands: list
    out_specs: list
    out_shapes: list
    runners: list
    steps_per_row: int


def _w_in_blocks():
    per = lambda width: width // SIDE_BLOCK
    src = lambda col: col // SIDE_BLOCK
    gla_w = 2 * GLA_HEADS * GLA_DK + 2 * GLA_HEADS * GLA_DV
    ret_w = 2 * RET_HEADS * RET_DK + 2 * RET_HEADS * RET_DV
    hgrn_w = 2 * HGRN_HEADS * HGRN_DK + 2 * HGRN_HEADS * HGRN_DV
    lr_col = gla_w
    blocks = []
    for t in range(per(N_BRANCH * D_MODEL)):
        blocks.append((src(lr_col + ret_w + hgrn_w) + t, SIDE_SHIFTED))
    for t in range(per(gla_w)):
        blocks.append((t, SIDE_COPY))
    for t in range(per(ret_w + hgrn_w)):
        blocks.append((src(lr_col) + t, SIDE_SHIFTED))
    blocks.append((src(lr_col), SIDE_LOW_RANK))
    blocks += [(src(lr_col), SIDE_ZERO)] * (per(LR_PAD) - 1)
    assert len(blocks) == per(W_PROJ)
    return [(s, t, mode) for t, (s, mode) in enumerate(blocks)]


def _side_plan(jobs, n_rows, steps_per_row):
    n_steps = n_rows * steps_per_row
    rows, in_specs, operands, out_specs, out_shapes, runners = [], [], [], [], [], []

    def add_rows(blocks, start):
        n = len(blocks)
        assert start + n <= n_steps
        arr = np.asarray(blocks, np.int32).T
        tab = np.zeros((3, n_steps), np.int32)
        tab[:2, :start] = arr[:2, :1]
        tab[:, start:start + n] = arr
        tab[:2, start + n:] = arr[:2, -1:]
        base = len(rows) * n_steps
        rows.extend(tab)
        return base

    step_of = lambda i, j: i * steps_per_row + j
    for job in jobs:
        if isinstance(job, _SideCast):
            r, c = job.src.shape[-2:]
            valid = job.src.shape[-2 + job.axis]
            assert valid % SIDE_BLOCK == 0 and job.padded % SIDE_BLOCK == 0
            n_valid, n_all = valid // SIDE_BLOCK, job.padded // SIDE_BLOCK
            base = add_rows([(min(t, n_valid - 1), t, SIDE_COPY if t < n_valid else SIDE_ZERO)
                             for t in range(n_all)], job.start)
            lead, nl = job.lead, len(job.lead)
            if job.axis == 0:
                block, out_shape = (SIDE_BLOCK, c), (job.padded, c)
                imap = lambda i, j, tbl, base=base, lead=lead: (*lead, tbl[base + step_of(i, j)], 0)
                omap = lambda i, j, tbl, base=base: (tbl[base + n_steps + step_of(i, j)], 0)
            else:
                block, out_shape = (r, SIDE_BLOCK), (r, job.padded)
                imap = lambda i, j, tbl, base=base, lead=lead: (*lead, 0, tbl[base + step_of(i, j)])
                omap = lambda i, j, tbl, base=base: (0, tbl[base + n_steps + step_of(i, j)])
            in_specs.append(pl.BlockSpec((None,) * nl + block, imap))
            operands.append(job.src)

            def run(tbl_ref, step, ins, out, base=base):
                mode = tbl_ref[base + 2 * n_steps + step]

                @pl.when(mode != SIDE_IDLE)
                def _():
                    out[...] = jnp.where(mode == SIDE_ZERO, 0.0, ins[0][...]).astype(BF16)

            runners.append((1, run))
        else:
            base = add_rows(_w_in_blocks(), job.start)
            layer = job.layer
            shifts_per_block = SIDE_BLOCK // W_IN_SHIFT
            in_specs.append(pl.BlockSpec(
                (None, SIDE_BLOCK, D_MODEL),
                lambda i, j, tbl, base=base, layer=layer: (layer, tbl[base + step_of(i, j)], 0)))
            in_specs.append(pl.BlockSpec(
                (None, W_IN_SHIFT, D_MODEL),
                lambda i, j, tbl, base=base, layer=layer:
                (layer, (tbl[base + step_of(i, j)] + 1) * shifts_per_block, 0)))
            operands += [job.src_t, job.src_t]
            block, out_shape = (D_MODEL, SIDE_BLOCK), (D_MODEL, W_PROJ)
            omap = lambda i, j, tbl, base=base: (0, tbl[base + n_steps + step_of(i, j)])

            def run(tbl_ref, step, ins, out, base=base):
                a_ref, b_ref = ins
                mode = tbl_ref[base + 2 * n_steps + step]

                @pl.when(mode != SIDE_IDLE)
                def _():
                    shifted = jnp.concatenate([a_ref[W_IN_SHIFT:, :], b_ref[...]], axis=0)
                    x = jnp.where(mode == SIDE_SHIFTED, shifted, a_ref[...])
                    keep = jnp.where(mode == SIDE_LOW_RANK, GLA_GATE_RANK,
                                     jnp.where(mode == SIDE_ZERO, 0, SIDE_BLOCK))
                    row = lax.broadcasted_iota(jnp.int32, x.shape, 0)
                    out[...] = jnp.where(row < keep, x, 0.0).T.astype(BF16)

            runners.append((2, run))
        out_specs.append(pl.BlockSpec(block, omap))
        out_shapes.append(jax.ShapeDtypeStruct(out_shape, BF16))

    table = np.concatenate(rows) if rows else np.zeros((1,), np.int32)
    return _SidePlan(table, in_specs, operands, out_specs, out_shapes, runners, steps_per_row)


def _run_sides(plan, tbl_ref, side_ins, side_outs):
    step = pl.program_id(0) * plan.steps_per_row + pl.program_id(1)
    k = 0
    for (n_in, run), out in zip(plan.runners, side_outs):
        run(tbl_ref, step, side_ins[k:k + n_in], out)
        k += n_in


def _hosted_call(kernel, plan, grid, in_specs, out_spec, out_shape, scratch_shapes, name, operands):
    n_side_in = len(plan.in_specs)
    n_main_in = len(in_specs)

    def body(tbl_ref, *refs):
        ins = refs[:n_main_in]
        side_ins = refs[n_main_in:n_main_in + n_side_in]
        out = refs[n_main_in + n_side_in]
        side_outs = refs[n_main_in + n_side_in + 1:n_main_in + n_side_in + 1 + len(plan.out_specs)]
        scratch = refs[n_main_in + n_side_in + 1 + len(plan.out_specs):]
        kernel(*ins, out, *scratch)
        _run_sides(plan, tbl_ref, side_ins, side_outs)

    res = pl.pallas_call(
        body,
        grid_spec=pltpu.PrefetchScalarGridSpec(
            num_scalar_prefetch=1,
            grid=grid,
            in_specs=list(in_specs) + plan.in_specs,
            out_specs=[out_spec] + plan.out_specs,
            scratch_shapes=scratch_shapes,
        ),
        out_shape=[out_shape] + plan.out_shapes,
        compiler_params=_params("arbitrary", "arbitrary"),
        name=name,
    )(jnp.asarray(plan.table), *operands, *plan.operands)
    return res[0], res[1:]


def _ffn_kernel(h_ref, g_ref, wg_ref, wu_ref, wd_ref, fg_ref, o_ref, u_scr, *, final, tail):
    j = pl.program_id(1)
    last = pl.num_programs(1) - 1
    tile = wg_ref.shape[1]

    @pl.when(j == 0)
    def _():
        u_scr[...] = _rms_rows(h_ref[...], g_ref[...]).astype(BF16)
        o_ref[...] = jnp.zeros_like(o_ref)

    def accumulate(width):
        u = u_scr[...]
        a = _dot(u, wg_ref[:, :width])
        b = _dot(u, wu_ref[:, :width])
        act = (_silu(a) * b).astype(BF16)
        o_ref[...] += _dot(act, wd_ref[:width, :])

    if tail == tile:
        accumulate(tile)
    else:
        @pl.when(j < last)
        def _():
            accumulate(tile)

        @pl.when(j == last)
        def _():
            accumulate(tail)

    @pl.when(j == last)
    def _():
        y = h_ref[...] + FFN_RES * o_ref[...]
        if final:
            y = _rms_rows(y, fg_ref[...])
        o_ref[...] = y


def _ff_padded(tf):
    return pl.cdiv(D_FF, tf) * tf


def _ffn(h, gain, wg, wu, wd, final_gain, final, tiles, sides=()):
    tm, tf = tiles
    assert ROWS % tm == 0 and wg.shape[1] == _ff_padded(tf)
    grid = (ROWS // tm, wg.shape[1] // tf)
    tail = D_FF - (grid[1] - 1) * tf
    plan = _side_plan(sides, *grid)
    if final:
        assert L_PAD % tm == 0
        per_seq = L_PAD // tm
        h = h.reshape(BATCH, L_PAD, D_MODEL)
        rows_spec = pl.BlockSpec((None, tm, D_MODEL), lambda i, j, tbl: (i // per_seq, i % per_seq, 0))
        out_shape = jax.ShapeDtypeStruct((BATCH, SEQ, D_MODEL), F32)
    else:
        rows_spec = pl.BlockSpec((tm, D_MODEL), lambda i, j, tbl: (i, 0))
        out_shape = jax.ShapeDtypeStruct((ROWS, D_MODEL), F32)
    return _hosted_call(
        functools.partial(_ffn_kernel, final=final, tail=tail), plan, grid,
        in_specs=[
            rows_spec,
            pl.BlockSpec((1, D_MODEL), lambda i, j, tbl: (0, 0)),
            pl.BlockSpec((D_MODEL, tf), lambda i, j, tbl: (0, j)),
            pl.BlockSpec((D_MODEL, tf), lambda i, j, tbl: (0, j)),
            pl.BlockSpec((tf, D_MODEL), lambda i, j, tbl: (j, 0)),
            pl.BlockSpec((1, D_MODEL), lambda i, j, tbl: (0, 0)),
        ],
        out_spec=rows_spec,
        out_shape=out_shape,
        scratch_shapes=[pltpu.VMEM((tm, D_MODEL), BF16)],
        name="ffn",
        operands=(h, gain, wg, wu, wd, final_gain),
    )


def _proj_kernel(h_ref, g_ref, w_ref, o_ref, u_scr):
    @pl.when(pl.program_id(1) == 0)
    def _():
        u_scr[...] = _rms_rows(h_ref[...], g_ref[...]).astype(BF16)

    o_ref[...] = _dot(u_scr[...], w_ref[...]).astype(o_ref.dtype)


def _proj(h, gain, w, sides=()):
    grid = (ROWS // PROJ_ROW_TILE, W_PROJ // PROJ_TILE)
    plan = _side_plan(sides, *grid)
    return _hosted_call(
        _proj_kernel, plan, grid,
        in_specs=[
            pl.BlockSpec((PROJ_ROW_TILE, D_MODEL), lambda i, j, tbl: (i, 0)),
            pl.BlockSpec((1, D_MODEL), lambda i, j, tbl: (0, 0)),
            pl.BlockSpec((D_MODEL, PROJ_TILE), lambda i, j, tbl: (0, j)),
        ],
        out_spec=pl.BlockSpec((PROJ_ROW_TILE, PROJ_TILE), lambda i, j, tbl: (i, j)),
        out_shape=jax.ShapeDtypeStruct((ROWS, W_PROJ), BF16),
        scratch_shapes=[pltpu.VMEM((PROJ_ROW_TILE, D_MODEL), BF16)],
        name="in_proj",
        operands=(h, gain, w),
    )


def _decay_tables():
    c = CHUNK
    blocks = []
    r = np.arange(c)
    for l in range(N_LEVELS):
        b = 1 << l
        m = np.zeros((c, c), np.float32)
        for i in range(c):
            p = i % (2 * b)
            s = i - p + b
            if p >= b:
                m[i, s:i + 1] = 1.0
            else:
                m[i, i + 1:s] = 1.0
        blocks.append(m)
    blocks.append((r[None, :] <= r[:, None]).astype(np.float32))
    blocks.append((r[None, :] > r[:, None]).astype(np.float32))
    blocks.append(np.ones((8, c), np.float32))
    wc = np.concatenate(blocks, axis=0)
    wc = np.concatenate([wc] * 3, axis=1)

    masks = np.zeros((N_LEVELS + 1, c, c), np.float32)
    for l in range(N_LEVELS):
        b = 1 << l
        same = (r[:, None] // (2 * b)) == (r[None, :] // (2 * b))
        up = (r[:, None] & b) != 0
        lo = (r[None, :] & b) == 0
        masks[l] = (same & up & lo).astype(np.float32)
    masks[N_LEVELS] = np.eye(c, dtype=np.float32)
    return wc, masks


def _gla_core(q, k, v, g, wc_ref, mk_ref, s_ref, s_base, n_heads, dk, dv):
    c = CHUNK
    g_hi = g.astype(BF16)
    r1 = g - g_hi.astype(F32)
    g_mid = r1.astype(BF16)
    g_lo = (r1 - g_mid.astype(F32)).astype(BF16)
    ex = jnp.exp(_dot(wc_ref[...], jnp.concatenate([g_hi, g_mid, g_lo], axis=0)))

    rows = lax.broadcasted_iota(jnp.int32, (c, 1), 0)
    qb = q.astype(BF16)
    kb = k.astype(BF16)
    vb = v.astype(BF16)
    scores = []
    for h in range(n_heads):
        scores.append(mk_ref[N_LEVELS] * _dot_nt(qb[:, h * dk:(h + 1) * dk], kb[:, h * dk:(h + 1) * dk]))
    for l in range(N_LEVELS):
        upper = (rows & (1 << l)) != 0
        x = (jnp.where(upper, q, k) * ex[l * c:(l + 1) * c]).astype(BF16)
        for h in range(n_heads):
            xh = x[:, h * dk:(h + 1) * dk]
            scores[h] = scores[h] + mk_ref[l] * _dot_nt(xh, xh)

    base = N_LEVELS * c
    qt = (q * ex[base:base + c]).astype(BF16)
    kt = (k * ex[base + c:base + 2 * c]).astype(BF16)
    tot = ex[base + 2 * c:base + 2 * c + 1]
    outs = []
    for h in range(n_heads):
        st = s_ref[s_base + h]
        vh = vb[:, h * dv:(h + 1) * dv]
        o = _dot(scores[h].astype(BF16), vh) + _dot_nt(qt[:, h * dk:(h + 1) * dk], st.astype(BF16))
        s_ref[s_base + h] = st * tot[:, h * dk:(h + 1) * dk] + _dot_tn(vh, kt[:, h * dk:(h + 1) * dk])
        outs.append(o)
    return outs


def _head_rms(outs, gain, gate):
    dv = outs[0].shape[-1]
    ys = []
    for h, o in enumerate(outs):
        ms = jnp.mean(o * o, axis=-1, keepdims=True)
        ys.append(o * lax.rsqrt(ms + EPS))
    y = jnp.concatenate(ys, axis=-1)
    return y * gain * _silu(gate)


def _log_sigmoid(z):
    return jnp.minimum(z, 0.0) - jnp.log(1.0 + jnp.exp(-jnp.abs(z)))


def _gla_kernel(q_ref, k_ref, v_ref, gate_ref, lr_ref, w2_ref, b_ref, norm_ref, wc_ref, mk_ref,
                o_ref, s_ref):
    @pl.when(pl.program_id(0) == 0)
    def _():
        s_ref[...] = jnp.zeros_like(s_ref)

    for b in range(BATCH):
        z = _dot(lr_ref[b], w2_ref[...]) + b_ref[...]
        g = _log_sigmoid(z) * (1.0 / GLA_GATE_TAU)
        q = q_ref[b].astype(F32) * (GLA_DK ** -0.5)
        outs = _gla_core(q, k_ref[b].astype(F32), v_ref[b].astype(F32), g, wc_ref, mk_ref, s_ref,
                         b * GLA_HEADS, GLA_HEADS, GLA_DK, GLA_DV)
        o_ref[b] = _head_rms(outs, norm_ref[...], gate_ref[b].astype(F32)).astype(o_ref.dtype)


def _hgrn_kernel(q_ref, f_ref, i_ref, gate_ref, lb_ref, norm_ref, wc_ref, mk_ref, o_ref, s_ref, *, layer):
    @pl.when(pl.program_id(0) == 0)
    def _():
        s_ref[...] = jnp.zeros_like(s_ref)

    logits = lb_ref[...]
    e = jnp.exp(logits - jnp.max(logits, axis=0, keepdims=True))
    soft = e / jnp.sum(e, axis=0, keepdims=True)
    lb = jnp.zeros((1, logits.shape[1]), F32)
    for l in range(1, layer + 1):
        lb = lb + soft[l:l + 1]

    for b in range(BATCH):
        hf = f_ref[b].astype(F32)
        forget = lb + (1.0 - lb) * _sigmoid(hf)
        g = jnp.log(jnp.maximum(forget, FORGET_FLOOR))
        k = (1.0 - lb) * _sigmoid(-hf)
        v = _silu(i_ref[b].astype(F32))
        outs = _gla_core(q_ref[b].astype(F32), k, v, g, wc_ref, mk_ref, s_ref,
                         b * HGRN_HEADS, HGRN_HEADS, HGRN_DK, HGRN_DV)
        o_ref[b] = _head_rms(outs, norm_ref[...], gate_ref[b].astype(F32)).astype(o_ref.dtype)


def _chunk_block(c):
    return (c + N_CHUNKS - 1) % N_CHUNKS


def _col_spec(width, offset):
    assert offset % width == 0
    blk = offset // width
    return pl.BlockSpec((BATCH, CHUNK, width), lambda c: (0, _chunk_block(c), blk))


def _const_spec(shape):
    nd = len(shape)
    return pl.BlockSpec(shape, lambda c: (0,) * nd)


def _mixer_out(width):
    return dict(
        out_specs=pl.BlockSpec((BATCH, CHUNK, width), lambda c: (0, _chunk_block(c), 0)),
        out_shape=jax.ShapeDtypeStruct((BATCH, L_PAD, width), BF16),
        compiler_params=_params("arbitrary"),
    )


def _gla(proj, w2, bias, norm, wc, masks):
    qk = GLA_HEADS * GLA_DK
    vw = GLA_HEADS * GLA_DV
    return pl.pallas_call(
        _gla_kernel,
        grid=(N_CHUNKS,),
        in_specs=[
            _col_spec(qk, OFF_GLA),
            _col_spec(qk, OFF_GLA + qk),
            _col_spec(vw, OFF_GLA + 2 * qk),
            _col_spec(vw, OFF_GLA + 2 * qk + vw),
            _col_spec(LANE, OFF_LR),
            _const_spec(w2.shape),
            _const_spec(bias.shape),
            _const_spec(norm.shape),
            _const_spec(wc.shape),
            _const_spec(masks.shape),
        ],
        scratch_shapes=[pltpu.VMEM((BATCH * GLA_HEADS, GLA_DV, GLA_DK), F32)],
        name="gla",
        **_mixer_out(vw),
    )(proj, proj, proj, proj, proj, w2, bias, norm, wc, masks)


def _hgrn(proj, lb_logits, norm, wc, masks, layer):
    w = HGRN_HEADS * HGRN_DK
    return pl.pallas_call(
        functools.partial(_hgrn_kernel, layer=layer),
        grid=(N_CHUNKS,),
        in_specs=[
            _col_spec(w, OFF_HGRN),
            _col_spec(w, OFF_HGRN + w),
            _col_spec(w, OFF_HGRN + 2 * w),
            _col_spec(w, OFF_HGRN + 3 * w),
            _const_spec(lb_logits.shape),
            _const_spec(norm.shape),
            _const_spec(wc.shape),
            _const_spec(masks.shape),
        ],
        scratch_shapes=[pltpu.VMEM((BATCH * HGRN_HEADS, HGRN_DV, HGRN_DK), F32)],
        name="hgrn",
        **_mixer_out(w),
    )(proj, proj, proj, proj, lb_logits, norm, wc, masks)


def _ret_kernel(q_ref, k_ref, v_ref, gate_ref, cos_ref, sin_ref, qd_ref, kd_ref, cd_ref, im_ref, norm_ref,
                o_ref, s_ref):
    @pl.when(pl.program_id(0) == 0)
    def _():
        s_ref[...] = jnp.zeros_like(s_ref)

    half = RET_DK // 2
    cos = cos_ref[...]
    sin = sin_ref[...]
    for b in range(BATCH):
        q = q_ref[b].astype(F32)
        k = k_ref[b].astype(F32)
        v = v_ref[b]
        ys = []
        for h in range(RET_HEADS):
            lo = h * RET_DK
            q1, q2 = q[:, lo:lo + half], q[:, lo + half:lo + RET_DK]
            k1, k2 = k[:, lo:lo + half], k[:, lo + half:lo + RET_DK]
            qr = jnp.concatenate([q1 * cos - q2 * sin, q1 * sin + q2 * cos], axis=-1) * (RET_DK ** -0.5)
            kr = jnp.concatenate([k1 * cos - k2 * sin, k1 * sin + k2 * cos], axis=-1)
            vh = v[:, h * RET_DV:(h + 1) * RET_DV]
            scores = _dot_nt(qr.astype(BF16), kr.astype(BF16)) * im_ref[h]
            st = s_ref[b * RET_HEADS + h]
            o = _dot(scores.astype(BF16), vh) + _dot_nt((qr * qd_ref[h]).astype(BF16), st.astype(BF16))
            s_ref[b * RET_HEADS + h] = st * cd_ref[h] + _dot_tn(vh, (kr * kd_ref[h]).astype(BF16))
            o = o - jnp.mean(o, axis=-1, keepdims=True)
            ms = jnp.mean(o * o, axis=-1, keepdims=True)
            ys.append(o * lax.rsqrt(ms + EPS))
        y = jnp.concatenate(ys, axis=-1)
        o_ref[b] = (y * norm_ref[...] * _silu(gate_ref[b].astype(F32))).astype(o_ref.dtype)


def _ret(proj, cos, sin, qd, kd, cd, im, norm):
    w = RET_HEADS * RET_DK
    half = RET_DK // 2
    return pl.pallas_call(
        _ret_kernel,
        grid=(N_CHUNKS,),
        in_specs=[
            _col_spec(w, OFF_RET),
            _col_spec(w, OFF_RET + w),
            _col_spec(w, OFF_RET + 2 * w),
            _col_spec(w, OFF_RET + 3 * w),
            pl.BlockSpec((CHUNK, half), lambda c: (c, 0)),
            pl.BlockSpec((CHUNK, half), lambda c: (c, 0)),
            _const_spec(qd.shape),
            _const_spec(kd.shape),
            _const_spec(cd.shape),
            _const_spec(im.shape),
            _const_spec(norm.shape),
        ],
        scratch_shapes=[pltpu.VMEM((BATCH * RET_HEADS, RET_DV, RET_DK), F32)],
        name="retention",
        **_mixer_out(w),
    )(proj, proj, proj, proj, cos, sin, qd, kd, cd, im, norm)


def _ret_tables():
    f32 = jnp.float32
    half = RET_DK // 2
    pos = jnp.arange(L_PAD, dtype=f32) - PAD
    inv_freq = ROPE_BASE ** (-jnp.arange(half, dtype=f32) / half)
    ang = pos[:, None] * inv_freq[None, :]
    log_gamma = jnp.log(1.0 - 2.0 ** (-5.0 - jnp.arange(RET_HEADS, dtype=f32)))
    idx = jnp.arange(CHUNK, dtype=f32)
    rel = idx[:, None] - idx[None, :]
    causal = (rel >= 0)[None]
    intra = jnp.where(causal, jnp.exp(jnp.where(causal, rel[None], 0.0) * log_gamma[:, None, None]), 0.0)
    q_decay = jnp.exp((idx[None, :] + 1.0) * log_gamma[:, None])[..., None]
    k_decay = jnp.exp((CHUNK - 1.0 - idx[None, :]) * log_gamma[:, None])[..., None]
    chunk_decay = jnp.exp(CHUNK * log_gamma)[:, None, None]
    return jnp.cos(ang), jnp.sin(ang), q_decay, k_decay, chunk_decay, intra


def _merge_kernel(y0_ref, y1_ref, y2_ref, mg_ref, wb_ref, wo_ref, h_ref, o_ref):
    merged = None
    for n, y_ref in enumerate((y0_ref, y1_ref, y2_ref)):
        gate = _sigmoid(mg_ref[:, n * D_MODEL:(n + 1) * D_MODEL].astype(F32))
        t = _dot(y_ref[...], wb_ref[n * BRANCH_WIDTH:(n + 1) * BRANCH_WIDTH, :]) * gate
        merged = t if merged is None else merged + t
    o_ref[...] = h_ref[...] + _dot(merged.astype(BF16), wo_ref[...])


def _merge(ys, proj, wb, wo, h):
    assert OFF_MG == 0
    rows = lambda width: pl.BlockSpec((MERGE_TILE, width), lambda i: (i, 0))
    whole = lambda a: pl.BlockSpec(a.shape, lambda i: (0, 0))
    return pl.pallas_call(
        _merge_kernel,
        grid=(ROWS // MERGE_TILE,),
        in_specs=[rows(BRANCH_WIDTH)] * N_BRANCH + [rows(N_BRANCH * D_MODEL)]
        + [whole(wb), whole(wo), rows(D_MODEL)],
        out_specs=rows(D_MODEL),
        out_shape=jax.ShapeDtypeStruct((ROWS, D_MODEL), F32),
        compiler_params=_params("parallel"),
        name="merge",
    )(*ys, proj, wb, wo, h)


def kernel(x, meta_tokens, ffn1_norm, ffn1_w_gate, ffn1_w_up, ffn1_w_down, mix_norm, w_in, gla_w_gate2, gla_b_gate, gla_norm, ret_norm, hgrn_lb_logits, hgrn_norm, w_branch, w_out, ffn2_norm, ffn2_w_gate, ffn2_w_up, ffn2_w_down, final_norm):
    b = x.shape[0]
    meta = jnp.broadcast_to(meta_tokens[None].astype(x.dtype), (b, N_META, D_MODEL))
    h = jnp.concatenate([x, jnp.zeros((b, PAD, D_MODEL), x.dtype), meta], axis=1).reshape(ROWS, D_MODEL)

    wc_np, masks_np = _decay_tables()
    wc = jnp.asarray(wc_np, BF16)
    masks = jnp.asarray(masks_np, F32)
    cos, sin, qd, kd, cd, im = _ret_tables()

    def ffn_steps(tiles):
        return (ROWS // tiles[0]) * pl.cdiv(D_FF, tiles[1])

    def ffn_casts(wg, wu, wd, layer, tiles, host_steps):
        padded = _ff_padded(tiles[1])
        n = padded // SIDE_BLOCK
        gap = min(n, (host_steps - n) // 2)
        return [_SideCast(wg, (layer,), 1, padded, 0), _SideCast(wu, (layer,), 1, padded, gap),
                _SideCast(wd, (layer,), 0, padded, 2 * gap)]

    row = lambda v: v.reshape(1, -1).astype(F32)
    final_gain = row(final_norm)
    w_in_t = jnp.swapaxes(w_in, 1, 2)
    wb_rows = w_branch.reshape(DEPTH, N_BRANCH * BRANCH_WIDTH, D_MODEL)
    padded = _ff_padded(FFN1_TILES[1])
    ffn1_w = (_cast(ffn1_w_gate, (0,), 1, padded), _cast(ffn1_w_up, (0,), 1, padded),
              _cast(ffn1_w_down, (0,), 0, padded))
    for layer in range(DEPTH):
        h, (w_proj, *ffn2_w) = _ffn(
            h, row(ffn1_norm[layer]), *ffn1_w, final_gain, False, FFN1_TILES,
            sides=[_SideWIn(w_in_t, layer, 0)]
            + ffn_casts(ffn2_w_gate, ffn2_w_up, ffn2_w_down, layer,
                        FINAL_TILES if layer == DEPTH - 1 else FFN2_TILES, ffn_steps(FFN1_TILES)))
        proj, (wb, wo) = _proj(
            h, row(mix_norm[layer]), w_proj,
            sides=[_SideCast(wb_rows, (layer,), 0, N_BRANCH * BRANCH_WIDTH, 0),
                   _SideCast(w_out, (layer,), 0, D_MODEL, N_BRANCH * BRANCH_WIDTH // SIDE_BLOCK)])
        w2 = jnp.pad(gla_w_gate2[layer], ((0, LANE - GLA_GATE_RANK), (0, 0))).astype(BF16)
        proj_seq = proj.reshape(b, L_PAD, W_PROJ)
        y_gla = _gla(proj_seq, w2, row(gla_b_gate[layer]), row(gla_norm[layer]), wc, masks)
        y_ret = _ret(proj_seq, cos, sin, qd, kd, cd, im, row(ret_norm[layer]))
        y_hgrn = _hgrn(proj_seq, hgrn_lb_logits.astype(F32), row(hgrn_norm[layer]), wc, masks, layer)
        ys = [y.reshape(ROWS, BRANCH_WIDTH) for y in (y_gla, y_ret, y_hgrn)]
        h = _merge(ys, proj, wb, wo, h)
        last = layer == DEPTH - 1
        tiles = FINAL_TILES if last else FFN2_TILES
        h, ffn1_w = _ffn(
            h, row(ffn2_norm[layer]), *ffn2_w, final_gain, last, tiles,
            sides=[] if last else ffn_casts(ffn1_w_gate, ffn1_w_up, ffn1_w_down, layer + 1, FFN1_TILES,
                                            ffn_steps(tiles)))
    return h
```

```python
import functools
from typing import NamedTuple

import numpy as np
import jax
import jax.numpy as jnp
from jax import lax
from jax.experimental import pallas as pl
from jax.experimental.pallas import tpu as pltpu

D_MODEL = 2048
BATCH = 2
SEQ = 4096
DEPTH = 2
N_META = 16
CHUNK = 64
PAD = CHUNK - N_META
D_FF = 5504
FFN_RES = 0.5
EPS = 1e-6
N_BRANCH = 3
BRANCH_WIDTH = 1024

GLA_HEADS, GLA_DK, GLA_DV = 4, 128, 256
GLA_GATE_RANK = 16
GLA_GATE_TAU = 16.0
RET_HEADS, RET_DK, RET_DV = 4, 256, 256
ROPE_BASE = 10000.0
HGRN_HEADS, HGRN_DK, HGRN_DV = 8, 128, 128
FORGET_FLOOR = 1e-20

LANE = 128
L_PAD = PAD + N_META + SEQ
N_CHUNKS = L_PAD // CHUNK
ROWS = BATCH * L_PAD
N_LEVELS = 6
assert 1 << N_LEVELS == CHUNK

FFN1_TILES = (640, 512)
FFN2_TILES = (640, 512)
FINAL_TILES = (832, 512)
PROJ_ROW_TILE = 832
PROJ_TILE = 1792
MERGE_TILE = 320
CAST_TILE = 512
SIDE_BLOCK = LANE
LR_PAD = CAST_TILE
W_IN_SHIFT = GLA_GATE_RANK

OFF_MG = 0
OFF_GLA = OFF_MG + N_BRANCH * D_MODEL
OFF_RET = OFF_GLA + 2 * GLA_HEADS * GLA_DK + 2 * GLA_HEADS * GLA_DV
OFF_HGRN = OFF_RET + 2 * RET_HEADS * RET_DK + 2 * RET_HEADS * RET_DV
OFF_LR = OFF_HGRN + 2 * HGRN_HEADS * HGRN_DK + 2 * HGRN_HEADS * HGRN_DV
W_PROJ = OFF_LR + LR_PAD

VMEM_LIMIT = 56 * 1024 * 1024

F32 = jnp.float32
BF16 = jnp.bfloat16


def _params(*sem):
    return pltpu.CompilerParams(dimension_semantics=sem, vmem_limit_bytes=VMEM_LIMIT)


def _rms_rows(x, gain):
    ms = jnp.mean(x * x, axis=-1, keepdims=True)
    return x * lax.rsqrt(ms + EPS) * gain


def _sigmoid(x):
    return 1.0 / (1.0 + jnp.exp(-x))


def _silu(x):
    return x * _sigmoid(x)


def _dot(a, b):
    return jnp.dot(a, b, preferred_element_type=F32)


def _dot_nt(a, b):
    return lax.dot_general(a, b, (((1,), (1,)), ((), ())), preferred_element_type=F32)


def _dot_tn(a, b):
    return lax.dot_general(a, b, (((0,), (0,)), ((), ())), preferred_element_type=F32)


def _cast_kernel(w_ref, o_ref, *, axis, valid):
    x = w_ref[...]
    idx = pl.program_id(0) * x.shape[axis] + lax.broadcasted_iota(jnp.int32, x.shape, axis)
    o_ref[...] = jnp.where(idx < valid, x, 0.0).astype(BF16)


def _cast(w, lead, axis, padded):
    rows, cols = w.shape[-2:]
    nl = len(lead)
    if axis == 0:
        block, out_shape, grid = (CAST_TILE, cols), (padded, cols), padded // CAST_TILE
        imap = lambda j: (*lead, j, 0)
        omap = lambda j: (j, 0)
    else:
        block, out_shape, grid = (rows, CAST_TILE), (padded // CAST_TILE, rows, CAST_TILE), padded // CAST_TILE
        imap = lambda j: (*lead, 0, j)
        omap = lambda j: (j, 0, 0)
    out_block = block if axis == 0 else (None,) + block
    return pl.pallas_call(
        functools.partial(_cast_kernel, axis=axis, valid=w.shape[-2 + axis]),
        grid=(grid,),
        in_specs=[pl.BlockSpec((None,) * nl + block, imap)],
        out_specs=pl.BlockSpec(out_block, omap),
        out_shape=jax.ShapeDtypeStruct(out_shape, BF16),
        compiler_params=_params("parallel"),
        name="cast",
    )(w)


SIDE_IDLE, SIDE_COPY, SIDE_ZERO, SIDE_SHIFTED, SIDE_LOW_RANK = 0, 1, 2, 3, 4


class _SideCast(NamedTuple):
    src: jax.Array
    lead: tuple
    axis: int
    padded: int
    start: int
    col_tile: int = 0


class _SideWIn(NamedTuple):
    src_t: jax.Array
    layer: int
    start: int


class _SidePlan(NamedTuple):
    table: np.ndarray
    in_specs: list
    operands: list
    out_specs: list
    out_shapes: list
    runners: list
    steps_per_row: int


def _w_in_blocks():
    per = lambda width: width // SIDE_BLOCK
    src = lambda col: col // SIDE_BLOCK
    gla_w = 2 * GLA_HEADS * GLA_DK + 2 * GLA_HEADS * GLA_DV
    ret_w = 2 * RET_HEADS * RET_DK + 2 * RET_HEADS * RET_DV
    hgrn_w = 2 * HGRN_HEADS * HGRN_DK + 2 * HGRN_HEADS * HGRN_DV
    lr_col = gla_w
    blocks = []
    for t in range(per(N_BRANCH * D_MODEL)):
        blocks.append((src(lr_col + ret_w + hgrn_w) + t, SIDE_SHIFTED))
    for t in range(per(gla_w)):
        blocks.append((t, SIDE_COPY))
    for t in range(per(ret_w + hgrn_w)):
        blocks.append((src(lr_col) + t, SIDE_SHIFTED))
    blocks.append((src(lr_col), SIDE_LOW_RANK))
    blocks += [(src(lr_col), SIDE_ZERO)] * (per(LR_PAD) - 1)
    assert len(blocks) == per(W_PROJ)
    return [(s, t, mode) for t, (s, mode) in enumerate(blocks)]


def _side_plan(jobs, n_rows, steps_per_row):
    n_steps = n_rows * steps_per_row
    rows, in_specs, operands, out_specs, out_shapes, runners = [], [], [], [], [], []

    def add_rows(blocks, start):
        n = len(blocks)
        assert start + n <= n_steps
        arr = np.asarray(blocks, np.int32).T
        tab = np.zeros((3, n_steps), np.int32)
        tab[:2, :start] = arr[:2, :1]
        tab[:, start:start + n] = arr
        tab[:2, start + n:] = arr[:2, -1:]
        base = len(rows) * n_steps
        rows.extend(tab)
        return base

    step_of = lambda i, j: i * steps_per_row + j
    for job in jobs:
        if isinstance(job, _SideCast):
            r, c = job.src.shape[-2:]
            valid = job.src.shape[-2 + job.axis]
            assert valid % SIDE_BLOCK == 0 and job.padded % SIDE_BLOCK == 0
            n_valid, n_all = valid // SIDE_BLOCK, job.padded // SIDE_BLOCK
            base = add_rows([(min(t, n_valid - 1), t, SIDE_COPY if t < n_valid else SIDE_ZERO)
                             for t in range(n_all)], job.start)
            lead, nl = job.lead, len(job.lead)
            if job.axis == 0:
                block, out_shape = (SIDE_BLOCK, c), (job.padded, c)
                imap = lambda i, j, tbl, base=base, lead=lead: (*lead, tbl[base + step_of(i, j)], 0)
                omap = lambda i, j, tbl, base=base: (tbl[base + n_steps + step_of(i, j)], 0)
            else:
                per_tile = job.col_tile // SIDE_BLOCK
                block, out_shape = (None, r, SIDE_BLOCK), (job.padded // job.col_tile, r, job.col_tile)
                imap = lambda i, j, tbl, base=base, lead=lead: (*lead, 0, tbl[base + step_of(i, j)])

                def omap(i, j, tbl, base=base, per_tile=per_tile):
                    t = tbl[base + n_steps + step_of(i, j)]
                    return (t // per_tile, 0, t % per_tile)

            in_block = (SIDE_BLOCK, c) if job.axis == 0 else (r, SIDE_BLOCK)
            in_specs.append(pl.BlockSpec((None,) * nl + in_block, imap))
            operands.append(job.src)

            def run(tbl_ref, step, ins, out, base=base):
                mode = tbl_ref[base + 2 * n_steps + step]

                @pl.when(mode != SIDE_IDLE)
                def _():
                    out[...] = jnp.where(mode == SIDE_ZERO, 0.0, ins[0][...]).astype(BF16)

            runners.append((1, run))
        else:
            base = add_rows(_w_in_blocks(), job.start)
            layer = job.layer
            shifts_per_block = SIDE_BLOCK // W_IN_SHIFT
            in_specs.append(pl.BlockSpec(
                (None, SIDE_BLOCK, D_MODEL),
                lambda i, j, tbl, base=base, layer=layer: (layer, tbl[base + step_of(i, j)], 0)))
            in_specs.append(pl.BlockSpec(
                (None, W_IN_SHIFT, D_MODEL),
                lambda i, j, tbl, base=base, layer=layer:
                (layer, (tbl[base + step_of(i, j)] + 1) * shifts_per_block, 0)))
            operands += [job.src_t, job.src_t]
            per_tile = PROJ_TILE // SIDE_BLOCK
            block, out_shape = (None, D_MODEL, SIDE_BLOCK), (W_PROJ // PROJ_TILE, D_MODEL, PROJ_TILE)

            def omap(i, j, tbl, base=base, per_tile=per_tile):
                t = tbl[base + n_steps + step_of(i, j)]
                return (t // per_tile, 0, t % per_tile)

            def run(tbl_ref, step, ins, out, base=base):
                a_ref, b_ref = ins
                mode = tbl_ref[base + 2 * n_steps + step]

                @pl.when(mode == SIDE_SHIFTED)
                def _():
                    x = jnp.concatenate([a_ref[W_IN_SHIFT:, :], b_ref[...]], axis=0)
                    out[...] = x.astype(BF16).T

                @pl.when(mode == SIDE_COPY)
                def _():
                    out[...] = a_ref[...].astype(BF16).T

                @pl.when(mode == SIDE_LOW_RANK)
                def _():
                    row = lax.broadcasted_iota(jnp.int32, a_ref.shape, 0)
                    out[...] = jnp.where(row < GLA_GATE_RANK, a_ref[...], 0.0).astype(BF16).T

                @pl.when(mode == SIDE_ZERO)
                def _():
                    out[...] = jnp.zeros_like(out)

            runners.append((2, run))
        out_specs.append(pl.BlockSpec(block, omap))
        out_shapes.append(jax.ShapeDtypeStruct(out_shape, BF16))

    table = np.concatenate(rows) if rows else np.zeros((1,), np.int32)
    return _SidePlan(table, in_specs, operands, out_specs, out_shapes, runners, steps_per_row)


def _run_sides(plan, tbl_ref, side_ins, side_outs):
    step = pl.program_id(0) * plan.steps_per_row + pl.program_id(1)
    k = 0
    for (n_in, run), out in zip(plan.runners, side_outs):
        run(tbl_ref, step, side_ins[k:k + n_in], out)
        k += n_in


def _hosted_call(kernel, plan, grid, in_specs, out_spec, out_shape, scratch_shapes, name, operands):
    n_side_in = len(plan.in_specs)
    n_main_in = len(in_specs)

    def body(tbl_ref, *refs):
        ins = refs[:n_main_in]
        side_ins = refs[n_main_in:n_main_in + n_side_in]
        out = refs[n_main_in + n_side_in]
        side_outs = refs[n_main_in + n_side_in + 1:n_main_in + n_side_in + 1 + len(plan.out_specs)]
        scratch = refs[n_main_in + n_side_in + 1 + len(plan.out_specs):]
        kernel(*ins, out, *scratch)
        _run_sides(plan, tbl_ref, side_ins, side_outs)

    res = pl.pallas_call(
        body,
        grid_spec=pltpu.PrefetchScalarGridSpec(
            num_scalar_prefetch=1,
            grid=grid,
            in_specs=list(in_specs) + plan.in_specs,
            out_specs=[out_spec] + plan.out_specs,
            scratch_shapes=scratch_shapes,
        ),
        out_shape=[out_shape] + plan.out_shapes,
        compiler_params=_params("arbitrary", "arbitrary"),
        name=name,
    )(jnp.asarray(plan.table), *operands, *plan.operands)
    return res[0], res[1:]


def _ffn_kernel(h_ref, g_ref, wg_ref, wu_ref, wd_ref, fg_ref, o_ref, u_scr, *, final, tail):
    j = pl.program_id(1)
    last = pl.num_programs(1) - 1
    tile = wg_ref.shape[1]

    @pl.when(j == 0)
    def _():
        u_scr[...] = _rms_rows(h_ref[...], g_ref[...]).astype(BF16)
        o_ref[...] = jnp.zeros_like(o_ref)

    def accumulate(width):
        u = u_scr[...]
        a = _dot(u, wg_ref[:, :width])
        b = _dot(u, wu_ref[:, :width])
        act = (_silu(a) * b).astype(BF16)
        o_ref[...] += _dot(act, wd_ref[:width, :])

    if tail == tile:
        accumulate(tile)
    else:
        @pl.when(j < last)
        def _():
            accumulate(tile)

        @pl.when(j == last)
        def _():
            accumulate(tail)

    @pl.when(j == last)
    def _():
        y = h_ref[...] + FFN_RES * o_ref[...]
        if final:
            y = _rms_rows(y, fg_ref[...])
        o_ref[...] = y


def _ff_padded(tf):
    return pl.cdiv(D_FF, tf) * tf


def _ffn(h, gain, wg, wu, wd, final_gain, final, tiles, sides=()):
    tm, tf = tiles
    assert ROWS % tm == 0 and wg.shape == (_ff_padded(tf) // tf, D_MODEL, tf) and wd.shape[0] == _ff_padded(tf)
    grid = (ROWS // tm, wg.shape[0])
    tail = D_FF - (grid[1] - 1) * tf
    plan = _side_plan(sides, *grid)
    if final:
        assert L_PAD % tm == 0
        per_seq = L_PAD // tm
        h = h.reshape(BATCH, L_PAD, D_MODEL)
        rows_spec = pl.BlockSpec((None, tm, D_MODEL), lambda i, j, tbl: (i // per_seq, i % per_seq, 0))
        out_shape = jax.ShapeDtypeStruct((BATCH, SEQ, D_MODEL), F32)
    else:
        rows_spec = pl.BlockSpec((tm, D_MODEL), lambda i, j, tbl: (i, 0))
        out_shape = jax.ShapeDtypeStruct((ROWS, D_MODEL), F32)
    return _hosted_call(
        functools.partial(_ffn_kernel, final=final, tail=tail), plan, grid,
        in_specs=[
            rows_spec,
            pl.BlockSpec((1, D_MODEL), lambda i, j, tbl: (0, 0)),
            pl.BlockSpec((None, D_MODEL, tf), lambda i, j, tbl: (j, 0, 0)),
            pl.BlockSpec((None, D_MODEL, tf), lambda i, j, tbl: (j, 0, 0)),
            pl.BlockSpec((tf, D_MODEL), lambda i, j, tbl: (j, 0)),
            pl.BlockSpec((1, D_MODEL), lambda i, j, tbl: (0, 0)),
        ],
        out_spec=rows_spec,
        out_shape=out_shape,
        scratch_shapes=[pltpu.VMEM((tm, D_MODEL), BF16)],
        name="ffn",
        operands=(h, gain, wg, wu, wd, final_gain),
    )


def _proj_kernel(h_ref, g_ref, w_ref, o_ref, u_scr):
    @pl.when(pl.program_id(1) == 0)
    def _():
        u_scr[...] = _rms_rows(h_ref[...], g_ref[...]).astype(BF16)

    o_ref[...] = _dot(u_scr[...], w_ref[...]).astype(o_ref.dtype)


def _proj(h, gain, w, sides=()):
    grid = (ROWS // PROJ_ROW_TILE, W_PROJ // PROJ_TILE)
    plan = _side_plan(sides, *grid)
    return _hosted_call(
        _proj_kernel, plan, grid,
        in_specs=[
            pl.BlockSpec((PROJ_ROW_TILE, D_MODEL), lambda i, j, tbl: (i, 0)),
            pl.BlockSpec((1, D_MODEL), lambda i, j, tbl: (0, 0)),
            pl.BlockSpec((None, D_MODEL, PROJ_TILE), lambda i, j, tbl: (j, 0, 0)),
        ],
        out_spec=pl.BlockSpec((PROJ_ROW_TILE, PROJ_TILE), lambda i, j, tbl: (i, j)),
        out_shape=jax.ShapeDtypeStruct((ROWS, W_PROJ), BF16),
        scratch_shapes=[pltpu.VMEM((PROJ_ROW_TILE, D_MODEL), BF16)],
        name="in_proj",
        operands=(h, gain, w),
    )


def _decay_tables():
    c = CHUNK
    blocks = []
    r = np.arange(c)
    for l in range(N_LEVELS):
        b = 1 << l
        m = np.zeros((c, c), np.float32)
        for i in range(c):
            p = i % (2 * b)
            s = i - p + b
            if p >= b:
                m[i, s:i + 1] = 1.0
            else:
                m[i, i + 1:s] = 1.0
        blocks.append(m)
    blocks.append((r[None, :] <= r[:, None]).astype(np.float32))
    blocks.append((r[None, :] > r[:, None]).astype(np.float32))
    blocks.append(np.ones((8, c), np.float32))
    wc = np.concatenate(blocks, axis=0)
    wc = np.concatenate([wc] * 3, axis=1)

    masks = np.zeros((N_LEVELS + 1, c, c), np.float32)
    for l in range(N_LEVELS):
        b = 1 << l
        same = (r[:, None] // (2 * b)) == (r[None, :] // (2 * b))
        up = (r[:, None] & b) != 0
        lo = (r[None, :] & b) == 0
        masks[l] = (same & up & lo).astype(np.float32)
    masks[N_LEVELS] = np.eye(c, dtype=np.float32)
    return wc, masks


def _gla_core(q, k, v, g, wc_ref, mk_ref, s_ref, s_base, n_heads, dk, dv):
    c = CHUNK
    g_hi = g.astype(BF16)
    r1 = g - g_hi.astype(F32)
    g_mid = r1.astype(BF16)
    g_lo = (r1 - g_mid.astype(F32)).astype(BF16)
    ex = jnp.exp(_dot(wc_ref[...], jnp.concatenate([g_hi, g_mid, g_lo], axis=0)))

    rows = lax.broadcasted_iota(jnp.int32, (c, 1), 0)
    qb = q.astype(BF16)
    kb = k.astype(BF16)
    vb = v.astype(BF16)
    scores = []
    for h in range(n_heads):
        scores.append(mk_ref[N_LEVELS] * _dot_nt(qb[:, h * dk:(h + 1) * dk], kb[:, h * dk:(h + 1) * dk]))
    for l in range(N_LEVELS):
        upper = (rows & (1 << l)) != 0
        x = (jnp.where(upper, q, k) * ex[l * c:(l + 1) * c]).astype(BF16)
        for h in range(n_heads):
            xh = x[:, h * dk:(h + 1) * dk]
            scores[h] = scores[h] + mk_ref[l] * _dot_nt(xh, xh)

    base = N_LEVELS * c
    qt = (q * ex[base:base + c]).astype(BF16)
    kt = (k * ex[base + c:base + 2 * c]).astype(BF16)
    tot = ex[base + 2 * c:base + 2 * c + 1]
    outs = []
    for h in range(n_heads):
        st = s_ref[s_base + h]
        vh = vb[:, h * dv:(h + 1) * dv]
        o = _dot(scores[h].astype(BF16), vh) + _dot_nt(qt[:, h * dk:(h + 1) * dk], st.astype(BF16))
        s_ref[s_base + h] = st * tot[:, h * dk:(h + 1) * dk] + _dot_tn(vh, kt[:, h * dk:(h + 1) * dk])
        outs.append(o)
    return outs


def _head_rms(outs, gain, gate):
    dv = outs[0].shape[-1]
    ys = []
    for h, o in enumerate(outs):
        ms = jnp.mean(o * o, axis=-1, keepdims=True)
        ys.append(o * lax.rsqrt(ms + EPS))
    y = jnp.concatenate(ys, axis=-1)
    return y * gain * _silu(gate)


def _log_sigmoid(z):
    return jnp.minimum(z, 0.0) - jnp.log(1.0 + jnp.exp(-jnp.abs(z)))


def _gla_kernel(q_ref, k_ref, v_ref, gate_ref, lr_ref, w2_ref, b_ref, norm_ref, wc_ref, mk_ref,
                o_ref, s_ref):
    @pl.when(pl.program_id(0) == 0)
    def _():
        s_ref[...] = jnp.zeros_like(s_ref)

    for b in range(BATCH):
        z = _dot(lr_ref[b], w2_ref[...]) + b_ref[...]
        g = _log_sigmoid(z) * (1.0 / GLA_GATE_TAU)
        q = q_ref[b].astype(F32) * (GLA_DK ** -0.5)
        outs = _gla_core(q, k_ref[b].astype(F32), v_ref[b].astype(F32), g, wc_ref, mk_ref, s_ref,
                         b * GLA_HEADS, GLA_HEADS, GLA_DK, GLA_DV)
        o_ref[b] = _head_rms(outs, norm_ref[...], gate_ref[b].astype(F32)).astype(o_ref.dtype)


def _hgrn_kernel(q_ref, f_ref, i_ref, gate_ref, lb_ref, norm_ref, wc_ref, mk_ref, o_ref, s_ref, *, layer):
    @pl.when(pl.program_id(0) == 0)
    def _():
        s_ref[...] = jnp.zeros_like(s_ref)

    logits = lb_ref[...]
    e = jnp.exp(logits - jnp.max(logits, axis=0, keepdims=True))
    soft = e / jnp.sum(e, axis=0, keepdims=True)
    lb = jnp.zeros((1, logits.shape[1]), F32)
    for l in range(1, layer + 1):
        lb = lb + soft[l:l + 1]

    for b in range(BATCH):
        hf = f_ref[b].astype(F32)
        forget = lb + (1.0 - lb) * _sigmoid(hf)
        g = jnp.log(jnp.maximum(forget, FORGET_FLOOR))
        k = (1.0 - lb) * _sigmoid(-hf)
        v = _silu(i_ref[b].astype(F32))
        outs = _gla_core(q_ref[b].astype(F32), k, v, g, wc_ref, mk_ref, s_ref,
                         b * HGRN_HEADS, HGRN_HEADS, HGRN_DK, HGRN_DV)
        o_ref[b] = _head_rms(outs, norm_ref[...], gate_ref[b].astype(F32)).astype(o_ref.dtype)


def _chunk_block(c):
    return (c + N_CHUNKS - 1) % N_CHUNKS


def _col_spec(width, offset):
    assert offset % width == 0
    blk = offset // width
    return pl.BlockSpec((BATCH, CHUNK, width), lambda c: (0, _chunk_block(c), blk))


def _const_spec(shape):
    nd = len(shape)
    return pl.BlockSpec(shape, lambda c: (0,) * nd)


def _mixer_out(width):
    return dict(
        out_specs=pl.BlockSpec((BATCH, CHUNK, width), lambda c: (0, _chunk_block(c), 0)),
        out_shape=jax.ShapeDtypeStruct((BATCH, L_PAD, width), BF16),
        compiler_params=_params("arbitrary"),
    )


def _gla(proj, w2, bias, norm, wc, masks):
    qk = GLA_HEADS * GLA_DK
    vw = GLA_HEADS * GLA_DV
    return pl.pallas_call(
        _gla_kernel,
        grid=(N_CHUNKS,),
        in_specs=[
            _col_spec(qk, OFF_GLA),
            _col_spec(qk, OFF_GLA + qk),
            _col_spec(vw, OFF_GLA + 2 * qk),
            _col_spec(vw, OFF_GLA + 2 * qk + vw),
            _col_spec(LANE, OFF_LR),
            _const_spec(w2.shape),
            _const_spec(bias.shape),
            _const_spec(norm.shape),
            _const_spec(wc.shape),
            _const_spec(masks.shape),
        ],
        scratch_shapes=[pltpu.VMEM((BATCH * GLA_HEADS, GLA_DV, GLA_DK), F32)],
        name="gla",
        **_mixer_out(vw),
    )(proj, proj, proj, proj, proj, w2, bias, norm, wc, masks)


def _hgrn(proj, lb_logits, norm, wc, masks, layer):
    w = HGRN_HEADS * HGRN_DK
    return pl.pallas_call(
        functools.partial(_hgrn_kernel, layer=layer),
        grid=(N_CHUNKS,),
        in_specs=[
            _col_spec(w, OFF_HGRN),
            _col_spec(w, OFF_HGRN + w),
            _col_spec(w, OFF_HGRN + 2 * w),
            _col_spec(w, OFF_HGRN + 3 * w),
            _const_spec(lb_logits.shape),
            _const_spec(norm.shape),
            _const_spec(wc.shape),
            _const_spec(masks.shape),
        ],
        scratch_shapes=[pltpu.VMEM((BATCH * HGRN_HEADS, HGRN_DV, HGRN_DK), F32)],
        name="hgrn",
        **_mixer_out(w),
    )(proj, proj, proj, proj, lb_logits, norm, wc, masks)


def _ret_kernel(q_ref, k_ref, v_ref, gate_ref, cos_ref, sin_ref, qd_ref, kd_ref, cd_ref, im_ref, norm_ref,
                o_ref, s_ref):
    @pl.when(pl.program_id(0) == 0)
    def _():
        s_ref[...] = jnp.zeros_like(s_ref)

    half = RET_DK // 2
    cos = cos_ref[...]
    sin = sin_ref[...]
    for b in range(BATCH):
        q = q_ref[b].astype(F32)
        k = k_ref[b].astype(F32)
        v = v_ref[b]
        ys = []
        for h in range(RET_HEADS):
            lo = h * RET_DK
            q1, q2 = q[:, lo:lo + half], q[:, lo + half:lo + RET_DK]
            k1, k2 = k[:, lo:lo + half], k[:, lo + half:lo + RET_DK]
            qr = jnp.concatenate([q1 * cos - q2 * sin, q1 * sin + q2 * cos], axis=-1) * (RET_DK ** -0.5)
            kr = jnp.concatenate([k1 * cos - k2 * sin, k1 * sin + k2 * cos], axis=-1)
            vh = v[:, h * RET_DV:(h + 1) * RET_DV]
            scores = _dot_nt(qr.astype(BF16), kr.astype(BF16)) * im_ref[h]
            st = s_ref[b * RET_HEADS + h]
            o = _dot(scores.astype(BF16), vh) + _dot_nt((qr * qd_ref[h]).astype(BF16), st.astype(BF16))
            s_ref[b * RET_HEADS + h] = st * cd_ref[h] + _dot_tn(vh, (kr * kd_ref[h]).astype(BF16))
            o = o - jnp.mean(o, axis=-1, keepdims=True)
            ms = jnp.mean(o * o, axis=-1, keepdims=True)
            ys.append(o * lax.rsqrt(ms + EPS))
        y = jnp.concatenate(ys, axis=-1)
        o_ref[b] = (y * norm_ref[...] * _silu(gate_ref[b].astype(F32))).astype(o_ref.dtype)


def _ret(proj, cos, sin, qd, kd, cd, im, norm):
    w = RET_HEADS * RET_DK
    half = RET_DK // 2
    return pl.pallas_call(
        _ret_kernel,
        grid=(N_CHUNKS,),
        in_specs=[
            _col_spec(w, OFF_RET),
            _col_spec(w, OFF_RET + w),
            _col_spec(w, OFF_RET + 2 * w),
            _col_spec(w, OFF_RET + 3 * w),
            pl.BlockSpec((CHUNK, half), lambda c: (c, 0)),
            pl.BlockSpec((CHUNK, half), lambda c: (c, 0)),
            _const_spec(qd.shape),
            _const_spec(kd.shape),
            _const_spec(cd.shape),
            _const_spec(im.shape),
            _const_spec(norm.shape),
        ],
        scratch_shapes=[pltpu.VMEM((BATCH * RET_HEADS, RET_DV, RET_DK), F32)],
        name="retention",
        **_mixer_out(w),
    )(proj, proj, proj, proj, cos, sin, qd, kd, cd, im, norm)


def _ret_tables():
    f32 = jnp.float32
    half = RET_DK // 2
    pos = jnp.arange(L_PAD, dtype=f32) - PAD
    inv_freq = ROPE_BASE ** (-jnp.arange(half, dtype=f32) / half)
    ang = pos[:, None] * inv_freq[None, :]
    log_gamma = jnp.log(1.0 - 2.0 ** (-5.0 - jnp.arange(RET_HEADS, dtype=f32)))
    idx = jnp.arange(CHUNK, dtype=f32)
    rel = idx[:, None] - idx[None, :]
    causal = (rel >= 0)[None]
    intra = jnp.where(causal, jnp.exp(jnp.where(causal, rel[None], 0.0) * log_gamma[:, None, None]), 0.0)
    q_decay = jnp.exp((idx[None, :] + 1.0) * log_gamma[:, None])[..., None]
    k_decay = jnp.exp((CHUNK - 1.0 - idx[None, :]) * log_gamma[:, None])[..., None]
    chunk_decay = jnp.exp(CHUNK * log_gamma)[:, None, None]
    return jnp.cos(ang), jnp.sin(ang), q_decay, k_decay, chunk_decay, intra


def _merge_kernel(y0_ref, y1_ref, y2_ref, mg_ref, wb_ref, wo_ref, h_ref, o_ref):
    merged = None
    for n, y_ref in enumerate((y0_ref, y1_ref, y2_ref)):
        gate = _sigmoid(mg_ref[:, n * D_MODEL:(n + 1) * D_MODEL].astype(F32))
        t = _dot(y_ref[...], wb_ref[n * BRANCH_WIDTH:(n + 1) * BRANCH_WIDTH, :]) * gate
        merged = t if merged is None else merged + t
    o_ref[...] = h_ref[...] + _dot(merged.astype(BF16), wo_ref[...])


def _merge(ys, proj, wb, wo, h):
    assert OFF_MG == 0
    rows = lambda width: pl.BlockSpec((MERGE_TILE, width), lambda i: (i, 0))
    whole = lambda a: pl.BlockSpec(a.shape, lambda i: (0, 0))
    return pl.pallas_call(
        _merge_kernel,
        grid=(ROWS // MERGE_TILE,),
        in_specs=[rows(BRANCH_WIDTH)] * N_BRANCH + [rows(N_BRANCH * D_MODEL)]
        + [whole(wb), whole(wo), rows(D_MODEL)],
        out_specs=rows(D_MODEL),
        out_shape=jax.ShapeDtypeStruct((ROWS, D_MODEL), F32),
        compiler_params=_params("parallel"),
        name="merge",
    )(*ys, proj, wb, wo, h)


def kernel(x, meta_tokens, ffn1_norm, ffn1_w_gate, ffn1_w_up, ffn1_w_down, mix_norm, w_in, gla_w_gate2, gla_b_gate, gla_norm, ret_norm, hgrn_lb_logits, hgrn_norm, w_branch, w_out, ffn2_norm, ffn2_w_gate, ffn2_w_up, ffn2_w_down, final_norm):
    b = x.shape[0]
    meta = jnp.broadcast_to(meta_tokens[None].astype(x.dtype), (b, N_META, D_MODEL))
    h = jnp.concatenate([x, jnp.zeros((b, PAD, D_MODEL), x.dtype), meta], axis=1).reshape(ROWS, D_MODEL)

    wc_np, masks_np = _decay_tables()
    wc = jnp.asarray(wc_np, BF16)
    masks = jnp.asarray(masks_np, F32)
    cos, sin, qd, kd, cd, im = _ret_tables()

    def ffn_steps(tiles):
        return (ROWS // tiles[0]) * pl.cdiv(D_FF, tiles[1])

    def ffn_casts(wg, wu, wd, layer, tiles, host_steps):
        padded = _ff_padded(tiles[1])
        n = padded // SIDE_BLOCK
        gap = min(n, (host_steps - n) // 2)
        return [_SideCast(wg, (layer,), 1, padded, 0, tiles[1]), _SideCast(wu, (layer,), 1, padded, gap, tiles[1]),
                _SideCast(wd, (layer,), 0, padded, 2 * gap)]

    row = lambda v: v.reshape(1, -1).astype(F32)
    final_gain = row(final_norm)
    w_in_t = jnp.swapaxes(w_in, 1, 2)
    wb_rows = w_branch.reshape(DEPTH, N_BRANCH * BRANCH_WIDTH, D_MODEL)
    padded = _ff_padded(FFN1_TILES[1])
    ffn1_w = (_cast(ffn1_w_gate, (0,), 1, padded), _cast(ffn1_w_up, (0,), 1, padded),
              _cast(ffn1_w_down, (0,), 0, padded))
    for layer in range(DEPTH):
        h, (w_proj, *ffn2_w) = _ffn(
            h, row(ffn1_norm[layer]), *ffn1_w, final_gain, False, FFN1_TILES,
            sides=[_SideWIn(w_in_t, layer, 0)]
            + ffn_casts(ffn2_w_gate, ffn2_w_up, ffn2_w_down, layer,
                        FINAL_TILES if layer == DEPTH - 1 else FFN2_TILES, ffn_steps(FFN1_TILES)))
        proj, (wb, wo) = _proj(
            h, row(mix_norm[layer]), w_proj,
            sides=[_SideCast(wb_rows, (layer,), 0, N_BRANCH * BRANCH_WIDTH, 0),
                   _SideCast(w_out, (layer,), 0, D_MODEL, N_BRANCH * BRANCH_WIDTH // SIDE_BLOCK)])
        w2 = jnp.pad(gla_w_gate2[layer], ((0, LANE - GLA_GATE_RANK), (0, 0))).astype(BF16)
        proj_seq = proj.reshape(b, L_PAD, W_PROJ)
        y_gla = _gla(proj_seq, w2, row(gla_b_gate[layer]), row(gla_norm[layer]), wc, masks)
        y_ret = _ret(proj_seq, cos, sin, qd, kd, cd, im, row(ret_norm[layer]))
        y_hgrn = _hgrn(proj_seq, hgrn_lb_logits.astype(F32), row(hgrn_norm[layer]), wc, masks, layer)
        ys = [y.reshape(ROWS, BRANCH_WIDTH) for y in (y_gla, y_ret, y_hgrn)]
        h = _merge(ys, proj, wb, wo, h)
        last = layer == DEPTH - 1
        tiles = FINAL_TILES if last else FFN2_TILES
        h, ffn1_w = _ffn(
            h, row(ffn2_norm[layer]), *ffn2_w, final_gain, last, tiles,
            sides=[] if last else ffn_casts(ffn1_w_gate, ffn1_w_up, ffn1_w_down, layer + 1, FFN1_TILES,
                                            ffn_steps(tiles)))
    return h
```

```python
import functools
from typing import NamedTuple

import numpy as np
import jax
import jax.numpy as jnp
from jax import lax
from jax.experimental import pallas as pl
from jax.experimental.pallas import tpu as pltpu

D_MODEL = 2048
BATCH = 2
SEQ = 4096
DEPTH = 2
N_META = 16
CHUNK = 64
PAD = CHUNK - N_META
D_FF = 5504
FFN_RES = 0.5
EPS = 1e-6
N_BRANCH = 3
BRANCH_WIDTH = 1024

GLA_HEADS, GLA_DK, GLA_DV = 4, 128, 256
GLA_GATE_RANK = 16
GLA_GATE_TAU = 16.0
RET_HEADS, RET_DK, RET_DV = 4, 256, 256
ROPE_BASE = 10000.0
HGRN_HEADS, HGRN_DK, HGRN_DV = 8, 128, 128
FORGET_FLOOR = 1e-20

LANE = 128
L_PAD = PAD + N_META + SEQ
N_CHUNKS = L_PAD // CHUNK
ROWS = BATCH * L_PAD
N_LEVELS = 6
assert 1 << N_LEVELS == CHUNK

FFN1_TILES = (640, 512)
FFN2_TILES = (640, 512)
FINAL_TILES = (832, 512)
PROJ_ROW_TILE = 832
PROJ_TILE = 1792
MERGE_TILE = 320
CAST_TILE = 512
SIDE_BLOCK = LANE
LR_PAD = CAST_TILE
W_IN_SHIFT = GLA_GATE_RANK

OFF_MG = 0
OFF_GLA = OFF_MG + N_BRANCH * D_MODEL
OFF_RET = OFF_GLA + 2 * GLA_HEADS * GLA_DK + 2 * GLA_HEADS * GLA_DV
OFF_HGRN = OFF_RET + 2 * RET_HEADS * RET_DK + 2 * RET_HEADS * RET_DV
OFF_LR = OFF_HGRN + 2 * HGRN_HEADS * HGRN_DK + 2 * HGRN_HEADS * HGRN_DV
W_PROJ = OFF_LR + LR_PAD

VMEM_LIMIT = 56 * 1024 * 1024

F32 = jnp.float32
BF16 = jnp.bfloat16


def _params(*sem):
    return pltpu.CompilerParams(dimension_semantics=sem, vmem_limit_bytes=VMEM_LIMIT)


def _rms_rows(x, gain):
    ms = jnp.mean(x * x, axis=-1, keepdims=True)
    return x * lax.rsqrt(ms + EPS) * gain


def _sigmoid(x):
    return 1.0 / (1.0 + jnp.exp(-x))


def _silu(x):
    return x * _sigmoid(x)


def _dot(a, b):
    return jnp.dot(a, b, preferred_element_type=F32)


def _dot_nt(a, b):
    return lax.dot_general(a, b, (((1,), (1,)), ((), ())), preferred_element_type=F32)


def _dot_tn(a, b):
    return lax.dot_general(a, b, (((0,), (0,)), ((), ())), preferred_element_type=F32)


def _cast_kernel(w_ref, o_ref, *, axis, valid):
    x = w_ref[...]
    idx = pl.program_id(0) * x.shape[axis] + lax.broadcasted_iota(jnp.int32, x.shape, axis)
    o_ref[...] = jnp.where(idx < valid, x, 0.0).astype(BF16)


def _cast(w, lead, axis, padded):
    rows, cols = w.shape[-2:]
    nl = len(lead)
    if axis == 0:
        block, out_shape, grid = (CAST_TILE, cols), (padded, cols), padded // CAST_TILE
        imap = lambda j: (*lead, j, 0)
        omap = lambda j: (j, 0)
    else:
        block, out_shape, grid = (rows, CAST_TILE), (padded // CAST_TILE, rows, CAST_TILE), padded // CAST_TILE
        imap = lambda j: (*lead, 0, j)
        omap = lambda j: (j, 0, 0)
    out_block = block if axis == 0 else (None,) + block
    return pl.pallas_call(
        functools.partial(_cast_kernel, axis=axis, valid=w.shape[-2 + axis]),
        grid=(grid,),
        in_specs=[pl.BlockSpec((None,) * nl + block, imap)],
        out_specs=pl.BlockSpec(out_block, omap),
        out_shape=jax.ShapeDtypeStruct(out_shape, BF16),
        compiler_params=_params("parallel"),
        name="cast",
    )(w)


SIDE_IDLE, SIDE_COPY, SIDE_ZERO, SIDE_SHIFTED, SIDE_LOW_RANK = 0, 1, 2, 3, 4


class _SideCast(NamedTuple):
    src: jax.Array
    lead: tuple
    axis: int
    padded: int
    start: int
    col_tile: int = 0


class _SideWIn(NamedTuple):
    src_t: jax.Array
    layer: int
    start: int


class _SidePlan(NamedTuple):
    table: np.ndarray
    in_specs: list
    operands: list
    out_specs: list
    out_shapes: list
    runners: list
    steps_per_row: int


def _w_in_blocks():
    per = lambda width: width // SIDE_BLOCK
    src = lambda col: col // SIDE_BLOCK
    gla_w = 2 * GLA_HEADS * GLA_DK + 2 * GLA_HEADS * GLA_DV
    ret_w = 2 * RET_HEADS * RET_DK + 2 * RET_HEADS * RET_DV
    hgrn_w = 2 * HGRN_HEADS * HGRN_DK + 2 * HGRN_HEADS * HGRN_DV
    lr_col = gla_w
    blocks = []
    for t in range(per(N_BRANCH * D_MODEL)):
        blocks.append((src(lr_col + ret_w + hgrn_w) + t, SIDE_SHIFTED))
    for t in range(per(gla_w)):
        blocks.append((t, SIDE_COPY))
    for t in range(per(ret_w + hgrn_w)):
        blocks.append((src(lr_col) + t, SIDE_SHIFTED))
    blocks.append((src(lr_col), SIDE_LOW_RANK))
    blocks += [(src(lr_col), SIDE_ZERO)] * (per(LR_PAD) - 1)
    assert len(blocks) == per(W_PROJ)
    return [(s, t, mode) for t, (s, mode) in enumerate(blocks)]


def _side_plan(jobs, n_rows, steps_per_row):
    n_steps = n_rows * steps_per_row
    rows, in_specs, operands, out_specs, out_shapes, runners = [], [], [], [], [], []

    def add_rows(blocks, start):
        n = len(blocks)
        assert start + n <= n_steps
        arr = np.asarray(blocks, np.int32).T
        tab = np.zeros((3, n_steps), np.int32)
        tab[:2, :start] = arr[:2, :1]
        tab[:, start:start + n] = arr
        tab[:2, start + n:] = arr[:2, -1:]
        base = len(rows) * n_steps
        rows.extend(tab)
        return base

    step_of = lambda i, j: i * steps_per_row + j
    for job in jobs:
        if isinstance(job, _SideCast):
            r, c = job.src.shape[-2:]
            valid = job.src.shape[-2 + job.axis]
            assert valid % SIDE_BLOCK == 0 and job.padded % SIDE_BLOCK == 0
            n_valid, n_all = valid // SIDE_BLOCK, job.padded // SIDE_BLOCK
            base = add_rows([(min(t, n_valid - 1), t, SIDE_COPY if t < n_valid else SIDE_ZERO)
                             for t in range(n_all)], job.start)
            lead, nl = job.lead, len(job.lead)
            if job.axis == 0:
                block, out_shape = (SIDE_BLOCK, c), (job.padded, c)
                imap = lambda i, j, tbl, base=base, lead=lead: (*lead, tbl[base + step_of(i, j)], 0)
                omap = lambda i, j, tbl, base=base: (tbl[base + n_steps + step_of(i, j)], 0)
            else:
                per_tile = job.col_tile // SIDE_BLOCK
                block, out_shape = (None, r, SIDE_BLOCK), (job.padded // job.col_tile, r, job.col_tile)
                imap = lambda i, j, tbl, base=base, lead=lead: (*lead, 0, tbl[base + step_of(i, j)])

                def omap(i, j, tbl, base=base, per_tile=per_tile):
                    t = tbl[base + n_steps + step_of(i, j)]
                    return (t // per_tile, 0, t % per_tile)

            in_block = (SIDE_BLOCK, c) if job.axis == 0 else (r, SIDE_BLOCK)
            in_specs.append(pl.BlockSpec((None,) * nl + in_block, imap))
            operands.append(job.src)

            def run(tbl_ref, step, ins, out, base=base):
                mode = tbl_ref[base + 2 * n_steps + step]

                @pl.when(mode != SIDE_IDLE)
                def _():
                    out[...] = jnp.where(mode == SIDE_ZERO, 0.0, ins[0][...]).astype(BF16)

            runners.append((1, run))
        else:
            base = add_rows(_w_in_blocks(), job.start)
            layer = job.layer
            shifts_per_block = SIDE_BLOCK // W_IN_SHIFT
            in_specs.append(pl.BlockSpec(
                (None, SIDE_BLOCK, D_MODEL),
                lambda i, j, tbl, base=base, layer=layer: (layer, tbl[base + step_of(i, j)], 0)))
            in_specs.append(pl.BlockSpec(
                (None, W_IN_SHIFT, D_MODEL),
                lambda i, j, tbl, base=base, layer=layer:
                (layer, (tbl[base + step_of(i, j)] + 1) * shifts_per_block, 0)))
            operands += [job.src_t, job.src_t]
            per_tile = PROJ_TILE // SIDE_BLOCK
            block, out_shape = (None, D_MODEL, SIDE_BLOCK), (W_PROJ // PROJ_TILE, D_MODEL, PROJ_TILE)

            def omap(i, j, tbl, base=base, per_tile=per_tile):
                t = tbl[base + n_steps + step_of(i, j)]
                return (t // per_tile, 0, t % per_tile)

            def run(tbl_ref, step, ins, out, base=base):
                a_ref, b_ref = ins
                mode = tbl_ref[base + 2 * n_steps + step]

                @pl.when(mode == SIDE_SHIFTED)
                def _():
                    x = jnp.concatenate([a_ref[W_IN_SHIFT:, :], b_ref[...]], axis=0)
                    out[...] = x.astype(BF16).T

                @pl.when(mode == SIDE_COPY)
                def _():
                    out[...] = a_ref[...].astype(BF16).T

                @pl.when(mode == SIDE_LOW_RANK)
                def _():
                    row = lax.broadcasted_iota(jnp.int32, a_ref.shape, 0)
                    out[...] = jnp.where(row < GLA_GATE_RANK, a_ref[...], 0.0).astype(BF16).T

                @pl.when(mode == SIDE_ZERO)
                def _():
                    out[...] = jnp.zeros_like(out)

            runners.append((2, run))
        out_specs.append(pl.BlockSpec(block, omap))
        out_shapes.append(jax.ShapeDtypeStruct(out_shape, BF16))

    table = np.concatenate(rows) if rows else np.zeros((1,), np.int32)
    return _SidePlan(table, in_specs, operands, out_specs, out_shapes, runners, steps_per_row)


def _run_sides(plan, tbl_ref, side_ins, side_outs):
    step = pl.program_id(0) * plan.steps_per_row + pl.program_id(1)
    k = 0
    for (n_in, run), out in zip(plan.runners, side_outs):
        run(tbl_ref, step, side_ins[k:k + n_in], out)
        k += n_in


def _hosted_call(kernel, plan, grid, in_specs, out_spec, out_shape, scratch_shapes, name, operands):
    n_side_in = len(plan.in_specs)
    n_main_in = len(in_specs)

    def body(tbl_ref, *refs):
        ins = refs[:n_main_in]
        side_ins = refs[n_main_in:n_main_in + n_side_in]
        out = refs[n_main_in + n_side_in]
        side_outs = refs[n_main_in + n_side_in + 1:n_main_in + n_side_in + 1 + len(plan.out_specs)]
        scratch = refs[n_main_in + n_side_in + 1 + len(plan.out_specs):]
        kernel(*ins, out, *scratch)
        _run_sides(plan, tbl_ref, side_ins, side_outs)

    res = pl.pallas_call(
        body,
        grid_spec=pltpu.PrefetchScalarGridSpec(
            num_scalar_prefetch=1,
            grid=grid,
            in_specs=list(in_specs) + plan.in_specs,
            out_specs=[out_spec] + plan.out_specs,
            scratch_shapes=scratch_shapes,
        ),
        out_shape=[out_shape] + plan.out_shapes,
        compiler_params=_params("arbitrary", "arbitrary"),
        name=name,
    )(jnp.asarray(plan.table), *operands, *plan.operands)
    return res[0], res[1:]


def _ffn_kernel(h_ref, g_ref, wg_ref, wu_ref, wd_ref, fg_ref, o_ref, u_scr, *, final, tail):
    j = pl.program_id(1)
    last = pl.num_programs(1) - 1
    tile = wg_ref.shape[1]

    @pl.when(j == 0)
    def _():
        u_scr[...] = _rms_rows(h_ref[...], g_ref[...]).astype(BF16)
        o_ref[...] = jnp.zeros_like(o_ref)

    def accumulate(width):
        u = u_scr[...]
        a = _dot(u, wg_ref[:, :width])
        b = _dot(u, wu_ref[:, :width])
        act = (_silu(a) * b).astype(BF16)
        o_ref[...] += _dot(act, wd_ref[:width, :])

    if tail == tile:
        accumulate(tile)
    else:
        @pl.when(j < last)
        def _():
            accumulate(tile)

        @pl.when(j == last)
        def _():
            accumulate(tail)

    @pl.when(j == last)
    def _():
        y = h_ref[...] + FFN_RES * o_ref[...]
        if final:
            y = _rms_rows(y, fg_ref[...])
        o_ref[...] = y


def _ff_padded(tf):
    return pl.cdiv(D_FF, tf) * tf


def _ffn(h, gain, wg, wu, wd, final_gain, final, tiles, sides=()):
    tm, tf = tiles
    assert ROWS % tm == 0 and wg.shape == (_ff_padded(tf) // tf, D_MODEL, tf) and wd.shape[0] == _ff_padded(tf)
    grid = (ROWS // tm, wg.shape[0])
    tail = D_FF - (grid[1] - 1) * tf
    plan = _side_plan(sides, *grid)
    if final:
        assert L_PAD % tm == 0
        per_seq = L_PAD // tm
        h = h.reshape(BATCH, L_PAD, D_MODEL)
        rows_spec = pl.BlockSpec((None, tm, D_MODEL), lambda i, j, tbl: (i // per_seq, i % per_seq, 0))
        out_shape = jax.ShapeDtypeStruct((BATCH, SEQ, D_MODEL), F32)
    else:
        rows_spec = pl.BlockSpec((tm, D_MODEL), lambda i, j, tbl: (i, 0))
        out_shape = jax.ShapeDtypeStruct((ROWS, D_MODEL), F32)
    return _hosted_call(
        functools.partial(_ffn_kernel, final=final, tail=tail), plan, grid,
        in_specs=[
            rows_spec,
            pl.BlockSpec((1, D_MODEL), lambda i, j, tbl: (0, 0)),
            pl.BlockSpec((None, D_MODEL, tf), lambda i, j, tbl: (j, 0, 0)),
            pl.BlockSpec((None, D_MODEL, tf), lambda i, j, tbl: (j, 0, 0)),
            pl.BlockSpec((tf, D_MODEL), lambda i, j, tbl: (j, 0)),
            pl.BlockSpec((1, D_MODEL), lambda i, j, tbl: (0, 0)),
        ],
        out_spec=rows_spec,
        out_shape=out_shape,
        scratch_shapes=[pltpu.VMEM((tm, D_MODEL), BF16)],
        name="ffn",
        operands=(h, gain, wg, wu, wd, final_gain),
    )


def _proj_kernel(h_ref, g_ref, w_ref, o_ref, u_scr):
    @pl.when(pl.program_id(1) == 0)
    def _():
        u_scr[...] = _rms_rows(h_ref[...], g_ref[...]).astype(BF16)

    o_ref[...] = _dot(u_scr[...], w_ref[...]).astype(o_ref.dtype)


def _proj(h, gain, w, sides=()):
    grid = (ROWS // PROJ_ROW_TILE, W_PROJ // PROJ_TILE)
    plan = _side_plan(sides, *grid)
    return _hosted_call(
        _proj_kernel, plan, grid,
        in_specs=[
            pl.BlockSpec((PROJ_ROW_TILE, D_MODEL), lambda i, j, tbl: (i, 0)),
            pl.BlockSpec((1, D_MODEL), lambda i, j, tbl: (0, 0)),
            pl.BlockSpec((None, D_MODEL, PROJ_TILE), lambda i, j, tbl: (j, 0, 0)),
        ],
        out_spec=pl.BlockSpec((PROJ_ROW_TILE, PROJ_TILE), lambda i, j, tbl: (i, j)),
        out_shape=jax.ShapeDtypeStruct((ROWS, W_PROJ), BF16),
        scratch_shapes=[pltpu.VMEM((PROJ_ROW_TILE, D_MODEL), BF16)],
        name="in_proj",
        operands=(h, gain, w),
    )


def _decay_tables():
    c = CHUNK
    blocks = []
    r = np.arange(c)
    for l in range(1, N_LEVELS):
        b = 1 << l
        m = np.zeros((c, c), np.float32)
        for i in range(c):
            p = i % (2 * b)
            s = i - p + b
            if p >= b:
                m[i, s:i + 1] = 1.0
            else:
                m[i, i + 1:s] = 1.0
        blocks.append(m)
    blocks.append((r[None, :] <= r[:, None]).astype(np.float32))
    wc = np.concatenate(blocks, axis=0)
    wc = np.concatenate([wc] * 3, axis=1)

    masks = np.zeros((N_LEVELS + 1, c, c), np.float32)
    for l in range(N_LEVELS):
        b = 1 << l
        same = (r[:, None] // (2 * b)) == (r[None, :] // (2 * b))
        up = (r[:, None] & b) != 0
        lo = (r[None, :] & b) == 0
        masks[l] = (same & up & lo).astype(np.float32)
    masks[N_LEVELS] = np.eye(c, dtype=np.float32)
    return wc, masks


def _gla_core(q, k, v, g, wc_ref, mk_ref, s_ref, s_base, n_heads, dk, dv):
    c = CHUNK
    g_hi = g.astype(BF16)
    r1 = g - g_hi.astype(F32)
    g_mid = r1.astype(BF16)
    g_lo = (r1 - g_mid.astype(F32)).astype(BF16)
    sums = _dot(wc_ref[...], jnp.concatenate([g_hi, g_mid, g_lo], axis=0))
    base = (N_LEVELS - 1) * c
    ex = jnp.exp(sums[:base])
    cum = sums[base:base + c]
    total = cum[c - 1:c]

    vb = v.astype(BF16)
    on_diag = q * k
    below = q * jnp.exp(g) * pltpu.roll(k, 1, axis=0)
    scores = []
    for h in range(n_heads):
        lanes = slice(h * dk, (h + 1) * dk)
        scores.append(mk_ref[N_LEVELS] * jnp.sum(on_diag[:, lanes], axis=-1, keepdims=True)
                      + mk_ref[0] * jnp.sum(below[:, lanes], axis=-1, keepdims=True))
    rows = lax.broadcasted_iota(jnp.int32, (c, 1), 0)
    for l in range(1, N_LEVELS):
        upper = (rows & (1 << l)) != 0
        x = (jnp.where(upper, q, k) * ex[(l - 1) * c:l * c]).astype(BF16)
        for h in range(n_heads):
            xh = x[:, h * dk:(h + 1) * dk]
            scores[h] = scores[h] + mk_ref[l] * _dot_nt(xh, xh)

    qt = (q * jnp.exp(cum)).astype(BF16)
    kt = (k * jnp.exp(total - cum)).astype(BF16)
    tot = jnp.exp(total)
    outs = []
    for h in range(n_heads):
        st = s_ref[s_base + h]
        vh = vb[:, h * dv:(h + 1) * dv]
        o = _dot(scores[h].astype(BF16), vh) + _dot_nt(qt[:, h * dk:(h + 1) * dk], st.astype(BF16))
        s_ref[s_base + h] = st * tot[:, h * dk:(h + 1) * dk] + _dot_tn(vh, kt[:, h * dk:(h + 1) * dk])
        outs.append(o)
    return outs


def _head_rms(outs, gain, gate):
    dv = outs[0].shape[-1]
    ys = []
    for h, o in enumerate(outs):
        ms = jnp.mean(o * o, axis=-1, keepdims=True)
        ys.append(o * lax.rsqrt(ms + EPS))
    y = jnp.concatenate(ys, axis=-1)
    return y * gain * _silu(gate)


def _log_sigmoid(z):
    return jnp.minimum(z, 0.0) - jnp.log(1.0 + jnp.exp(-jnp.abs(z)))


def _gla_kernel(q_ref, k_ref, v_ref, gate_ref, lr_ref, w2_ref, b_ref, norm_ref, wc_ref, mk_ref,
                o_ref, s_ref):
    @pl.when(pl.program_id(0) == 0)
    def _():
        s_ref[...] = jnp.zeros_like(s_ref)

    for b in range(BATCH):
        z = _dot(lr_ref[b], w2_ref[...]) + b_ref[...]
        g = _log_sigmoid(z) * (1.0 / GLA_GATE_TAU)
        q = q_ref[b].astype(F32) * (GLA_DK ** -0.5)
        outs = _gla_core(q, k_ref[b].astype(F32), v_ref[b].astype(F32), g, wc_ref, mk_ref, s_ref,
                         b * GLA_HEADS, GLA_HEADS, GLA_DK, GLA_DV)
        o_ref[b] = _head_rms(outs, norm_ref[...], gate_ref[b].astype(F32)).astype(o_ref.dtype)


def _hgrn_kernel(q_ref, f_ref, i_ref, gate_ref, lb_ref, norm_ref, wc_ref, mk_ref, o_ref, s_ref, *, layer):
    @pl.when(pl.program_id(0) == 0)
    def _():
        s_ref[...] = jnp.zeros_like(s_ref)

    logits = lb_ref[...]
    e = jnp.exp(logits - jnp.max(logits, axis=0, keepdims=True))
    soft = e / jnp.sum(e, axis=0, keepdims=True)
    lb = jnp.zeros((1, logits.shape[1]), F32)
    for l in range(1, layer + 1):
        lb = lb + soft[l:l + 1]

    for b in range(BATCH):
        hf = f_ref[b].astype(F32)
        forget = lb + (1.0 - lb) * _sigmoid(hf)
        g = jnp.log(jnp.maximum(forget, FORGET_FLOOR))
        k = (1.0 - lb) * _sigmoid(-hf)
        v = _silu(i_ref[b].astype(F32))
        outs = _gla_core(q_ref[b].astype(F32), k, v, g, wc_ref, mk_ref, s_ref,
                         b * HGRN_HEADS, HGRN_HEADS, HGRN_DK, HGRN_DV)
        o_ref[b] = _head_rms(outs, norm_ref[...], gate_ref[b].astype(F32)).astype(o_ref.dtype)


def _chunk_block(c):
    return (c + N_CHUNKS - 1) % N_CHUNKS


def _col_spec(width, offset):
    assert offset % width == 0
    blk = offset // width
    return pl.BlockSpec((BATCH, CHUNK, width), lambda c: (0, _chunk_block(c), blk))


def _const_spec(shape):
    nd = len(shape)
    return pl.BlockSpec(shape, lambda c: (0,) * nd)


def _mixer_out(width):
    return dict(
        out_specs=pl.BlockSpec((BATCH, CHUNK, width), lambda c: (0, _chunk_block(c), 0)),
        out_shape=jax.ShapeDtypeStruct((BATCH, L_PAD, width), BF16),
        compiler_params=_params("arbitrary"),
    )


def _gla(proj, w2, bias, norm, wc, masks):
    qk = GLA_HEADS * GLA_DK
    vw = GLA_HEADS * GLA_DV
    return pl.pallas_call(
        _gla_kernel,
        grid=(N_CHUNKS,),
        in_specs=[
            _col_spec(qk, OFF_GLA),
            _col_spec(qk, OFF_GLA + qk),
            _col_spec(vw, OFF_GLA + 2 * qk),
            _col_spec(vw, OFF_GLA + 2 * qk + vw),
            _col_spec(LANE, OFF_LR),
            _const_spec(w2.shape),
            _const_spec(bias.shape),
            _const_spec(norm.shape),
            _const_spec(wc.shape),
            _const_spec(masks.shape),
        ],
        scratch_shapes=[pltpu.VMEM((BATCH * GLA_HEADS, GLA_DV, GLA_DK), F32)],
        name="gla",
        **_mixer_out(vw),
    )(proj, proj, proj, proj, proj, w2, bias, norm, wc, masks)


def _hgrn(proj, lb_logits, norm, wc, masks, layer):
    w = HGRN_HEADS * HGRN_DK
    return pl.pallas_call(
        functools.partial(_hgrn_kernel, layer=layer),
        grid=(N_CHUNKS,),
        in_specs=[
            _col_spec(w, OFF_HGRN),
            _col_spec(w, OFF_HGRN + w),
            _col_spec(w, OFF_HGRN + 2 * w),
            _col_spec(w, OFF_HGRN + 3 * w),
            _const_spec(lb_logits.shape),
            _const_spec(norm.shape),
            _const_spec(wc.shape),
            _const_spec(masks.shape),
        ],
        scratch_shapes=[pltpu.VMEM((BATCH * HGRN_HEADS, HGRN_DV, HGRN_DK), F32)],
        name="hgrn",
        **_mixer_out(w),
    )(proj, proj, proj, proj, lb_logits, norm, wc, masks)


def _ret_kernel(q_ref, k_ref, v_ref, gate_ref, cos_ref, sin_ref, qd_ref, kd_ref, cd_ref, im_ref, norm_ref,
                o_ref, s_ref):
    @pl.when(pl.program_id(0) == 0)
    def _():
        s_ref[...] = jnp.zeros_like(s_ref)

    half = RET_DK // 2
    cos = cos_ref[...]
    sin = sin_ref[...]
    for b in range(BATCH):
        q = q_ref[b].astype(F32)
        k = k_ref[b].astype(F32)
        v = v_ref[b]
        ys = []
        for h in range(RET_HEADS):
            lo = h * RET_DK
            q1, q2 = q[:, lo:lo + half], q[:, lo + half:lo + RET_DK]
            k1, k2 = k[:, lo:lo + half], k[:, lo + half:lo + RET_DK]
            qr = jnp.concatenate([q1 * cos - q2 * sin, q1 * sin + q2 * cos], axis=-1) * (RET_DK ** -0.5)
            kr = jnp.concatenate([k1 * cos - k2 * sin, k1 * sin + k2 * cos], axis=-1)
            vh = v[:, h * RET_DV:(h + 1) * RET_DV]
            scores = _dot_nt(qr.astype(BF16), kr.astype(BF16)) * im_ref[h]
            st = s_ref[b * RET_HEADS + h]
            o = _dot(scores.astype(BF16), vh) + _dot_nt((qr * qd_ref[h]).astype(BF16), st.astype(BF16))
            s_ref[b * RET_HEADS + h] = st * cd_ref[h] + _dot_tn(vh, (kr * kd_ref[h]).astype(BF16))
            o = o - jnp.mean(o, axis=-1, keepdims=True)
            ms = jnp.mean(o * o, axis=-1, keepdims=True)
            ys.append(o * lax.rsqrt(ms + EPS))
        y = jnp.concatenate(ys, axis=-1)
        o_ref[b] = (y * norm_ref[...] * _silu(gate_ref[b].astype(F32))).astype(o_ref.dtype)


def _ret(proj, cos, sin, qd, kd, cd, im, norm):
    w = RET_HEADS * RET_DK
    half = RET_DK // 2
    return pl.pallas_call(
        _ret_kernel,
        grid=(N_CHUNKS,),
        in_specs=[
            _col_spec(w, OFF_RET),
            _col_spec(w, OFF_RET + w),
            _col_spec(w, OFF_RET + 2 * w),
            _col_spec(w, OFF_RET + 3 * w),
            pl.BlockSpec((CHUNK, half), lambda c: (c, 0)),
            pl.BlockSpec((CHUNK, half), lambda c: (c, 0)),
            _const_spec(qd.shape),
            _const_spec(kd.shape),
            _const_spec(cd.shape),
            _const_spec(im.shape),
            _const_spec(norm.shape),
        ],
        scratch_shapes=[pltpu.VMEM((BATCH * RET_HEADS, RET_DV, RET_DK), F32)],
        name="retention",
        **_mixer_out(w),
    )(proj, proj, proj, proj, cos, sin, qd, kd, cd, im, norm)


def _ret_tables():
    f32 = jnp.float32
    half = RET_DK // 2
    pos = jnp.arange(L_PAD, dtype=f32) - PAD
    inv_freq = ROPE_BASE ** (-jnp.arange(half, dtype=f32) / half)
    ang = pos[:, None] * inv_freq[None, :]
    log_gamma = jnp.log(1.0 - 2.0 ** (-5.0 - jnp.arange(RET_HEADS, dtype=f32)))
    idx = jnp.arange(CHUNK, dtype=f32)
    rel = idx[:, None] - idx[None, :]
    causal = (rel >= 0)[None]
    intra = jnp.where(causal, jnp.exp(jnp.where(causal, rel[None], 0.0) * log_gamma[:, None, None]), 0.0)
    q_decay = jnp.exp((idx[None, :] + 1.0) * log_gamma[:, None])[..., None]
    k_decay = jnp.exp((CHUNK - 1.0 - idx[None, :]) * log_gamma[:, None])[..., None]
    chunk_decay = jnp.exp(CHUNK * log_gamma)[:, None, None]
    return jnp.cos(ang), jnp.sin(ang), q_decay, k_decay, chunk_decay, intra


def _merge_kernel(y0_ref, y1_ref, y2_ref, mg_ref, wb_ref, wo_ref, h_ref, o_ref):
    merged = None
    for n, y_ref in enumerate((y0_ref, y1_ref, y2_ref)):
        gate = _sigmoid(mg_ref[:, n * D_MODEL:(n + 1) * D_MODEL].astype(F32))
        t = _dot(y_ref[...], wb_ref[n * BRANCH_WIDTH:(n + 1) * BRANCH_WIDTH, :]) * gate
        merged = t if merged is None else merged + t
    o_ref[...] = h_ref[...] + _dot(merged.astype(BF16), wo_ref[...])


def _merge(ys, proj, wb, wo, h):
    assert OFF_MG == 0
    rows = lambda width: pl.BlockSpec((MERGE_TILE, width), lambda i: (i, 0))
    whole = lambda a: pl.BlockSpec(a.shape, lambda i: (0, 0))
    return pl.pallas_call(
        _merge_kernel,
        grid=(ROWS // MERGE_TILE,),
        in_specs=[rows(BRANCH_WIDTH)] * N_BRANCH + [rows(N_BRANCH * D_MODEL)]
        + [whole(wb), whole(wo), rows(D_MODEL)],
        out_specs=rows(D_MODEL),
        out_shape=jax.ShapeDtypeStruct((ROWS, D_MODEL), F32),
        compiler_params=_params("parallel"),
        name="merge",
    )(*ys, proj, wb, wo, h)


def kernel(x, meta_tokens, ffn1_norm, ffn1_w_gate, ffn1_w_up, ffn1_w_down, mix_norm, w_in, gla_w_gate2, gla_b_gate, gla_norm, ret_norm, hgrn_lb_logits, hgrn_norm, w_branch, w_out, ffn2_norm, ffn2_w_gate, ffn2_w_up, ffn2_w_down, final_norm):
    b = x.shape[0]
    meta = jnp.broadcast_to(meta_tokens[None].astype(x.dtype), (b, N_META, D_MODEL))
    h = jnp.concatenate([x, jnp.zeros((b, PAD, D_MODEL), x.dtype), meta], axis=1).reshape(ROWS, D_MODEL)

    wc_np, masks_np = _decay_tables()
    wc = jnp.asarray(wc_np, BF16)
    masks = jnp.asarray(masks_np, F32)
    cos, sin, qd, kd, cd, im = _ret_tables()

    def ffn_steps(tiles):
        return (ROWS // tiles[0]) * pl.cdiv(D_FF, tiles[1])

    def ffn_casts(wg, wu, wd, layer, tiles, host_steps):
        padded = _ff_padded(tiles[1])
        n = padded // SIDE_BLOCK
        gap = min(n, (host_steps - n) // 2)
        return [_SideCast(wg, (layer,), 1, padded, 0, tiles[1]), _SideCast(wu, (layer,), 1, padded, gap, tiles[1]),
                _SideCast(wd, (layer,), 0, padded, 2 * gap)]

    row = lambda v: v.reshape(1, -1).astype(F32)
    final_gain = row(final_norm)
    w_in_t = jnp.swapaxes(w_in, 1, 2)
    wb_rows = w_branch.reshape(DEPTH, N_BRANCH * BRANCH_WIDTH, D_MODEL)
    padded = _ff_padded(FFN1_TILES[1])
    ffn1_w = (_cast(ffn1_w_gate, (0,), 1, padded), _cast(ffn1_w_up, (0,), 1, padded),
              _cast(ffn1_w_down, (0,), 0, padded))
    for layer in range(DEPTH):
        h, (w_proj, *ffn2_w) = _ffn(
            h, row(ffn1_norm[layer]), *ffn1_w, final_gain, False, FFN1_TILES,
            sides=[_SideWIn(w_in_t, layer, 0)]
            + ffn_casts(ffn2_w_gate, ffn2_w_up, ffn2_w_down, layer,
                        FINAL_TILES if layer == DEPTH - 1 else FFN2_TILES, ffn_steps(FFN1_TILES)))
        proj, (wb, wo) = _proj(
            h, row(mix_norm[layer]), w_proj,
            sides=[_SideCast(wb_rows, (layer,), 0, N_BRANCH * BRANCH_WIDTH, 0),
                   _SideCast(w_out, (layer,), 0, D_MODEL, N_BRANCH * BRANCH_WIDTH // SIDE_BLOCK)])
        w2 = jnp.pad(gla_w_gate2[layer], ((0, LANE - GLA_GATE_RANK), (0, 0))).astype(BF16)
        proj_seq = proj.reshape(b, L_PAD, W_PROJ)
        y_gla = _gla(proj_seq, w2, row(gla_b_gate[layer]), row(gla_norm[layer]), wc, masks)
        y_ret = _ret(proj_seq, cos, sin, qd, kd, cd, im, row(ret_norm[layer]))
        y_hgrn = _hgrn(proj_seq, hgrn_lb_logits.astype(F32), row(hgrn_norm[layer]), wc, masks, layer)
        ys = [y.reshape(ROWS, BRANCH_WIDTH) for y in (y_gla, y_ret, y_hgrn)]
        h = _merge(ys, proj, wb, wo, h)
        last = layer == DEPTH - 1
        tiles = FINAL_TILES if last else FFN2_TILES
        h, ffn1_w = _ffn(
            h, row(ffn2_norm[layer]), *ffn2_w, final_gain, last, tiles,
            sides=[] if last else ffn_casts(ffn1_w_gate, ffn1_w_up, ffn1_w_down, layer + 1, FFN1_TILES,
                                            ffn_steps(tiles)))
    return h
```

```python
import functools
from typing import NamedTuple

import numpy as np
import jax
import jax.numpy as jnp
from jax import lax
from jax.experimental import pallas as pl
from jax.experimental.pallas import tpu as pltpu

D_MODEL = 2048
BATCH = 2
SEQ = 4096
DEPTH = 2
N_META = 16
CHUNK = 64
PAD = CHUNK - N_META
D_FF = 5504
FFN_RES = 0.5
EPS = 1e-6
N_BRANCH = 3
BRANCH_WIDTH = 1024

GLA_HEADS, GLA_DK, GLA_DV = 4, 128, 256
GLA_GATE_RANK = 16
GLA_GATE_TAU = 16.0
RET_HEADS, RET_DK, RET_DV = 4, 256, 256
ROPE_BASE = 10000.0
HGRN_HEADS, HGRN_DK, HGRN_DV = 8, 128, 128
FORGET_FLOOR = 1e-20

LANE = 128
L_PAD = PAD + N_META + SEQ
N_CHUNKS = L_PAD // CHUNK
ROWS = BATCH * L_PAD
N_LEVELS = 6
assert 1 << N_LEVELS == CHUNK

FFN1_TILES = (640, 512)
FFN2_TILES = (640, 512)
FINAL_TILES = (832, 512)
PROJ_ROW_TILE = 832
PROJ_TILE = 1792
MERGE_TILE = 320
CAST_TILE = 512
SIDE_BLOCK = LANE
LR_PAD = CAST_TILE
W_IN_SHIFT = GLA_GATE_RANK

OFF_MG = 0
OFF_GLA = OFF_MG + N_BRANCH * D_MODEL
OFF_RET = OFF_GLA + 2 * GLA_HEADS * GLA_DK + 2 * GLA_HEADS * GLA_DV
OFF_HGRN = OFF_RET + 2 * RET_HEADS * RET_DK + 2 * RET_HEADS * RET_DV
OFF_LR = OFF_HGRN + 2 * HGRN_HEADS * HGRN_DK + 2 * HGRN_HEADS * HGRN_DV
W_PROJ = OFF_LR + LR_PAD

VMEM_LIMIT = 56 * 1024 * 1024

F32 = jnp.float32
BF16 = jnp.bfloat16


def _params(*sem):
    return pltpu.CompilerParams(dimension_semantics=sem, vmem_limit_bytes=VMEM_LIMIT)


def _rms_rows(x, gain):
    ms = jnp.mean(x * x, axis=-1, keepdims=True)
    return x * lax.rsqrt(ms + EPS) * gain


def _sigmoid(x):
    return 1.0 / (1.0 + jnp.exp(-x))


def _silu(x):
    return x * _sigmoid(x)


def _dot(a, b):
    return jnp.dot(a, b, preferred_element_type=F32)


def _dot_nt(a, b):
    return lax.dot_general(a, b, (((1,), (1,)), ((), ())), preferred_element_type=F32)


def _dot_tn(a, b):
    return lax.dot_general(a, b, (((0,), (0,)), ((), ())), preferred_element_type=F32)


def _cast_kernel(w_ref, o_ref, *, axis, valid):
    x = w_ref[...]
    idx = pl.program_id(0) * x.shape[axis] + lax.broadcasted_iota(jnp.int32, x.shape, axis)
    o_ref[...] = jnp.where(idx < valid, x, 0.0).astype(BF16)


def _cast(w, lead, axis, padded):
    rows, cols = w.shape[-2:]
    nl = len(lead)
    if axis == 0:
        block, out_shape, grid = (CAST_TILE, cols), (padded, cols), padded // CAST_TILE
        imap = lambda j: (*lead, j, 0)
        omap = lambda j: (j, 0)
    else:
        block, out_shape, grid = (rows, CAST_TILE), (padded // CAST_TILE, rows, CAST_TILE), padded // CAST_TILE
        imap = lambda j: (*lead, 0, j)
        omap = lambda j: (j, 0, 0)
    out_block = block if axis == 0 else (None,) + block
    return pl.pallas_call(
        functools.partial(_cast_kernel, axis=axis, valid=w.shape[-2 + axis]),
        grid=(grid,),
        in_specs=[pl.BlockSpec((None,) * nl + block, imap)],
        out_specs=pl.BlockSpec(out_block, omap),
        out_shape=jax.ShapeDtypeStruct(out_shape, BF16),
        compiler_params=_params("parallel"),
        name="cast",
    )(w)


SIDE_IDLE, SIDE_COPY, SIDE_ZERO, SIDE_SHIFTED, SIDE_LOW_RANK = 0, 1, 2, 3, 4


class _SideCast(NamedTuple):
    src: jax.Array
    lead: tuple
    axis: int
    padded: int
    start: int
    col_tile: int = 0


class _SideWIn(NamedTuple):
    src_t: jax.Array
    layer: int
    start: int


class _SidePlan(NamedTuple):
    table: np.ndarray
    in_specs: list
    operands: list
    out_specs: list
    out_shapes: list
    runners: list
    steps_per_row: int


def _w_in_blocks():
    per = lambda width: width // SIDE_BLOCK
    src = lambda col: col // SIDE_BLOCK
    gla_w = 2 * GLA_HEADS * GLA_DK + 2 * GLA_HEADS * GLA_DV
    ret_w = 2 * RET_HEADS * RET_DK + 2 * RET_HEADS * RET_DV
    hgrn_w = 2 * HGRN_HEADS * HGRN_DK + 2 * HGRN_HEADS * HGRN_DV
    lr_col = gla_w
    blocks = []
    for t in range(per(N_BRANCH * D_MODEL)):
        blocks.append((src(lr_col + ret_w + hgrn_w) + t, SIDE_SHIFTED))
    for t in range(per(gla_w)):
        blocks.append((t, SIDE_COPY))
    for t in range(per(ret_w + hgrn_w)):
        blocks.append((src(lr_col) + t, SIDE_SHIFTED))
    blocks.append((src(lr_col), SIDE_LOW_RANK))
    blocks += [(src(lr_col), SIDE_ZERO)] * (per(LR_PAD) - 1)
    assert len(blocks) == per(W_PROJ)
    return [(s, t, mode) for t, (s, mode) in enumerate(blocks)]


def _side_plan(jobs, n_rows, steps_per_row):
    n_steps = n_rows * steps_per_row
    rows, in_specs, operands, out_specs, out_shapes, runners = [], [], [], [], [], []

    def add_rows(blocks, start):
        n = len(blocks)
        assert start + n <= n_steps
        arr = np.asarray(blocks, np.int32).T
        tab = np.zeros((3, n_steps), np.int32)
        tab[:2, :start] = arr[:2, :1]
        tab[:, start:start + n] = arr
        tab[:2, start + n:] = arr[:2, -1:]
        base = len(rows) * n_steps
        rows.extend(tab)
        return base

    step_of = lambda i, j: i * steps_per_row + j
    for job in jobs:
        if isinstance(job, _SideCast):
            r, c = job.src.shape[-2:]
            valid = job.src.shape[-2 + job.axis]
            assert valid % SIDE_BLOCK == 0 and job.padded % SIDE_BLOCK == 0
            n_valid, n_all = valid // SIDE_BLOCK, job.padded // SIDE_BLOCK
            base = add_rows([(min(t, n_valid - 1), t, SIDE_COPY if t < n_valid else SIDE_ZERO)
                             for t in range(n_all)], job.start)
            lead, nl = job.lead, len(job.lead)
            if job.axis == 0:
                block, out_shape = (SIDE_BLOCK, c), (job.padded, c)
                imap = lambda i, j, tbl, base=base, lead=lead: (*lead, tbl[base + step_of(i, j)], 0)
                omap = lambda i, j, tbl, base=base: (tbl[base + n_steps + step_of(i, j)], 0)
            else:
                per_tile = job.col_tile // SIDE_BLOCK
                block, out_shape = (None, r, SIDE_BLOCK), (job.padded // job.col_tile, r, job.col_tile)
                imap = lambda i, j, tbl, base=base, lead=lead: (*lead, 0, tbl[base + step_of(i, j)])

                def omap(i, j, tbl, base=base, per_tile=per_tile):
                    t = tbl[base + n_steps + step_of(i, j)]
                    return (t // per_tile, 0, t % per_tile)

            in_block = (SIDE_BLOCK, c) if job.axis == 0 else (r, SIDE_BLOCK)
            in_specs.append(pl.BlockSpec((None,) * nl + in_block, imap))
            operands.append(job.src)

            def run(tbl_ref, step, ins, out, base=base):
                mode = tbl_ref[base + 2 * n_steps + step]

                @pl.when(mode != SIDE_IDLE)
                def _():
                    out[...] = jnp.where(mode == SIDE_ZERO, 0.0, ins[0][...]).astype(BF16)

            runners.append((1, run))
        else:
            base = add_rows(_w_in_blocks(), job.start)
            layer = job.layer
            shifts_per_block = SIDE_BLOCK // W_IN_SHIFT
            in_specs.append(pl.BlockSpec(
                (None, SIDE_BLOCK, D_MODEL),
                lambda i, j, tbl, base=base, layer=layer: (layer, tbl[base + step_of(i, j)], 0)))
            in_specs.append(pl.BlockSpec(
                (None, W_IN_SHIFT, D_MODEL),
                lambda i, j, tbl, base=base, layer=layer:
                (layer, (tbl[base + step_of(i, j)] + 1) * shifts_per_block, 0)))
            operands += [job.src_t, job.src_t]
            per_tile = PROJ_TILE // SIDE_BLOCK
            block, out_shape = (None, D_MODEL, SIDE_BLOCK), (W_PROJ // PROJ_TILE, D_MODEL, PROJ_TILE)

            def omap(i, j, tbl, base=base, per_tile=per_tile):
                t = tbl[base + n_steps + step_of(i, j)]
                return (t // per_tile, 0, t % per_tile)

            def run(tbl_ref, step, ins, out, base=base):
                a_ref, b_ref = ins
                mode = tbl_ref[base + 2 * n_steps + step]

                @pl.when(mode == SIDE_SHIFTED)
                def _():
                    x = jnp.concatenate([a_ref[W_IN_SHIFT:, :], b_ref[...]], axis=0)
                    out[...] = x.astype(BF16).T

                @pl.when(mode == SIDE_COPY)
                def _():
                    out[...] = a_ref[...].astype(BF16).T

                @pl.when(mode == SIDE_LOW_RANK)
                def _():
                    row = lax.broadcasted_iota(jnp.int32, a_ref.shape, 0)
                    out[...] = jnp.where(row < GLA_GATE_RANK, a_ref[...], 0.0).astype(BF16).T

                @pl.when(mode == SIDE_ZERO)
                def _():
                    out[...] = jnp.zeros_like(out)

            runners.append((2, run))
        out_specs.append(pl.BlockSpec(block, omap))
        out_shapes.append(jax.ShapeDtypeStruct(out_shape, BF16))

    table = np.concatenate(rows) if rows else np.zeros((1,), np.int32)
    return _SidePlan(table, in_specs, operands, out_specs, out_shapes, runners, steps_per_row)


def _run_sides(plan, tbl_ref, side_ins, side_outs):
    step = pl.program_id(0) * plan.steps_per_row + pl.program_id(1)
    k = 0
    for (n_in, run), out in zip(plan.runners, side_outs):
        run(tbl_ref, step, side_ins[k:k + n_in], out)
        k += n_in


def _hosted_call(kernel, plan, grid, in_specs, out_spec, out_shape, scratch_shapes, name, operands):
    n_side_in = len(plan.in_specs)
    n_main_in = len(in_specs)

    def body(tbl_ref, *refs):
        ins = refs[:n_main_in]
        side_ins = refs[n_main_in:n_main_in + n_side_in]
        out = refs[n_main_in + n_side_in]
        side_outs = refs[n_main_in + n_side_in + 1:n_main_in + n_side_in + 1 + len(plan.out_specs)]
        scratch = refs[n_main_in + n_side_in + 1 + len(plan.out_specs):]
        kernel(*ins, out, *scratch)
        _run_sides(plan, tbl_ref, side_ins, side_outs)

    res = pl.pallas_call(
        body,
        grid_spec=pltpu.PrefetchScalarGridSpec(
            num_scalar_prefetch=1,
            grid=grid,
            in_specs=list(in_specs) + plan.in_specs,
            out_specs=[out_spec] + plan.out_specs,
            scratch_shapes=scratch_shapes,
        ),
        out_shape=[out_shape] + plan.out_shapes,
        compiler_params=_params("arbitrary", "arbitrary"),
        name=name,
    )(jnp.asarray(plan.table), *operands, *plan.operands)
    return res[0], res[1:]


def _ffn_kernel(h_ref, g_ref, wg_ref, wu_ref, wd_ref, fg_ref, o_ref, u_scr, *, final, tail):
    j = pl.program_id(1)
    last = pl.num_programs(1) - 1
    tile = wg_ref.shape[1]

    @pl.when(j == 0)
    def _():
        u_scr[...] = _rms_rows(h_ref[...], g_ref[...]).astype(BF16)
        o_ref[...] = jnp.zeros_like(o_ref)

    def accumulate(width):
        u = u_scr[...]
        a = _dot(u, wg_ref[:, :width])
        b = _dot(u, wu_ref[:, :width])
        act = (_silu(a) * b).astype(BF16)
        o_ref[...] += _dot(act, wd_ref[:width, :])

    if tail == tile:
        accumulate(tile)
    else:
        @pl.when(j < last)
        def _():
            accumulate(tile)

        @pl.when(j == last)
        def _():
            accumulate(tail)

    @pl.when(j == last)
    def _():
        y = h_ref[...] + FFN_RES * o_ref[...]
        if final:
            y = _rms_rows(y, fg_ref[...])
        o_ref[...] = y


def _ff_padded(tf):
    return pl.cdiv(D_FF, tf) * tf


def _ffn(h, gain, wg, wu, wd, final_gain, final, tiles, sides=()):
    tm, tf = tiles
    assert ROWS % tm == 0 and wg.shape == (_ff_padded(tf) // tf, D_MODEL, tf) and wd.shape[0] == _ff_padded(tf)
    grid = (ROWS // tm, wg.shape[0])
    tail = D_FF - (grid[1] - 1) * tf
    plan = _side_plan(sides, *grid)
    if final:
        assert L_PAD % tm == 0
        per_seq = L_PAD // tm
        h = h.reshape(BATCH, L_PAD, D_MODEL)
        rows_spec = pl.BlockSpec((None, tm, D_MODEL), lambda i, j, tbl: (i // per_seq, i % per_seq, 0))
        out_shape = jax.ShapeDtypeStruct((BATCH, SEQ, D_MODEL), F32)
    else:
        rows_spec = pl.BlockSpec((tm, D_MODEL), lambda i, j, tbl: (i, 0))
        out_shape = jax.ShapeDtypeStruct((ROWS, D_MODEL), F32)
    return _hosted_call(
        functools.partial(_ffn_kernel, final=final, tail=tail), plan, grid,
        in_specs=[
            rows_spec,
            pl.BlockSpec((1, D_MODEL), lambda i, j, tbl: (0, 0)),
            pl.BlockSpec((None, D_MODEL, tf), lambda i, j, tbl: (j, 0, 0)),
            pl.BlockSpec((None, D_MODEL, tf), lambda i, j, tbl: (j, 0, 0)),
            pl.BlockSpec((tf, D_MODEL), lambda i, j, tbl: (j, 0)),
            pl.BlockSpec((1, D_MODEL), lambda i, j, tbl: (0, 0)),
        ],
        out_spec=rows_spec,
        out_shape=out_shape,
        scratch_shapes=[pltpu.VMEM((tm, D_MODEL), BF16)],
        name="ffn",
        operands=(h, gain, wg, wu, wd, final_gain),
    )


def _proj_kernel(h_ref, g_ref, w_ref, o_ref, u_scr):
    @pl.when(pl.program_id(1) == 0)
    def _():
        u_scr[...] = _rms_rows(h_ref[...], g_ref[...]).astype(BF16)

    o_ref[...] = _dot(u_scr[...], w_ref[...]).astype(o_ref.dtype)


def _proj(h, gain, w, sides=()):
    grid = (ROWS // PROJ_ROW_TILE, W_PROJ // PROJ_TILE)
    plan = _side_plan(sides, *grid)
    return _hosted_call(
        _proj_kernel, plan, grid,
        in_specs=[
            pl.BlockSpec((PROJ_ROW_TILE, D_MODEL), lambda i, j, tbl: (i, 0)),
            pl.BlockSpec((1, D_MODEL), lambda i, j, tbl: (0, 0)),
            pl.BlockSpec((None, D_MODEL, PROJ_TILE), lambda i, j, tbl: (j, 0, 0)),
        ],
        out_spec=pl.BlockSpec((PROJ_ROW_TILE, PROJ_TILE), lambda i, j, tbl: (i, j)),
        out_shape=jax.ShapeDtypeStruct((ROWS, W_PROJ), BF16),
        scratch_shapes=[pltpu.VMEM((PROJ_ROW_TILE, D_MODEL), BF16)],
        name="in_proj",
        operands=(h, gain, w),
    )


def _decay_tables():
    c = CHUNK
    blocks = []
    r = np.arange(c)
    for l in range(1, N_LEVELS):
        b = 1 << l
        m = np.zeros((c, c), np.float32)
        for i in range(c):
            p = i % (2 * b)
            s = i - p + b
            if p >= b:
                m[i, s:i + 1] = 1.0
            else:
                m[i, i + 1:s] = 1.0
        blocks.append(m)
    blocks.append((r[None, :] <= r[:, None]).astype(np.float32))
    wc = np.concatenate(blocks, axis=0)
    wc = np.concatenate([wc] * 3, axis=1)

    masks = np.zeros((N_LEVELS + 1, c, c), np.float32)
    for l in range(N_LEVELS):
        b = 1 << l
        same = (r[:, None] // (2 * b)) == (r[None, :] // (2 * b))
        up = (r[:, None] & b) != 0
        lo = (r[None, :] & b) == 0
        masks[l] = (same & up & lo).astype(np.float32)
    masks[N_LEVELS] = np.eye(c, dtype=np.float32)
    return wc, masks


def _gla_core(qs, ks, vs, gs, wc_ref, mk_ref, s_ref, slot_of, n_heads, dk, dv):
    c = CHUNK
    seqs = range(len(qs))
    heads = range(n_heads)
    base = (N_LEVELS - 1) * c

    def split3(g):
        g_hi = g.astype(BF16)
        r1 = g - g_hi.astype(F32)
        g_mid = r1.astype(BF16)
        g_lo = (r1 - g_mid.astype(F32)).astype(BF16)
        return jnp.concatenate([g_hi, g_mid, g_lo], axis=0)

    sums = [_dot(wc_ref[...], split3(gs[b])) for b in seqs]
    ex = [jnp.exp(sums[b][:base]) for b in seqs]
    cum = [sums[b][base:base + c] for b in seqs]
    total = [cum[b][c - 1:c] for b in seqs]

    on_diag = [qs[b] * ks[b] for b in seqs]
    below = [qs[b] * jnp.exp(gs[b]) * pltpu.roll(ks[b], 1, axis=0) for b in seqs]
    scores = [[None] * n_heads for _ in seqs]
    for h in heads:
        lanes = slice(h * dk, (h + 1) * dk)
        for b in seqs:
            scores[b][h] = (mk_ref[N_LEVELS] * jnp.sum(on_diag[b][:, lanes], axis=-1, keepdims=True)
                            + mk_ref[0] * jnp.sum(below[b][:, lanes], axis=-1, keepdims=True))
    rows = lax.broadcasted_iota(jnp.int32, (c, 1), 0)
    for l in range(1, N_LEVELS):
        upper = (rows & (1 << l)) != 0
        x = [(jnp.where(upper, qs[b], ks[b]) * ex[b][(l - 1) * c:l * c]).astype(BF16) for b in seqs]
        for h in heads:
            for b in seqs:
                xh = x[b][:, h * dk:(h + 1) * dk]
                scores[b][h] = scores[b][h] + mk_ref[l] * _dot_nt(xh, xh)

    vb = [vs[b].astype(BF16) for b in seqs]
    qt = [(qs[b] * jnp.exp(cum[b])).astype(BF16) for b in seqs]
    kt = [(ks[b] * jnp.exp(total[b] - cum[b])).astype(BF16) for b in seqs]
    tot = [jnp.exp(total[b]) for b in seqs]
    outs = [[None] * n_heads for _ in seqs]
    for h in heads:
        for b in seqs:
            slot = slot_of(b, h)
            st = s_ref[slot]
            vh = vb[b][:, h * dv:(h + 1) * dv]
            outs[b][h] = (_dot(scores[b][h].astype(BF16), vh)
                          + _dot_nt(qt[b][:, h * dk:(h + 1) * dk], st.astype(BF16)))
            s_ref[slot] = st * tot[b][:, h * dk:(h + 1) * dk] + _dot_tn(vh, kt[b][:, h * dk:(h + 1) * dk])
    return outs


def _head_rms(outs, gain, gate):
    dv = outs[0].shape[-1]
    ys = []
    for h, o in enumerate(outs):
        ms = jnp.mean(o * o, axis=-1, keepdims=True)
        ys.append(o * lax.rsqrt(ms + EPS))
    y = jnp.concatenate(ys, axis=-1)
    return y * gain * _silu(gate)


def _log_sigmoid(z):
    return jnp.minimum(z, 0.0) - jnp.log(1.0 + jnp.exp(-jnp.abs(z)))


def _gla_kernel(q_ref, k_ref, v_ref, gate_ref, lr_ref, w2_ref, b_ref, norm_ref, wc_ref, mk_ref,
                o_ref, s_ref):
    @pl.when(pl.program_id(0) == 0)
    def _():
        s_ref[...] = jnp.zeros_like(s_ref)

    seqs = range(BATCH)
    z = [_dot(lr_ref[b], w2_ref[...]) + b_ref[...] for b in seqs]
    g = [_log_sigmoid(z[b]) * (1.0 / GLA_GATE_TAU) for b in seqs]
    q = [q_ref[b].astype(F32) * (GLA_DK ** -0.5) for b in seqs]
    k = [k_ref[b].astype(F32) for b in seqs]
    v = [v_ref[b].astype(F32) for b in seqs]
    outs = _gla_core(q, k, v, g, wc_ref, mk_ref, s_ref, lambda b, h: b * GLA_HEADS + h,
                     GLA_HEADS, GLA_DK, GLA_DV)
    for b in seqs:
        o_ref[b] = _head_rms(outs[b], norm_ref[...], gate_ref[b].astype(F32)).astype(o_ref.dtype)


def _hgrn_kernel(q_ref, f_ref, i_ref, gate_ref, lb_ref, norm_ref, wc_ref, mk_ref, o_ref, s_ref, *, layer):
    @pl.when(pl.program_id(0) == 0)
    def _():
        s_ref[...] = jnp.zeros_like(s_ref)

    logits = lb_ref[...]
    e = jnp.exp(logits - jnp.max(logits, axis=0, keepdims=True))
    soft = e / jnp.sum(e, axis=0, keepdims=True)
    lb = jnp.zeros((1, logits.shape[1]), F32)
    for l in range(1, layer + 1):
        lb = lb + soft[l:l + 1]

    seqs = range(BATCH)
    group = HGRN_HEADS // 2
    for first in range(0, HGRN_HEADS, group):
        qk = slice(first * HGRN_DK, (first + group) * HGRN_DK)
        vv = slice(first * HGRN_DV, (first + group) * HGRN_DV)
        lbg = lb[:, qk]
        hf = [f_ref[b, :, qk].astype(F32) for b in seqs]
        forget = [lbg + (1.0 - lbg) * _sigmoid(hf[b]) for b in seqs]
        g = [jnp.log(jnp.maximum(forget[b], FORGET_FLOOR)) for b in seqs]
        k = [(1.0 - lbg) * _sigmoid(-hf[b]) for b in seqs]
        v = [_silu(i_ref[b, :, vv].astype(F32)) for b in seqs]
        q = [q_ref[b, :, qk].astype(F32) for b in seqs]
        outs = _gla_core(q, k, v, g, wc_ref, mk_ref, s_ref,
                         lambda b, h, first=first: b * HGRN_HEADS + first + h, group, HGRN_DK, HGRN_DV)
        for b in seqs:
            y = _head_rms(outs[b], norm_ref[:, vv], gate_ref[b, :, vv].astype(F32))
            o_ref[b, :, vv] = y.astype(o_ref.dtype)


def _chunk_block(c):
    return (c + N_CHUNKS - 1) % N_CHUNKS


def _col_spec(width, offset):
    assert offset % width == 0
    blk = offset // width
    return pl.BlockSpec((BATCH, CHUNK, width), lambda c: (0, _chunk_block(c), blk))


def _const_spec(shape):
    nd = len(shape)
    return pl.BlockSpec(shape, lambda c: (0,) * nd)


def _mixer_out(width):
    return dict(
        out_specs=pl.BlockSpec((BATCH, CHUNK, width), lambda c: (0, _chunk_block(c), 0)),
        out_shape=jax.ShapeDtypeStruct((BATCH, L_PAD, width), BF16),
        compiler_params=_params("arbitrary"),
    )


def _gla(proj, w2, bias, norm, wc, masks):
    qk = GLA_HEADS * GLA_DK
    vw = GLA_HEADS * GLA_DV
    return pl.pallas_call(
        _gla_kernel,
        grid=(N_CHUNKS,),
        in_specs=[
            _col_spec(qk, OFF_GLA),
            _col_spec(qk, OFF_GLA + qk),
            _col_spec(vw, OFF_GLA + 2 * qk),
            _col_spec(vw, OFF_GLA + 2 * qk + vw),
            _col_spec(LANE, OFF_LR),
            _const_spec(w2.shape),
            _const_spec(bias.shape),
            _const_spec(norm.shape),
            _const_spec(wc.shape),
            _const_spec(masks.shape),
        ],
        scratch_shapes=[pltpu.VMEM((BATCH * GLA_HEADS, GLA_DV, GLA_DK), F32)],
        name="gla",
        **_mixer_out(vw),
    )(proj, proj, proj, proj, proj, w2, bias, norm, wc, masks)


def _hgrn(proj, lb_logits, norm, wc, masks, layer):
    w = HGRN_HEADS * HGRN_DK
    return pl.pallas_call(
        functools.partial(_hgrn_kernel, layer=layer),
        grid=(N_CHUNKS,),
        in_specs=[
            _col_spec(w, OFF_HGRN),
            _col_spec(w, OFF_HGRN + w),
            _col_spec(w, OFF_HGRN + 2 * w),
            _col_spec(w, OFF_HGRN + 3 * w),
            _const_spec(lb_logits.shape),
            _const_spec(norm.shape),
            _const_spec(wc.shape),
            _const_spec(masks.shape),
        ],
        scratch_shapes=[pltpu.VMEM((BATCH * HGRN_HEADS, HGRN_DV, HGRN_DK), F32)],
        name="hgrn",
        **_mixer_out(w),
    )(proj, proj, proj, proj, lb_logits, norm, wc, masks)


def _ret_kernel(q_ref, k_ref, v_ref, gate_ref, cos_ref, sin_ref, qd_ref, kd_ref, cd_ref, im_ref, norm_ref,
                o_ref, s_ref):
    @pl.when(pl.program_id(0) == 0)
    def _():
        s_ref[...] = jnp.zeros_like(s_ref)

    half = RET_DK // 2
    cos = cos_ref[...]
    sin = sin_ref[...]
    seqs = range(BATCH)
    q = [q_ref[b].astype(F32) for b in seqs]
    k = [k_ref[b].astype(F32) for b in seqs]
    ys = [[] for _ in seqs]
    for b in seqs:
        for h in range(RET_HEADS):
            lo = h * RET_DK
            q1, q2 = q[b][:, lo:lo + half], q[b][:, lo + half:lo + RET_DK]
            k1, k2 = k[b][:, lo:lo + half], k[b][:, lo + half:lo + RET_DK]
            qr = jnp.concatenate([q1 * cos - q2 * sin, q1 * sin + q2 * cos], axis=-1) * (RET_DK ** -0.5)
            kr = jnp.concatenate([k1 * cos - k2 * sin, k1 * sin + k2 * cos], axis=-1)
            vh = v_ref[b, :, h * RET_DV:(h + 1) * RET_DV]
            scores = _dot_nt(qr.astype(BF16), kr.astype(BF16)) * im_ref[h]
            st = s_ref[b * RET_HEADS + h]
            o = _dot(scores.astype(BF16), vh) + _dot_nt((qr * qd_ref[h]).astype(BF16), st.astype(BF16))
            s_ref[b * RET_HEADS + h] = st * cd_ref[h] + _dot_tn(vh, (kr * kd_ref[h]).astype(BF16))
            o = o - jnp.mean(o, axis=-1, keepdims=True)
            ms = jnp.mean(o * o, axis=-1, keepdims=True)
            ys[b].append(o * lax.rsqrt(ms + EPS))
    for b in seqs:
        y = jnp.concatenate(ys[b], axis=-1)
        o_ref[b] = (y * norm_ref[...] * _silu(gate_ref[b].astype(F32))).astype(o_ref.dtype)


def _ret(proj, cos, sin, qd, kd, cd, im, norm):
    w = RET_HEADS * RET_DK
    half = RET_DK // 2
    return pl.pallas_call(
        _ret_kernel,
        grid=(N_CHUNKS,),
        in_specs=[
            _col_spec(w, OFF_RET),
            _col_spec(w, OFF_RET + w),
            _col_spec(w, OFF_RET + 2 * w),
            _col_spec(w, OFF_RET + 3 * w),
            pl.BlockSpec((CHUNK, half), lambda c: (c, 0)),
            pl.BlockSpec((CHUNK, half), lambda c: (c, 0)),
            _const_spec(qd.shape),
            _const_spec(kd.shape),
            _const_spec(cd.shape),
            _const_spec(im.shape),
            _const_spec(norm.shape),
        ],
        scratch_shapes=[pltpu.VMEM((BATCH * RET_HEADS, RET_DV, RET_DK), F32)],
        name="retention",
        **_mixer_out(w),
    )(proj, proj, proj, proj, cos, sin, qd, kd, cd, im, norm)


def _ret_tables():
    f32 = jnp.float32
    half = RET_DK // 2
    pos = jnp.arange(L_PAD, dtype=f32) - PAD
    inv_freq = ROPE_BASE ** (-jnp.arange(half, dtype=f32) / half)
    ang = pos[:, None] * inv_freq[None, :]
    log_gamma = jnp.log(1.0 - 2.0 ** (-5.0 - jnp.arange(RET_HEADS, dtype=f32)))
    idx = jnp.arange(CHUNK, dtype=f32)
    rel = idx[:, None] - idx[None, :]
    causal = (rel >= 0)[None]
    intra = jnp.where(causal, jnp.exp(jnp.where(causal, rel[None], 0.0) * log_gamma[:, None, None]), 0.0)
    q_decay = jnp.exp((idx[None, :] + 1.0) * log_gamma[:, None])[..., None]
    k_decay = jnp.exp((CHUNK - 1.0 - idx[None, :]) * log_gamma[:, None])[..., None]
    chunk_decay = jnp.exp(CHUNK * log_gamma)[:, None, None]
    return jnp.cos(ang), jnp.sin(ang), q_decay, k_decay, chunk_decay, intra


def _merge_kernel(y0_ref, y1_ref, y2_ref, mg_ref, wb_ref, wo_ref, h_ref, o_ref):
    merged = None
    for n, y_ref in enumerate((y0_ref, y1_ref, y2_ref)):
        gate = _sigmoid(mg_ref[:, n * D_MODEL:(n + 1) * D_MODEL].astype(F32))
        t = _dot(y_ref[...], wb_ref[n * BRANCH_WIDTH:(n + 1) * BRANCH_WIDTH, :]) * gate
        merged = t if merged is None else merged + t
    o_ref[...] = h_ref[...] + _dot(merged.astype(BF16), wo_ref[...])


def _merge(ys, proj, wb, wo, h):
    assert OFF_MG == 0
    rows = lambda width: pl.BlockSpec((MERGE_TILE, width), lambda i: (i, 0))
    whole = lambda a: pl.BlockSpec(a.shape, lambda i: (0, 0))
    return pl.pallas_call(
        _merge_kernel,
        grid=(ROWS // MERGE_TILE,),
        in_specs=[rows(BRANCH_WIDTH)] * N_BRANCH + [rows(N_BRANCH * D_MODEL)]
        + [whole(wb), whole(wo), rows(D_MODEL)],
        out_specs=rows(D_MODEL),
        out_shape=jax.ShapeDtypeStruct((ROWS, D_MODEL), F32),
        compiler_params=_params("parallel"),
        name="merge",
    )(*ys, proj, wb, wo, h)


def kernel(x, meta_tokens, ffn1_norm, ffn1_w_gate, ffn1_w_up, ffn1_w_down, mix_norm, w_in, gla_w_gate2, gla_b_gate, gla_norm, ret_norm, hgrn_lb_logits, hgrn_norm, w_branch, w_out, ffn2_norm, ffn2_w_gate, ffn2_w_up, ffn2_w_down, final_norm):
    b = x.shape[0]
    meta = jnp.broadcast_to(meta_tokens[None].astype(x.dtype), (b, N_META, D_MODEL))
    h = jnp.concatenate([x, jnp.zeros((b, PAD, D_MODEL), x.dtype), meta], axis=1).reshape(ROWS, D_MODEL)

    wc_np, masks_np = _decay_tables()
    wc = jnp.asarray(wc_np, BF16)
    masks = jnp.asarray(masks_np, F32)
    cos, sin, qd, kd, cd, im = _ret_tables()

    def ffn_steps(tiles):
        return (ROWS // tiles[0]) * pl.cdiv(D_FF, tiles[1])

    def ffn_casts(wg, wu, wd, layer, tiles, host_steps):
        padded = _ff_padded(tiles[1])
        n = padded // SIDE_BLOCK
        gap = min(n, (host_steps - n) // 2)
        return [_SideCast(wg, (layer,), 1, padded, 0, tiles[1]), _SideCast(wu, (layer,), 1, padded, gap, tiles[1]),
                _SideCast(wd, (layer,), 0, padded, 2 * gap)]

    row = lambda v: v.reshape(1, -1).astype(F32)
    final_gain = row(final_norm)
    w_in_t = jnp.swapaxes(w_in, 1, 2)
    wb_rows = w_branch.reshape(DEPTH, N_BRANCH * BRANCH_WIDTH, D_MODEL)
    padded = _ff_padded(FFN1_TILES[1])
    ffn1_w = (_cast(ffn1_w_gate, (0,), 1, padded), _cast(ffn1_w_up, (0,), 1, padded),
              _cast(ffn1_w_down, (0,), 0, padded))
    for layer in range(DEPTH):
        h, (w_proj, *ffn2_w) = _ffn(
            h, row(ffn1_norm[layer]), *ffn1_w, final_gain, False, FFN1_TILES,
            sides=[_SideWIn(w_in_t, layer, 0)]
            + ffn_casts(ffn2_w_gate, ffn2_w_up, ffn2_w_down, layer,
                        FINAL_TILES if layer == DEPTH - 1 else FFN2_TILES, ffn_steps(FFN1_TILES)))
        proj, (wb, wo) = _proj(
            h, row(mix_norm[layer]), w_proj,
            sides=[_SideCast(wb_rows, (layer,), 0, N_BRANCH * BRANCH_WIDTH, 0),
                   _SideCast(w_out, (layer,), 0, D_MODEL, N_BRANCH * BRANCH_WIDTH // SIDE_BLOCK)])
        w2 = jnp.pad(gla_w_gate2[layer], ((0, LANE - GLA_GATE_RANK), (0, 0))).astype(BF16)
        proj_seq = proj.reshape(b, L_PAD, W_PROJ)
        y_gla = _gla(proj_seq, w2, row(gla_b_gate[layer]), row(gla_norm[layer]), wc, masks)
        y_ret = _ret(proj_seq, cos, sin, qd, kd, cd, im, row(ret_norm[layer]))
        y_hgrn = _hgrn(proj_seq, hgrn_lb_logits.astype(F32), row(hgrn_norm[layer]), wc, masks, layer)
        ys = [y.reshape(ROWS, BRANCH_WIDTH) for y in (y_gla, y_ret, y_hgrn)]
        h = _merge(ys, proj, wb, wo, h)
        last = layer == DEPTH - 1
        tiles = FINAL_TILES if last else FFN2_TILES
        h, ffn1_w = _ffn(
            h, row(ffn2_norm[layer]), *ffn2_w, final_gain, last, tiles,
            sides=[] if last else ffn_casts(ffn1_w_gate, ffn1_w_up, ffn1_w_down, layer + 1, FFN1_TILES,
                                            ffn_steps(tiles)))
    return h
```

```python
import functools
from typing import NamedTuple

import numpy as np
import jax
import jax.numpy as jnp
from jax import lax
from jax.experimental import pallas as pl
from jax.experimental.pallas import tpu as pltpu

D_MODEL = 2048
BATCH = 2
SEQ = 4096
DEPTH = 2
N_META = 16
CHUNK = 64
PAD = CHUNK - N_META
D_FF = 5504
FFN_RES = 0.5
EPS = 1e-6
N_BRANCH = 3
BRANCH_WIDTH = 1024

GLA_HEADS, GLA_DK, GLA_DV = 4, 128, 256
GLA_GATE_RANK = 16
GLA_GATE_TAU = 16.0
RET_HEADS, RET_DK, RET_DV = 4, 256, 256
ROPE_BASE = 10000.0
HGRN_HEADS, HGRN_DK, HGRN_DV = 8, 128, 128
FORGET_FLOOR = 1e-20

LANE = 128
L_PAD = PAD + N_META + SEQ
N_CHUNKS = L_PAD // CHUNK
ROWS = BATCH * L_PAD
N_LEVELS = 6
assert 1 << N_LEVELS == CHUNK

FFN1_TILES = (640, 512)
FFN2_TILES = (640, 512)
FINAL_TILES = (832, 512)
PROJ_ROW_TILE = 832
PROJ_TILE = 1792
MERGE_TILE = 320
CAST_TILE = 512
SIDE_BLOCK = LANE
LR_PAD = CAST_TILE
W_IN_SHIFT = GLA_GATE_RANK

OFF_MG = 0
OFF_GLA = OFF_MG + N_BRANCH * D_MODEL
OFF_RET = OFF_GLA + 2 * GLA_HEADS * GLA_DK + 2 * GLA_HEADS * GLA_DV
OFF_HGRN = OFF_RET + 2 * RET_HEADS * RET_DK + 2 * RET_HEADS * RET_DV
OFF_LR = OFF_HGRN + 2 * HGRN_HEADS * HGRN_DK + 2 * HGRN_HEADS * HGRN_DV
W_PROJ = OFF_LR + LR_PAD

VMEM_LIMIT = 56 * 1024 * 1024

F32 = jnp.float32
BF16 = jnp.bfloat16


def _params(*sem):
    return pltpu.CompilerParams(dimension_semantics=sem, vmem_limit_bytes=VMEM_LIMIT)


def _rms_rows(x, gain):
    ms = jnp.mean(x * x, axis=-1, keepdims=True)
    return x * lax.rsqrt(ms + EPS) * gain


def _sigmoid(x):
    return 1.0 / (1.0 + jnp.exp(-x))


def _silu(x):
    return x * _sigmoid(x)


def _dot(a, b):
    return jnp.dot(a, b, preferred_element_type=F32)


def _dot_nt(a, b):
    return lax.dot_general(a, b, (((1,), (1,)), ((), ())), preferred_element_type=F32)


def _dot_tn(a, b):
    return lax.dot_general(a, b, (((0,), (0,)), ((), ())), preferred_element_type=F32)


def _cast_kernel(w_ref, o_ref, *, axis, valid):
    x = w_ref[...]
    idx = pl.program_id(0) * x.shape[axis] + lax.broadcasted_iota(jnp.int32, x.shape, axis)
    o_ref[...] = jnp.where(idx < valid, x, 0.0).astype(BF16)


def _cast(w, lead, axis, padded):
    rows, cols = w.shape[-2:]
    nl = len(lead)
    if axis == 0:
        block, out_shape, grid = (CAST_TILE, cols), (padded, cols), padded // CAST_TILE
        imap = lambda j: (*lead, j, 0)
        omap = lambda j: (j, 0)
    else:
        block, out_shape, grid = (rows, CAST_TILE), (padded // CAST_TILE, rows, CAST_TILE), padded // CAST_TILE
        imap = lambda j: (*lead, 0, j)
        omap = lambda j: (j, 0, 0)
    out_block = block if axis == 0 else (None,) + block
    return pl.pallas_call(
        functools.partial(_cast_kernel, axis=axis, valid=w.shape[-2 + axis]),
        grid=(grid,),
        in_specs=[pl.BlockSpec((None,) * nl + block, imap)],
        out_specs=pl.BlockSpec(out_block, omap),
        out_shape=jax.ShapeDtypeStruct(out_shape, BF16),
        compiler_params=_params("parallel"),
        name="cast",
    )(w)


SIDE_IDLE, SIDE_COPY, SIDE_ZERO, SIDE_SHIFTED, SIDE_LOW_RANK = 0, 1, 2, 3, 4


class _SideCast(NamedTuple):
    src: jax.Array
    lead: tuple
    axis: int
    padded: int
    start: int
    col_tile: int = 0


class _SideWIn(NamedTuple):
    src_t: jax.Array
    layer: int
    start: int


class _SidePlan(NamedTuple):
    table: np.ndarray
    in_specs: list
    operands: list
    out_specs: list
    out_shapes: list
    runners: list
    steps_per_row: int


def _w_in_blocks():
    per = lambda width: width // SIDE_BLOCK
    src = lambda col: col // SIDE_BLOCK
    gla_w = 2 * GLA_HEADS * GLA_DK + 2 * GLA_HEADS * GLA_DV
    ret_w = 2 * RET_HEADS * RET_DK + 2 * RET_HEADS * RET_DV
    hgrn_w = 2 * HGRN_HEADS * HGRN_DK + 2 * HGRN_HEADS * HGRN_DV
    lr_col = gla_w
    blocks = []
    for t in range(per(N_BRANCH * D_MODEL)):
        blocks.append((src(lr_col + ret_w + hgrn_w) + t, SIDE_SHIFTED))
    for t in range(per(gla_w)):
        blocks.append((t, SIDE_COPY))
    for t in range(per(ret_w + hgrn_w)):
        blocks.append((src(lr_col) + t, SIDE_SHIFTED))
    blocks.append((src(lr_col), SIDE_LOW_RANK))
    blocks += [(src(lr_col), SIDE_ZERO)] * (per(LR_PAD) - 1)
    assert len(blocks) == per(W_PROJ)
    return [(s, t, mode) for t, (s, mode) in enumerate(blocks)]


def _side_plan(jobs, n_rows, steps_per_row):
    n_steps = n_rows * steps_per_row
    rows, in_specs, operands, out_specs, out_shapes, runners = [], [], [], [], [], []

    def add_rows(blocks, start):
        n = len(blocks)
        assert start + n <= n_steps
        arr = np.asarray(blocks, np.int32).T
        tab = np.zeros((3, n_steps), np.int32)
        tab[:2, :start] = arr[:2, :1]
        tab[:, start:start + n] = arr
        tab[:2, start + n:] = arr[:2, -1:]
        base = len(rows) * n_steps
        rows.extend(tab)
        return base

    step_of = lambda i, j: i * steps_per_row + j
    for job in jobs:
        if isinstance(job, _SideCast):
            r, c = job.src.shape[-2:]
            valid = job.src.shape[-2 + job.axis]
            assert valid % SIDE_BLOCK == 0 and job.padded % SIDE_BLOCK == 0
            n_valid, n_all = valid // SIDE_BLOCK, job.padded // SIDE_BLOCK
            base = add_rows([(min(t, n_valid - 1), t, SIDE_COPY if t < n_valid else SIDE_ZERO)
                             for t in range(n_all)], job.start)
            lead, nl = job.lead, len(job.lead)
            if job.axis == 0:
                block, out_shape = (SIDE_BLOCK, c), (job.padded, c)
                imap = lambda i, j, tbl, base=base, lead=lead: (*lead, tbl[base + step_of(i, j)], 0)
                omap = lambda i, j, tbl, base=base: (tbl[base + n_steps + step_of(i, j)], 0)
            else:
                per_tile = job.col_tile // SIDE_BLOCK
                block, out_shape = (None, r, SIDE_BLOCK), (job.padded // job.col_tile, r, job.col_tile)
                imap = lambda i, j, tbl, base=base, lead=lead: (*lead, 0, tbl[base + step_of(i, j)])

                def omap(i, j, tbl, base=base, per_tile=per_tile):
                    t = tbl[base + n_steps + step_of(i, j)]
                    return (t // per_tile, 0, t % per_tile)

            in_block = (SIDE_BLOCK, c) if job.axis == 0 else (r, SIDE_BLOCK)
            in_specs.append(pl.BlockSpec((None,) * nl + in_block, imap))
            operands.append(job.src)

            def run(tbl_ref, step, ins, out, base=base):
                mode = tbl_ref[base + 2 * n_steps + step]

                @pl.when(mode != SIDE_IDLE)
                def _():
                    out[...] = jnp.where(mode == SIDE_ZERO, 0.0, ins[0][...]).astype(BF16)

            runners.append((1, run))
        else:
            base = add_rows(_w_in_blocks(), job.start)
            layer = job.layer
            shifts_per_block = SIDE_BLOCK // W_IN_SHIFT
            in_specs.append(pl.BlockSpec(
                (None, SIDE_BLOCK, D_MODEL),
                lambda i, j, tbl, base=base, layer=layer: (layer, tbl[base + step_of(i, j)], 0)))
            in_specs.append(pl.BlockSpec(
                (None, W_IN_SHIFT, D_MODEL),
                lambda i, j, tbl, base=base, layer=layer:
                (layer, (tbl[base + step_of(i, j)] + 1) * shifts_per_block, 0)))
            operands += [job.src_t, job.src_t]
            per_tile = PROJ_TILE // SIDE_BLOCK
            block, out_shape = (None, D_MODEL, SIDE_BLOCK), (W_PROJ // PROJ_TILE, D_MODEL, PROJ_TILE)

            def omap(i, j, tbl, base=base, per_tile=per_tile):
                t = tbl[base + n_steps + step_of(i, j)]
                return (t // per_tile, 0, t % per_tile)

            def run(tbl_ref, step, ins, out, base=base):
                a_ref, b_ref = ins
                mode = tbl_ref[base + 2 * n_steps + step]

                @pl.when(mode == SIDE_SHIFTED)
                def _():
                    x = jnp.concatenate([a_ref[W_IN_SHIFT:, :], b_ref[...]], axis=0)
                    out[...] = x.astype(BF16).T

                @pl.when(mode == SIDE_COPY)
                def _():
                    out[...] = a_ref[...].astype(BF16).T

                @pl.when(mode == SIDE_LOW_RANK)
                def _():
                    row = lax.broadcasted_iota(jnp.int32, a_ref.shape, 0)
                    out[...] = jnp.where(row < GLA_GATE_RANK, a_ref[...], 0.0).astype(BF16).T

                @pl.when(mode == SIDE_ZERO)
                def _():
                    out[...] = jnp.zeros_like(out)

            runners.append((2, run))
        out_specs.append(pl.BlockSpec(block, omap))
        out_shapes.append(jax.ShapeDtypeStruct(out_shape, BF16))

    table = np.concatenate(rows) if rows else np.zeros((1,), np.int32)
    return _SidePlan(table, in_specs, operands, out_specs, out_shapes, runners, steps_per_row)


def _run_sides(plan, tbl_ref, side_ins, side_outs):
    step = pl.program_id(0) * plan.steps_per_row + pl.program_id(1)
    k = 0
    for (n_in, run), out in zip(plan.runners, side_outs):
        run(tbl_ref, step, side_ins[k:k + n_in], out)
        k += n_in


def _hosted_call(kernel, plan, grid, in_specs, out_spec, out_shape, scratch_shapes, name, operands):
    n_side_in = len(plan.in_specs)
    n_main_in = len(in_specs)

    def body(tbl_ref, *refs):
        ins = refs[:n_main_in]
        side_ins = refs[n_main_in:n_main_in + n_side_in]
        out = refs[n_main_in + n_side_in]
        side_outs = refs[n_main_in + n_side_in + 1:n_main_in + n_side_in + 1 + len(plan.out_specs)]
        scratch = refs[n_main_in + n_side_in + 1 + len(plan.out_specs):]
        kernel(*ins, out, *scratch)
        _run_sides(plan, tbl_ref, side_ins, side_outs)

    res = pl.pallas_call(
        body,
        grid_spec=pltpu.PrefetchScalarGridSpec(
            num_scalar_prefetch=1,
            grid=grid,
            in_specs=list(in_specs) + plan.in_specs,
            out_specs=[out_spec] + plan.out_specs,
            scratch_shapes=scratch_shapes,
        ),
        out_shape=[out_shape] + plan.out_shapes,
        compiler_params=_params("arbitrary", "arbitrary"),
        name=name,
    )(jnp.asarray(plan.table), *operands, *plan.operands)
    return res[0], res[1:]


def _ffn_kernel(h_ref, g_ref, wg_ref, wu_ref, wd_ref, fg_ref, o_ref, u_scr, *, final, tail):
    j = pl.program_id(1)
    last = pl.num_programs(1) - 1
    tile = wg_ref.shape[1]

    def partial_sum(u, width):
        a = _dot(u, wg_ref[:, :width])
        b = _dot(u, wu_ref[:, :width])
        act = (_silu(a) * b).astype(BF16)
        return _dot(act, wd_ref[:width, :])

    @pl.when(j == 0)
    def _():
        half = h_ref.shape[0] // 2
        for r in (slice(0, half), slice(half, 2 * half)):
            u = _rms_rows(h_ref[r, :], g_ref[...]).astype(BF16)
            u_scr[r, :] = u
            o_ref[r, :] = partial_sum(u, tile)

    @pl.when((j > 0) & (j < last))
    def _():
        o_ref[...] += partial_sum(u_scr[...], tile)

    @pl.when(j == last)
    def _():
        o_ref[...] += partial_sum(u_scr[...], tail)

    @pl.when(j == last)
    def _():
        y = h_ref[...] + FFN_RES * o_ref[...]
        if final:
            y = _rms_rows(y, fg_ref[...])
        o_ref[...] = y


def _ff_padded(tf):
    return pl.cdiv(D_FF, tf) * tf


def _ffn(h, gain, wg, wu, wd, final_gain, final, tiles, sides=()):
    tm, tf = tiles
    assert ROWS % tm == 0 and wg.shape == (_ff_padded(tf) // tf, D_MODEL, tf) and wd.shape[0] == _ff_padded(tf)
    grid = (ROWS // tm, wg.shape[0])
    tail = D_FF - (grid[1] - 1) * tf
    plan = _side_plan(sides, *grid)
    if final:
        assert L_PAD % tm == 0
        per_seq = L_PAD // tm
        h = h.reshape(BATCH, L_PAD, D_MODEL)
        rows_spec = pl.BlockSpec((None, tm, D_MODEL), lambda i, j, tbl: (i // per_seq, i % per_seq, 0))
        out_shape = jax.ShapeDtypeStruct((BATCH, SEQ, D_MODEL), F32)
    else:
        rows_spec = pl.BlockSpec((tm, D_MODEL), lambda i, j, tbl: (i, 0))
        out_shape = jax.ShapeDtypeStruct((ROWS, D_MODEL), F32)
    return _hosted_call(
        functools.partial(_ffn_kernel, final=final, tail=tail), plan, grid,
        in_specs=[
            rows_spec,
            pl.BlockSpec((1, D_MODEL), lambda i, j, tbl: (0, 0)),
            pl.BlockSpec((None, D_MODEL, tf), lambda i, j, tbl: (j, 0, 0)),
            pl.BlockSpec((None, D_MODEL, tf), lambda i, j, tbl: (j, 0, 0)),
            pl.BlockSpec((tf, D_MODEL), lambda i, j, tbl: (j, 0)),
            pl.BlockSpec((1, D_MODEL), lambda i, j, tbl: (0, 0)),
        ],
        out_spec=rows_spec,
        out_shape=out_shape,
        scratch_shapes=[pltpu.VMEM((tm, D_MODEL), BF16)],
        name="ffn",
        operands=(h, gain, wg, wu, wd, final_gain),
    )


def _proj_kernel(h_ref, g_ref, w_ref, o_ref, u_scr):
    @pl.when(pl.program_id(1) == 0)
    def _():
        u_scr[...] = _rms_rows(h_ref[...], g_ref[...]).astype(BF16)

    o_ref[...] = _dot(u_scr[...], w_ref[...]).astype(o_ref.dtype)


def _proj(h, gain, w, sides=()):
    grid = (ROWS // PROJ_ROW_TILE, W_PROJ // PROJ_TILE)
    plan = _side_plan(sides, *grid)
    return _hosted_call(
        _proj_kernel, plan, grid,
        in_specs=[
            pl.BlockSpec((PROJ_ROW_TILE, D_MODEL), lambda i, j, tbl: (i, 0)),
            pl.BlockSpec((1, D_MODEL), lambda i, j, tbl: (0, 0)),
            pl.BlockSpec((None, D_MODEL, PROJ_TILE), lambda i, j, tbl: (j, 0, 0)),
        ],
        out_spec=pl.BlockSpec((PROJ_ROW_TILE, PROJ_TILE), lambda i, j, tbl: (i, j)),
        out_shape=jax.ShapeDtypeStruct((ROWS, W_PROJ), BF16),
        scratch_shapes=[pltpu.VMEM((PROJ_ROW_TILE, D_MODEL), BF16)],
        name="in_proj",
        operands=(h, gain, w),
    )


def _decay_tables():
    c = CHUNK
    blocks = []
    r = np.arange(c)
    for l in range(1, N_LEVELS):
        b = 1 << l
        m = np.zeros((c, c), np.float32)
        for i in range(c):
            p = i % (2 * b)
            s = i - p + b
            if p >= b:
                m[i, s:i + 1] = 1.0
            else:
                m[i, i + 1:s] = 1.0
        blocks.append(m)
    blocks.append((r[None, :] <= r[:, None]).astype(np.float32))
    wc = np.concatenate(blocks, axis=0)
    wc = np.concatenate([wc] * 3, axis=1)

    masks = np.zeros((N_LEVELS + 1, c, c), np.float32)
    for l in range(N_LEVELS):
        b = 1 << l
        same = (r[:, None] // (2 * b)) == (r[None, :] // (2 * b))
        up = (r[:, None] & b) != 0
        lo = (r[None, :] & b) == 0
        masks[l] = (same & up & lo).astype(np.float32)
    masks[N_LEVELS] = np.eye(c, dtype=np.float32)
    return wc, masks


def _gla_core(qs, ks, vs, gs, wc_ref, mk_ref, s_ref, slot_of, n_heads, dk, dv):
    c = CHUNK
    seqs = range(len(qs))
    heads = range(n_heads)
    base = (N_LEVELS - 1) * c

    def split3(g):
        g_hi = g.astype(BF16)
        r1 = g - g_hi.astype(F32)
        g_mid = r1.astype(BF16)
        g_lo = (r1 - g_mid.astype(F32)).astype(BF16)
        return jnp.concatenate([g_hi, g_mid, g_lo], axis=0)

    sums = [_dot(wc_ref[...], split3(gs[b])) for b in seqs]
    ex = [jnp.exp(sums[b][:base]) for b in seqs]
    cum = [sums[b][base:base + c] for b in seqs]
    total = [cum[b][c - 1:c] for b in seqs]

    on_diag = [qs[b] * ks[b] for b in seqs]
    below = [qs[b] * jnp.exp(gs[b]) * pltpu.roll(ks[b], 1, axis=0) for b in seqs]
    scores = [[None] * n_heads for _ in seqs]
    for h in heads:
        lanes = slice(h * dk, (h + 1) * dk)
        for b in seqs:
            scores[b][h] = (mk_ref[N_LEVELS] * jnp.sum(on_diag[b][:, lanes], axis=-1, keepdims=True)
                            + mk_ref[0] * jnp.sum(below[b][:, lanes], axis=-1, keepdims=True))
    rows = lax.broadcasted_iota(jnp.int32, (c, 1), 0)
    for l in range(1, N_LEVELS):
        upper = (rows & (1 << l)) != 0
        x = [(jnp.where(upper, qs[b], ks[b]) * ex[b][(l - 1) * c:l * c]).astype(BF16) for b in seqs]
        for h in heads:
            for b in seqs:
                xh = x[b][:, h * dk:(h + 1) * dk]
                scores[b][h] = scores[b][h] + mk_ref[l] * _dot_nt(xh, xh)

    vb = [vs[b].astype(BF16) for b in seqs]
    qt = [(qs[b] * jnp.exp(cum[b])).astype(BF16) for b in seqs]
    kt = [(ks[b] * jnp.exp(total[b] - cum[b])).astype(BF16) for b in seqs]
    tot = [jnp.exp(total[b]) for b in seqs]
    outs = [[None] * n_heads for _ in seqs]
    for h in heads:
        for b in seqs:
            slot = slot_of(b, h)
            st = s_ref[slot]
            vh = vb[b][:, h * dv:(h + 1) * dv]
            outs[b][h] = (_dot(scores[b][h].astype(BF16), vh)
                          + _dot_nt(qt[b][:, h * dk:(h + 1) * dk], st.astype(BF16)))
            s_ref[slot] = st * tot[b][:, h * dk:(h + 1) * dk] + _dot_tn(vh, kt[b][:, h * dk:(h + 1) * dk])
    return outs


def _head_rms(outs, gain, gate):
    dv = outs[0].shape[-1]
    ys = []
    for h, o in enumerate(outs):
        ms = jnp.mean(o * o, axis=-1, keepdims=True)
        ys.append(o * lax.rsqrt(ms + EPS))
    y = jnp.concatenate(ys, axis=-1)
    return y * gain * _silu(gate)


def _log_sigmoid(z):
    return jnp.minimum(z, 0.0) - jnp.log(1.0 + jnp.exp(-jnp.abs(z)))


def _gla_kernel(q_ref, k_ref, v_ref, gate_ref, lr_ref, w2_ref, b_ref, norm_ref, wc_ref, mk_ref,
                o_ref, s_ref):
    @pl.when(pl.program_id(0) == 0)
    def _():
        s_ref[...] = jnp.zeros_like(s_ref)

    seqs = range(BATCH)
    z = [_dot(lr_ref[b], w2_ref[...]) + b_ref[...] for b in seqs]
    g = [_log_sigmoid(z[b]) * (1.0 / GLA_GATE_TAU) for b in seqs]
    q = [q_ref[b].astype(F32) * (GLA_DK ** -0.5) for b in seqs]
    k = [k_ref[b].astype(F32) for b in seqs]
    v = [v_ref[b].astype(F32) for b in seqs]
    outs = _gla_core(q, k, v, g, wc_ref, mk_ref, s_ref, lambda b, h: b * GLA_HEADS + h,
                     GLA_HEADS, GLA_DK, GLA_DV)
    for b in seqs:
        o_ref[b] = _head_rms(outs[b], norm_ref[...], gate_ref[b].astype(F32)).astype(o_ref.dtype)


def _hgrn_kernel(q_ref, f_ref, i_ref, gate_ref, lb_ref, norm_ref, wc_ref, mk_ref, o_ref, s_ref, *, layer):
    @pl.when(pl.program_id(0) == 0)
    def _():
        s_ref[...] = jnp.zeros_like(s_ref)

    logits = lb_ref[...]
    e = jnp.exp(logits - jnp.max(logits, axis=0, keepdims=True))
    soft = e / jnp.sum(e, axis=0, keepdims=True)
    lb = jnp.zeros((1, logits.shape[1]), F32)
    for l in range(1, layer + 1):
        lb = lb + soft[l:l + 1]

    seqs = range(BATCH)
    group = HGRN_HEADS // 2
    for first in range(0, HGRN_HEADS, group):
        qk = slice(first * HGRN_DK, (first + group) * HGRN_DK)
        vv = slice(first * HGRN_DV, (first + group) * HGRN_DV)
        lbg = lb[:, qk]
        hf = [f_ref[b, :, qk].astype(F32) for b in seqs]
        forget = [lbg + (1.0 - lbg) * _sigmoid(hf[b]) for b in seqs]
        g = [jnp.log(jnp.maximum(forget[b], FORGET_FLOOR)) for b in seqs]
        k = [1.0 - forget[b] for b in seqs]
        v = [_silu(i_ref[b, :, vv].astype(F32)) for b in seqs]
        q = [q_ref[b, :, qk].astype(F32) for b in seqs]
        outs = _gla_core(q, k, v, g, wc_ref, mk_ref, s_ref,
                         lambda b, h, first=first: b * HGRN_HEADS + first + h, group, HGRN_DK, HGRN_DV)
        for b in seqs:
            y = _head_rms(outs[b], norm_ref[:, vv], gate_ref[b, :, vv].astype(F32))
            o_ref[b, :, vv] = y.astype(o_ref.dtype)


def _chunk_block(c):
    return (c + N_CHUNKS - 1) % N_CHUNKS


def _col_spec(width, offset):
    assert offset % width == 0
    blk = offset // width
    return pl.BlockSpec((BATCH, CHUNK, width), lambda c: (0, _chunk_block(c), blk))


def _const_spec(shape):
    nd = len(shape)
    return pl.BlockSpec(shape, lambda c: (0,) * nd)


def _mixer_out(width):
    return dict(
        out_specs=pl.BlockSpec((BATCH, CHUNK, width), lambda c: (0, _chunk_block(c), 0)),
        out_shape=jax.ShapeDtypeStruct((BATCH, L_PAD, width), BF16),
        compiler_params=_params("arbitrary"),
    )


def _gla(proj, w2, bias, norm, wc, masks):
    qk = GLA_HEADS * GLA_DK
    vw = GLA_HEADS * GLA_DV
    return pl.pallas_call(
        _gla_kernel,
        grid=(N_CHUNKS,),
        in_specs=[
            _col_spec(qk, OFF_GLA),
            _col_spec(qk, OFF_GLA + qk),
            _col_spec(vw, OFF_GLA + 2 * qk),
            _col_spec(vw, OFF_GLA + 2 * qk + vw),
            _col_spec(LANE, OFF_LR),
            _const_spec(w2.shape),
            _const_spec(bias.shape),
            _const_spec(norm.shape),
            _const_spec(wc.shape),
            _const_spec(masks.shape),
        ],
        scratch_shapes=[pltpu.VMEM((BATCH * GLA_HEADS, GLA_DV, GLA_DK), F32)],
        name="gla",
        **_mixer_out(vw),
    )(proj, proj, proj, proj, proj, w2, bias, norm, wc, masks)


def _hgrn(proj, lb_logits, norm, wc, masks, layer):
    w = HGRN_HEADS * HGRN_DK
    return pl.pallas_call(
        functools.partial(_hgrn_kernel, layer=layer),
        grid=(N_CHUNKS,),
        in_specs=[
            _col_spec(w, OFF_HGRN),
            _col_spec(w, OFF_HGRN + w),
            _col_spec(w, OFF_HGRN + 2 * w),
            _col_spec(w, OFF_HGRN + 3 * w),
            _const_spec(lb_logits.shape),
            _const_spec(norm.shape),
            _const_spec(wc.shape),
            _const_spec(masks.shape),
        ],
        scratch_shapes=[pltpu.VMEM((BATCH * HGRN_HEADS, HGRN_DV, HGRN_DK), F32)],
        name="hgrn",
        **_mixer_out(w),
    )(proj, proj, proj, proj, lb_logits, norm, wc, masks)


def _ret_kernel(q_ref, k_ref, v_ref, gate_ref, cos_ref, sin_ref, qd_ref, kd_ref, cd_ref, im_ref, norm_ref,
                o_ref, s_ref):
    @pl.when(pl.program_id(0) == 0)
    def _():
        s_ref[...] = jnp.zeros_like(s_ref)

    half = RET_DK // 2
    cos = cos_ref[...]
    sin = sin_ref[...]
    seqs = range(BATCH)
    q = [q_ref[b].astype(F32) for b in seqs]
    k = [k_ref[b].astype(F32) for b in seqs]
    ys = [[] for _ in seqs]
    for b in seqs:
        for h in range(RET_HEADS):
            lo = h * RET_DK
            q1, q2 = q[b][:, lo:lo + half], q[b][:, lo + half:lo + RET_DK]
            k1, k2 = k[b][:, lo:lo + half], k[b][:, lo + half:lo + RET_DK]
            qr = jnp.concatenate([q1 * cos - q2 * sin, q1 * sin + q2 * cos], axis=-1) * (RET_DK ** -0.5)
            kr = jnp.concatenate([k1 * cos - k2 * sin, k1 * sin + k2 * cos], axis=-1)
            vh = v_ref[b, :, h * RET_DV:(h + 1) * RET_DV]
            scores = _dot_nt(qr.astype(BF16), kr.astype(BF16)) * im_ref[h]
            st = s_ref[b * RET_HEADS + h]
            o = _dot(scores.astype(BF16), vh) + _dot_nt((qr * qd_ref[h]).astype(BF16), st.astype(BF16))
            s_ref[b * RET_HEADS + h] = st * cd_ref[h] + _dot_tn(vh, (kr * kd_ref[h]).astype(BF16))
            o = o - jnp.mean(o, axis=-1, keepdims=True)
            ms = jnp.mean(o * o, axis=-1, keepdims=True)
            ys[b].append(o * lax.rsqrt(ms + EPS))
    for b in seqs:
        y = jnp.concatenate(ys[b], axis=-1)
        o_ref[b] = (y * norm_ref[...] * _silu(gate_ref[b].astype(F32))).astype(o_ref.dtype)


def _ret(proj, cos, sin, qd, kd, cd, im, norm):
    w = RET_HEADS * RET_DK
    half = RET_DK // 2
    return pl.pallas_call(
        _ret_kernel,
        grid=(N_CHUNKS,),
        in_specs=[
            _col_spec(w, OFF_RET),
            _col_spec(w, OFF_RET + w),
            _col_spec(w, OFF_RET + 2 * w),
            _col_spec(w, OFF_RET + 3 * w),
            pl.BlockSpec((CHUNK, half), lambda c: (c, 0)),
            pl.BlockSpec((CHUNK, half), lambda c: (c, 0)),
            _const_spec(qd.shape),
            _const_spec(kd.shape),
            _const_spec(cd.shape),
            _const_spec(im.shape),
            _const_spec(norm.shape),
        ],
        scratch_shapes=[pltpu.VMEM((BATCH * RET_HEADS, RET_DV, RET_DK), F32)],
        name="retention",
        **_mixer_out(w),
    )(proj, proj, proj, proj, cos, sin, qd, kd, cd, im, norm)


def _ret_tables():
    f32 = jnp.float32
    half = RET_DK // 2
    pos = jnp.arange(L_PAD, dtype=f32) - PAD
    inv_freq = ROPE_BASE ** (-jnp.arange(half, dtype=f32) / half)
    ang = pos[:, None] * inv_freq[None, :]
    log_gamma = jnp.log(1.0 - 2.0 ** (-5.0 - jnp.arange(RET_HEADS, dtype=f32)))
    idx = jnp.arange(CHUNK, dtype=f32)
    rel = idx[:, None] - idx[None, :]
    causal = (rel >= 0)[None]
    intra = jnp.where(causal, jnp.exp(jnp.where(causal, rel[None], 0.0) * log_gamma[:, None, None]), 0.0)
    q_decay = jnp.exp((idx[None, :] + 1.0) * log_gamma[:, None])[..., None]
    k_decay = jnp.exp((CHUNK - 1.0 - idx[None, :]) * log_gamma[:, None])[..., None]
    chunk_decay = jnp.exp(CHUNK * log_gamma)[:, None, None]
    return jnp.cos(ang), jnp.sin(ang), q_decay, k_decay, chunk_decay, intra


def _merge_kernel(y0_ref, y1_ref, y2_ref, mg_ref, wb_ref, wo_ref, h_ref, o_ref):
    merged = None
    for n, y_ref in enumerate((y0_ref, y1_ref, y2_ref)):
        gate = _sigmoid(mg_ref[:, n * D_MODEL:(n + 1) * D_MODEL].astype(F32))
        t = _dot(y_ref[...], wb_ref[n * BRANCH_WIDTH:(n + 1) * BRANCH_WIDTH, :]) * gate
        merged = t if merged is None else merged + t
    o_ref[...] = h_ref[...] + _dot(merged.astype(BF16), wo_ref[...])


def _merge(ys, proj, wb, wo, h):
    assert OFF_MG == 0
    rows = lambda width: pl.BlockSpec((MERGE_TILE, width), lambda i: (i, 0))
    whole = lambda a: pl.BlockSpec(a.shape, lambda i: (0, 0))
    return pl.pallas_call(
        _merge_kernel,
        grid=(ROWS // MERGE_TILE,),
        in_specs=[rows(BRANCH_WIDTH)] * N_BRANCH + [rows(N_BRANCH * D_MODEL)]
        + [whole(wb), whole(wo), rows(D_MODEL)],
        out_specs=rows(D_MODEL),
        out_shape=jax.ShapeDtypeStruct((ROWS, D_MODEL), F32),
        compiler_params=_params("parallel"),
        name="merge",
    )(*ys, proj, wb, wo, h)


def kernel(x, meta_tokens, ffn1_norm, ffn1_w_gate, ffn1_w_up, ffn1_w_down, mix_norm, w_in, gla_w_gate2, gla_b_gate, gla_norm, ret_norm, hgrn_lb_logits, hgrn_norm, w_branch, w_out, ffn2_norm, ffn2_w_gate, ffn2_w_up, ffn2_w_down, final_norm):
    b = x.shape[0]
    meta = jnp.broadcast_to(meta_tokens[None].astype(x.dtype), (b, N_META, D_MODEL))
    h = jnp.concatenate([x, jnp.zeros((b, PAD, D_MODEL), x.dtype), meta], axis=1).reshape(ROWS, D_MODEL)

    wc_np, masks_np = _decay_tables()
    wc = jnp.asarray(wc_np, BF16)
    masks = jnp.asarray(masks_np, F32)
    cos, sin, qd, kd, cd, im = _ret_tables()

    def ffn_steps(tiles):
        return (ROWS // tiles[0]) * pl.cdiv(D_FF, tiles[1])

    def ffn_casts(wg, wu, wd, layer, tiles, host_steps):
        padded = _ff_padded(tiles[1])
        n = padded // SIDE_BLOCK
        gap = min(n, (host_steps - n) // 2)
        return [_SideCast(wg, (layer,), 1, padded, 0, tiles[1]), _SideCast(wu, (layer,), 1, padded, gap, tiles[1]),
                _SideCast(wd, (layer,), 0, padded, 2 * gap)]

    row = lambda v: v.reshape(1, -1).astype(F32)
    final_gain = row(final_norm)
    w_in_t = jnp.swapaxes(w_in, 1, 2)
    wb_rows = w_branch.reshape(DEPTH, N_BRANCH * BRANCH_WIDTH, D_MODEL)
    padded = _ff_padded(FFN1_TILES[1])
    ffn1_w = (_cast(ffn1_w_gate, (0,), 1, padded), _cast(ffn1_w_up, (0,), 1, padded),
              _cast(ffn1_w_down, (0,), 0, padded))
    for layer in range(DEPTH):
        h, (w_proj, *ffn2_w) = _ffn(
            h, row(ffn1_norm[layer]), *ffn1_w, final_gain, False, FFN1_TILES,
            sides=[_SideWIn(w_in_t, layer, 0)]
            + ffn_casts(ffn2_w_gate, ffn2_w_up, ffn2_w_down, layer,
                        FINAL_TILES if layer == DEPTH - 1 else FFN2_TILES, ffn_steps(FFN1_TILES)))
        proj, (wb, wo) = _proj(
            h, row(mix_norm[layer]), w_proj,
            sides=[_SideCast(wb_rows, (layer,), 0, N_BRANCH * BRANCH_WIDTH, 0),
                   _SideCast(w_out, (layer,), 0, D_MODEL, N_BRANCH * BRANCH_WIDTH // SIDE_BLOCK)])
        w2 = jnp.pad(gla_w_gate2[layer], ((0, LANE - GLA_GATE_RANK), (0, 0))).astype(BF16)
        proj_seq = proj.reshape(b, L_PAD, W_PROJ)
        y_gla = _gla(proj_seq, w2, row(gla_b_gate[layer]), row(gla_norm[layer]), wc, masks)
        y_ret = _ret(proj_seq, cos, sin, qd, kd, cd, im, row(ret_norm[layer]))
        y_hgrn = _hgrn(proj_seq, hgrn_lb_logits.astype(F32), row(hgrn_norm[layer]), wc, masks, layer)
        ys = [y.reshape(ROWS, BRANCH_WIDTH) for y in (y_gla, y_ret, y_hgrn)]
        h = _merge(ys, proj, wb, wo, h)
        last = layer == DEPTH - 1
        tiles = FINAL_TILES if last else FFN2_TILES
        h, ffn1_w = _ffn(
            h, row(ffn2_norm[layer]), *ffn2_w, final_gain, last, tiles,
            sides=[] if last else ffn_casts(ffn1_w_gate, ffn1_w_up, ffn1_w_down, layer + 1, FFN1_TILES,
                                            ffn_steps(tiles)))
    return h
```

```python
import functools
from typing import NamedTuple

import numpy as np
import jax
import jax.numpy as jnp
from jax import lax
from jax.experimental import pallas as pl
from jax.experimental.pallas import tpu as pltpu

D_MODEL = 2048
BATCH = 2
SEQ = 4096
DEPTH = 2
N_META = 16
CHUNK = 64
PAD = CHUNK - N_META
D_FF = 5504
FFN_RES = 0.5
EPS = 1e-6
N_BRANCH = 3
BRANCH_WIDTH = 1024

GLA_HEADS, GLA_DK, GLA_DV = 4, 128, 256
GLA_GATE_RANK = 16
GLA_GATE_TAU = 16.0
RET_HEADS, RET_DK, RET_DV = 4, 256, 256
ROPE_BASE = 10000.0
HGRN_HEADS, HGRN_DK, HGRN_DV = 8, 128, 128
FORGET_FLOOR = 1e-20

LANE = 128
L_PAD = PAD + N_META + SEQ
N_CHUNKS = L_PAD // CHUNK
ROWS = BATCH * L_PAD
N_LEVELS = 6
assert 1 << N_LEVELS == CHUNK

FFN1_TILES = (640, 512)
FFN2_TILES = (640, 512)
FINAL_TILES = (832, 512)
PROJ_ROW_TILE = 832
PROJ_TILE = 1792
MERGE_TILE = 320
CAST_TILE = 512
SIDE_BLOCK = LANE
LR_PAD = CAST_TILE
W_IN_SHIFT = GLA_GATE_RANK

OFF_MG = 0
OFF_GLA = OFF_MG + N_BRANCH * D_MODEL
OFF_RET = OFF_GLA + 2 * GLA_HEADS * GLA_DK + 2 * GLA_HEADS * GLA_DV
OFF_HGRN = OFF_RET + 2 * RET_HEADS * RET_DK + 2 * RET_HEADS * RET_DV
OFF_LR = OFF_HGRN + 2 * HGRN_HEADS * HGRN_DK + 2 * HGRN_HEADS * HGRN_DV
W_PROJ = OFF_LR + LR_PAD

VMEM_LIMIT = 56 * 1024 * 1024

F32 = jnp.float32
BF16 = jnp.bfloat16


def _params(*sem):
    return pltpu.CompilerParams(dimension_semantics=sem, vmem_limit_bytes=VMEM_LIMIT)


def _rms_rows(x, gain):
    ms = jnp.mean(x * x, axis=-1, keepdims=True)
    return x * lax.rsqrt(ms + EPS) * gain


def _sigmoid(x):
    return 1.0 / (1.0 + jnp.exp(-x))


def _silu(x):
    return x * _sigmoid(x)


def _dot(a, b):
    return jnp.dot(a, b, preferred_element_type=F32)


def _dot_nt(a, b):
    return lax.dot_general(a, b, (((1,), (1,)), ((), ())), preferred_element_type=F32)


def _dot_tn(a, b):
    return lax.dot_general(a, b, (((0,), (0,)), ((), ())), preferred_element_type=F32)


def _cast_kernel(w_ref, o_ref, *, axis, valid):
    x = w_ref[...]
    idx = pl.program_id(0) * x.shape[axis] + lax.broadcasted_iota(jnp.int32, x.shape, axis)
    o_ref[...] = jnp.where(idx < valid, x, 0.0).astype(BF16)


def _cast(w, lead, axis, padded):
    rows, cols = w.shape[-2:]
    nl = len(lead)
    if axis == 0:
        block, out_shape, grid = (CAST_TILE, cols), (padded, cols), padded // CAST_TILE
        imap = lambda j: (*lead, j, 0)
        omap = lambda j: (j, 0)
    else:
        block, out_shape, grid = (rows, CAST_TILE), (padded // CAST_TILE, rows, CAST_TILE), padded // CAST_TILE
        imap = lambda j: (*lead, 0, j)
        omap = lambda j: (j, 0, 0)
    out_block = block if axis == 0 else (None,) + block
    return pl.pallas_call(
        functools.partial(_cast_kernel, axis=axis, valid=w.shape[-2 + axis]),
        grid=(grid,),
        in_specs=[pl.BlockSpec((None,) * nl + block, imap)],
        out_specs=pl.BlockSpec(out_block, omap),
        out_shape=jax.ShapeDtypeStruct(out_shape, BF16),
        compiler_params=_params("parallel"),
        name="cast",
    )(w)


SIDE_IDLE, SIDE_COPY, SIDE_ZERO, SIDE_SHIFTED, SIDE_LOW_RANK = 0, 1, 2, 3, 4


class _SideCast(NamedTuple):
    src: jax.Array
    lead: tuple
    axis: int
    padded: int
    start: int
    col_tile: int = 0


class _SideWIn(NamedTuple):
    src_t: jax.Array
    layer: int
    start: int


class _SidePlan(NamedTuple):
    table: np.ndarray
    in_specs: list
    operands: list
    out_specs: list
    out_shapes: list
    runners: list
    steps_per_row: int


def _w_in_blocks():
    per = lambda width: width // SIDE_BLOCK
    src = lambda col: col // SIDE_BLOCK
    gla_w = 2 * GLA_HEADS * GLA_DK + 2 * GLA_HEADS * GLA_DV
    ret_w = 2 * RET_HEADS * RET_DK + 2 * RET_HEADS * RET_DV
    hgrn_w = 2 * HGRN_HEADS * HGRN_DK + 2 * HGRN_HEADS * HGRN_DV
    lr_col = gla_w
    blocks = []
    for t in range(per(N_BRANCH * D_MODEL)):
        blocks.append((src(lr_col + ret_w + hgrn_w) + t, SIDE_SHIFTED))
    for t in range(per(gla_w)):
        blocks.append((t, SIDE_COPY))
    for t in range(per(ret_w + hgrn_w)):
        blocks.append((src(lr_col) + t, SIDE_SHIFTED))
    blocks.append((src(lr_col), SIDE_LOW_RANK))
    blocks += [(src(lr_col), SIDE_ZERO)] * (per(LR_PAD) - 1)
    assert len(blocks) == per(W_PROJ)
    return [(s, t, mode) for t, (s, mode) in enumerate(blocks)]


def _side_plan(jobs, n_rows, steps_per_row):
    n_steps = n_rows * steps_per_row
    rows, in_specs, operands, out_specs, out_shapes, runners = [], [], [], [], [], []

    def add_rows(blocks, start):
        n = len(blocks)
        assert start + n <= n_steps
        arr = np.asarray(blocks, np.int32).T
        tab = np.zeros((3, n_steps), np.int32)
        tab[:2, :start] = arr[:2, :1]
        tab[:, start:start + n] = arr
        tab[:2, start + n:] = arr[:2, -1:]
        base = len(rows) * n_steps
        rows.extend(tab)
        return base

    step_of = lambda i, j: i * steps_per_row + j
    for job in jobs:
        if isinstance(job, _SideCast):
            r, c = job.src.shape[-2:]
            valid = job.src.shape[-2 + job.axis]
            assert valid % SIDE_BLOCK == 0 and job.padded % SIDE_BLOCK == 0
            n_valid, n_all = valid // SIDE_BLOCK, job.padded // SIDE_BLOCK
            base = add_rows([(min(t, n_valid - 1), t, SIDE_COPY if t < n_valid else SIDE_ZERO)
                             for t in range(n_all)], job.start)
            lead, nl = job.lead, len(job.lead)
            if job.axis == 0:
                block, out_shape = (SIDE_BLOCK, c), (job.padded, c)
                imap = lambda i, j, tbl, base=base, lead=lead: (*lead, tbl[base + step_of(i, j)], 0)
                omap = lambda i, j, tbl, base=base: (tbl[base + n_steps + step_of(i, j)], 0)
            else:
                per_tile = job.col_tile // SIDE_BLOCK
                block, out_shape = (None, r, SIDE_BLOCK), (job.padded // job.col_tile, r, job.col_tile)
                imap = lambda i, j, tbl, base=base, lead=lead: (*lead, 0, tbl[base + step_of(i, j)])

                def omap(i, j, tbl, base=base, per_tile=per_tile):
                    t = tbl[base + n_steps + step_of(i, j)]
                    return (t // per_tile, 0, t % per_tile)

            in_block = (SIDE_BLOCK, c) if job.axis == 0 else (r, SIDE_BLOCK)
            in_specs.append(pl.BlockSpec((None,) * nl + in_block, imap))
            operands.append(job.src)

            def run(tbl_ref, step, ins, out, base=base):
                mode = tbl_ref[base + 2 * n_steps + step]

                @pl.when(mode != SIDE_IDLE)
                def _():
                    out[...] = jnp.where(mode == SIDE_ZERO, 0.0, ins[0][...]).astype(BF16)

            runners.append((1, run))
        else:
            base = add_rows(_w_in_blocks(), job.start)
            layer = job.layer
            shifts_per_block = SIDE_BLOCK // W_IN_SHIFT
            in_specs.append(pl.BlockSpec(
                (None, SIDE_BLOCK, D_MODEL),
                lambda i, j, tbl, base=base, layer=layer: (layer, tbl[base + step_of(i, j)], 0)))
            in_specs.append(pl.BlockSpec(
                (None, W_IN_SHIFT, D_MODEL),
                lambda i, j, tbl, base=base, layer=layer:
                (layer, (tbl[base + step_of(i, j)] + 1) * shifts_per_block, 0)))
            operands += [job.src_t, job.src_t]
            per_tile = PROJ_TILE // SIDE_BLOCK
            block, out_shape = (None, D_MODEL, SIDE_BLOCK), (W_PROJ // PROJ_TILE, D_MODEL, PROJ_TILE)

            def omap(i, j, tbl, base=base, per_tile=per_tile):
                t = tbl[base + n_steps + step_of(i, j)]
                return (t // per_tile, 0, t % per_tile)

            def run(tbl_ref, step, ins, out, base=base):
                a_ref, b_ref = ins
                mode = tbl_ref[base + 2 * n_steps + step]

                @pl.when(mode == SIDE_SHIFTED)
                def _():
                    x = jnp.concatenate([a_ref[W_IN_SHIFT:, :], b_ref[...]], axis=0)
                    out[...] = x.astype(BF16).T

                @pl.when(mode == SIDE_COPY)
                def _():
                    out[...] = a_ref[...].astype(BF16).T

                @pl.when(mode == SIDE_LOW_RANK)
                def _():
                    row = lax.broadcasted_iota(jnp.int32, a_ref.shape, 0)
                    out[...] = jnp.where(row < GLA_GATE_RANK, a_ref[...], 0.0).astype(BF16).T

                @pl.when(mode == SIDE_ZERO)
                def _():
                    out[...] = jnp.zeros_like(out)

            runners.append((2, run))
        out_specs.append(pl.BlockSpec(block, omap))
        out_shapes.append(jax.ShapeDtypeStruct(out_shape, BF16))

    table = np.concatenate(rows) if rows else np.zeros((1,), np.int32)
    return _SidePlan(table, in_specs, operands, out_specs, out_shapes, runners, steps_per_row)


def _run_sides(plan, tbl_ref, side_ins, side_outs):
    step = pl.program_id(0) * plan.steps_per_row + pl.program_id(1)
    k = 0
    for (n_in, run), out in zip(plan.runners, side_outs):
        run(tbl_ref, step, side_ins[k:k + n_in], out)
        k += n_in


def _hosted_call(kernel, plan, grid, in_specs, out_spec, out_shape, scratch_shapes, name, operands):
    n_side_in = len(plan.in_specs)
    n_main_in = len(in_specs)

    def body(tbl_ref, *refs):
        ins = refs[:n_main_in]
        side_ins = refs[n_main_in:n_main_in + n_side_in]
        out = refs[n_main_in + n_side_in]
        side_outs = refs[n_main_in + n_side_in + 1:n_main_in + n_side_in + 1 + len(plan.out_specs)]
        scratch = refs[n_main_in + n_side_in + 1 + len(plan.out_specs):]
        kernel(*ins, out, *scratch)
        _run_sides(plan, tbl_ref, side_ins, side_outs)

    res = pl.pallas_call(
        body,
        grid_spec=pltpu.PrefetchScalarGridSpec(
            num_scalar_prefetch=1,
            grid=grid,
            in_specs=list(in_specs) + plan.in_specs,
            out_specs=[out_spec] + plan.out_specs,
            scratch_shapes=scratch_shapes,
        ),
        out_shape=[out_shape] + plan.out_shapes,
        compiler_params=_params("arbitrary", "arbitrary"),
        name=name,
    )(jnp.asarray(plan.table), *operands, *plan.operands)
    return res[0], res[1:]


def _ffn_kernel(h_ref, g_ref, wg_ref, wu_ref, wd_ref, fg_ref, o_ref, u_scr, *, final, tail):
    j = pl.program_id(1)
    last = pl.num_programs(1) - 1
    tile = wg_ref.shape[1]
    half = h_ref.shape[0] // 2
    halves = (slice(0, half), slice(half, 2 * half))

    def partial_sum(u, width):
        a = _dot(u, wg_ref[:, :width])
        b = _dot(u, wu_ref[:, :width])
        act = (_silu(a) * b).astype(BF16)
        return _dot(act, wd_ref[:width, :])

    @pl.when(j == 0)
    def _():
        for r in halves:
            u = _rms_rows(h_ref[r, :], g_ref[...]).astype(BF16)
            u_scr[r, :] = u
            o_ref[r, :] = partial_sum(u, tile)

    @pl.when((j > 0) & (j < last))
    def _():
        o_ref[...] += partial_sum(u_scr[...], tile)

    @pl.when(j == last)
    def _():
        for r in (halves if final else (slice(None),)):
            y = h_ref[r, :] + FFN_RES * (o_ref[r, :] + partial_sum(u_scr[r, :], tail))
            if final:
                y = _rms_rows(y, fg_ref[...])
            o_ref[r, :] = y


def _ff_padded(tf):
    return pl.cdiv(D_FF, tf) * tf


def _ffn(h, gain, wg, wu, wd, final_gain, final, tiles, sides=()):
    tm, tf = tiles
    assert ROWS % tm == 0 and wg.shape == (_ff_padded(tf) // tf, D_MODEL, tf) and wd.shape[0] == _ff_padded(tf)
    grid = (ROWS // tm, wg.shape[0])
    tail = D_FF - (grid[1] - 1) * tf
    plan = _side_plan(sides, *grid)
    if final:
        assert L_PAD % tm == 0
        per_seq = L_PAD // tm
        h = h.reshape(BATCH, L_PAD, D_MODEL)
        rows_spec = pl.BlockSpec((None, tm, D_MODEL), lambda i, j, tbl: (i // per_seq, i % per_seq, 0))
        out_shape = jax.ShapeDtypeStruct((BATCH, SEQ, D_MODEL), F32)
    else:
        rows_spec = pl.BlockSpec((tm, D_MODEL), lambda i, j, tbl: (i, 0))
        out_shape = jax.ShapeDtypeStruct((ROWS, D_MODEL), F32)
    return _hosted_call(
        functools.partial(_ffn_kernel, final=final, tail=tail), plan, grid,
        in_specs=[
            rows_spec,
            pl.BlockSpec((1, D_MODEL), lambda i, j, tbl: (0, 0)),
            pl.BlockSpec((None, D_MODEL, tf), lambda i, j, tbl: (j, 0, 0)),
            pl.BlockSpec((None, D_MODEL, tf), lambda i, j, tbl: (j, 0, 0)),
            pl.BlockSpec((tf, D_MODEL), lambda i, j, tbl: (j, 0)),
            pl.BlockSpec((1, D_MODEL), lambda i, j, tbl: (0, 0)),
        ],
        out_spec=rows_spec,
        out_shape=out_shape,
        scratch_shapes=[pltpu.VMEM((tm, D_MODEL), BF16)],
        name="ffn",
        operands=(h, gain, wg, wu, wd, final_gain),
    )


def _proj_kernel(h_ref, g_ref, w_ref, o_ref, u_scr):
    j = pl.program_id(1)

    @pl.when(j == 0)
    def _():
        half = h_ref.shape[0] // 2
        for r in (slice(0, half), slice(half, 2 * half)):
            u = _rms_rows(h_ref[r, :], g_ref[...]).astype(BF16)
            u_scr[r, :] = u
            o_ref[r, :] = _dot(u, w_ref[...]).astype(o_ref.dtype)

    @pl.when(j > 0)
    def _():
        o_ref[...] = _dot(u_scr[...], w_ref[...]).astype(o_ref.dtype)


def _proj(h, gain, w, sides=()):
    grid = (ROWS // PROJ_ROW_TILE, W_PROJ // PROJ_TILE)
    plan = _side_plan(sides, *grid)
    return _hosted_call(
        _proj_kernel, plan, grid,
        in_specs=[
            pl.BlockSpec((PROJ_ROW_TILE, D_MODEL), lambda i, j, tbl: (i, 0)),
            pl.BlockSpec((1, D_MODEL), lambda i, j, tbl: (0, 0)),
            pl.BlockSpec((None, D_MODEL, PROJ_TILE), lambda i, j, tbl: (j, 0, 0)),
        ],
        out_spec=pl.BlockSpec((PROJ_ROW_TILE, PROJ_TILE), lambda i, j, tbl: (i, j)),
        out_shape=jax.ShapeDtypeStruct((ROWS, W_PROJ), BF16),
        scratch_shapes=[pltpu.VMEM((PROJ_ROW_TILE, D_MODEL), BF16)],
        name="in_proj",
        operands=(h, gain, w),
    )


def _decay_tables():
    c = CHUNK
    blocks = []
    r = np.arange(c)
    for l in range(1, N_LEVELS):
        b = 1 << l
        m = np.zeros((c, c), np.float32)
        for i in range(c):
            p = i % (2 * b)
            s = i - p + b
            if p >= b:
                m[i, s:i + 1] = 1.0
            else:
                m[i, i + 1:s] = 1.0
        blocks.append(m)
    blocks.append((r[None, :] <= r[:, None]).astype(np.float32))
    wc = np.concatenate(blocks, axis=0)
    wc = np.concatenate([wc] * 3, axis=1)

    masks = np.zeros((N_LEVELS + 1, c, c), np.float32)
    for l in range(N_LEVELS):
        b = 1 << l
        same = (r[:, None] // (2 * b)) == (r[None, :] // (2 * b))
        up = (r[:, None] & b) != 0
        lo = (r[None, :] & b) == 0
        masks[l] = (same & up & lo).astype(np.float32)
    masks[N_LEVELS] = np.eye(c, dtype=np.float32)
    return wc, masks


def _gla_core(qs, ks, vs, gs, wc_ref, mk_ref, s_ref, slot_of, n_heads, dk, dv):
    c = CHUNK
    seqs = range(len(qs))
    heads = range(n_heads)
    base = (N_LEVELS - 1) * c

    def split3(g):
        g_hi = g.astype(BF16)
        r1 = g - g_hi.astype(F32)
        g_mid = r1.astype(BF16)
        g_lo = (r1 - g_mid.astype(F32)).astype(BF16)
        return jnp.concatenate([g_hi, g_mid, g_lo], axis=0)

    sums = [_dot(wc_ref[...], split3(gs[b])) for b in seqs]
    ex = [jnp.exp(sums[b][:base]) for b in seqs]
    cum = [sums[b][base:base + c] for b in seqs]
    total = [cum[b][c - 1:c] for b in seqs]

    on_diag = [qs[b] * ks[b] for b in seqs]
    below = [qs[b] * jnp.exp(gs[b]) * pltpu.roll(ks[b], 1, axis=0) for b in seqs]
    scores = [[None] * n_heads for _ in seqs]
    for h in heads:
        lanes = slice(h * dk, (h + 1) * dk)
        for b in seqs:
            scores[b][h] = (mk_ref[N_LEVELS] * jnp.sum(on_diag[b][:, lanes], axis=-1, keepdims=True)
                            + mk_ref[0] * jnp.sum(below[b][:, lanes], axis=-1, keepdims=True))
    rows = lax.broadcasted_iota(jnp.int32, (c, 1), 0)
    for l in range(1, N_LEVELS):
        upper = (rows & (1 << l)) != 0
        x = [(jnp.where(upper, qs[b], ks[b]) * ex[b][(l - 1) * c:l * c]).astype(BF16) for b in seqs]
        for h in heads:
            for b in seqs:
                xh = x[b][:, h * dk:(h + 1) * dk]
                scores[b][h] = scores[b][h] + mk_ref[l] * _dot_nt(xh, xh)

    vb = [vs[b].astype(BF16) for b in seqs]
    qt = [(qs[b] * jnp.exp(cum[b])).astype(BF16) for b in seqs]
    kt = [(ks[b] * jnp.exp(total[b] - cum[b])).astype(BF16) for b in seqs]
    tot = [jnp.exp(total[b]) for b in seqs]
    outs = [[None] * n_heads for _ in seqs]
    for h in heads:
        for b in seqs:
            slot = slot_of(b, h)
            st = s_ref[slot]
            vh = vb[b][:, h * dv:(h + 1) * dv]
            outs[b][h] = (_dot(scores[b][h].astype(BF16), vh)
                          + _dot_nt(qt[b][:, h * dk:(h + 1) * dk], st.astype(BF16)))
            s_ref[slot] = st * tot[b][:, h * dk:(h + 1) * dk] + _dot_tn(vh, kt[b][:, h * dk:(h + 1) * dk])
    return outs


def _head_rms(outs, gain, gate):
    dv = outs[0].shape[-1]
    ys = []
    for h, o in enumerate(outs):
        ms = jnp.mean(o * o, axis=-1, keepdims=True)
        ys.append(o * lax.rsqrt(ms + EPS))
    y = jnp.concatenate(ys, axis=-1)
    return y * gain * _silu(gate)


def _log_sigmoid(z):
    return jnp.minimum(z, 0.0) - jnp.log(1.0 + jnp.exp(-jnp.abs(z)))


def _gla_kernel(q_ref, k_ref, v_ref, gate_ref, lr_ref, w2_ref, b_ref, norm_ref, wc_ref, mk_ref,
                o_ref, s_ref):
    @pl.when(pl.program_id(0) == 0)
    def _():
        s_ref[...] = jnp.zeros_like(s_ref)

    seqs = range(BATCH)
    z = [_dot(lr_ref[b], w2_ref[...]) + b_ref[...] for b in seqs]
    g = [_log_sigmoid(z[b]) * (1.0 / GLA_GATE_TAU) for b in seqs]
    q = [q_ref[b].astype(F32) * (GLA_DK ** -0.5) for b in seqs]
    k = [k_ref[b].astype(F32) for b in seqs]
    v = [v_ref[b].astype(F32) for b in seqs]
    outs = _gla_core(q, k, v, g, wc_ref, mk_ref, s_ref, lambda b, h: b * GLA_HEADS + h,
                     GLA_HEADS, GLA_DK, GLA_DV)
    for b in seqs:
        o_ref[b] = _head_rms(outs[b], norm_ref[...], gate_ref[b].astype(F32)).astype(o_ref.dtype)


def _hgrn_kernel(q_ref, f_ref, i_ref, gate_ref, lb_ref, norm_ref, wc_ref, mk_ref, o_ref, s_ref, *, layer):
    @pl.when(pl.program_id(0) == 0)
    def _():
        s_ref[...] = jnp.zeros_like(s_ref)

    logits = lb_ref[...]
    e = jnp.exp(logits - jnp.max(logits, axis=0, keepdims=True))
    soft = e / jnp.sum(e, axis=0, keepdims=True)
    lb = jnp.zeros((1, logits.shape[1]), F32)
    for l in range(1, layer + 1):
        lb = lb + soft[l:l + 1]

    seqs = range(BATCH)
    group = HGRN_HEADS // 2
    for first in range(0, HGRN_HEADS, group):
        qk = slice(first * HGRN_DK, (first + group) * HGRN_DK)
        vv = slice(first * HGRN_DV, (first + group) * HGRN_DV)
        lbg = lb[:, qk]
        hf = [f_ref[b, :, qk].astype(F32) for b in seqs]
        forget = [lbg + (1.0 - lbg) * _sigmoid(hf[b]) for b in seqs]
        g = [jnp.log(jnp.maximum(forget[b], FORGET_FLOOR)) for b in seqs]
        k = [1.0 - forget[b] for b in seqs]
        v = [_silu(i_ref[b, :, vv].astype(F32)) for b in seqs]
        q = [q_ref[b, :, qk].astype(F32) for b in seqs]
        outs = _gla_core(q, k, v, g, wc_ref, mk_ref, s_ref,
                         lambda b, h, first=first: b * HGRN_HEADS + first + h, group, HGRN_DK, HGRN_DV)
        for b in seqs:
            y = _head_rms(outs[b], norm_ref[:, vv], gate_ref[b, :, vv].astype(F32))
            o_ref[b, :, vv] = y.astype(o_ref.dtype)


def _chunk_block(c):
    return (c + N_CHUNKS - 1) % N_CHUNKS


def _col_spec(width, offset):
    assert offset % width == 0
    blk = offset // width
    return pl.BlockSpec((BATCH, CHUNK, width), lambda c: (0, _chunk_block(c), blk))


def _const_spec(shape):
    nd = len(shape)
    return pl.BlockSpec(shape, lambda c: (0,) * nd)


def _mixer_out(width):
    return dict(
        out_specs=pl.BlockSpec((BATCH, CHUNK, width), lambda c: (0, _chunk_block(c), 0)),
        out_shape=jax.ShapeDtypeStruct((BATCH, L_PAD, width), BF16),
        compiler_params=_params("arbitrary"),
    )


def _gla(proj, w2, bias, norm, wc, masks):
    qk = GLA_HEADS * GLA_DK
    vw = GLA_HEADS * GLA_DV
    return pl.pallas_call(
        _gla_kernel,
        grid=(N_CHUNKS,),
        in_specs=[
            _col_spec(qk, OFF_GLA),
            _col_spec(qk, OFF_GLA + qk),
            _col_spec(vw, OFF_GLA + 2 * qk),
            _col_spec(vw, OFF_GLA + 2 * qk + vw),
            _col_spec(LANE, OFF_LR),
            _const_spec(w2.shape),
            _const_spec(bias.shape),
            _const_spec(norm.shape),
            _const_spec(wc.shape),
            _const_spec(masks.shape),
        ],
        scratch_shapes=[pltpu.VMEM((BATCH * GLA_HEADS, GLA_DV, GLA_DK), F32)],
        name="gla",
        **_mixer_out(vw),
    )(proj, proj, proj, proj, proj, w2, bias, norm, wc, masks)


def _hgrn(proj, lb_logits, norm, wc, masks, layer):
    w = HGRN_HEADS * HGRN_DK
    return pl.pallas_call(
        functools.partial(_hgrn_kernel, layer=layer),
        grid=(N_CHUNKS,),
        in_specs=[
            _col_spec(w, OFF_HGRN),
            _col_spec(w, OFF_HGRN + w),
            _col_spec(w, OFF_HGRN + 2 * w),
            _col_spec(w, OFF_HGRN + 3 * w),
            _const_spec(lb_logits.shape),
            _const_spec(norm.shape),
            _const_spec(wc.shape),
            _const_spec(masks.shape),
        ],
        scratch_shapes=[pltpu.VMEM((BATCH * HGRN_HEADS, HGRN_DV, HGRN_DK), F32)],
        name="hgrn",
        **_mixer_out(w),
    )(proj, proj, proj, proj, lb_logits, norm, wc, masks)


def _ret_kernel(q_ref, k_ref, v_ref, gate_ref, cos_ref, sin_ref, qd_ref, kd_ref, cd_ref, im_ref, norm_ref,
                o_ref, s_ref):
    @pl.when(pl.program_id(0) == 0)
    def _():
        s_ref[...] = jnp.zeros_like(s_ref)

    half = RET_DK // 2
    cos = cos_ref[...]
    sin = sin_ref[...]
    seqs = range(BATCH)
    q = [q_ref[b].astype(F32) for b in seqs]
    k = [k_ref[b].astype(F32) for b in seqs]
    ys = [[] for _ in seqs]
    for b in seqs:
        for h in range(RET_HEADS):
            lo = h * RET_DK
            q1, q2 = q[b][:, lo:lo + half], q[b][:, lo + half:lo + RET_DK]
            k1, k2 = k[b][:, lo:lo + half], k[b][:, lo + half:lo + RET_DK]
            qr = jnp.concatenate([q1 * cos - q2 * sin, q1 * sin + q2 * cos], axis=-1) * (RET_DK ** -0.5)
            kr = jnp.concatenate([k1 * cos - k2 * sin, k1 * sin + k2 * cos], axis=-1)
            vh = v_ref[b, :, h * RET_DV:(h + 1) * RET_DV]
            scores = _dot_nt(qr.astype(BF16), kr.astype(BF16)) * im_ref[h]
            st = s_ref[b * RET_HEADS + h]
            o = _dot(scores.astype(BF16), vh) + _dot_nt((qr * qd_ref[h]).astype(BF16), st.astype(BF16))
            s_ref[b * RET_HEADS + h] = st * cd_ref[h] + _dot_tn(vh, (kr * kd_ref[h]).astype(BF16))
            o = o - jnp.mean(o, axis=-1, keepdims=True)
            ms = jnp.mean(o * o, axis=-1, keepdims=True)
            ys[b].append(o * lax.rsqrt(ms + EPS))
    for b in seqs:
        y = jnp.concatenate(ys[b], axis=-1)
        o_ref[b] = (y * norm_ref[...] * _silu(gate_ref[b].astype(F32))).astype(o_ref.dtype)


def _ret(proj, cos, sin, qd, kd, cd, im, norm):
    w = RET_HEADS * RET_DK
    half = RET_DK // 2
    return pl.pallas_call(
        _ret_kernel,
        grid=(N_CHUNKS,),
        in_specs=[
            _col_spec(w, OFF_RET),
            _col_spec(w, OFF_RET + w),
            _col_spec(w, OFF_RET + 2 * w),
            _col_spec(w, OFF_RET + 3 * w),
            pl.BlockSpec((CHUNK, half), lambda c: (c, 0)),
            pl.BlockSpec((CHUNK, half), lambda c: (c, 0)),
            _const_spec(qd.shape),
            _const_spec(kd.shape),
            _const_spec(cd.shape),
            _const_spec(im.shape),
            _const_spec(norm.shape),
        ],
        scratch_shapes=[pltpu.VMEM((BATCH * RET_HEADS, RET_DV, RET_DK), F32)],
        name="retention",
        **_mixer_out(w),
    )(proj, proj, proj, proj, cos, sin, qd, kd, cd, im, norm)


def _ret_tables():
    f32 = jnp.float32
    half = RET_DK // 2
    pos = jnp.arange(L_PAD, dtype=f32) - PAD
    inv_freq = ROPE_BASE ** (-jnp.arange(half, dtype=f32) / half)
    ang = pos[:, None] * inv_freq[None, :]
    log_gamma = jnp.log(1.0 - 2.0 ** (-5.0 - jnp.arange(RET_HEADS, dtype=f32)))
    idx = jnp.arange(CHUNK, dtype=f32)
    rel = idx[:, None] - idx[None, :]
    causal = (rel >= 0)[None]
    intra = jnp.where(causal, jnp.exp(jnp.where(causal, rel[None], 0.0) * log_gamma[:, None, None]), 0.0)
    q_decay = jnp.exp((idx[None, :] + 1.0) * log_gamma[:, None])[..., None]
    k_decay = jnp.exp((CHUNK - 1.0 - idx[None, :]) * log_gamma[:, None])[..., None]
    chunk_decay = jnp.exp(CHUNK * log_gamma)[:, None, None]
    return jnp.cos(ang), jnp.sin(ang), q_decay, k_decay, chunk_decay, intra


def _merge_kernel(y0_ref, y1_ref, y2_ref, mg_ref, wb_ref, wo_ref, h_ref, o_ref):
    merged = None
    for n, y_ref in enumerate((y0_ref, y1_ref, y2_ref)):
        gate = _sigmoid(mg_ref[:, n * D_MODEL:(n + 1) * D_MODEL].astype(F32))
        t = _dot(y_ref[...], wb_ref[n * BRANCH_WIDTH:(n + 1) * BRANCH_WIDTH, :]) * gate
        merged = t if merged is None else merged + t
    o_ref[...] = h_ref[...] + _dot(merged.astype(BF16), wo_ref[...])


def _merge(ys, proj, wb, wo, h):
    assert OFF_MG == 0
    rows = lambda width: pl.BlockSpec((MERGE_TILE, width), lambda i: (i, 0))
    whole = lambda a: pl.BlockSpec(a.shape, lambda i: (0, 0))
    return pl.pallas_call(
        _merge_kernel,
        grid=(ROWS // MERGE_TILE,),
        in_specs=[rows(BRANCH_WIDTH)] * N_BRANCH + [rows(N_BRANCH * D_MODEL)]
        + [whole(wb), whole(wo), rows(D_MODEL)],
        out_specs=rows(D_MODEL),
        out_shape=jax.ShapeDtypeStruct((ROWS, D_MODEL), F32),
        compiler_params=_params("parallel"),
        name="merge",
    )(*ys, proj, wb, wo, h)


def kernel(x, meta_tokens, ffn1_norm, ffn1_w_gate, ffn1_w_up, ffn1_w_down, mix_norm, w_in, gla_w_gate2, gla_b_gate, gla_norm, ret_norm, hgrn_lb_logits, hgrn_norm, w_branch, w_out, ffn2_norm, ffn2_w_gate, ffn2_w_up, ffn2_w_down, final_norm):
    b = x.shape[0]
    meta = jnp.broadcast_to(meta_tokens[None].astype(x.dtype), (b, N_META, D_MODEL))
    h = jnp.concatenate([x, jnp.zeros((b, PAD, D_MODEL), x.dtype), meta], axis=1).reshape(ROWS, D_MODEL)

    wc_np, masks_np = _decay_tables()
    wc = jnp.asarray(wc_np, BF16)
    masks = jnp.asarray(masks_np, F32)
    cos, sin, qd, kd, cd, im = _ret_tables()

    def ffn_steps(tiles):
        return (ROWS // tiles[0]) * pl.cdiv(D_FF, tiles[1])

    def ffn_casts(wg, wu, wd, layer, tiles, host_steps):
        padded = _ff_padded(tiles[1])
        n = padded // SIDE_BLOCK
        gap = min(n, (host_steps - n) // 2)
        return [_SideCast(wg, (layer,), 1, padded, 0, tiles[1]), _SideCast(wu, (layer,), 1, padded, gap, tiles[1]),
                _SideCast(wd, (layer,), 0, padded, 2 * gap)]

    row = lambda v: v.reshape(1, -1).astype(F32)
    final_gain = row(final_norm)
    w_in_t = jnp.swapaxes(w_in, 1, 2)
    wb_rows = w_branch.reshape(DEPTH, N_BRANCH * BRANCH_WIDTH, D_MODEL)
    padded = _ff_padded(FFN1_TILES[1])
    ffn1_w = (_cast(ffn1_w_gate, (0,), 1, padded), _cast(ffn1_w_up, (0,), 1, padded),
              _cast(ffn1_w_down, (0,), 0, padded))
    for layer in range(DEPTH):
        h, (w_proj, *ffn2_w) = _ffn(
            h, row(ffn1_norm[layer]), *ffn1_w, final_gain, False, FFN1_TILES,
            sides=[_SideWIn(w_in_t, layer, 0)]
            + ffn_casts(ffn2_w_gate, ffn2_w_up, ffn2_w_down, layer,
                        FINAL_TILES if layer == DEPTH - 1 else FFN2_TILES, ffn_steps(FFN1_TILES)))
        proj, (wb, wo) = _proj(
            h, row(mix_norm[layer]), w_proj,
            sides=[_SideCast(wb_rows, (layer,), 0, N_BRANCH * BRANCH_WIDTH, 0),
                   _SideCast(w_out, (layer,), 0, D_MODEL, N_BRANCH * BRANCH_WIDTH // SIDE_BLOCK)])
        w2 = jnp.pad(gla_w_gate2[layer], ((0, LANE - GLA_GATE_RANK), (0, 0))).astype(BF16)
        proj_seq = proj.reshape(b, L_PAD, W_PROJ)
        y_gla = _gla(proj_seq, w2, row(gla_b_gate[layer]), row(gla_norm[layer]), wc, masks)
        y_ret = _ret(proj_seq, cos, sin, qd, kd, cd, im, row(ret_norm[layer]))
        y_hgrn = _hgrn(proj_seq, hgrn_lb_logits.astype(F32), row(hgrn_norm[layer]), wc, masks, layer)
        ys = [y.reshape(ROWS, BRANCH_WIDTH) for y in (y_gla, y_ret, y_hgrn)]
        h = _merge(ys, proj, wb, wo, h)
        last = layer == DEPTH - 1
        tiles = FINAL_TILES if last else FFN2_TILES
        h, ffn1_w = _ffn(
            h, row(ffn2_norm[layer]), *ffn2_w, final_gain, last, tiles,
            sides=[] if last else ffn_casts(ffn1_w_gate, ffn1_w_up, ffn1_w_down, layer + 1, FFN1_TILES,
                                            ffn_steps(tiles)))
    return h
```

```python
import functools
from typing import NamedTuple

import numpy as np
import jax
import jax.numpy as jnp
from jax import lax
from jax.experimental import pallas as pl
from jax.experimental.pallas import tpu as pltpu

D_MODEL = 2048
BATCH = 2
SEQ = 4096
DEPTH = 2
N_META = 16
CHUNK = 64
PAD = CHUNK - N_META
D_FF = 5504
FFN_RES = 0.5
EPS = 1e-6
N_BRANCH = 3
BRANCH_WIDTH = 1024

GLA_HEADS, GLA_DK, GLA_DV = 4, 128, 256
GLA_GATE_RANK = 16
GLA_GATE_TAU = 16.0
RET_HEADS, RET_DK, RET_DV = 4, 256, 256
ROPE_BASE = 10000.0
HGRN_HEADS, HGRN_DK, HGRN_DV = 8, 128, 128
FORGET_FLOOR = 1e-20

LANE = 128
L_PAD = PAD + N_META + SEQ
N_CHUNKS = L_PAD // CHUNK
ROWS = BATCH * L_PAD
N_LEVELS = 6
assert 1 << N_LEVELS == CHUNK

FFN1_TILES = (640, 512)
FFN2_TILES = (640, 512)
FINAL_TILES = (832, 512)
PROJ_ROW_TILE = 832
PROJ_TILE = 1792
MERGE_TILE = 320
CAST_TILE = 512
SIDE_BLOCK = LANE
SIDE_ROWS = 32
LR_PAD = CAST_TILE
W_IN_SHIFT = GLA_GATE_RANK

OFF_MG = 0
OFF_GLA = OFF_MG + N_BRANCH * D_MODEL
OFF_RET = OFF_GLA + 2 * GLA_HEADS * GLA_DK + 2 * GLA_HEADS * GLA_DV
OFF_HGRN = OFF_RET + 2 * RET_HEADS * RET_DK + 2 * RET_HEADS * RET_DV
OFF_LR = OFF_HGRN + 2 * HGRN_HEADS * HGRN_DK + 2 * HGRN_HEADS * HGRN_DV
W_PROJ = OFF_LR + LR_PAD

VMEM_LIMIT = 56 * 1024 * 1024

F32 = jnp.float32
BF16 = jnp.bfloat16


def _params(*sem):
    return pltpu.CompilerParams(dimension_semantics=sem, vmem_limit_bytes=VMEM_LIMIT)


def _rms_rows(x, gain):
    ms = jnp.mean(x * x, axis=-1, keepdims=True)
    return x * lax.rsqrt(ms + EPS) * gain


def _sigmoid(x):
    return 1.0 / (1.0 + jnp.exp(-x))


def _silu(x):
    return x * _sigmoid(x)


def _dot(a, b):
    return jnp.dot(a, b, preferred_element_type=F32)


def _dot_nt(a, b):
    return lax.dot_general(a, b, (((1,), (1,)), ((), ())), preferred_element_type=F32)


def _dot_tn(a, b):
    return lax.dot_general(a, b, (((0,), (0,)), ((), ())), preferred_element_type=F32)


def _cast_kernel(w_ref, o_ref, *, axis, valid):
    x = w_ref[...]
    idx = pl.program_id(0) * x.shape[axis] + lax.broadcasted_iota(jnp.int32, x.shape, axis)
    o_ref[...] = jnp.where(idx < valid, x, 0.0).astype(BF16)


def _cast(w, lead, axis, padded):
    rows, cols = w.shape[-2:]
    nl = len(lead)
    if axis == 0:
        block, out_shape, grid = (CAST_TILE, cols), (padded, cols), padded // CAST_TILE
        imap = lambda j: (*lead, j, 0)
        omap = lambda j: (j, 0)
    else:
        block, out_shape, grid = (rows, CAST_TILE), (padded // CAST_TILE, rows, CAST_TILE), padded // CAST_TILE
        imap = lambda j: (*lead, 0, j)
        omap = lambda j: (j, 0, 0)
    out_block = block if axis == 0 else (None,) + block
    return pl.pallas_call(
        functools.partial(_cast_kernel, axis=axis, valid=w.shape[-2 + axis]),
        grid=(grid,),
        in_specs=[pl.BlockSpec((None,) * nl + block, imap)],
        out_specs=pl.BlockSpec(out_block, omap),
        out_shape=jax.ShapeDtypeStruct(out_shape, BF16),
        compiler_params=_params("parallel"),
        name="cast",
    )(w)


SIDE_IDLE, SIDE_COPY, SIDE_ZERO, SIDE_SHIFTED, SIDE_LOW_RANK = 0, 1, 2, 3, 4


class _SideCast(NamedTuple):
    src: jax.Array
    lead: tuple
    axis: int
    padded: int
    start: int
    col_tile: int = 0


class _SideWIn(NamedTuple):
    src_t: jax.Array
    layer: int
    start: int


class _SidePlan(NamedTuple):
    table: np.ndarray
    in_specs: list
    operands: list
    out_specs: list
    out_shapes: list
    runners: list
    steps_per_row: int


def _w_in_blocks():
    per = lambda width: width // SIDE_BLOCK
    src = lambda col: col // SIDE_BLOCK
    gla_w = 2 * GLA_HEADS * GLA_DK + 2 * GLA_HEADS * GLA_DV
    ret_w = 2 * RET_HEADS * RET_DK + 2 * RET_HEADS * RET_DV
    hgrn_w = 2 * HGRN_HEADS * HGRN_DK + 2 * HGRN_HEADS * HGRN_DV
    lr_col = gla_w
    blocks = []
    for t in range(per(N_BRANCH * D_MODEL)):
        blocks.append((src(lr_col + ret_w + hgrn_w) + t, SIDE_SHIFTED))
    for t in range(per(gla_w)):
        blocks.append((t, SIDE_COPY))
    for t in range(per(ret_w + hgrn_w)):
        blocks.append((src(lr_col) + t, SIDE_SHIFTED))
    blocks.append((src(lr_col), SIDE_LOW_RANK))
    blocks += [(src(lr_col), SIDE_ZERO)] * (per(LR_PAD) - 1)
    assert len(blocks) == per(W_PROJ)
    return [(s, t, mode) for t, (s, mode) in enumerate(blocks)]


def _side_plan(jobs, n_rows, steps_per_row):
    n_steps = n_rows * steps_per_row
    rows, in_specs, operands, out_specs, out_shapes, runners = [], [], [], [], [], []

    def add_rows(blocks, start):
        n = len(blocks)
        assert start + n <= n_steps
        arr = np.asarray(blocks, np.int32).T
        tab = np.zeros((3, n_steps), np.int32)
        tab[:2, :start] = arr[:2, :1]
        tab[:, start:start + n] = arr
        tab[:2, start + n:] = arr[:2, -1:]
        base = len(rows) * n_steps
        rows.extend(tab)
        return base

    step_of = lambda i, j: i * steps_per_row + j
    for job in jobs:
        if isinstance(job, _SideCast):
            r, c = job.src.shape[-2:]
            lead, nl = job.lead, len(job.lead)
            if job.axis == 0:
                assert r % SIDE_BLOCK == 0 and job.padded % SIDE_BLOCK == 0
                n_valid, n_all = r // SIDE_BLOCK, job.padded // SIDE_BLOCK
                base = add_rows([(min(t, n_valid - 1), t, SIDE_COPY if t < n_valid else SIDE_ZERO)
                                 for t in range(n_all)], job.start)
                in_block = block = (SIDE_BLOCK, c)
                out_shape = (job.padded, c)
                omap = lambda i, j, tbl, base=base: (tbl[base + n_steps + step_of(i, j)], 0)

                def run(tbl_ref, step, ins, out, base=base):
                    mode = tbl_ref[base + 2 * n_steps + step]

                    @pl.when(mode != SIDE_IDLE)
                    def _():
                        out[...] = jnp.where(mode == SIDE_ZERO, 0.0, ins[0][...]).astype(BF16)
            else:
                assert r % SIDE_ROWS == 0 and c % LANE == 0 and job.padded % job.col_tile == 0
                n_tiles, tile = job.padded // job.col_tile, job.col_tile
                base = add_rows([(t, t, SIDE_COPY) for t in range(r // SIDE_ROWS)], job.start)
                in_block = (SIDE_ROWS, c)
                block, out_shape = (n_tiles, SIDE_ROWS, tile), (n_tiles, r, tile)
                omap = lambda i, j, tbl, base=base: (0, tbl[base + n_steps + step_of(i, j)], 0)

                def run(tbl_ref, step, ins, out, base=base, c=c, n_tiles=n_tiles, tile=tile):
                    mode = tbl_ref[base + 2 * n_steps + step]

                    @pl.when(mode != SIDE_IDLE)
                    def _():
                        for kk in range(n_tiles):
                            width = min(tile, c - kk * tile)
                            assert width > 0
                            piece = ins[0][:, kk * tile:kk * tile + width].astype(BF16)
                            if width < tile:
                                piece = jnp.concatenate(
                                    [piece, jnp.zeros((SIDE_ROWS, tile - width), BF16)], axis=1)
                            out[kk] = piece

            imap = lambda i, j, tbl, base=base, lead=lead: (*lead, tbl[base + step_of(i, j)], 0)
            in_specs.append(pl.BlockSpec((None,) * nl + in_block, imap))
            operands.append(job.src)
            runners.append((1, run))
        else:
            base = add_rows(_w_in_blocks(), job.start)
            layer = job.layer
            shifts_per_block = SIDE_BLOCK // W_IN_SHIFT
            in_specs.append(pl.BlockSpec(
                (None, SIDE_BLOCK, D_MODEL),
                lambda i, j, tbl, base=base, layer=layer: (layer, tbl[base + step_of(i, j)], 0)))
            in_specs.append(pl.BlockSpec(
                (None, W_IN_SHIFT, D_MODEL),
                lambda i, j, tbl, base=base, layer=layer:
                (layer, (tbl[base + step_of(i, j)] + 1) * shifts_per_block, 0)))
            operands += [job.src_t, job.src_t]
            per_tile = PROJ_TILE // SIDE_BLOCK
            block, out_shape = (None, D_MODEL, SIDE_BLOCK), (W_PROJ // PROJ_TILE, D_MODEL, PROJ_TILE)

            def omap(i, j, tbl, base=base, per_tile=per_tile):
                t = tbl[base + n_steps + step_of(i, j)]
                return (t // per_tile, 0, t % per_tile)

            def run(tbl_ref, step, ins, out, base=base):
                a_ref, b_ref = ins
                mode = tbl_ref[base + 2 * n_steps + step]

                @pl.when(mode == SIDE_SHIFTED)
                def _():
                    x = jnp.concatenate([a_ref[W_IN_SHIFT:, :], b_ref[...]], axis=0)
                    out[...] = x.astype(BF16).T

                @pl.when(mode == SIDE_COPY)
                def _():
                    out[...] = a_ref[...].astype(BF16).T

                @pl.when(mode == SIDE_LOW_RANK)
                def _():
                    row = lax.broadcasted_iota(jnp.int32, a_ref.shape, 0)
                    out[...] = jnp.where(row < GLA_GATE_RANK, a_ref[...], 0.0).astype(BF16).T

                @pl.when(mode == SIDE_ZERO)
                def _():
                    out[...] = jnp.zeros_like(out)

            runners.append((2, run))
        out_specs.append(pl.BlockSpec(block, omap))
        out_shapes.append(jax.ShapeDtypeStruct(out_shape, BF16))

    table = np.concatenate(rows) if rows else np.zeros((1,), np.int32)
    return _SidePlan(table, in_specs, operands, out_specs, out_shapes, runners, steps_per_row)


def _run_sides(plan, tbl_ref, side_ins, side_outs):
    step = pl.program_id(0) * plan.steps_per_row + pl.program_id(1)
    k = 0
    for (n_in, run), out in zip(plan.runners, side_outs):
        run(tbl_ref, step, side_ins[k:k + n_in], out)
        k += n_in


def _hosted_call(kernel, plan, grid, in_specs, out_spec, out_shape, scratch_shapes, name, operands):
    n_side_in = len(plan.in_specs)
    n_main_in = len(in_specs)

    def body(tbl_ref, *refs):
        ins = refs[:n_main_in]
        side_ins = refs[n_main_in:n_main_in + n_side_in]
        out = refs[n_main_in + n_side_in]
        side_outs = refs[n_main_in + n_side_in + 1:n_main_in + n_side_in + 1 + len(plan.out_specs)]
        scratch = refs[n_main_in + n_side_in + 1 + len(plan.out_specs):]
        kernel(*ins, out, *scratch)
        _run_sides(plan, tbl_ref, side_ins, side_outs)

    res = pl.pallas_call(
        body,
        grid_spec=pltpu.PrefetchScalarGridSpec(
            num_scalar_prefetch=1,
            grid=grid,
            in_specs=list(in_specs) + plan.in_specs,
            out_specs=[out_spec] + plan.out_specs,
            scratch_shapes=scratch_shapes,
        ),
        out_shape=[out_shape] + plan.out_shapes,
        compiler_params=_params("arbitrary", "arbitrary"),
        name=name,
    )(jnp.asarray(plan.table), *operands, *plan.operands)
    return res[0], res[1:]


def _ffn_kernel(h_ref, g_ref, wg_ref, wu_ref, wd_ref, fg_ref, o_ref, u_scr, *, final, tail):
    j = pl.program_id(1)
    last = pl.num_programs(1) - 1
    tile = wg_ref.shape[1]
    half = h_ref.shape[0] // 2
    halves = (slice(0, half), slice(half, 2 * half))

    def partial_sum(u, width):
        a = _dot(u, wg_ref[:, :width])
        b = _dot(u, wu_ref[:, :width])
        act = (_silu(a) * b).astype(BF16)
        return _dot(act, wd_ref[:width, :])

    @pl.when(j == 0)
    def _():
        for r in halves:
            u = _rms_rows(h_ref[r, :], g_ref[...]).astype(BF16)
            u_scr[r, :] = u
            o_ref[r, :] = partial_sum(u, tile)

    @pl.when((j > 0) & (j < last))
    def _():
        o_ref[...] += partial_sum(u_scr[...], tile)

    @pl.when(j == last)
    def _():
        for r in (halves if final else (slice(None),)):
            y = h_ref[r, :] + FFN_RES * (o_ref[r, :] + partial_sum(u_scr[r, :], tail))
            if final:
                y = _rms_rows(y, fg_ref[...])
            o_ref[r, :] = y


def _ff_padded(tf):
    return pl.cdiv(D_FF, tf) * tf


def _ffn(h, gain, wg, wu, wd, final_gain, final, tiles, sides=()):
    tm, tf = tiles
    assert ROWS % tm == 0 and wg.shape == (_ff_padded(tf) // tf, D_MODEL, tf) and wd.shape[0] == _ff_padded(tf)
    grid = (ROWS // tm, wg.shape[0])
    tail = D_FF - (grid[1] - 1) * tf
    plan = _side_plan(sides, *grid)
    if final:
        assert L_PAD % tm == 0
        per_seq = L_PAD // tm
        h = h.reshape(BATCH, L_PAD, D_MODEL)
        rows_spec = pl.BlockSpec((None, tm, D_MODEL), lambda i, j, tbl: (i // per_seq, i % per_seq, 0))
        out_shape = jax.ShapeDtypeStruct((BATCH, SEQ, D_MODEL), F32)
    else:
        rows_spec = pl.BlockSpec((tm, D_MODEL), lambda i, j, tbl: (i, 0))
        out_shape = jax.ShapeDtypeStruct((ROWS, D_MODEL), F32)
    return _hosted_call(
        functools.partial(_ffn_kernel, final=final, tail=tail), plan, grid,
        in_specs=[
            rows_spec,
            pl.BlockSpec((1, D_MODEL), lambda i, j, tbl: (0, 0)),
            pl.BlockSpec((None, D_MODEL, tf), lambda i, j, tbl: (j, 0, 0)),
            pl.BlockSpec((None, D_MODEL, tf), lambda i, j, tbl: (j, 0, 0)),
            pl.BlockSpec((tf, D_MODEL), lambda i, j, tbl: (j, 0)),
            pl.BlockSpec((1, D_MODEL), lambda i, j, tbl: (0, 0)),
        ],
        out_spec=rows_spec,
        out_shape=out_shape,
        scratch_shapes=[pltpu.VMEM((tm, D_MODEL), BF16)],
        name="ffn",
        operands=(h, gain, wg, wu, wd, final_gain),
    )


def _proj_kernel(h_ref, g_ref, w_ref, o_ref, u_scr):
    j = pl.program_id(1)

    @pl.when(j == 0)
    def _():
        half = h_ref.shape[0] // 2
        for r in (slice(0, half), slice(half, 2 * half)):
            u = _rms_rows(h_ref[r, :], g_ref[...]).astype(BF16)
            u_scr[r, :] = u
            o_ref[r, :] = _dot(u, w_ref[...]).astype(o_ref.dtype)

    @pl.when(j > 0)
    def _():
        o_ref[...] = _dot(u_scr[...], w_ref[...]).astype(o_ref.dtype)


def _proj(h, gain, w, sides=()):
    grid = (ROWS // PROJ_ROW_TILE, W_PROJ // PROJ_TILE)
    plan = _side_plan(sides, *grid)
    return _hosted_call(
        _proj_kernel, plan, grid,
        in_specs=[
            pl.BlockSpec((PROJ_ROW_TILE, D_MODEL), lambda i, j, tbl: (i, 0)),
            pl.BlockSpec((1, D_MODEL), lambda i, j, tbl: (0, 0)),
            pl.BlockSpec((None, D_MODEL, PROJ_TILE), lambda i, j, tbl: (j, 0, 0)),
        ],
        out_spec=pl.BlockSpec((PROJ_ROW_TILE, PROJ_TILE), lambda i, j, tbl: (i, j)),
        out_shape=jax.ShapeDtypeStruct((ROWS, W_PROJ), BF16),
        scratch_shapes=[pltpu.VMEM((PROJ_ROW_TILE, D_MODEL), BF16)],
        name="in_proj",
        operands=(h, gain, w),
    )


def _decay_tables():
    c = CHUNK
    blocks = []
    r = np.arange(c)
    for l in range(1, N_LEVELS):
        b = 1 << l
        m = np.zeros((c, c), np.float32)
        for i in range(c):
            p = i % (2 * b)
            s = i - p + b
            if p >= b:
                m[i, s:i + 1] = 1.0
            else:
                m[i, i + 1:s] = 1.0
        blocks.append(m)
    blocks.append((r[None, :] <= r[:, None]).astype(np.float32))
    wc = np.concatenate(blocks, axis=0)
    wc = np.concatenate([wc] * 3, axis=1)

    masks = np.zeros((N_LEVELS + 1, c, c), np.float32)
    for l in range(N_LEVELS):
        b = 1 << l
        same = (r[:, None] // (2 * b)) == (r[None, :] // (2 * b))
        up = (r[:, None] & b) != 0
        lo = (r[None, :] & b) == 0
        masks[l] = (same & up & lo).astype(np.float32)
    masks[N_LEVELS] = np.eye(c, dtype=np.float32)
    return wc, masks


def _gla_core(qs, ks, vs, gs, wc_ref, mk_ref, s_ref, slot_of, n_heads, dk, dv):
    c = CHUNK
    seqs = range(len(qs))
    heads = range(n_heads)
    base = (N_LEVELS - 1) * c

    def split3(g):
        g_hi = g.astype(BF16)
        r1 = g - g_hi.astype(F32)
        g_mid = r1.astype(BF16)
        g_lo = (r1 - g_mid.astype(F32)).astype(BF16)
        return jnp.concatenate([g_hi, g_mid, g_lo], axis=0)

    sums = [_dot(wc_ref[...], split3(gs[b])) for b in seqs]
    ex = [jnp.exp(sums[b][:base]) for b in seqs]
    cum = [sums[b][base:base + c] for b in seqs]
    total = [cum[b][c - 1:c] for b in seqs]

    on_diag = [qs[b] * ks[b] for b in seqs]
    below = [qs[b] * jnp.exp(gs[b]) * pltpu.roll(ks[b], 1, axis=0) for b in seqs]
    scores = [[None] * n_heads for _ in seqs]
    for h in heads:
        lanes = slice(h * dk, (h + 1) * dk)
        for b in seqs:
            scores[b][h] = (mk_ref[N_LEVELS] * jnp.sum(on_diag[b][:, lanes], axis=-1, keepdims=True)
                            + mk_ref[0] * jnp.sum(below[b][:, lanes], axis=-1, keepdims=True))
    rows = lax.broadcasted_iota(jnp.int32, (c, 1), 0)
    for l in range(1, N_LEVELS):
        upper = (rows & (1 << l)) != 0
        x = [(jnp.where(upper, qs[b], ks[b]) * ex[b][(l - 1) * c:l * c]).astype(BF16) for b in seqs]
        for h in heads:
            for b in seqs:
                xh = x[b][:, h * dk:(h + 1) * dk]
                scores[b][h] = scores[b][h] + mk_ref[l] * _dot_nt(xh, xh)

    vb = [vs[b].astype(BF16) for b in seqs]
    qt = [(qs[b] * jnp.exp(cum[b])).astype(BF16) for b in seqs]
    kt = [(ks[b] * jnp.exp(total[b] - cum[b])).astype(BF16) for b in seqs]
    tot = [jnp.exp(total[b]) for b in seqs]
    outs = [[None] * n_heads for _ in seqs]
    for h in heads:
        for b in seqs:
            slot = slot_of(b, h)
            st = s_ref[slot]
            vh = vb[b][:, h * dv:(h + 1) * dv]
            outs[b][h] = (_dot(scores[b][h].astype(BF16), vh)
                          + _dot_nt(qt[b][:, h * dk:(h + 1) * dk], st.astype(BF16)))
            s_ref[slot] = st * tot[b][:, h * dk:(h + 1) * dk] + _dot_tn(vh, kt[b][:, h * dk:(h + 1) * dk])
    return outs


def _head_rms(outs, gain, gate):
    dv = outs[0].shape[-1]
    ys = []
    for h, o in enumerate(outs):
        ms = jnp.mean(o * o, axis=-1, keepdims=True)
        ys.append(o * lax.rsqrt(ms + EPS))
    y = jnp.concatenate(ys, axis=-1)
    return y * gain * _silu(gate)


def _log_sigmoid(z):
    return jnp.minimum(z, 0.0) - jnp.log(1.0 + jnp.exp(-jnp.abs(z)))


def _gla_kernel(q_ref, k_ref, v_ref, gate_ref, lr_ref, w2_ref, b_ref, norm_ref, wc_ref, mk_ref,
                o_ref, s_ref):
    @pl.when(pl.program_id(0) == 0)
    def _():
        s_ref[...] = jnp.zeros_like(s_ref)

    seqs = range(BATCH)
    z = [_dot(lr_ref[b], w2_ref[...]) + b_ref[...] for b in seqs]
    g = [_log_sigmoid(z[b]) * (1.0 / GLA_GATE_TAU) for b in seqs]
    q = [q_ref[b].astype(F32) * (GLA_DK ** -0.5) for b in seqs]
    k = [k_ref[b].astype(F32) for b in seqs]
    v = [v_ref[b].astype(F32) for b in seqs]
    outs = _gla_core(q, k, v, g, wc_ref, mk_ref, s_ref, lambda b, h: b * GLA_HEADS + h,
                     GLA_HEADS, GLA_DK, GLA_DV)
    for b in seqs:
        o_ref[b] = _head_rms(outs[b], norm_ref[...], gate_ref[b].astype(F32)).astype(o_ref.dtype)


def _hgrn_kernel(q_ref, f_ref, i_ref, gate_ref, lb_ref, norm_ref, wc_ref, mk_ref, o_ref, s_ref, *, layer):
    @pl.when(pl.program_id(0) == 0)
    def _():
        s_ref[...] = jnp.zeros_like(s_ref)

    logits = lb_ref[...]
    e = jnp.exp(logits - jnp.max(logits, axis=0, keepdims=True))
    soft = e / jnp.sum(e, axis=0, keepdims=True)
    lb = jnp.zeros((1, logits.shape[1]), F32)
    for l in range(1, layer + 1):
        lb = lb + soft[l:l + 1]

    seqs = range(BATCH)
    group = HGRN_HEADS // 2
    for first in range(0, HGRN_HEADS, group):
        qk = slice(first * HGRN_DK, (first + group) * HGRN_DK)
        vv = slice(first * HGRN_DV, (first + group) * HGRN_DV)
        lbg = lb[:, qk]
        hf = [f_ref[b, :, qk].astype(F32) for b in seqs]
        forget = [lbg + (1.0 - lbg) * _sigmoid(hf[b]) for b in seqs]
        g = [jnp.log(jnp.maximum(forget[b], FORGET_FLOOR)) for b in seqs]
        k = [1.0 - forget[b] for b in seqs]
        v = [_silu(i_ref[b, :, vv].astype(F32)) for b in seqs]
        q = [q_ref[b, :, qk].astype(F32) for b in seqs]
        outs = _gla_core(q, k, v, g, wc_ref, mk_ref, s_ref,
                         lambda b, h, first=first: b * HGRN_HEADS + first + h, group, HGRN_DK, HGRN_DV)
        for b in seqs:
            y = _head_rms(outs[b], norm_ref[:, vv], gate_ref[b, :, vv].astype(F32))
            o_ref[b, :, vv] = y.astype(o_ref.dtype)


def _chunk_block(c):
    return (c + N_CHUNKS - 1) % N_CHUNKS


def _col_spec(width, offset):
    assert offset % width == 0
    blk = offset // width
    return pl.BlockSpec((BATCH, CHUNK, width), lambda c: (0, _chunk_block(c), blk))


def _const_spec(shape):
    nd = len(shape)
    return pl.BlockSpec(shape, lambda c: (0,) * nd)


def _mixer_out(width):
    return dict(
        out_specs=pl.BlockSpec((BATCH, CHUNK, width), lambda c: (0, _chunk_block(c), 0)),
        out_shape=jax.ShapeDtypeStruct((BATCH, L_PAD, width), BF16),
        compiler_params=_params("arbitrary"),
    )


def _gla(proj, w2, bias, norm, wc, masks):
    qk = GLA_HEADS * GLA_DK
    vw = GLA_HEADS * GLA_DV
    return pl.pallas_call(
        _gla_kernel,
        grid=(N_CHUNKS,),
        in_specs=[
            _col_spec(qk, OFF_GLA),
            _col_spec(qk, OFF_GLA + qk),
            _col_spec(vw, OFF_GLA + 2 * qk),
            _col_spec(vw, OFF_GLA + 2 * qk + vw),
            _col_spec(LANE, OFF_LR),
            _const_spec(w2.shape),
            _const_spec(bias.shape),
            _const_spec(norm.shape),
            _const_spec(wc.shape),
            _const_spec(masks.shape),
        ],
        scratch_shapes=[pltpu.VMEM((BATCH * GLA_HEADS, GLA_DV, GLA_DK), F32)],
        name="gla",
        **_mixer_out(vw),
    )(proj, proj, proj, proj, proj, w2, bias, norm, wc, masks)


def _hgrn(proj, lb_logits, norm, wc, masks, layer):
    w = HGRN_HEADS * HGRN_DK
    return pl.pallas_call(
        functools.partial(_hgrn_kernel, layer=layer),
        grid=(N_CHUNKS,),
        in_specs=[
            _col_spec(w, OFF_HGRN),
            _col_spec(w, OFF_HGRN + w),
            _col_spec(w, OFF_HGRN + 2 * w),
            _col_spec(w, OFF_HGRN + 3 * w),
            _const_spec(lb_logits.shape),
            _const_spec(norm.shape),
            _const_spec(wc.shape),
            _const_spec(masks.shape),
        ],
        scratch_shapes=[pltpu.VMEM((BATCH * HGRN_HEADS, HGRN_DV, HGRN_DK), F32)],
        name="hgrn",
        **_mixer_out(w),
    )(proj, proj, proj, proj, lb_logits, norm, wc, masks)


def _ret_kernel(q_ref, k_ref, v_ref, gate_ref, cos_ref, sin_ref, qd_ref, kd_ref, cd_ref, im_ref, norm_ref,
                o_ref, s_ref):
    @pl.when(pl.program_id(0) == 0)
    def _():
        s_ref[...] = jnp.zeros_like(s_ref)

    half = RET_DK // 2
    cos = cos_ref[...]
    sin = sin_ref[...]
    seqs = range(BATCH)
    pairs = [(b, h) for h in range(RET_HEADS) for b in seqs]

    def rotate(ref, b, h):
        lo = h * RET_DK
        t1 = ref[b, :, lo:lo + half].astype(F32)
        t2 = ref[b, :, lo + half:lo + RET_DK].astype(F32)
        return jnp.concatenate([t1 * cos - t2 * sin, t1 * sin + t2 * cos], axis=-1)

    qr = {p: rotate(q_ref, *p) * (RET_DK ** -0.5) for p in pairs}
    kr = {p: rotate(k_ref, *p) for p in pairs}
    scores = {p: _dot_nt(qr[p].astype(BF16), kr[p].astype(BF16)) * im_ref[p[1]] for p in pairs}
    outs = {}
    for b, h in pairs:
        p = (b, h)
        vh = v_ref[b, :, h * RET_DV:(h + 1) * RET_DV]
        st = s_ref[b * RET_HEADS + h]
        outs[p] = (_dot(scores[p].astype(BF16), vh)
                   + _dot_nt((qr[p] * qd_ref[h]).astype(BF16), st.astype(BF16)))
        s_ref[b * RET_HEADS + h] = st * cd_ref[h] + _dot_tn(vh, (kr[p] * kd_ref[h]).astype(BF16))
    for b in seqs:
        ys = []
        for h in range(RET_HEADS):
            o = outs[(b, h)]
            o = o - jnp.mean(o, axis=-1, keepdims=True)
            ms = jnp.mean(o * o, axis=-1, keepdims=True)
            ys.append(o * lax.rsqrt(ms + EPS))
        y = jnp.concatenate(ys, axis=-1)
        o_ref[b] = (y * norm_ref[...] * _silu(gate_ref[b].astype(F32))).astype(o_ref.dtype)


def _ret(proj, cos, sin, qd, kd, cd, im, norm):
    w = RET_HEADS * RET_DK
    half = RET_DK // 2
    return pl.pallas_call(
        _ret_kernel,
        grid=(N_CHUNKS,),
        in_specs=[
            _col_spec(w, OFF_RET),
            _col_spec(w, OFF_RET + w),
            _col_spec(w, OFF_RET + 2 * w),
            _col_spec(w, OFF_RET + 3 * w),
            pl.BlockSpec((CHUNK, half), lambda c: (c, 0)),
            pl.BlockSpec((CHUNK, half), lambda c: (c, 0)),
            _const_spec(qd.shape),
            _const_spec(kd.shape),
            _const_spec(cd.shape),
            _const_spec(im.shape),
            _const_spec(norm.shape),
        ],
        scratch_shapes=[pltpu.VMEM((BATCH * RET_HEADS, RET_DV, RET_DK), F32)],
        name="retention",
        **_mixer_out(w),
    )(proj, proj, proj, proj, cos, sin, qd, kd, cd, im, norm)


def _ret_tables():
    f32 = jnp.float32
    half = RET_DK // 2
    pos = jnp.arange(L_PAD, dtype=f32) - PAD
    inv_freq = ROPE_BASE ** (-jnp.arange(half, dtype=f32) / half)
    ang = pos[:, None] * inv_freq[None, :]
    log_gamma = jnp.log(1.0 - 2.0 ** (-5.0 - jnp.arange(RET_HEADS, dtype=f32)))
    idx = jnp.arange(CHUNK, dtype=f32)
    rel = idx[:, None] - idx[None, :]
    causal = (rel >= 0)[None]
    intra = jnp.where(causal, jnp.exp(jnp.where(causal, rel[None], 0.0) * log_gamma[:, None, None]), 0.0)
    q_decay = jnp.exp((idx[None, :] + 1.0) * log_gamma[:, None])[..., None]
    k_decay = jnp.exp((CHUNK - 1.0 - idx[None, :]) * log_gamma[:, None])[..., None]
    chunk_decay = jnp.exp(CHUNK * log_gamma)[:, None, None]
    return jnp.cos(ang), jnp.sin(ang), q_decay, k_decay, chunk_decay, intra


def _merge_kernel(y0_ref, y1_ref, y2_ref, mg_ref, wb_ref, wo_ref, h_ref, o_ref):
    merged = None
    for n, y_ref in enumerate((y0_ref, y1_ref, y2_ref)):
        gate = _sigmoid(mg_ref[:, n * D_MODEL:(n + 1) * D_MODEL].astype(F32))
        t = _dot(y_ref[...], wb_ref[n * BRANCH_WIDTH:(n + 1) * BRANCH_WIDTH, :]) * gate
        merged = t if merged is None else merged + t
    o_ref[...] = h_ref[...] + _dot(merged.astype(BF16), wo_ref[...])


def _merge(ys, proj, wb, wo, h):
    assert OFF_MG == 0
    rows = lambda width: pl.BlockSpec((MERGE_TILE, width), lambda i: (i, 0))
    whole = lambda a: pl.BlockSpec(a.shape, lambda i: (0, 0))
    return pl.pallas_call(
        _merge_kernel,
        grid=(ROWS // MERGE_TILE,),
        in_specs=[rows(BRANCH_WIDTH)] * N_BRANCH + [rows(N_BRANCH * D_MODEL)]
        + [whole(wb), whole(wo), rows(D_MODEL)],
        out_specs=rows(D_MODEL),
        out_shape=jax.ShapeDtypeStruct((ROWS, D_MODEL), F32),
        compiler_params=_params("parallel"),
        name="merge",
    )(*ys, proj, wb, wo, h)


def kernel(x, meta_tokens, ffn1_norm, ffn1_w_gate, ffn1_w_up, ffn1_w_down, mix_norm, w_in, gla_w_gate2, gla_b_gate, gla_norm, ret_norm, hgrn_lb_logits, hgrn_norm, w_branch, w_out, ffn2_norm, ffn2_w_gate, ffn2_w_up, ffn2_w_down, final_norm):
    b = x.shape[0]
    meta = jnp.broadcast_to(meta_tokens[None].astype(x.dtype), (b, N_META, D_MODEL))
    h = jnp.concatenate([x, jnp.zeros((b, PAD, D_MODEL), x.dtype), meta], axis=1).reshape(ROWS, D_MODEL)

    wc_np, masks_np = _decay_tables()
    wc = jnp.asarray(wc_np, BF16)
    masks = jnp.asarray(masks_np, F32)
    cos, sin, qd, kd, cd, im = _ret_tables()

    def ffn_steps(tiles):
        return (ROWS // tiles[0]) * pl.cdiv(D_FF, tiles[1])

    def ffn_casts(wg, wu, wd, layer, tiles, host_steps):
        padded = _ff_padded(tiles[1])
        n_cols, n_rows = D_MODEL // SIDE_ROWS, padded // SIDE_BLOCK
        return [_SideCast(wg, (layer,), 1, padded, 0, tiles[1]),
                _SideCast(wu, (layer,), 1, padded, min(n_cols, host_steps - n_cols), tiles[1]),
                _SideCast(wd, (layer,), 0, padded, min(2 * n_cols, host_steps - n_rows))]

    row = lambda v: v.reshape(1, -1).astype(F32)
    final_gain = row(final_norm)
    w_in_t = jnp.swapaxes(w_in, 1, 2)
    wb_rows = w_branch.reshape(DEPTH, N_BRANCH * BRANCH_WIDTH, D_MODEL)
    padded = _ff_padded(FFN1_TILES[1])
    ffn1_w = (_cast(ffn1_w_gate, (0,), 1, padded), _cast(ffn1_w_up, (0,), 1, padded),
              _cast(ffn1_w_down, (0,), 0, padded))
    for layer in range(DEPTH):
        h, (w_proj, *ffn2_w) = _ffn(
            h, row(ffn1_norm[layer]), *ffn1_w, final_gain, False, FFN1_TILES,
            sides=[_SideWIn(w_in_t, layer, 0)]
            + ffn_casts(ffn2_w_gate, ffn2_w_up, ffn2_w_down, layer,
                        FINAL_TILES if layer == DEPTH - 1 else FFN2_TILES, ffn_steps(FFN1_TILES)))
        proj, (wb, wo) = _proj(
            h, row(mix_norm[layer]), w_proj,
            sides=[_SideCast(wb_rows, (layer,), 0, N_BRANCH * BRANCH_WIDTH, 0),
                   _SideCast(w_out, (layer,), 0, D_MODEL, N_BRANCH * BRANCH_WIDTH // SIDE_BLOCK)])
        w2 = jnp.pad(gla_w_gate2[layer], ((0, LANE - GLA_GATE_RANK), (0, 0))).astype(BF16)
        proj_seq = proj.reshape(b, L_PAD, W_PROJ)
        y_gla = _gla(proj_seq, w2, row(gla_b_gate[layer]), row(gla_norm[layer]), wc, masks)
        y_ret = _ret(proj_seq, cos, sin, qd, kd, cd, im, row(ret_norm[layer]))
        y_hgrn = _hgrn(proj_seq, hgrn_lb_logits.astype(F32), row(hgrn_norm[layer]), wc, masks, layer)
        ys = [y.reshape(ROWS, BRANCH_WIDTH) for y in (y_gla, y_ret, y_hgrn)]
        h = _merge(ys, proj, wb, wo, h)
        last = layer == DEPTH - 1
        tiles = FINAL_TILES if last else FFN2_TILES
        h, ffn1_w = _ffn(
            h, row(ffn2_norm[layer]), *ffn2_w, final_gain, last, tiles,
            sides=[] if last else ffn_casts(ffn1_w_gate, ffn1_w_up, ffn1_w_down, layer + 1, FFN1_TILES,
                                            ffn_steps(tiles)))
    return h
```

```python
import functools
from typing import NamedTuple

import numpy as np
import jax
import jax.numpy as jnp
from jax import lax
from jax.experimental import pallas as pl
from jax.experimental.pallas import tpu as pltpu

D_MODEL = 2048
BATCH = 2
SEQ = 4096
DEPTH = 2
N_META = 16
CHUNK = 64
PAD = CHUNK - N_META
D_FF = 5504
FFN_RES = 0.5
EPS = 1e-6
N_BRANCH = 3
BRANCH_WIDTH = 1024

GLA_HEADS, GLA_DK, GLA_DV = 4, 128, 256
GLA_GATE_RANK = 16
GLA_GATE_TAU = 16.0
RET_HEADS, RET_DK, RET_DV = 4, 256, 256
ROPE_BASE = 10000.0
HGRN_HEADS, HGRN_DK, HGRN_DV = 8, 128, 128
FORGET_FLOOR = 1e-20

LANE = 128
L_PAD = PAD + N_META + SEQ
N_CHUNKS = L_PAD // CHUNK
ROWS = BATCH * L_PAD
N_LEVELS = 6
assert 1 << N_LEVELS == CHUNK

FFN1_TILES = (640, 512)
FFN2_TILES = (640, 512)
FINAL_TILES = (832, 512)
PROJ_ROW_TILE = 832
PROJ_TILE = 1792
MERGE_TILE = 320
CAST_TILE = 512
SIDE_BLOCK = LANE
SIDE_ROWS = 32
W_IN_BLOCK = 256
LR_PAD = CAST_TILE
W_IN_SHIFT = GLA_GATE_RANK

OFF_MG = 0
OFF_GLA = OFF_MG + N_BRANCH * D_MODEL
OFF_RET = OFF_GLA + 2 * GLA_HEADS * GLA_DK + 2 * GLA_HEADS * GLA_DV
OFF_HGRN = OFF_RET + 2 * RET_HEADS * RET_DK + 2 * RET_HEADS * RET_DV
OFF_LR = OFF_HGRN + 2 * HGRN_HEADS * HGRN_DK + 2 * HGRN_HEADS * HGRN_DV
W_PROJ = OFF_LR + LR_PAD

VMEM_LIMIT = 56 * 1024 * 1024

F32 = jnp.float32
BF16 = jnp.bfloat16


def _params(*sem):
    return pltpu.CompilerParams(dimension_semantics=sem, vmem_limit_bytes=VMEM_LIMIT)


def _rms_rows(x, gain):
    ms = jnp.mean(x * x, axis=-1, keepdims=True)
    return x * lax.rsqrt(ms + EPS) * gain


def _sigmoid(x):
    return 1.0 / (1.0 + jnp.exp(-x))


def _silu(x):
    return x * _sigmoid(x)


def _dot(a, b):
    return jnp.dot(a, b, preferred_element_type=F32)


def _dot_nt(a, b):
    return lax.dot_general(a, b, (((1,), (1,)), ((), ())), preferred_element_type=F32)


def _dot_tn(a, b):
    return lax.dot_general(a, b, (((0,), (0,)), ((), ())), preferred_element_type=F32)


def _cast_kernel(w_ref, o_ref, *, axis, valid):
    x = w_ref[...]
    idx = pl.program_id(0) * x.shape[axis] + lax.broadcasted_iota(jnp.int32, x.shape, axis)
    o_ref[...] = jnp.where(idx < valid, x, 0.0).astype(BF16)


def _cast(w, lead, axis, padded):
    rows, cols = w.shape[-2:]
    nl = len(lead)
    if axis == 0:
        block, out_shape, grid = (CAST_TILE, cols), (padded, cols), padded // CAST_TILE
        imap = lambda j: (*lead, j, 0)
        omap = lambda j: (j, 0)
    else:
        block, out_shape, grid = (rows, CAST_TILE), (padded // CAST_TILE, rows, CAST_TILE), padded // CAST_TILE
        imap = lambda j: (*lead, 0, j)
        omap = lambda j: (j, 0, 0)
    out_block = block if axis == 0 else (None,) + block
    return pl.pallas_call(
        functools.partial(_cast_kernel, axis=axis, valid=w.shape[-2 + axis]),
        grid=(grid,),
        in_specs=[pl.BlockSpec((None,) * nl + block, imap)],
        out_specs=pl.BlockSpec(out_block, omap),
        out_shape=jax.ShapeDtypeStruct(out_shape, BF16),
        compiler_params=_params("parallel"),
        name="cast",
    )(w)


SIDE_IDLE, SIDE_COPY, SIDE_ZERO, SIDE_SHIFTED, SIDE_LOW_RANK = 0, 1, 2, 3, 4


class _SideCast(NamedTuple):
    src: jax.Array
    lead: tuple
    axis: int
    padded: int
    start: int
    col_tile: int = 0


class _SideWIn(NamedTuple):
    src_t: jax.Array
    layer: int
    start: int


class _SidePlan(NamedTuple):
    table: np.ndarray
    in_specs: list
    operands: list
    out_specs: list
    out_shapes: list
    runners: list
    steps_per_row: int


def _w_in_blocks():
    per = lambda width: width // W_IN_BLOCK
    src = lambda col: col // W_IN_BLOCK
    gla_w = 2 * GLA_HEADS * GLA_DK + 2 * GLA_HEADS * GLA_DV
    ret_w = 2 * RET_HEADS * RET_DK + 2 * RET_HEADS * RET_DV
    hgrn_w = 2 * HGRN_HEADS * HGRN_DK + 2 * HGRN_HEADS * HGRN_DV
    lr_col = gla_w
    blocks = []
    for t in range(per(N_BRANCH * D_MODEL)):
        blocks.append((src(lr_col + ret_w + hgrn_w) + t, SIDE_SHIFTED))
    for t in range(per(gla_w)):
        blocks.append((t, SIDE_COPY))
    for t in range(per(ret_w + hgrn_w)):
        blocks.append((src(lr_col) + t, SIDE_SHIFTED))
    blocks.append((src(lr_col), SIDE_LOW_RANK))
    blocks += [(src(lr_col), SIDE_ZERO)] * (per(LR_PAD) - 1)
    assert len(blocks) == per(W_PROJ)
    return [(s, t, mode) for t, (s, mode) in enumerate(blocks)]


def _side_plan(jobs, n_rows, steps_per_row):
    n_steps = n_rows * steps_per_row
    rows, in_specs, operands, out_specs, out_shapes, runners = [], [], [], [], [], []

    def add_rows(blocks, start):
        n = len(blocks)
        assert start + n <= n_steps
        arr = np.asarray(blocks, np.int32).T
        tab = np.zeros((3, n_steps), np.int32)
        tab[:2, :start] = arr[:2, :1]
        tab[:, start:start + n] = arr
        tab[:2, start + n:] = arr[:2, -1:]
        base = len(rows) * n_steps
        rows.extend(tab)
        return base

    step_of = lambda i, j: i * steps_per_row + j
    for job in jobs:
        if isinstance(job, _SideCast):
            r, c = job.src.shape[-2:]
            lead, nl = job.lead, len(job.lead)
            if job.axis == 0:
                assert r % SIDE_BLOCK == 0 and job.padded % SIDE_BLOCK == 0
                n_valid, n_all = r // SIDE_BLOCK, job.padded // SIDE_BLOCK
                base = add_rows([(min(t, n_valid - 1), t, SIDE_COPY if t < n_valid else SIDE_ZERO)
                                 for t in range(n_all)], job.start)
                in_block = block = (SIDE_BLOCK, c)
                out_shape = (job.padded, c)
                omap = lambda i, j, tbl, base=base: (tbl[base + n_steps + step_of(i, j)], 0)

                def run(tbl_ref, step, ins, out, base=base):
                    mode = tbl_ref[base + 2 * n_steps + step]

                    @pl.when(mode != SIDE_IDLE)
                    def _():
                        out[...] = jnp.where(mode == SIDE_ZERO, 0.0, ins[0][...]).astype(BF16)
            else:
                assert r % SIDE_ROWS == 0 and c % LANE == 0 and job.padded % job.col_tile == 0
                n_tiles, tile = job.padded // job.col_tile, job.col_tile
                base = add_rows([(t, t, SIDE_COPY) for t in range(r // SIDE_ROWS)], job.start)
                in_block = (SIDE_ROWS, c)
                block, out_shape = (n_tiles, SIDE_ROWS, tile), (n_tiles, r, tile)
                omap = lambda i, j, tbl, base=base: (0, tbl[base + n_steps + step_of(i, j)], 0)

                def run(tbl_ref, step, ins, out, base=base, c=c, n_tiles=n_tiles, tile=tile):
                    mode = tbl_ref[base + 2 * n_steps + step]

                    @pl.when(mode != SIDE_IDLE)
                    def _():
                        for kk in range(n_tiles):
                            width = min(tile, c - kk * tile)
                            assert width > 0
                            piece = ins[0][:, kk * tile:kk * tile + width].astype(BF16)
                            if width < tile:
                                piece = jnp.concatenate(
                                    [piece, jnp.zeros((SIDE_ROWS, tile - width), BF16)], axis=1)
                            out[kk] = piece

            imap = lambda i, j, tbl, base=base, lead=lead: (*lead, tbl[base + step_of(i, j)], 0)
            in_specs.append(pl.BlockSpec((None,) * nl + in_block, imap))
            operands.append(job.src)
            runners.append((1, run))
        else:
            base = add_rows(_w_in_blocks(), job.start)
            layer = job.layer
            shifts_per_block = W_IN_BLOCK // W_IN_SHIFT
            in_specs.append(pl.BlockSpec(
                (None, W_IN_BLOCK, D_MODEL),
                lambda i, j, tbl, base=base, layer=layer: (layer, tbl[base + step_of(i, j)], 0)))
            in_specs.append(pl.BlockSpec(
                (None, W_IN_SHIFT, D_MODEL),
                lambda i, j, tbl, base=base, layer=layer:
                (layer, (tbl[base + step_of(i, j)] + 1) * shifts_per_block, 0)))
            operands += [job.src_t, job.src_t]
            per_tile = PROJ_TILE // W_IN_BLOCK
            block, out_shape = (None, D_MODEL, W_IN_BLOCK), (W_PROJ // PROJ_TILE, D_MODEL, PROJ_TILE)

            def omap(i, j, tbl, base=base, per_tile=per_tile):
                t = tbl[base + n_steps + step_of(i, j)]
                return (t // per_tile, 0, t % per_tile)

            def run(tbl_ref, step, ins, out, base=base):
                a_ref, b_ref = ins
                mode = tbl_ref[base + 2 * n_steps + step]

                @pl.when(mode == SIDE_SHIFTED)
                def _():
                    x = jnp.concatenate([a_ref[W_IN_SHIFT:, :], b_ref[...]], axis=0)
                    out[...] = x.astype(BF16).T

                @pl.when(mode == SIDE_COPY)
                def _():
                    out[...] = a_ref[...].astype(BF16).T

                @pl.when(mode == SIDE_LOW_RANK)
                def _():
                    row = lax.broadcasted_iota(jnp.int32, a_ref.shape, 0)
                    out[...] = jnp.where(row < GLA_GATE_RANK, a_ref[...], 0.0).astype(BF16).T

                @pl.when(mode == SIDE_ZERO)
                def _():
                    out[...] = jnp.zeros_like(out)

            runners.append((2, run))
        out_specs.append(pl.BlockSpec(block, omap))
        out_shapes.append(jax.ShapeDtypeStruct(out_shape, BF16))

    table = np.concatenate(rows) if rows else np.zeros((1,), np.int32)
    return _SidePlan(table, in_specs, operands, out_specs, out_shapes, runners, steps_per_row)


def _run_sides(plan, tbl_ref, side_ins, side_outs):
    step = pl.program_id(0) * plan.steps_per_row + pl.program_id(1)
    k = 0
    for (n_in, run), out in zip(plan.runners, side_outs):
        run(tbl_ref, step, side_ins[k:k + n_in], out)
        k += n_in


def _hosted_call(kernel, plan, grid, in_specs, out_spec, out_shape, scratch_shapes, name, operands):
    n_side_in = len(plan.in_specs)
    n_main_in = len(in_specs)

    def body(tbl_ref, *refs):
        ins = refs[:n_main_in]
        side_ins = refs[n_main_in:n_main_in + n_side_in]
        out = refs[n_main_in + n_side_in]
        side_outs = refs[n_main_in + n_side_in + 1:n_main_in + n_side_in + 1 + len(plan.out_specs)]
        scratch = refs[n_main_in + n_side_in + 1 + len(plan.out_specs):]
        kernel(*ins, out, *scratch)
        _run_sides(plan, tbl_ref, side_ins, side_outs)

    res = pl.pallas_call(
        body,
        grid_spec=pltpu.PrefetchScalarGridSpec(
            num_scalar_prefetch=1,
            grid=grid,
            in_specs=list(in_specs) + plan.in_specs,
            out_specs=[out_spec] + plan.out_specs,
            scratch_shapes=scratch_shapes,
        ),
        out_shape=[out_shape] + plan.out_shapes,
        compiler_params=_params("arbitrary", "arbitrary"),
        name=name,
    )(jnp.asarray(plan.table), *operands, *plan.operands)
    return res[0], res[1:]


def _ffn_kernel(h_ref, g_ref, wg_ref, wu_ref, wd_ref, fg_ref, o_ref, u_scr, *, final, tail):
    j = pl.program_id(1)
    last = pl.num_programs(1) - 1
    tile = wg_ref.shape[1]
    half = h_ref.shape[0] // 2
    halves = (slice(0, half), slice(half, 2 * half))

    def partial_sum(u, width):
        a = _dot(u, wg_ref[:, :width])
        b = _dot(u, wu_ref[:, :width])
        act = (_silu(a) * b).astype(BF16)
        return _dot(act, wd_ref[:width, :])

    @pl.when(j == 0)
    def _():
        for r in halves:
            u = _rms_rows(h_ref[r, :], g_ref[...]).astype(BF16)
            u_scr[r, :] = u
            o_ref[r, :] = partial_sum(u, tile)

    @pl.when((j > 0) & (j < last))
    def _():
        o_ref[...] += partial_sum(u_scr[...], tile)

    @pl.when(j == last)
    def _():
        for r in (halves if final else (slice(None),)):
            y = h_ref[r, :] + FFN_RES * (o_ref[r, :] + partial_sum(u_scr[r, :], tail))
            if final:
                y = _rms_rows(y, fg_ref[...])
            o_ref[r, :] = y


def _ff_padded(tf):
    return pl.cdiv(D_FF, tf) * tf


def _ffn(h, gain, wg, wu, wd, final_gain, final, tiles, sides=()):
    tm, tf = tiles
    assert ROWS % tm == 0 and wg.shape == (_ff_padded(tf) // tf, D_MODEL, tf) and wd.shape[0] == _ff_padded(tf)
    grid = (ROWS // tm, wg.shape[0])
    tail = D_FF - (grid[1] - 1) * tf
    plan = _side_plan(sides, *grid)
    if final:
        assert L_PAD % tm == 0
        per_seq = L_PAD // tm
        h = h.reshape(BATCH, L_PAD, D_MODEL)
        rows_spec = pl.BlockSpec((None, tm, D_MODEL), lambda i, j, tbl: (i // per_seq, i % per_seq, 0))
        out_shape = jax.ShapeDtypeStruct((BATCH, SEQ, D_MODEL), F32)
    else:
        rows_spec = pl.BlockSpec((tm, D_MODEL), lambda i, j, tbl: (i, 0))
        out_shape = jax.ShapeDtypeStruct((ROWS, D_MODEL), F32)
    return _hosted_call(
        functools.partial(_ffn_kernel, final=final, tail=tail), plan, grid,
        in_specs=[
            rows_spec,
            pl.BlockSpec((1, D_MODEL), lambda i, j, tbl: (0, 0)),
            pl.BlockSpec((None, D_MODEL, tf), lambda i, j, tbl: (j, 0, 0)),
            pl.BlockSpec((None, D_MODEL, tf), lambda i, j, tbl: (j, 0, 0)),
            pl.BlockSpec((tf, D_MODEL), lambda i, j, tbl: (j, 0)),
            pl.BlockSpec((1, D_MODEL), lambda i, j, tbl: (0, 0)),
        ],
        out_spec=rows_spec,
        out_shape=out_shape,
        scratch_shapes=[pltpu.VMEM((tm, D_MODEL), BF16)],
        name="ffn",
        operands=(h, gain, wg, wu, wd, final_gain),
    )


def _proj_kernel(h_ref, g_ref, w_ref, o_ref, u_scr):
    j = pl.program_id(1)

    @pl.when(j == 0)
    def _():
        half = h_ref.shape[0] // 2
        for r in (slice(0, half), slice(half, 2 * half)):
            u = _rms_rows(h_ref[r, :], g_ref[...]).astype(BF16)
            u_scr[r, :] = u
            o_ref[r, :] = _dot(u, w_ref[...]).astype(o_ref.dtype)

    @pl.when(j > 0)
    def _():
        o_ref[...] = _dot(u_scr[...], w_ref[...]).astype(o_ref.dtype)


def _proj(h, gain, w, sides=()):
    grid = (ROWS // PROJ_ROW_TILE, W_PROJ // PROJ_TILE)
    plan = _side_plan(sides, *grid)
    return _hosted_call(
        _proj_kernel, plan, grid,
        in_specs=[
            pl.BlockSpec((PROJ_ROW_TILE, D_MODEL), lambda i, j, tbl: (i, 0)),
            pl.BlockSpec((1, D_MODEL), lambda i, j, tbl: (0, 0)),
            pl.BlockSpec((None, D_MODEL, PROJ_TILE), lambda i, j, tbl: (j, 0, 0)),
        ],
        out_spec=pl.BlockSpec((PROJ_ROW_TILE, PROJ_TILE), lambda i, j, tbl: (i, j)),
        out_shape=jax.ShapeDtypeStruct((ROWS, W_PROJ), BF16),
        scratch_shapes=[pltpu.VMEM((PROJ_ROW_TILE, D_MODEL), BF16)],
        name="in_proj",
        operands=(h, gain, w),
    )


def _decay_tables():
    c = CHUNK
    blocks = []
    r = np.arange(c)
    for l in range(1, N_LEVELS):
        b = 1 << l
        m = np.zeros((c, c), np.float32)
        for i in range(c):
            p = i % (2 * b)
            s = i - p + b
            if p >= b:
                m[i, s:i + 1] = 1.0
            else:
                m[i, i + 1:s] = 1.0
        blocks.append(m)
    blocks.append((r[None, :] <= r[:, None]).astype(np.float32))
    wc = np.concatenate(blocks, axis=0)
    wc = np.concatenate([wc] * 3, axis=1)

    masks = np.zeros((N_LEVELS + 1, c, c), np.float32)
    for l in range(N_LEVELS):
        b = 1 << l
        same = (r[:, None] // (2 * b)) == (r[None, :] // (2 * b))
        up = (r[:, None] & b) != 0
        lo = (r[None, :] & b) == 0
        masks[l] = (same & up & lo).astype(np.float32)
    masks[N_LEVELS] = np.eye(c, dtype=np.float32)
    return wc, masks


def _gla_core(qs, ks, vs, gs, wc_ref, mk_ref, s_ref, slot_of, n_heads, dk, dv):
    c = CHUNK
    seqs = range(len(qs))
    heads = range(n_heads)
    base = (N_LEVELS - 1) * c

    def split3(g):
        g_hi = g.astype(BF16)
        r1 = g - g_hi.astype(F32)
        g_mid = r1.astype(BF16)
        g_lo = (r1 - g_mid.astype(F32)).astype(BF16)
        return jnp.concatenate([g_hi, g_mid, g_lo], axis=0)

    sums = [_dot(wc_ref[...], split3(gs[b])) for b in seqs]
    ex = [jnp.exp(sums[b][:base]) for b in seqs]
    cum = [sums[b][base:base + c] for b in seqs]
    total = [cum[b][c - 1:c] for b in seqs]

    on_diag = [qs[b] * ks[b] for b in seqs]
    below = [qs[b] * jnp.exp(gs[b]) * pltpu.roll(ks[b], 1, axis=0) for b in seqs]
    scores = [[None] * n_heads for _ in seqs]
    for h in heads:
        lanes = slice(h * dk, (h + 1) * dk)
        for b in seqs:
            scores[b][h] = (mk_ref[N_LEVELS] * jnp.sum(on_diag[b][:, lanes], axis=-1, keepdims=True)
                            + mk_ref[0] * jnp.sum(below[b][:, lanes], axis=-1, keepdims=True))
    rows = lax.broadcasted_iota(jnp.int32, (c, 1), 0)
    for l in range(1, N_LEVELS):
        upper = (rows & (1 << l)) != 0
        x = [(jnp.where(upper, qs[b], ks[b]) * ex[b][(l - 1) * c:l * c]).astype(BF16) for b in seqs]
        for h in heads:
            for b in seqs:
                xh = x[b][:, h * dk:(h + 1) * dk]
                scores[b][h] = scores[b][h] + mk_ref[l] * _dot_nt(xh, xh)

    vb = [vs[b].astype(BF16) for b in seqs]
    qt = [(qs[b] * jnp.exp(cum[b])).astype(BF16) for b in seqs]
    kt = [(ks[b] * jnp.exp(total[b] - cum[b])).astype(BF16) for b in seqs]
    tot = [jnp.exp(total[b]) for b in seqs]
    outs = [[None] * n_heads for _ in seqs]
    for h in heads:
        for b in seqs:
            slot = slot_of(b, h)
            st = s_ref[slot]
            vh = vb[b][:, h * dv:(h + 1) * dv]
            outs[b][h] = (_dot(scores[b][h].astype(BF16), vh)
                          + _dot_nt(qt[b][:, h * dk:(h + 1) * dk], st.astype(BF16)))
            s_ref[slot] = st * tot[b][:, h * dk:(h + 1) * dk] + _dot_tn(vh, kt[b][:, h * dk:(h + 1) * dk])
    return outs


def _head_rms(outs, gain, gate):
    dv = outs[0].shape[-1]
    ys = []
    for h, o in enumerate(outs):
        ms = jnp.mean(o * o, axis=-1, keepdims=True)
        ys.append(o * lax.rsqrt(ms + EPS))
    y = jnp.concatenate(ys, axis=-1)
    return y * gain * _silu(gate)


def _log_sigmoid(z):
    return jnp.minimum(z, 0.0) - jnp.log(1.0 + jnp.exp(-jnp.abs(z)))


def _gla_kernel(q_ref, k_ref, v_ref, gate_ref, lr_ref, w2_ref, b_ref, norm_ref, wc_ref, mk_ref,
                o_ref, s_ref):
    @pl.when(pl.program_id(0) == 0)
    def _():
        s_ref[...] = jnp.zeros_like(s_ref)

    seqs = range(BATCH)
    z = [_dot(lr_ref[b], w2_ref[...]) + b_ref[...] for b in seqs]
    g = [_log_sigmoid(z[b]) * (1.0 / GLA_GATE_TAU) for b in seqs]
    q = [q_ref[b].astype(F32) * (GLA_DK ** -0.5) for b in seqs]
    k = [k_ref[b].astype(F32) for b in seqs]
    v = [v_ref[b].astype(F32) for b in seqs]
    outs = _gla_core(q, k, v, g, wc_ref, mk_ref, s_ref, lambda b, h: b * GLA_HEADS + h,
                     GLA_HEADS, GLA_DK, GLA_DV)
    for b in seqs:
        o_ref[b] = _head_rms(outs[b], norm_ref[...], gate_ref[b].astype(F32)).astype(o_ref.dtype)


def _hgrn_kernel(q_ref, f_ref, i_ref, gate_ref, lb_ref, norm_ref, wc_ref, mk_ref, o_ref, s_ref, *, layer):
    @pl.when(pl.program_id(0) == 0)
    def _():
        s_ref[...] = jnp.zeros_like(s_ref)

    logits = lb_ref[...]
    e = jnp.exp(logits - jnp.max(logits, axis=0, keepdims=True))
    soft = e / jnp.sum(e, axis=0, keepdims=True)
    lb = jnp.zeros((1, logits.shape[1]), F32)
    for l in range(1, layer + 1):
        lb = lb + soft[l:l + 1]

    seqs = range(BATCH)
    group = HGRN_HEADS // 2
    for first in range(0, HGRN_HEADS, group):
        qk = slice(first * HGRN_DK, (first + group) * HGRN_DK)
        vv = slice(first * HGRN_DV, (first + group) * HGRN_DV)
        lbg = lb[:, qk]
        hf = [f_ref[b, :, qk].astype(F32) for b in seqs]
        forget = [lbg + (1.0 - lbg) * _sigmoid(hf[b]) for b in seqs]
        g = [jnp.log(jnp.maximum(forget[b], FORGET_FLOOR)) for b in seqs]
        k = [1.0 - forget[b] for b in seqs]
        v = [_silu(i_ref[b, :, vv].astype(F32)) for b in seqs]
        q = [q_ref[b, :, qk].astype(F32) for b in seqs]
        outs = _gla_core(q, k, v, g, wc_ref, mk_ref, s_ref,
                         lambda b, h, first=first: b * HGRN_HEADS + first + h, group, HGRN_DK, HGRN_DV)
        for b in seqs:
            y = _head_rms(outs[b], norm_ref[:, vv], gate_ref[b, :, vv].astype(F32))
            o_ref[b, :, vv] = y.astype(o_ref.dtype)


def _chunk_block(c):
    return (c + N_CHUNKS - 1) % N_CHUNKS


def _col_spec(width, offset):
    assert offset % width == 0
    blk = offset // width
    return pl.BlockSpec((BATCH, CHUNK, width), lambda c: (0, _chunk_block(c), blk))


def _const_spec(shape):
    nd = len(shape)
    return pl.BlockSpec(shape, lambda c: (0,) * nd)


def _mixer_out(width):
    return dict(
        out_specs=pl.BlockSpec((BATCH, CHUNK, width), lambda c: (0, _chunk_block(c), 0)),
        out_shape=jax.ShapeDtypeStruct((BATCH, L_PAD, width), BF16),
        compiler_params=_params("arbitrary"),
    )


def _gla(proj, w2, bias, norm, wc, masks):
    qk = GLA_HEADS * GLA_DK
    vw = GLA_HEADS * GLA_DV
    return pl.pallas_call(
        _gla_kernel,
        grid=(N_CHUNKS,),
        in_specs=[
            _col_spec(qk, OFF_GLA),
            _col_spec(qk, OFF_GLA + qk),
            _col_spec(vw, OFF_GLA + 2 * qk),
            _col_spec(vw, OFF_GLA + 2 * qk + vw),
            _col_spec(LANE, OFF_LR),
            _const_spec(w2.shape),
            _const_spec(bias.shape),
            _const_spec(norm.shape),
            _const_spec(wc.shape),
            _const_spec(masks.shape),
        ],
        scratch_shapes=[pltpu.VMEM((BATCH * GLA_HEADS, GLA_DV, GLA_DK), F32)],
        name="gla",
        **_mixer_out(vw),
    )(proj, proj, proj, proj, proj, w2, bias, norm, wc, masks)


def _hgrn(proj, lb_logits, norm, wc, masks, layer):
    w = HGRN_HEADS * HGRN_DK
    return pl.pallas_call(
        functools.partial(_hgrn_kernel, layer=layer),
        grid=(N_CHUNKS,),
        in_specs=[
            _col_spec(w, OFF_HGRN),
            _col_spec(w, OFF_HGRN + w),
            _col_spec(w, OFF_HGRN + 2 * w),
            _col_spec(w, OFF_HGRN + 3 * w),
            _const_spec(lb_logits.shape),
            _const_spec(norm.shape),
            _const_spec(wc.shape),
            _const_spec(masks.shape),
        ],
        scratch_shapes=[pltpu.VMEM((BATCH * HGRN_HEADS, HGRN_DV, HGRN_DK), F32)],
        name="hgrn",
        **_mixer_out(w),
    )(proj, proj, proj, proj, lb_logits, norm, wc, masks)


def _ret_kernel(q_ref, k_ref, v_ref, gate_ref, cos_ref, sin_ref, qd_ref, kd_ref, cd_ref, im_ref, norm_ref,
                o_ref, s_ref):
    @pl.when(pl.program_id(0) == 0)
    def _():
        s_ref[...] = jnp.zeros_like(s_ref)

    half = RET_DK // 2
    cos = cos_ref[...]
    sin = sin_ref[...]
    seqs = range(BATCH)
    pairs = [(b, h) for h in range(RET_HEADS) for b in seqs]

    def rotate(ref, b, h):
        lo = h * RET_DK
        t1 = ref[b, :, lo:lo + half].astype(F32)
        t2 = ref[b, :, lo + half:lo + RET_DK].astype(F32)
        return jnp.concatenate([t1 * cos - t2 * sin, t1 * sin + t2 * cos], axis=-1)

    qr = {p: rotate(q_ref, *p) * (RET_DK ** -0.5) for p in pairs}
    kr = {p: rotate(k_ref, *p) for p in pairs}
    scores = {p: _dot_nt(qr[p].astype(BF16), kr[p].astype(BF16)) * im_ref[p[1]] for p in pairs}
    outs = {}
    for b, h in pairs:
        p = (b, h)
        vh = v_ref[b, :, h * RET_DV:(h + 1) * RET_DV]
        st = s_ref[b * RET_HEADS + h]
        outs[p] = (_dot(scores[p].astype(BF16), vh)
                   + _dot_nt((qr[p] * qd_ref[h]).astype(BF16), st.astype(BF16)))
        s_ref[b * RET_HEADS + h] = st * cd_ref[h] + _dot_tn(vh, (kr[p] * kd_ref[h]).astype(BF16))
    for b in seqs:
        ys = []
        for h in range(RET_HEADS):
            o = outs[(b, h)]
            o = o - jnp.mean(o, axis=-1, keepdims=True)
            ms = jnp.mean(o * o, axis=-1, keepdims=True)
            ys.append(o * lax.rsqrt(ms + EPS))
        y = jnp.concatenate(ys, axis=-1)
        o_ref[b] = (y * norm_ref[...] * _silu(gate_ref[b].astype(F32))).astype(o_ref.dtype)


def _ret(proj, cos, sin, qd, kd, cd, im, norm):
    w = RET_HEADS * RET_DK
    half = RET_DK // 2
    return pl.pallas_call(
        _ret_kernel,
        grid=(N_CHUNKS,),
        in_specs=[
            _col_spec(w, OFF_RET),
            _col_spec(w, OFF_RET + w),
            _col_spec(w, OFF_RET + 2 * w),
            _col_spec(w, OFF_RET + 3 * w),
            pl.BlockSpec((CHUNK, half), lambda c: (c, 0)),
            pl.BlockSpec((CHUNK, half), lambda c: (c, 0)),
            _const_spec(qd.shape),
            _const_spec(kd.shape),
            _const_spec(cd.shape),
            _const_spec(im.shape),
            _const_spec(norm.shape),
        ],
        scratch_shapes=[pltpu.VMEM((BATCH * RET_HEADS, RET_DV, RET_DK), F32)],
        name="retention",
        **_mixer_out(w),
    )(proj, proj, proj, proj, cos, sin, qd, kd, cd, im, norm)


def _ret_tables():
    f32 = jnp.float32
    half = RET_DK // 2
    pos = jnp.arange(L_PAD, dtype=f32) - PAD
    inv_freq = ROPE_BASE ** (-jnp.arange(half, dtype=f32) / half)
    ang = pos[:, None] * inv_freq[None, :]
    log_gamma = jnp.log(1.0 - 2.0 ** (-5.0 - jnp.arange(RET_HEADS, dtype=f32)))
    idx = jnp.arange(CHUNK, dtype=f32)
    rel = idx[:, None] - idx[None, :]
    causal = (rel >= 0)[None]
    intra = jnp.where(causal, jnp.exp(jnp.where(causal, rel[None], 0.0) * log_gamma[:, None, None]), 0.0)
    q_decay = jnp.exp((idx[None, :] + 1.0) * log_gamma[:, None])[..., None]
    k_decay = jnp.exp((CHUNK - 1.0 - idx[None, :]) * log_gamma[:, None])[..., None]
    chunk_decay = jnp.exp(CHUNK * log_gamma)[:, None, None]
    return jnp.cos(ang), jnp.sin(ang), q_decay, k_decay, chunk_decay, intra


def _merge_kernel(y0_ref, y1_ref, y2_ref, mg_ref, wb_ref, wo_ref, h_ref, o_ref):
    merged = None
    for n, y_ref in enumerate((y0_ref, y1_ref, y2_ref)):
        gate = _sigmoid(mg_ref[:, n * D_MODEL:(n + 1) * D_MODEL].astype(F32))
        t = _dot(y_ref[...], wb_ref[n * BRANCH_WIDTH:(n + 1) * BRANCH_WIDTH, :]) * gate
        merged = t if merged is None else merged + t
    o_ref[...] = h_ref[...] + _dot(merged.astype(BF16), wo_ref[...])


def _merge(ys, proj, wb, wo, h):
    assert OFF_MG == 0
    rows = lambda width: pl.BlockSpec((MERGE_TILE, width), lambda i: (i, 0))
    whole = lambda a: pl.BlockSpec(a.shape, lambda i: (0, 0))
    return pl.pallas_call(
        _merge_kernel,
        grid=(ROWS // MERGE_TILE,),
        in_specs=[rows(BRANCH_WIDTH)] * N_BRANCH + [rows(N_BRANCH * D_MODEL)]
        + [whole(wb), whole(wo), rows(D_MODEL)],
        out_specs=rows(D_MODEL),
        out_shape=jax.ShapeDtypeStruct((ROWS, D_MODEL), F32),
        compiler_params=_params("parallel"),
        name="merge",
    )(*ys, proj, wb, wo, h)


def kernel(x, meta_tokens, ffn1_norm, ffn1_w_gate, ffn1_w_up, ffn1_w_down, mix_norm, w_in, gla_w_gate2, gla_b_gate, gla_norm, ret_norm, hgrn_lb_logits, hgrn_norm, w_branch, w_out, ffn2_norm, ffn2_w_gate, ffn2_w_up, ffn2_w_down, final_norm):
    b = x.shape[0]
    meta = jnp.broadcast_to(meta_tokens[None].astype(x.dtype), (b, N_META, D_MODEL))
    h = jnp.concatenate([x, jnp.zeros((b, PAD, D_MODEL), x.dtype), meta], axis=1).reshape(ROWS, D_MODEL)

    wc_np, masks_np = _decay_tables()
    wc = jnp.asarray(wc_np, BF16)
    masks = jnp.asarray(masks_np, F32)
    cos, sin, qd, kd, cd, im = _ret_tables()

    def ffn_steps(tiles):
        return (ROWS // tiles[0]) * pl.cdiv(D_FF, tiles[1])

    def ffn_casts(wg, wu, wd, layer, tiles, host_steps):
        padded = _ff_padded(tiles[1])
        n_cols, n_rows = D_MODEL // SIDE_ROWS, padded // SIDE_BLOCK
        return [_SideCast(wg, (layer,), 1, padded, 0, tiles[1]),
                _SideCast(wu, (layer,), 1, padded, min(n_cols, host_steps - n_cols), tiles[1]),
                _SideCast(wd, (layer,), 0, padded, min(2 * n_cols, host_steps - n_rows))]

    row = lambda v: v.reshape(1, -1).astype(F32)
    final_gain = row(final_norm)
    w_in_t = jnp.swapaxes(w_in, 1, 2)
    wb_rows = w_branch.reshape(DEPTH, N_BRANCH * BRANCH_WIDTH, D_MODEL)
    padded = _ff_padded(FFN1_TILES[1])
    ffn1_w = (_cast(ffn1_w_gate, (0,), 1, padded), _cast(ffn1_w_up, (0,), 1, padded),
              _cast(ffn1_w_down, (0,), 0, padded))
    for layer in range(DEPTH):
        h, (w_proj, *ffn2_w) = _ffn(
            h, row(ffn1_norm[layer]), *ffn1_w, final_gain, False, FFN1_TILES,
            sides=[_SideWIn(w_in_t, layer, 0)]
            + ffn_casts(ffn2_w_gate, ffn2_w_up, ffn2_w_down, layer,
                        FINAL_TILES if layer == DEPTH - 1 else FFN2_TILES, ffn_steps(FFN1_TILES)))
        proj, (wb, wo) = _proj(
            h, row(mix_norm[layer]), w_proj,
            sides=[_SideCast(wb_rows, (layer,), 0, N_BRANCH * BRANCH_WIDTH, 0),
                   _SideCast(w_out, (layer,), 0, D_MODEL, N_BRANCH * BRANCH_WIDTH // SIDE_BLOCK)])
        w2 = jnp.pad(gla_w_gate2[layer], ((0, LANE - GLA_GATE_RANK), (0, 0))).astype(BF16)
        proj_seq = proj.reshape(b, L_PAD, W_PROJ)
        y_gla = _gla(proj_seq, w2, row(gla_b_gate[layer]), row(gla_norm[layer]), wc, masks)
        y_ret = _ret(proj_seq, cos, sin, qd, kd, cd, im, row(ret_norm[layer]))
        y_hgrn = _hgrn(proj_seq, hgrn_lb_logits.astype(F32), row(hgrn_norm[layer]), wc, masks, layer)
        ys = [y.reshape(ROWS, BRANCH_WIDTH) for y in (y_gla, y_ret, y_hgrn)]
        h = _merge(ys, proj, wb, wo, h)
        last = layer == DEPTH - 1
        tiles = FINAL_TILES if last else FFN2_TILES
        h, ffn1_w = _ffn(
            h, row(ffn2_norm[layer]), *ffn2_w, final_gain, last, tiles,
            sides=[] if last else ffn_casts(ffn1_w_gate, ffn1_w_up, ffn1_w_down, layer + 1, FFN1_TILES,
                                            ffn_steps(tiles)))
    return h
```

```python
import functools
from typing import NamedTuple

import numpy as np
import jax
import jax.numpy as jnp
from jax import lax
from jax.experimental import pallas as pl
from jax.experimental.pallas import tpu as pltpu

D_MODEL = 2048
BATCH = 2
SEQ = 4096
DEPTH = 2
N_META = 16
CHUNK = 64
PAD = CHUNK - N_META
D_FF = 5504
FFN_RES = 0.5
EPS = 1e-6
N_BRANCH = 3
BRANCH_WIDTH = 1024

GLA_HEADS, GLA_DK, GLA_DV = 4, 128, 256
GLA_GATE_RANK = 16
GLA_GATE_TAU = 16.0
RET_HEADS, RET_DK, RET_DV = 4, 256, 256
ROPE_BASE = 10000.0
HGRN_HEADS, HGRN_DK, HGRN_DV = 8, 128, 128
FORGET_FLOOR = 1e-20

LANE = 128
L_PAD = PAD + N_META + SEQ
N_CHUNKS = L_PAD // CHUNK
ROWS = BATCH * L_PAD
N_LEVELS = 6
assert 1 << N_LEVELS == CHUNK

FFN1_TILES = (640, 512)
FFN2_TILES = (640, 512)
FINAL_TILES = (832, 512)
PROJ_ROW_TILE = 832
PROJ_TILE = 1792
MERGE_TILE = 320
CAST_TILE = 512
SIDE_BLOCK = LANE
SIDE_ROWS = 32
W_IN_BLOCK = 256
LR_PAD = CAST_TILE
W_IN_SHIFT = GLA_GATE_RANK

OFF_MG = 0
OFF_GLA = OFF_MG + N_BRANCH * D_MODEL
OFF_RET = OFF_GLA + 2 * GLA_HEADS * GLA_DK + 2 * GLA_HEADS * GLA_DV
OFF_HGRN = OFF_RET + 2 * RET_HEADS * RET_DK + 2 * RET_HEADS * RET_DV
OFF_LR = OFF_HGRN + 2 * HGRN_HEADS * HGRN_DK + 2 * HGRN_HEADS * HGRN_DV
W_PROJ = OFF_LR + LR_PAD

VMEM_LIMIT = 56 * 1024 * 1024

F32 = jnp.float32
BF16 = jnp.bfloat16


def _params(*sem):
    return pltpu.CompilerParams(dimension_semantics=sem, vmem_limit_bytes=VMEM_LIMIT)


def _rms_rows(x, gain):
    ms = jnp.mean(x * x, axis=-1, keepdims=True)
    return x * lax.rsqrt(ms + EPS) * gain


def _sigmoid(x):
    return 1.0 / (1.0 + jnp.exp(-x))


def _silu(x):
    return x * _sigmoid(x)


def _dot(a, b):
    return jnp.dot(a, b, preferred_element_type=F32)


def _dot_nt(a, b):
    return lax.dot_general(a, b, (((1,), (1,)), ((), ())), preferred_element_type=F32)


def _dot_tn(a, b):
    return lax.dot_general(a, b, (((0,), (0,)), ((), ())), preferred_element_type=F32)


def _cast_kernel(w_ref, o_ref, *, axis, valid):
    x = w_ref[...]
    idx = pl.program_id(0) * x.shape[axis] + lax.broadcasted_iota(jnp.int32, x.shape, axis)
    o_ref[...] = jnp.where(idx < valid, x, 0.0).astype(BF16)


def _cast(w, lead, axis, padded):
    rows, cols = w.shape[-2:]
    nl = len(lead)
    if axis == 0:
        block, out_shape, grid = (CAST_TILE, cols), (padded, cols), padded // CAST_TILE
        imap = lambda j: (*lead, j, 0)
        omap = lambda j: (j, 0)
    else:
        block, out_shape, grid = (rows, CAST_TILE), (padded // CAST_TILE, rows, CAST_TILE), padded // CAST_TILE
        imap = lambda j: (*lead, 0, j)
        omap = lambda j: (j, 0, 0)
    out_block = block if axis == 0 else (None,) + block
    return pl.pallas_call(
        functools.partial(_cast_kernel, axis=axis, valid=w.shape[-2 + axis]),
        grid=(grid,),
        in_specs=[pl.BlockSpec((None,) * nl + block, imap)],
        out_specs=pl.BlockSpec(out_block, omap),
        out_shape=jax.ShapeDtypeStruct(out_shape, BF16),
        compiler_params=_params("parallel"),
        name="cast",
    )(w)


SIDE_IDLE, SIDE_COPY, SIDE_ZERO, SIDE_SHIFTED, SIDE_LOW_RANK = 0, 1, 2, 3, 4


class _SideCast(NamedTuple):
    src: jax.Array
    lead: tuple
    axis: int
    padded: int
    start: int
    col_tile: int = 0


class _SideWIn(NamedTuple):
    src_t: jax.Array
    layer: int
    start: int


class _SidePlan(NamedTuple):
    table: np.ndarray
    in_specs: list
    operands: list
    out_specs: list
    out_shapes: list
    runners: list
    grid: tuple


def _w_in_blocks():
    per = lambda width: width // W_IN_BLOCK
    src = lambda col: col // W_IN_BLOCK
    gla_w = 2 * GLA_HEADS * GLA_DK + 2 * GLA_HEADS * GLA_DV
    ret_w = 2 * RET_HEADS * RET_DK + 2 * RET_HEADS * RET_DV
    hgrn_w = 2 * HGRN_HEADS * HGRN_DK + 2 * HGRN_HEADS * HGRN_DV
    lr_col = gla_w
    blocks = []
    for t in range(per(N_BRANCH * D_MODEL)):
        blocks.append((src(lr_col + ret_w + hgrn_w) + t, SIDE_SHIFTED))
    for t in range(per(gla_w)):
        blocks.append((t, SIDE_COPY))
    for t in range(per(ret_w + hgrn_w)):
        blocks.append((src(lr_col) + t, SIDE_SHIFTED))
    blocks.append((src(lr_col), SIDE_LOW_RANK))
    blocks += [(src(lr_col), SIDE_ZERO)] * (per(LR_PAD) - 1)
    assert len(blocks) == per(W_PROJ)
    return [(s, t, mode) for t, (s, mode) in enumerate(blocks)]


def _side_plan(jobs, grid):
    n_steps = int(np.prod(grid))
    rows, in_specs, operands, out_specs, out_shapes, runners = [], [], [], [], [], []

    def add_rows(blocks, start):
        n = len(blocks)
        assert start + n <= n_steps
        arr = np.asarray(blocks, np.int32).T
        tab = np.zeros((3, n_steps), np.int32)
        tab[:2, :start] = arr[:2, :1]
        tab[:, start:start + n] = arr
        tab[:2, start + n:] = arr[:2, -1:]
        base = len(rows) * n_steps
        rows.extend(tab)
        return base

    def entry(args, offset):
        *idx, tbl = args
        return tbl[offset + _linear_step(idx, grid)]

    for job in jobs:
        if isinstance(job, _SideCast):
            r, c = job.src.shape[-2:]
            lead, nl = job.lead, len(job.lead)
            if job.axis == 0:
                assert r % SIDE_BLOCK == 0 and job.padded % SIDE_BLOCK == 0
                n_valid, n_all = r // SIDE_BLOCK, job.padded // SIDE_BLOCK
                base = add_rows([(min(t, n_valid - 1), t, SIDE_COPY if t < n_valid else SIDE_ZERO)
                                 for t in range(n_all)], job.start)
                in_block = block = (SIDE_BLOCK, c)
                out_shape = (job.padded, c)
                omap = lambda *a, base=base: (entry(a, base + n_steps), 0)

                def run(tbl_ref, step, ins, out, base=base):
                    mode = tbl_ref[base + 2 * n_steps + step]

                    @pl.when(mode != SIDE_IDLE)
                    def _():
                        out[...] = jnp.where(mode == SIDE_ZERO, 0.0, ins[0][...]).astype(BF16)
            else:
                assert r % SIDE_ROWS == 0 and c % LANE == 0 and job.padded % job.col_tile == 0
                n_tiles, tile = job.padded // job.col_tile, job.col_tile
                base = add_rows([(t, t, SIDE_COPY) for t in range(r // SIDE_ROWS)], job.start)
                in_block = (SIDE_ROWS, c)
                block, out_shape = (n_tiles, SIDE_ROWS, tile), (n_tiles, r, tile)
                omap = lambda *a, base=base: (0, entry(a, base + n_steps), 0)

                def run(tbl_ref, step, ins, out, base=base, c=c, n_tiles=n_tiles, tile=tile):
                    mode = tbl_ref[base + 2 * n_steps + step]

                    @pl.when(mode != SIDE_IDLE)
                    def _():
                        for kk in range(n_tiles):
                            width = min(tile, c - kk * tile)
                            assert width > 0
                            piece = ins[0][:, kk * tile:kk * tile + width].astype(BF16)
                            if width < tile:
                                piece = jnp.concatenate(
                                    [piece, jnp.zeros((SIDE_ROWS, tile - width), BF16)], axis=1)
                            out[kk] = piece

            imap = lambda *a, base=base, lead=lead: (*lead, entry(a, base), 0)
            in_specs.append(pl.BlockSpec((None,) * nl + in_block, imap))
            operands.append(job.src)
            runners.append((1, run))
        else:
            base = add_rows(_w_in_blocks(), job.start)
            layer = job.layer
            shifts_per_block = W_IN_BLOCK // W_IN_SHIFT
            in_specs.append(pl.BlockSpec(
                (None, W_IN_BLOCK, D_MODEL),
                lambda *a, base=base, layer=layer: (layer, entry(a, base), 0)))
            in_specs.append(pl.BlockSpec(
                (None, W_IN_SHIFT, D_MODEL),
                lambda *a, base=base, layer=layer: (layer, (entry(a, base) + 1) * shifts_per_block, 0)))
            operands += [job.src_t, job.src_t]
            per_tile = PROJ_TILE // W_IN_BLOCK
            block, out_shape = (None, D_MODEL, W_IN_BLOCK), (W_PROJ // PROJ_TILE, D_MODEL, PROJ_TILE)

            def omap(*a, base=base, per_tile=per_tile):
                t = entry(a, base + n_steps)
                return (t // per_tile, 0, t % per_tile)

            def run(tbl_ref, step, ins, out, base=base):
                a_ref, b_ref = ins
                mode = tbl_ref[base + 2 * n_steps + step]

                @pl.when(mode == SIDE_SHIFTED)
                def _():
                    x = jnp.concatenate([a_ref[W_IN_SHIFT:, :], b_ref[...]], axis=0)
                    out[...] = x.astype(BF16).T

                @pl.when(mode == SIDE_COPY)
                def _():
                    out[...] = a_ref[...].astype(BF16).T

                @pl.when(mode == SIDE_LOW_RANK)
                def _():
                    row = lax.broadcasted_iota(jnp.int32, a_ref.shape, 0)
                    out[...] = jnp.where(row < GLA_GATE_RANK, a_ref[...], 0.0).astype(BF16).T

                @pl.when(mode == SIDE_ZERO)
                def _():
                    out[...] = jnp.zeros_like(out)

            runners.append((2, run))
        out_specs.append(pl.BlockSpec(block, omap))
        out_shapes.append(jax.ShapeDtypeStruct(out_shape, BF16))

    table = np.concatenate(rows) if rows else np.zeros((1,), np.int32)
    return _SidePlan(table, in_specs, operands, out_specs, out_shapes, runners, tuple(grid))


def _linear_step(idx, grid):
    step = idx[0]
    for i, n in zip(idx[1:], grid[1:]):
        step = step * n + i
    return step


def _run_sides(plan, tbl_ref, side_ins, side_outs):
    step = _linear_step([pl.program_id(d) for d in range(len(plan.grid))], plan.grid)
    k = 0
    for (n_in, run), out in zip(plan.runners, side_outs):
        run(tbl_ref, step, side_ins[k:k + n_in], out)
        k += n_in


def _hosted_call(kernel, plan, grid, in_specs, out_spec, out_shape, scratch_shapes, name, operands):
    n_side_in = len(plan.in_specs)
    n_main_in = len(in_specs)

    def body(tbl_ref, *refs):
        ins = refs[:n_main_in]
        side_ins = refs[n_main_in:n_main_in + n_side_in]
        out = refs[n_main_in + n_side_in]
        side_outs = refs[n_main_in + n_side_in + 1:n_main_in + n_side_in + 1 + len(plan.out_specs)]
        scratch = refs[n_main_in + n_side_in + 1 + len(plan.out_specs):]
        kernel(*ins, out, *scratch)
        _run_sides(plan, tbl_ref, side_ins, side_outs)

    res = pl.pallas_call(
        body,
        grid_spec=pltpu.PrefetchScalarGridSpec(
            num_scalar_prefetch=1,
            grid=grid,
            in_specs=list(in_specs) + plan.in_specs,
            out_specs=[out_spec] + plan.out_specs,
            scratch_shapes=scratch_shapes,
        ),
        out_shape=[out_shape] + plan.out_shapes,
        compiler_params=_params(*["arbitrary"] * len(grid)),
        name=name,
    )(jnp.asarray(plan.table), *operands, *plan.operands)
    return res[0], res[1:]


def _ffn_kernel(h_ref, g_ref, wg_ref, wu_ref, wd_ref, fg_ref, o_ref, u_scr, *, final, tail):
    j = pl.program_id(1)
    last = pl.num_programs(1) - 1
    tile = wg_ref.shape[1]
    half = h_ref.shape[0] // 2
    halves = (slice(0, half), slice(half, 2 * half))

    def partial_sum(u, width):
        a = _dot(u, wg_ref[:, :width])
        b = _dot(u, wu_ref[:, :width])
        act = (_silu(a) * b).astype(BF16)
        return _dot(act, wd_ref[:width, :])

    @pl.when(j == 0)
    def _():
        for r in halves:
            u = _rms_rows(h_ref[r, :], g_ref[...]).astype(BF16)
            u_scr[r, :] = u
            o_ref[r, :] = partial_sum(u, tile)

    @pl.when((j > 0) & (j < last))
    def _():
        o_ref[...] += partial_sum(u_scr[...], tile)

    @pl.when(j == last)
    def _():
        for r in (halves if final else (slice(None),)):
            y = h_ref[r, :] + FFN_RES * (o_ref[r, :] + partial_sum(u_scr[r, :], tail))
            if final:
                y = _rms_rows(y, fg_ref[...])
            o_ref[r, :] = y


def _ff_padded(tf):
    return pl.cdiv(D_FF, tf) * tf


def _ffn(h, gain, wg, wu, wd, final_gain, final, tiles, sides=()):
    tm, tf = tiles
    assert ROWS % tm == 0 and wg.shape == (_ff_padded(tf) // tf, D_MODEL, tf) and wd.shape[0] == _ff_padded(tf)
    grid = (ROWS // tm, wg.shape[0])
    tail = D_FF - (grid[1] - 1) * tf
    plan = _side_plan(sides, grid)
    if final:
        assert L_PAD % tm == 0
        per_seq = L_PAD // tm
        h = h.reshape(BATCH, L_PAD, D_MODEL)
        rows_spec = pl.BlockSpec((None, tm, D_MODEL), lambda i, j, tbl: (i // per_seq, i % per_seq, 0))
        out_shape = jax.ShapeDtypeStruct((BATCH, SEQ, D_MODEL), F32)
    else:
        rows_spec = pl.BlockSpec((tm, D_MODEL), lambda i, j, tbl: (i, 0))
        out_shape = jax.ShapeDtypeStruct((ROWS, D_MODEL), F32)
    return _hosted_call(
        functools.partial(_ffn_kernel, final=final, tail=tail), plan, grid,
        in_specs=[
            rows_spec,
            pl.BlockSpec((1, D_MODEL), lambda i, j, tbl: (0, 0)),
            pl.BlockSpec((None, D_MODEL, tf), lambda i, j, tbl: (j, 0, 0)),
            pl.BlockSpec((None, D_MODEL, tf), lambda i, j, tbl: (j, 0, 0)),
            pl.BlockSpec((tf, D_MODEL), lambda i, j, tbl: (j, 0)),
            pl.BlockSpec((1, D_MODEL), lambda i, j, tbl: (0, 0)),
        ],
        out_spec=rows_spec,
        out_shape=out_shape,
        scratch_shapes=[pltpu.VMEM((tm, D_MODEL), BF16)],
        name="ffn",
        operands=(h, gain, wg, wu, wd, final_gain),
    )


def _proj_kernel(h_ref, g_ref, w_ref, o_ref, u_scr):
    j = pl.program_id(1)

    @pl.when(j == 0)
    def _():
        half = h_ref.shape[0] // 2
        for r in (slice(0, half), slice(half, 2 * half)):
            u = _rms_rows(h_ref[r, :], g_ref[...]).astype(BF16)
            u_scr[r, :] = u
            o_ref[r, :] = _dot(u, w_ref[...]).astype(o_ref.dtype)

    @pl.when(j > 0)
    def _():
        o_ref[...] = _dot(u_scr[...], w_ref[...]).astype(o_ref.dtype)


def _proj(h, gain, w, sides=()):
    grid = (ROWS // PROJ_ROW_TILE, W_PROJ // PROJ_TILE)
    plan = _side_plan(sides, grid)
    return _hosted_call(
        _proj_kernel, plan, grid,
        in_specs=[
            pl.BlockSpec((PROJ_ROW_TILE, D_MODEL), lambda i, j, tbl: (i, 0)),
            pl.BlockSpec((1, D_MODEL), lambda i, j, tbl: (0, 0)),
            pl.BlockSpec((None, D_MODEL, PROJ_TILE), lambda i, j, tbl: (j, 0, 0)),
        ],
        out_spec=pl.BlockSpec((PROJ_ROW_TILE, PROJ_TILE), lambda i, j, tbl: (i, j)),
        out_shape=jax.ShapeDtypeStruct((ROWS, W_PROJ), BF16),
        scratch_shapes=[pltpu.VMEM((PROJ_ROW_TILE, D_MODEL), BF16)],
        name="in_proj",
        operands=(h, gain, w),
    )


def _decay_tables():
    c = CHUNK
    blocks = []
    r = np.arange(c)
    for l in range(1, N_LEVELS):
        b = 1 << l
        m = np.zeros((c, c), np.float32)
        for i in range(c):
            p = i % (2 * b)
            s = i - p + b
            if p >= b:
                m[i, s:i + 1] = 1.0
            else:
                m[i, i + 1:s] = 1.0
        blocks.append(m)
    blocks.append((r[None, :] <= r[:, None]).astype(np.float32))
    wc = np.concatenate(blocks, axis=0)
    wc = np.concatenate([wc] * 3, axis=1)

    masks = np.zeros((N_LEVELS + 1, c, c), np.float32)
    for l in range(N_LEVELS):
        b = 1 << l
        same = (r[:, None] // (2 * b)) == (r[None, :] // (2 * b))
        up = (r[:, None] & b) != 0
        lo = (r[None, :] & b) == 0
        masks[l] = (same & up & lo).astype(np.float32)
    masks[N_LEVELS] = np.eye(c, dtype=np.float32)
    return wc, masks


def _gla_core(qs, ks, vs, gs, wc_ref, mk_ref, s_ref, slot_of, n_heads, dk, dv):
    c = CHUNK
    seqs = range(len(qs))
    heads = range(n_heads)
    base = (N_LEVELS - 1) * c

    def split3(g):
        g_hi = g.astype(BF16)
        r1 = g - g_hi.astype(F32)
        g_mid = r1.astype(BF16)
        g_lo = (r1 - g_mid.astype(F32)).astype(BF16)
        return jnp.concatenate([g_hi, g_mid, g_lo], axis=0)

    sums = [_dot(wc_ref[...], split3(gs[b])) for b in seqs]
    ex = [jnp.exp(sums[b][:base]) for b in seqs]
    cum = [sums[b][base:base + c] for b in seqs]
    total = [cum[b][c - 1:c] for b in seqs]

    on_diag = [qs[b] * ks[b] for b in seqs]
    below = [qs[b] * jnp.exp(gs[b]) * pltpu.roll(ks[b], 1, axis=0) for b in seqs]
    scores = [[None] * n_heads for _ in seqs]
    for h in heads:
        lanes = slice(h * dk, (h + 1) * dk)
        for b in seqs:
            scores[b][h] = (mk_ref[N_LEVELS] * jnp.sum(on_diag[b][:, lanes], axis=-1, keepdims=True)
                            + mk_ref[0] * jnp.sum(below[b][:, lanes], axis=-1, keepdims=True))
    rows = lax.broadcasted_iota(jnp.int32, (c, 1), 0)
    for l in range(1, N_LEVELS):
        upper = (rows & (1 << l)) != 0
        x = [(jnp.where(upper, qs[b], ks[b]) * ex[b][(l - 1) * c:l * c]).astype(BF16) for b in seqs]
        for h in heads:
            for b in seqs:
                xh = x[b][:, h * dk:(h + 1) * dk]
                scores[b][h] = scores[b][h] + mk_ref[l] * _dot_nt(xh, xh)

    vb = [vs[b].astype(BF16) for b in seqs]
    qt = [(qs[b] * jnp.exp(cum[b])).astype(BF16) for b in seqs]
    kt = [(ks[b] * jnp.exp(total[b] - cum[b])).astype(BF16) for b in seqs]
    tot = [jnp.exp(total[b]) for b in seqs]
    outs = [[None] * n_heads for _ in seqs]
    for h in heads:
        for b in seqs:
            slot = slot_of(b, h)
            st = s_ref[slot]
            vh = vb[b][:, h * dv:(h + 1) * dv]
            outs[b][h] = (_dot(scores[b][h].astype(BF16), vh)
                          + _dot_nt(qt[b][:, h * dk:(h + 1) * dk], st.astype(BF16)))
            s_ref[slot] = st * tot[b][:, h * dk:(h + 1) * dk] + _dot_tn(vh, kt[b][:, h * dk:(h + 1) * dk])
    return outs


def _head_rms(outs, gain, gate):
    dv = outs[0].shape[-1]
    ys = []
    for h, o in enumerate(outs):
        ms = jnp.mean(o * o, axis=-1, keepdims=True)
        ys.append(o * lax.rsqrt(ms + EPS))
    y = jnp.concatenate(ys, axis=-1)
    return y * gain * _silu(gate)


def _log_sigmoid(z):
    return jnp.minimum(z, 0.0) - jnp.log(1.0 + jnp.exp(-jnp.abs(z)))


def _gla_kernel(q_ref, k_ref, v_ref, gate_ref, lr_ref, w2_ref, b_ref, norm_ref, wc_ref, mk_ref,
                o_ref, s_ref):
    @pl.when(pl.program_id(0) == 0)
    def _():
        s_ref[...] = jnp.zeros_like(s_ref)

    seqs = range(BATCH)
    z = [_dot(lr_ref[b], w2_ref[...]) + b_ref[...] for b in seqs]
    g = [_log_sigmoid(z[b]) * (1.0 / GLA_GATE_TAU) for b in seqs]
    q = [q_ref[b].astype(F32) * (GLA_DK ** -0.5) for b in seqs]
    k = [k_ref[b].astype(F32) for b in seqs]
    v = [v_ref[b].astype(F32) for b in seqs]
    outs = _gla_core(q, k, v, g, wc_ref, mk_ref, s_ref, lambda b, h: b * GLA_HEADS + h,
                     GLA_HEADS, GLA_DK, GLA_DV)
    for b in seqs:
        o_ref[b] = _head_rms(outs[b], norm_ref[...], gate_ref[b].astype(F32)).astype(o_ref.dtype)


def _hgrn_kernel(q_ref, f_ref, i_ref, gate_ref, lb_ref, norm_ref, wc_ref, mk_ref, o_ref, s_ref, *, layer):
    @pl.when(pl.program_id(0) == 0)
    def _():
        s_ref[...] = jnp.zeros_like(s_ref)

    logits = lb_ref[...]
    e = jnp.exp(logits - jnp.max(logits, axis=0, keepdims=True))
    soft = e / jnp.sum(e, axis=0, keepdims=True)
    lb = jnp.zeros((1, logits.shape[1]), F32)
    for l in range(1, layer + 1):
        lb = lb + soft[l:l + 1]

    seqs = range(BATCH)
    group = HGRN_HEADS // 2
    for first in range(0, HGRN_HEADS, group):
        qk = slice(first * HGRN_DK, (first + group) * HGRN_DK)
        vv = slice(first * HGRN_DV, (first + group) * HGRN_DV)
        lbg = lb[:, qk]
        hf = [f_ref[b, :, qk].astype(F32) for b in seqs]
        forget = [lbg + (1.0 - lbg) * _sigmoid(hf[b]) for b in seqs]
        g = [jnp.log(jnp.maximum(forget[b], FORGET_FLOOR)) for b in seqs]
        k = [1.0 - forget[b] for b in seqs]
        v = [_silu(i_ref[b, :, vv].astype(F32)) for b in seqs]
        q = [q_ref[b, :, qk].astype(F32) for b in seqs]
        outs = _gla_core(q, k, v, g, wc_ref, mk_ref, s_ref,
                         lambda b, h, first=first: b * HGRN_HEADS + first + h, group, HGRN_DK, HGRN_DV)
        for b in seqs:
            y = _head_rms(outs[b], norm_ref[:, vv], gate_ref[b, :, vv].astype(F32))
            o_ref[b, :, vv] = y.astype(o_ref.dtype)


def _chunk_block(c):
    return (c + N_CHUNKS - 1) % N_CHUNKS


def _col_spec(width, offset):
    assert offset % width == 0
    blk = offset // width
    return pl.BlockSpec((BATCH, CHUNK, width), lambda c, tbl: (0, _chunk_block(c), blk))


def _const_spec(shape):
    nd = len(shape)
    return pl.BlockSpec(shape, lambda c, tbl: (0,) * nd)


def _mixer_call(kernel, name, width, n_heads, dv, dk, in_specs, operands, sides):
    grid = (N_CHUNKS,)
    return _hosted_call(
        kernel, _side_plan(sides, grid), grid,
        in_specs=in_specs,
        out_spec=pl.BlockSpec((BATCH, CHUNK, width), lambda c, tbl: (0, _chunk_block(c), 0)),
        out_shape=jax.ShapeDtypeStruct((BATCH, L_PAD, width), BF16),
        scratch_shapes=[pltpu.VMEM((BATCH * n_heads, dv, dk), F32)],
        name=name,
        operands=operands,
    )


def _gla(proj, w2, bias, norm, wc, masks, sides=()):
    qk = GLA_HEADS * GLA_DK
    vw = GLA_HEADS * GLA_DV
    return _mixer_call(
        _gla_kernel, "gla", vw, GLA_HEADS, GLA_DV, GLA_DK,
        in_specs=[
            _col_spec(qk, OFF_GLA),
            _col_spec(qk, OFF_GLA + qk),
            _col_spec(vw, OFF_GLA + 2 * qk),
            _col_spec(vw, OFF_GLA + 2 * qk + vw),
            _col_spec(LANE, OFF_LR),
            _const_spec(w2.shape),
            _const_spec(bias.shape),
            _const_spec(norm.shape),
            _const_spec(wc.shape),
            _const_spec(masks.shape),
        ],
        operands=(proj, proj, proj, proj, proj, w2, bias, norm, wc, masks),
        sides=sides,
    )


def _hgrn(proj, lb_logits, norm, wc, masks, layer, sides=()):
    w = HGRN_HEADS * HGRN_DK
    return _mixer_call(
        functools.partial(_hgrn_kernel, layer=layer), "hgrn", w, HGRN_HEADS, HGRN_DV, HGRN_DK,
        in_specs=[
            _col_spec(w, OFF_HGRN),
            _col_spec(w, OFF_HGRN + w),
            _col_spec(w, OFF_HGRN + 2 * w),
            _col_spec(w, OFF_HGRN + 3 * w),
            _const_spec(lb_logits.shape),
            _const_spec(norm.shape),
            _const_spec(wc.shape),
            _const_spec(masks.shape),
        ],
        operands=(proj, proj, proj, proj, lb_logits, norm, wc, masks),
        sides=sides,
    )


def _ret_kernel(q_ref, k_ref, v_ref, gate_ref, cos_ref, sin_ref, qd_ref, kd_ref, cd_ref, im_ref, norm_ref,
                o_ref, s_ref):
    @pl.when(pl.program_id(0) == 0)
    def _():
        s_ref[...] = jnp.zeros_like(s_ref)

    half = RET_DK // 2
    cos = cos_ref[...]
    sin = sin_ref[...]
    seqs = range(BATCH)
    pairs = [(b, h) for h in range(RET_HEADS) for b in seqs]

    def rotate(ref, b, h):
        lo = h * RET_DK
        t1 = ref[b, :, lo:lo + half].astype(F32)
        t2 = ref[b, :, lo + half:lo + RET_DK].astype(F32)
        return jnp.concatenate([t1 * cos - t2 * sin, t1 * sin + t2 * cos], axis=-1)

    qr = {p: rotate(q_ref, *p) * (RET_DK ** -0.5) for p in pairs}
    kr = {p: rotate(k_ref, *p) for p in pairs}
    scores = {p: _dot_nt(qr[p].astype(BF16), kr[p].astype(BF16)) * im_ref[p[1]] for p in pairs}
    outs = {}
    for b, h in pairs:
        p = (b, h)
        vh = v_ref[b, :, h * RET_DV:(h + 1) * RET_DV]
        st = s_ref[b * RET_HEADS + h]
        outs[p] = (_dot(scores[p].astype(BF16), vh)
                   + _dot_nt((qr[p] * qd_ref[h]).astype(BF16), st.astype(BF16)))
        s_ref[b * RET_HEADS + h] = st * cd_ref[h] + _dot_tn(vh, (kr[p] * kd_ref[h]).astype(BF16))
    for b in seqs:
        ys = []
        for h in range(RET_HEADS):
            o = outs[(b, h)]
            o = o - jnp.mean(o, axis=-1, keepdims=True)
            ms = jnp.mean(o * o, axis=-1, keepdims=True)
            ys.append(o * lax.rsqrt(ms + EPS))
        y = jnp.concatenate(ys, axis=-1)
        o_ref[b] = (y * norm_ref[...] * _silu(gate_ref[b].astype(F32))).astype(o_ref.dtype)


def _ret(proj, cos, sin, qd, kd, cd, im, norm, sides=()):
    w = RET_HEADS * RET_DK
    half = RET_DK // 2
    return _mixer_call(
        _ret_kernel, "retention", w, RET_HEADS, RET_DV, RET_DK,
        in_specs=[
            _col_spec(w, OFF_RET),
            _col_spec(w, OFF_RET + w),
            _col_spec(w, OFF_RET + 2 * w),
            _col_spec(w, OFF_RET + 3 * w),
            pl.BlockSpec((CHUNK, half), lambda c, tbl: (c, 0)),
            pl.BlockSpec((CHUNK, half), lambda c, tbl: (c, 0)),
            _const_spec(qd.shape),
            _const_spec(kd.shape),
            _const_spec(cd.shape),
            _const_spec(im.shape),
            _const_spec(norm.shape),
        ],
        operands=(proj, proj, proj, proj, cos, sin, qd, kd, cd, im, norm),
        sides=sides,
    )


def _ret_tables():
    f32 = jnp.float32
    half = RET_DK // 2
    pos = jnp.arange(L_PAD, dtype=f32) - PAD
    inv_freq = ROPE_BASE ** (-jnp.arange(half, dtype=f32) / half)
    ang = pos[:, None] * inv_freq[None, :]
    log_gamma = jnp.log(1.0 - 2.0 ** (-5.0 - jnp.arange(RET_HEADS, dtype=f32)))
    idx = jnp.arange(CHUNK, dtype=f32)
    rel = idx[:, None] - idx[None, :]
    causal = (rel >= 0)[None]
    intra = jnp.where(causal, jnp.exp(jnp.where(causal, rel[None], 0.0) * log_gamma[:, None, None]), 0.0)
    q_decay = jnp.exp((idx[None, :] + 1.0) * log_gamma[:, None])[..., None]
    k_decay = jnp.exp((CHUNK - 1.0 - idx[None, :]) * log_gamma[:, None])[..., None]
    chunk_decay = jnp.exp(CHUNK * log_gamma)[:, None, None]
    return jnp.cos(ang), jnp.sin(ang), q_decay, k_decay, chunk_decay, intra


def _merge_kernel(y0_ref, y1_ref, y2_ref, mg_ref, wb_ref, wo_ref, h_ref, o_ref):
    merged = None
    for n, y_ref in enumerate((y0_ref, y1_ref, y2_ref)):
        gate = _sigmoid(mg_ref[:, n * D_MODEL:(n + 1) * D_MODEL].astype(F32))
        t = _dot(y_ref[...], wb_ref[n * BRANCH_WIDTH:(n + 1) * BRANCH_WIDTH, :]) * gate
        merged = t if merged is None else merged + t
    o_ref[...] = h_ref[...] + _dot(merged.astype(BF16), wo_ref[...])


def _merge(ys, proj, wb, wo, h):
    assert OFF_MG == 0
    rows = lambda width: pl.BlockSpec((MERGE_TILE, width), lambda i: (i, 0))
    whole = lambda a: pl.BlockSpec(a.shape, lambda i: (0, 0))
    return pl.pallas_call(
        _merge_kernel,
        grid=(ROWS // MERGE_TILE,),
        in_specs=[rows(BRANCH_WIDTH)] * N_BRANCH + [rows(N_BRANCH * D_MODEL)]
        + [whole(wb), whole(wo), rows(D_MODEL)],
        out_specs=rows(D_MODEL),
        out_shape=jax.ShapeDtypeStruct((ROWS, D_MODEL), F32),
        compiler_params=_params("parallel"),
        name="merge",
    )(*ys, proj, wb, wo, h)


def kernel(x, meta_tokens, ffn1_norm, ffn1_w_gate, ffn1_w_up, ffn1_w_down, mix_norm, w_in, gla_w_gate2, gla_b_gate, gla_norm, ret_norm, hgrn_lb_logits, hgrn_norm, w_branch, w_out, ffn2_norm, ffn2_w_gate, ffn2_w_up, ffn2_w_down, final_norm):
    b = x.shape[0]
    meta = jnp.broadcast_to(meta_tokens[None].astype(x.dtype), (b, N_META, D_MODEL))
    h = jnp.concatenate([x, jnp.zeros((b, PAD, D_MODEL), x.dtype), meta], axis=1).reshape(ROWS, D_MODEL)

    wc_np, masks_np = _decay_tables()
    wc = jnp.asarray(wc_np, BF16)
    masks = jnp.asarray(masks_np, F32)
    cos, sin, qd, kd, cd, im = _ret_tables()

    def ffn_casts(wg, wu, wd, layer, tiles):
        padded = _ff_padded(tiles[1])
        return (_SideCast(wg, (layer,), 1, padded, 0, tiles[1]), _SideCast(wu, (layer,), 1, padded, 0, tiles[1]),
                _SideCast(wd, (layer,), 0, padded, 0))

    row = lambda v: v.reshape(1, -1).astype(F32)
    final_gain = row(final_norm)
    w_in_t = jnp.swapaxes(w_in, 1, 2)
    wb_rows = w_branch.reshape(DEPTH, N_BRANCH * BRANCH_WIDTH, D_MODEL)
    padded = _ff_padded(FFN1_TILES[1])
    ffn1_w = (_cast(ffn1_w_gate, (0,), 1, padded), _cast(ffn1_w_up, (0,), 1, padded),
              _cast(ffn1_w_down, (0,), 0, padded))
    for layer in range(DEPTH):
        last = layer == DEPTH - 1
        tiles2 = FINAL_TILES if last else FFN2_TILES
        h, (w_proj,) = _ffn(h, row(ffn1_norm[layer]), *ffn1_w, final_gain, False, FFN1_TILES,
                            sides=[_SideWIn(w_in_t, layer, 0)])
        proj, _ = _proj(h, row(mix_norm[layer]), w_proj)
        jobs = [[j] for j in ffn_casts(ffn2_w_gate, ffn2_w_up, ffn2_w_down, layer, tiles2)]
        jobs[0].append(_SideCast(wb_rows, (layer,), 0, N_BRANCH * BRANCH_WIDTH, 0))
        jobs[1].append(_SideCast(w_out, (layer,), 0, D_MODEL, 0))
        if not last:
            for mine, j in zip(jobs, ffn_casts(ffn1_w_gate, ffn1_w_up, ffn1_w_down, layer + 1, FFN1_TILES)):
                mine.append(j)
        w2 = jnp.pad(gla_w_gate2[layer], ((0, LANE - GLA_GATE_RANK), (0, 0))).astype(BF16)
        proj_seq = proj.reshape(b, L_PAD, W_PROJ)
        y_gla, (wg2, wb, *wg1) = _gla(proj_seq, w2, row(gla_b_gate[layer]), row(gla_norm[layer]), wc, masks,
                                      sides=jobs[0])
        y_ret, (wu2, wo, *wu1) = _ret(proj_seq, cos, sin, qd, kd, cd, im, row(ret_norm[layer]), sides=jobs[1])
        y_hgrn, (wd2, *wd1) = _hgrn(proj_seq, hgrn_lb_logits.astype(F32), row(hgrn_norm[layer]), wc, masks,
                                    layer, sides=jobs[2])
        ys = [y.reshape(ROWS, BRANCH_WIDTH) for y in (y_gla, y_ret, y_hgrn)]
        h = _merge(ys, proj, wb, wo, h)
        h, _ = _ffn(h, row(ffn2_norm[layer]), wg2, wu2, wd2, final_gain, last, tiles2)
        if not last:
            ffn1_w = (*wg1, *wu1, *wd1)
    return h
```

```python
import functools
from typing import NamedTuple

import numpy as np
import jax
import jax.numpy as jnp
from jax import lax
from jax.experimental import pallas as pl
from jax.experimental.pallas import tpu as pltpu

D_MODEL = 2048
BATCH = 2
SEQ = 4096
DEPTH = 2
N_META = 16
CHUNK = 64
PAD = CHUNK - N_META
D_FF = 5504
FFN_RES = 0.5
EPS = 1e-6
N_BRANCH = 3
BRANCH_WIDTH = 1024

GLA_HEADS, GLA_DK, GLA_DV = 4, 128, 256
GLA_GATE_RANK = 16
GLA_GATE_TAU = 16.0
RET_HEADS, RET_DK, RET_DV = 4, 256, 256
ROPE_BASE = 10000.0
HGRN_HEADS, HGRN_DK, HGRN_DV = 8, 128, 128
FORGET_FLOOR = 1e-20

LANE = 128
L_PAD = PAD + N_META + SEQ
N_CHUNKS = L_PAD // CHUNK
ROWS = BATCH * L_PAD
N_LEVELS = 6
assert 1 << N_LEVELS == CHUNK

FFN1_TILES = (640, 512)
FFN2_TILES = (640, 512)
FINAL_TILES = (832, 512)
PROJ_ROW_TILE = 832
PROJ_TILE = 1792
MERGE_TILE = 320
CAST_TILE = 512
SIDE_BLOCK = LANE
SIDE_ROWS = 32
W_IN_BLOCK = 256
LR_PAD = CAST_TILE
W_IN_SHIFT = GLA_GATE_RANK

OFF_MG = 0
OFF_GLA = OFF_MG + N_BRANCH * D_MODEL
OFF_RET = OFF_GLA + 2 * GLA_HEADS * GLA_DK + 2 * GLA_HEADS * GLA_DV
OFF_HGRN = OFF_RET + 2 * RET_HEADS * RET_DK + 2 * RET_HEADS * RET_DV
OFF_LR = OFF_HGRN + 2 * HGRN_HEADS * HGRN_DK + 2 * HGRN_HEADS * HGRN_DV
W_PROJ = OFF_LR + LR_PAD

VMEM_LIMIT = 56 * 1024 * 1024

F32 = jnp.float32
BF16 = jnp.bfloat16


def _params(*sem):
    return pltpu.CompilerParams(dimension_semantics=sem, vmem_limit_bytes=VMEM_LIMIT)


def _rms_rows(x, gain):
    ms = jnp.mean(x * x, axis=-1, keepdims=True)
    return x * lax.rsqrt(ms + EPS) * gain


def _sigmoid(x):
    return 1.0 / (1.0 + jnp.exp(-x))


def _silu(x):
    return x * _sigmoid(x)


def _dot(a, b):
    return jnp.dot(a, b, preferred_element_type=F32)


def _dot_nt(a, b):
    return lax.dot_general(a, b, (((1,), (1,)), ((), ())), preferred_element_type=F32)


def _dot_tn(a, b):
    return lax.dot_general(a, b, (((0,), (0,)), ((), ())), preferred_element_type=F32)


def _cast_kernel(w_ref, o_ref, *, axis, valid):
    x = w_ref[...]
    idx = pl.program_id(0) * x.shape[axis] + lax.broadcasted_iota(jnp.int32, x.shape, axis)
    o_ref[...] = jnp.where(idx < valid, x, 0.0).astype(BF16)


def _cast(w, lead, axis, padded):
    rows, cols = w.shape[-2:]
    nl = len(lead)
    if axis == 0:
        block, out_shape, grid = (CAST_TILE, cols), (padded, cols), padded // CAST_TILE
        imap = lambda j: (*lead, j, 0)
        omap = lambda j: (j, 0)
    else:
        block, out_shape, grid = (rows, CAST_TILE), (rows, padded), padded // CAST_TILE
        imap = lambda j: (*lead, 0, j)
        omap = lambda j: (0, j)
    return pl.pallas_call(
        functools.partial(_cast_kernel, axis=axis, valid=w.shape[-2 + axis]),
        grid=(grid,),
        in_specs=[pl.BlockSpec((None,) * nl + block, imap)],
        out_specs=pl.BlockSpec(block, omap),
        out_shape=jax.ShapeDtypeStruct(out_shape, BF16),
        compiler_params=_params("parallel"),
        name="cast",
    )(w)


SIDE_IDLE, SIDE_COPY, SIDE_ZERO, SIDE_SHIFTED, SIDE_LOW_RANK = 0, 1, 2, 3, 4


class _SideCast(NamedTuple):
    src: jax.Array
    lead: tuple
    axis: int
    padded: int
    start: int


class _SideWIn(NamedTuple):
    src_t: jax.Array
    layer: int
    start: int


class _SidePlan(NamedTuple):
    table: np.ndarray
    in_specs: list
    operands: list
    out_specs: list
    out_shapes: list
    runners: list
    grid: tuple


def _w_in_blocks():
    per = lambda width: width // W_IN_BLOCK
    src = lambda col: col // W_IN_BLOCK
    gla_w = 2 * GLA_HEADS * GLA_DK + 2 * GLA_HEADS * GLA_DV
    ret_w = 2 * RET_HEADS * RET_DK + 2 * RET_HEADS * RET_DV
    hgrn_w = 2 * HGRN_HEADS * HGRN_DK + 2 * HGRN_HEADS * HGRN_DV
    lr_col = gla_w
    blocks = []
    for t in range(per(N_BRANCH * D_MODEL)):
        blocks.append((src(lr_col + ret_w + hgrn_w) + t, SIDE_SHIFTED))
    for t in range(per(gla_w)):
        blocks.append((t, SIDE_COPY))
    for t in range(per(ret_w + hgrn_w)):
        blocks.append((src(lr_col) + t, SIDE_SHIFTED))
    blocks.append((src(lr_col), SIDE_LOW_RANK))
    blocks += [(src(lr_col), SIDE_ZERO)] * (per(LR_PAD) - 1)
    assert len(blocks) == per(W_PROJ)
    return [(s, t, mode) for t, (s, mode) in enumerate(blocks)]


def _side_plan(jobs, grid):
    n_steps = int(np.prod(grid))
    rows, in_specs, operands, out_specs, out_shapes, runners = [], [], [], [], [], []

    def add_rows(blocks, start):
        n = len(blocks)
        assert start + n <= n_steps
        arr = np.asarray(blocks, np.int32).T
        tab = np.zeros((3, n_steps), np.int32)
        tab[:2, :start] = arr[:2, :1]
        tab[:, start:start + n] = arr
        tab[:2, start + n:] = arr[:2, -1:]
        base = len(rows) * n_steps
        rows.extend(tab)
        return base

    def entry(args, offset):
        *idx, tbl = args
        return tbl[offset + _linear_step(idx, grid)]

    for job in jobs:
        if isinstance(job, _SideCast):
            r, c = job.src.shape[-2:]
            lead, nl = job.lead, len(job.lead)
            if job.axis == 0:
                assert r % SIDE_BLOCK == 0 and job.padded % SIDE_BLOCK == 0
                n_valid, n_all = r // SIDE_BLOCK, job.padded // SIDE_BLOCK
                base = add_rows([(min(t, n_valid - 1), t, SIDE_COPY if t < n_valid else SIDE_ZERO)
                                 for t in range(n_all)], job.start)
                in_block = block = (SIDE_BLOCK, c)
                out_shape = (job.padded, c)
                omap = lambda *a, base=base: (entry(a, base + n_steps), 0)

                def run(tbl_ref, step, ins, out, base=base):
                    mode = tbl_ref[base + 2 * n_steps + step]

                    @pl.when(mode != SIDE_IDLE)
                    def _():
                        out[...] = jnp.where(mode == SIDE_ZERO, 0.0, ins[0][...]).astype(BF16)
            else:
                assert r % SIDE_ROWS == 0 and c % LANE == 0 and job.padded % LANE == 0
                base = add_rows([(t, t, SIDE_COPY) for t in range(r // SIDE_ROWS)], job.start)
                in_block = (SIDE_ROWS, c)
                block, out_shape = (SIDE_ROWS, job.padded), (r, job.padded)
                omap = lambda *a, base=base: (entry(a, base + n_steps), 0)

                def run(tbl_ref, step, ins, out, base=base, c=c, padded=job.padded):
                    mode = tbl_ref[base + 2 * n_steps + step]

                    @pl.when(mode != SIDE_IDLE)
                    def _():
                        out[:, :c] = ins[0][...].astype(BF16)
                        if padded > c:
                            out[:, c:] = jnp.zeros((SIDE_ROWS, padded - c), BF16)

            imap = lambda *a, base=base, lead=lead: (*lead, entry(a, base), 0)
            in_specs.append(pl.BlockSpec((None,) * nl + in_block, imap))
            operands.append(job.src)
            runners.append((1, run))
        else:
            base = add_rows(_w_in_blocks(), job.start)
            layer = job.layer
            shifts_per_block = W_IN_BLOCK // W_IN_SHIFT
            in_specs.append(pl.BlockSpec(
                (None, W_IN_BLOCK, D_MODEL),
                lambda *a, base=base, layer=layer: (layer, entry(a, base), 0)))
            in_specs.append(pl.BlockSpec(
                (None, W_IN_SHIFT, D_MODEL),
                lambda *a, base=base, layer=layer: (layer, (entry(a, base) + 1) * shifts_per_block, 0)))
            operands += [job.src_t, job.src_t]
            per_tile = PROJ_TILE // W_IN_BLOCK
            block, out_shape = (None, D_MODEL, W_IN_BLOCK), (W_PROJ // PROJ_TILE, D_MODEL, PROJ_TILE)

            def omap(*a, base=base, per_tile=per_tile):
                t = entry(a, base + n_steps)
                return (t // per_tile, 0, t % per_tile)

            def run(tbl_ref, step, ins, out, base=base):
                a_ref, b_ref = ins
                mode = tbl_ref[base + 2 * n_steps + step]

                @pl.when(mode == SIDE_SHIFTED)
                def _():
                    x = jnp.concatenate([a_ref[W_IN_SHIFT:, :], b_ref[...]], axis=0)
                    out[...] = x.astype(BF16).T

                @pl.when(mode == SIDE_COPY)
                def _():
                    out[...] = a_ref[...].astype(BF16).T

                @pl.when(mode == SIDE_LOW_RANK)
                def _():
                    row = lax.broadcasted_iota(jnp.int32, a_ref.shape, 0)
                    out[...] = jnp.where(row < GLA_GATE_RANK, a_ref[...], 0.0).astype(BF16).T

                @pl.when(mode == SIDE_ZERO)
                def _():
                    out[...] = jnp.zeros_like(out)

            runners.append((2, run))
        out_specs.append(pl.BlockSpec(block, omap))
        out_shapes.append(jax.ShapeDtypeStruct(out_shape, BF16))

    table = np.concatenate(rows) if rows else np.zeros((1,), np.int32)
    return _SidePlan(table, in_specs, operands, out_specs, out_shapes, runners, tuple(grid))


def _linear_step(idx, grid):
    step = idx[0]
    for i, n in zip(idx[1:], grid[1:]):
        step = step * n + i
    return step


def _run_sides(plan, tbl_ref, side_ins, side_outs):
    step = _linear_step([pl.program_id(d) for d in range(len(plan.grid))], plan.grid)
    k = 0
    for (n_in, run), out in zip(plan.runners, side_outs):
        run(tbl_ref, step, side_ins[k:k + n_in], out)
        k += n_in


def _hosted_call(kernel, plan, grid, in_specs, out_spec, out_shape, scratch_shapes, name, operands):
    n_side_in = len(plan.in_specs)
    n_main_in = len(in_specs)

    def body(tbl_ref, *refs):
        ins = refs[:n_main_in]
        side_ins = refs[n_main_in:n_main_in + n_side_in]
        out = refs[n_main_in + n_side_in]
        side_outs = refs[n_main_in + n_side_in + 1:n_main_in + n_side_in + 1 + len(plan.out_specs)]
        scratch = refs[n_main_in + n_side_in + 1 + len(plan.out_specs):]
        kernel(*ins, out, *scratch)
        _run_sides(plan, tbl_ref, side_ins, side_outs)

    res = pl.pallas_call(
        body,
        grid_spec=pltpu.PrefetchScalarGridSpec(
            num_scalar_prefetch=1,
            grid=grid,
            in_specs=list(in_specs) + plan.in_specs,
            out_specs=[out_spec] + plan.out_specs,
            scratch_shapes=scratch_shapes,
        ),
        out_shape=[out_shape] + plan.out_shapes,
        compiler_params=_params(*["arbitrary"] * len(grid)),
        name=name,
    )(jnp.asarray(plan.table), *operands, *plan.operands)
    return res[0], res[1:]


def _ffn_kernel(h_ref, g_ref, wg_ref, wu_ref, wd_ref, fg_ref, o_ref, u_scr, *, final, tail):
    j = pl.program_id(1)
    last = pl.num_programs(1) - 1
    tile = wg_ref.shape[1]
    half = h_ref.shape[0] // 2
    halves = (slice(0, half), slice(half, 2 * half))

    def partial_sum(u, width):
        a = _dot(u, wg_ref[:, :width])
        b = _dot(u, wu_ref[:, :width])
        act = (_silu(a) * b).astype(BF16)
        return _dot(act, wd_ref[:width, :])

    @pl.when(j == 0)
    def _():
        for r in halves:
            u = _rms_rows(h_ref[r, :], g_ref[...]).astype(BF16)
            u_scr[r, :] = u
            o_ref[r, :] = partial_sum(u, tile)

    @pl.when((j > 0) & (j < last))
    def _():
        o_ref[...] += partial_sum(u_scr[...], tile)

    @pl.when(j == last)
    def _():
        for r in (halves if final else (slice(None),)):
            y = h_ref[r, :] + FFN_RES * (o_ref[r, :] + partial_sum(u_scr[r, :], tail))
            if final:
                y = _rms_rows(y, fg_ref[...])
            o_ref[r, :] = y


def _ff_padded(tf):
    return pl.cdiv(D_FF, tf) * tf


def _ffn(h, gain, wg, wu, wd, final_gain, final, tiles, sides=()):
    tm, tf = tiles
    assert ROWS % tm == 0 and wg.shape[1] == wd.shape[0] == _ff_padded(tf)
    grid = (ROWS // tm, wg.shape[1] // tf)
    tail = D_FF - (grid[1] - 1) * tf
    plan = _side_plan(sides, grid)
    if final:
        assert L_PAD % tm == 0
        per_seq = L_PAD // tm
        h = h.reshape(BATCH, L_PAD, D_MODEL)
        rows_spec = pl.BlockSpec((None, tm, D_MODEL), lambda i, j, tbl: (i // per_seq, i % per_seq, 0))
        out_shape = jax.ShapeDtypeStruct((BATCH, SEQ, D_MODEL), F32)
    else:
        rows_spec = pl.BlockSpec((tm, D_MODEL), lambda i, j, tbl: (i, 0))
        out_shape = jax.ShapeDtypeStruct((ROWS, D_MODEL), F32)
    return _hosted_call(
        functools.partial(_ffn_kernel, final=final, tail=tail), plan, grid,
        in_specs=[
            rows_spec,
            pl.BlockSpec((1, D_MODEL), lambda i, j, tbl: (0, 0)),
            pl.BlockSpec((D_MODEL, tf), lambda i, j, tbl: (0, j)),
            pl.BlockSpec((D_MODEL, tf), lambda i, j, tbl: (0, j)),
            pl.BlockSpec((tf, D_MODEL), lambda i, j, tbl: (j, 0)),
            pl.BlockSpec((1, D_MODEL), lambda i, j, tbl: (0, 0)),
        ],
        out_spec=rows_spec,
        out_shape=out_shape,
        scratch_shapes=[pltpu.VMEM((tm, D_MODEL), BF16)],
        name="ffn",
        operands=(h, gain, wg, wu, wd, final_gain),
    )


def _proj_kernel(h_ref, g_ref, w_ref, o_ref, u_scr):
    j = pl.program_id(1)

    @pl.when(j == 0)
    def _():
        half = h_ref.shape[0] // 2
        for r in (slice(0, half), slice(half, 2 * half)):
            u = _rms_rows(h_ref[r, :], g_ref[...]).astype(BF16)
            u_scr[r, :] = u
            o_ref[r, :] = _dot(u, w_ref[...]).astype(o_ref.dtype)

    @pl.when(j > 0)
    def _():
        o_ref[...] = _dot(u_scr[...], w_ref[...]).astype(o_ref.dtype)


def _proj(h, gain, w, sides=()):
    grid = (ROWS // PROJ_ROW_TILE, W_PROJ // PROJ_TILE)
    plan = _side_plan(sides, grid)
    return _hosted_call(
        _proj_kernel, plan, grid,
        in_specs=[
            pl.BlockSpec((PROJ_ROW_TILE, D_MODEL), lambda i, j, tbl: (i, 0)),
            pl.BlockSpec((1, D_MODEL), lambda i, j, tbl: (0, 0)),
            pl.BlockSpec((None, D_MODEL, PROJ_TILE), lambda i, j, tbl: (j, 0, 0)),
        ],
        out_spec=pl.BlockSpec((PROJ_ROW_TILE, PROJ_TILE), lambda i, j, tbl: (i, j)),
        out_shape=jax.ShapeDtypeStruct((ROWS, W_PROJ), BF16),
        scratch_shapes=[pltpu.VMEM((PROJ_ROW_TILE, D_MODEL), BF16)],
        name="in_proj",
        operands=(h, gain, w),
    )


def _decay_tables():
    c = CHUNK
    blocks = []
    r = np.arange(c)
    for l in range(1, N_LEVELS):
        b = 1 << l
        m = np.zeros((c, c), np.float32)
        for i in range(c):
            p = i % (2 * b)
            s = i - p + b
            if p >= b:
                m[i, s:i + 1] = 1.0
            else:
                m[i, i + 1:s] = 1.0
        blocks.append(m)
    blocks.append((r[None, :] <= r[:, None]).astype(np.float32))
    wc = np.concatenate(blocks, axis=0)
    wc = np.concatenate([wc] * 3, axis=1)

    masks = np.zeros((N_LEVELS + 1, c, c), np.float32)
    for l in range(N_LEVELS):
        b = 1 << l
        same = (r[:, None] // (2 * b)) == (r[None, :] // (2 * b))
        up = (r[:, None] & b) != 0
        lo = (r[None, :] & b) == 0
        masks[l] = (same & up & lo).astype(np.float32)
    masks[N_LEVELS] = np.eye(c, dtype=np.float32)
    return wc, masks


def _gla_core(qs, ks, vs, gs, wc_ref, mk_ref, s_ref, slot_of, n_heads, dk, dv):
    c = CHUNK
    seqs = range(len(qs))
    heads = range(n_heads)
    base = (N_LEVELS - 1) * c

    def split3(g):
        g_hi = g.astype(BF16)
        r1 = g - g_hi.astype(F32)
        g_mid = r1.astype(BF16)
        g_lo = (r1 - g_mid.astype(F32)).astype(BF16)
        return jnp.concatenate([g_hi, g_mid, g_lo], axis=0)

    sums = [_dot(wc_ref[...], split3(gs[b])) for b in seqs]
    ex = [jnp.exp(sums[b][:base]) for b in seqs]
    cum = [sums[b][base:base + c] for b in seqs]
    total = [cum[b][c - 1:c] for b in seqs]

    on_diag = [qs[b] * ks[b] for b in seqs]
    below = [qs[b] * jnp.exp(gs[b]) * pltpu.roll(ks[b], 1, axis=0) for b in seqs]
    scores = [[None] * n_heads for _ in seqs]
    for h in heads:
        lanes = slice(h * dk, (h + 1) * dk)
        for b in seqs:
            scores[b][h] = (mk_ref[N_LEVELS] * jnp.sum(on_diag[b][:, lanes], axis=-1, keepdims=True)
                            + mk_ref[0] * jnp.sum(below[b][:, lanes], axis=-1, keepdims=True))
    rows = lax.broadcasted_iota(jnp.int32, (c, 1), 0)
    for l in range(1, N_LEVELS):
        upper = (rows & (1 << l)) != 0
        x = [(jnp.where(upper, qs[b], ks[b]) * ex[b][(l - 1) * c:l * c]).astype(BF16) for b in seqs]
        for h in heads:
            for b in seqs:
                xh = x[b][:, h * dk:(h + 1) * dk]
                scores[b][h] = scores[b][h] + mk_ref[l] * _dot_nt(xh, xh)

    vb = [vs[b].astype(BF16) for b in seqs]
    qt = [(qs[b] * jnp.exp(cum[b])).astype(BF16) for b in seqs]
    kt = [(ks[b] * jnp.exp(total[b] - cum[b])).astype(BF16) for b in seqs]
    tot = [jnp.exp(total[b]) for b in seqs]
    outs = [[None] * n_heads for _ in seqs]
    for h in heads:
        for b in seqs:
            slot = slot_of(b, h)
            st = s_ref[slot]
            vh = vb[b][:, h * dv:(h + 1) * dv]
            outs[b][h] = (_dot(scores[b][h].astype(BF16), vh)
                          + _dot_nt(qt[b][:, h * dk:(h + 1) * dk], st.astype(BF16)))
            s_ref[slot] = st * tot[b][:, h * dk:(h + 1) * dk] + _dot_tn(vh, kt[b][:, h * dk:(h + 1) * dk])
    return outs


def _head_rms(outs, gain, gate):
    dv = outs[0].shape[-1]
    ys = []
    for h, o in enumerate(outs):
        ms = jnp.mean(o * o, axis=-1, keepdims=True)
        ys.append(o * lax.rsqrt(ms + EPS))
    y = jnp.concatenate(ys, axis=-1)
    return y * gain * _silu(gate)


def _log_sigmoid(z):
    return jnp.minimum(z, 0.0) - jnp.log(1.0 + jnp.exp(-jnp.abs(z)))


def _gla_kernel(q_ref, k_ref, v_ref, gate_ref, lr_ref, w2_ref, b_ref, norm_ref, wc_ref, mk_ref,
                o_ref, s_ref):
    @pl.when(pl.program_id(0) == 0)
    def _():
        s_ref[...] = jnp.zeros_like(s_ref)

    seqs = range(BATCH)
    z = [_dot(lr_ref[b], w2_ref[...]) + b_ref[...] for b in seqs]
    g = [_log_sigmoid(z[b]) * (1.0 / GLA_GATE_TAU) for b in seqs]
    q = [q_ref[b].astype(F32) * (GLA_DK ** -0.5) for b in seqs]
    k = [k_ref[b].astype(F32) for b in seqs]
    v = [v_ref[b].astype(F32) for b in seqs]
    outs = _gla_core(q, k, v, g, wc_ref, mk_ref, s_ref, lambda b, h: b * GLA_HEADS + h,
                     GLA_HEADS, GLA_DK, GLA_DV)
    for b in seqs:
        o_ref[b] = _head_rms(outs[b], norm_ref[...], gate_ref[b].astype(F32)).astype(o_ref.dtype)


def _hgrn_kernel(q_ref, f_ref, i_ref, gate_ref, lb_ref, norm_ref, wc_ref, mk_ref, o_ref, s_ref, *, layer):
    @pl.when(pl.program_id(0) == 0)
    def _():
        s_ref[...] = jnp.zeros_like(s_ref)

    logits = lb_ref[...]
    e = jnp.exp(logits - jnp.max(logits, axis=0, keepdims=True))
    soft = e / jnp.sum(e, axis=0, keepdims=True)
    lb = jnp.zeros((1, logits.shape[1]), F32)
    for l in range(1, layer + 1):
        lb = lb + soft[l:l + 1]

    seqs = range(BATCH)
    group = HGRN_HEADS // 2
    for first in range(0, HGRN_HEADS, group):
        qk = slice(first * HGRN_DK, (first + group) * HGRN_DK)
        vv = slice(first * HGRN_DV, (first + group) * HGRN_DV)
        lbg = lb[:, qk]
        hf = [f_ref[b, :, qk].astype(F32) for b in seqs]
        forget = [lbg + (1.0 - lbg) * _sigmoid(hf[b]) for b in seqs]
        g = [jnp.log(jnp.maximum(forget[b], FORGET_FLOOR)) for b in seqs]
        k = [1.0 - forget[b] for b in seqs]
        v = [_silu(i_ref[b, :, vv].astype(F32)) for b in seqs]
        q = [q_ref[b, :, qk].astype(F32) for b in seqs]
        outs = _gla_core(q, k, v, g, wc_ref, mk_ref, s_ref,
                         lambda b, h, first=first: b * HGRN_HEADS + first + h, group, HGRN_DK, HGRN_DV)
        for b in seqs:
            y = _head_rms(outs[b], norm_ref[:, vv], gate_ref[b, :, vv].astype(F32))
            o_ref[b, :, vv] = y.astype(o_ref.dtype)


def _chunk_block(c):
    return (c + N_CHUNKS - 1) % N_CHUNKS


def _col_spec(width, offset):
    assert offset % width == 0
    blk = offset // width
    return pl.BlockSpec((BATCH, CHUNK, width), lambda c, tbl: (0, _chunk_block(c), blk))


def _const_spec(shape):
    nd = len(shape)
    return pl.BlockSpec(shape, lambda c, tbl: (0,) * nd)


def _mixer_call(kernel, name, width, n_heads, dv, dk, in_specs, operands, sides):
    grid = (N_CHUNKS,)
    return _hosted_call(
        kernel, _side_plan(sides, grid), grid,
        in_specs=in_specs,
        out_spec=pl.BlockSpec((BATCH, CHUNK, width), lambda c, tbl: (0, _chunk_block(c), 0)),
        out_shape=jax.ShapeDtypeStruct((BATCH, L_PAD, width), BF16),
        scratch_shapes=[pltpu.VMEM((BATCH * n_heads, dv, dk), F32)],
        name=name,
        operands=operands,
    )


def _gla(proj, w2, bias, norm, wc, masks, sides=()):
    qk = GLA_HEADS * GLA_DK
    vw = GLA_HEADS * GLA_DV
    return _mixer_call(
        _gla_kernel, "gla", vw, GLA_HEADS, GLA_DV, GLA_DK,
        in_specs=[
            _col_spec(qk, OFF_GLA),
            _col_spec(qk, OFF_GLA + qk),
            _col_spec(vw, OFF_GLA + 2 * qk),
            _col_spec(vw, OFF_GLA + 2 * qk + vw),
            _col_spec(LANE, OFF_LR),
            _const_spec(w2.shape),
            _const_spec(bias.shape),
            _const_spec(norm.shape),
            _const_spec(wc.shape),
            _const_spec(masks.shape),
        ],
        operands=(proj, proj, proj, proj, proj, w2, bias, norm, wc, masks),
        sides=sides,
    )


def _hgrn(proj, lb_logits, norm, wc, masks, layer, sides=()):
    w = HGRN_HEADS * HGRN_DK
    return _mixer_call(
        functools.partial(_hgrn_kernel, layer=layer), "hgrn", w, HGRN_HEADS, HGRN_DV, HGRN_DK,
        in_specs=[
            _col_spec(w, OFF_HGRN),
            _col_spec(w, OFF_HGRN + w),
            _col_spec(w, OFF_HGRN + 2 * w),
            _col_spec(w, OFF_HGRN + 3 * w),
            _const_spec(lb_logits.shape),
            _const_spec(norm.shape),
            _const_spec(wc.shape),
            _const_spec(masks.shape),
        ],
        operands=(proj, proj, proj, proj, lb_logits, norm, wc, masks),
        sides=sides,
    )


def _ret_kernel(q_ref, k_ref, v_ref, gate_ref, cos_ref, sin_ref, qd_ref, kd_ref, cd_ref, im_ref, norm_ref,
                o_ref, s_ref):
    @pl.when(pl.program_id(0) == 0)
    def _():
        s_ref[...] = jnp.zeros_like(s_ref)

    half = RET_DK // 2
    cos = cos_ref[...]
    sin = sin_ref[...]
    seqs = range(BATCH)
    pairs = [(b, h) for h in range(RET_HEADS) for b in seqs]

    def rotate(ref, b, h):
        lo = h * RET_DK
        t1 = ref[b, :, lo:lo + half].astype(F32)
        t2 = ref[b, :, lo + half:lo + RET_DK].astype(F32)
        return jnp.concatenate([t1 * cos - t2 * sin, t1 * sin + t2 * cos], axis=-1)

    qr = {p: rotate(q_ref, *p) * (RET_DK ** -0.5) for p in pairs}
    kr = {p: rotate(k_ref, *p) for p in pairs}
    scores = {p: _dot_nt(qr[p].astype(BF16), kr[p].astype(BF16)) * im_ref[p[1]] for p in pairs}
    outs = {}
    for b, h in pairs:
        p = (b, h)
        vh = v_ref[b, :, h * RET_DV:(h + 1) * RET_DV]
        st = s_ref[b * RET_HEADS + h]
        outs[p] = (_dot(scores[p].astype(BF16), vh)
                   + _dot_nt((qr[p] * qd_ref[h]).astype(BF16), st.astype(BF16)))
        s_ref[b * RET_HEADS + h] = st * cd_ref[h] + _dot_tn(vh, (kr[p] * kd_ref[h]).astype(BF16))
    for b in seqs:
        ys = []
        for h in range(RET_HEADS):
            o = outs[(b, h)]
            o = o - jnp.mean(o, axis=-1, keepdims=True)
            ms = jnp.mean(o * o, axis=-1, keepdims=True)
            ys.append(o * lax.rsqrt(ms + EPS))
        y = jnp.concatenate(ys, axis=-1)
        o_ref[b] = (y * norm_ref[...] * _silu(gate_ref[b].astype(F32))).astype(o_ref.dtype)


def _ret(proj, cos, sin, qd, kd, cd, im, norm, sides=()):
    w = RET_HEADS * RET_DK
    half = RET_DK // 2
    return _mixer_call(
        _ret_kernel, "retention", w, RET_HEADS, RET_DV, RET_DK,
        in_specs=[
            _col_spec(w, OFF_RET),
            _col_spec(w, OFF_RET + w),
            _col_spec(w, OFF_RET + 2 * w),
            _col_spec(w, OFF_RET + 3 * w),
            pl.BlockSpec((CHUNK, half), lambda c, tbl: (c, 0)),
            pl.BlockSpec((CHUNK, half), lambda c, tbl: (c, 0)),
            _const_spec(qd.shape),
            _const_spec(kd.shape),
            _const_spec(cd.shape),
            _const_spec(im.shape),
            _const_spec(norm.shape),
        ],
        operands=(proj, proj, proj, proj, cos, sin, qd, kd, cd, im, norm),
        sides=sides,
    )


def _ret_tables():
    f32 = jnp.float32
    half = RET_DK // 2
    pos = jnp.arange(L_PAD, dtype=f32) - PAD
    inv_freq = ROPE_BASE ** (-jnp.arange(half, dtype=f32) / half)
    ang = pos[:, None] * inv_freq[None, :]
    log_gamma = jnp.log(1.0 - 2.0 ** (-5.0 - jnp.arange(RET_HEADS, dtype=f32)))
    idx = jnp.arange(CHUNK, dtype=f32)
    rel = idx[:, None] - idx[None, :]
    causal = (rel >= 0)[None]
    intra = jnp.where(causal, jnp.exp(jnp.where(causal, rel[None], 0.0) * log_gamma[:, None, None]), 0.0)
    q_decay = jnp.exp((idx[None, :] + 1.0) * log_gamma[:, None])[..., None]
    k_decay = jnp.exp((CHUNK - 1.0 - idx[None, :]) * log_gamma[:, None])[..., None]
    chunk_decay = jnp.exp(CHUNK * log_gamma)[:, None, None]
    return jnp.cos(ang), jnp.sin(ang), q_decay, k_decay, chunk_decay, intra


def _merge_kernel(y0_ref, y1_ref, y2_ref, mg_ref, wb_ref, wo_ref, h_ref, o_ref):
    merged = None
    for n, y_ref in enumerate((y0_ref, y1_ref, y2_ref)):
        gate = _sigmoid(mg_ref[:, n * D_MODEL:(n + 1) * D_MODEL].astype(F32))
        t = _dot(y_ref[...], wb_ref[n * BRANCH_WIDTH:(n + 1) * BRANCH_WIDTH, :]) * gate
        merged = t if merged is None else merged + t
    o_ref[...] = h_ref[...] + _dot(merged.astype(BF16), wo_ref[...])


def _merge(ys, proj, wb, wo, h):
    assert OFF_MG == 0
    rows = lambda width: pl.BlockSpec((MERGE_TILE, width), lambda i: (i, 0))
    whole = lambda a: pl.BlockSpec(a.shape, lambda i: (0, 0))
    return pl.pallas_call(
        _merge_kernel,
        grid=(ROWS // MERGE_TILE,),
        in_specs=[rows(BRANCH_WIDTH)] * N_BRANCH + [rows(N_BRANCH * D_MODEL)]
        + [whole(wb), whole(wo), rows(D_MODEL)],
        out_specs=rows(D_MODEL),
        out_shape=jax.ShapeDtypeStruct((ROWS, D_MODEL), F32),
        compiler_params=_params("parallel"),
        name="merge",
    )(*ys, proj, wb, wo, h)


def kernel(x, meta_tokens, ffn1_norm, ffn1_w_gate, ffn1_w_up, ffn1_w_down, mix_norm, w_in, gla_w_gate2, gla_b_gate, gla_norm, ret_norm, hgrn_lb_logits, hgrn_norm, w_branch, w_out, ffn2_norm, ffn2_w_gate, ffn2_w_up, ffn2_w_down, final_norm):
    b = x.shape[0]
    meta = jnp.broadcast_to(meta_tokens[None].astype(x.dtype), (b, N_META, D_MODEL))
    h = jnp.concatenate([x, jnp.zeros((b, PAD, D_MODEL), x.dtype), meta], axis=1).reshape(ROWS, D_MODEL)

    wc_np, masks_np = _decay_tables()
    wc = jnp.asarray(wc_np, BF16)
    masks = jnp.asarray(masks_np, F32)
    cos, sin, qd, kd, cd, im = _ret_tables()

    def ffn_casts(wg, wu, wd, layer, tiles):
        padded = _ff_padded(tiles[1])
        return (_SideCast(wg, (layer,), 1, padded, 0), _SideCast(wu, (layer,), 1, padded, 0),
                _SideCast(wd, (layer,), 0, padded, 0))

    row = lambda v: v.reshape(1, -1).astype(F32)
    final_gain = row(final_norm)
    w_in_t = jnp.swapaxes(w_in, 1, 2)
    wb_rows = w_branch.reshape(DEPTH, N_BRANCH * BRANCH_WIDTH, D_MODEL)
    padded = _ff_padded(FFN1_TILES[1])
    ffn1_w = (_cast(ffn1_w_gate, (0,), 1, padded), _cast(ffn1_w_up, (0,), 1, padded),
              _cast(ffn1_w_down, (0,), 0, padded))
    for layer in range(DEPTH):
        last = layer == DEPTH - 1
        tiles2 = FINAL_TILES if last else FFN2_TILES
        h, (w_proj,) = _ffn(h, row(ffn1_norm[layer]), *ffn1_w, final_gain, False, FFN1_TILES,
                            sides=[_SideWIn(w_in_t, layer, 0)])
        proj, _ = _proj(h, row(mix_norm[layer]), w_proj)
        jobs = [[j] for j in ffn_casts(ffn2_w_gate, ffn2_w_up, ffn2_w_down, layer, tiles2)]
        jobs[0].append(_SideCast(wb_rows, (layer,), 0, N_BRANCH * BRANCH_WIDTH, 0))
        jobs[1].append(_SideCast(w_out, (layer,), 0, D_MODEL, 0))
        if not last:
            for mine, j in zip(jobs, ffn_casts(ffn1_w_gate, ffn1_w_up, ffn1_w_down, layer + 1, FFN1_TILES)):
                mine.append(j)
        w2 = jnp.pad(gla_w_gate2[layer], ((0, LANE - GLA_GATE_RANK), (0, 0))).astype(BF16)
        proj_seq = proj.reshape(b, L_PAD, W_PROJ)
        y_gla, (wg2, wb, *wg1) = _gla(proj_seq, w2, row(gla_b_gate[layer]), row(gla_norm[layer]), wc, masks,
                                      sides=jobs[0])
        y_ret, (wu2, wo, *wu1) = _ret(proj_seq, cos, sin, qd, kd, cd, im, row(ret_norm[layer]), sides=jobs[1])
        y_hgrn, (wd2, *wd1) = _hgrn(proj_seq, hgrn_lb_logits.astype(F32), row(hgrn_norm[layer]), wc, masks,
                                    layer, sides=jobs[2])
        ys = [y.reshape(ROWS, BRANCH_WIDTH) for y in (y_gla, y_ret, y_hgrn)]
        h = _merge(ys, proj, wb, wo, h)
        h, _ = _ffn(h, row(ffn2_norm[layer]), wg2, wu2, wd2, final_gain, last, tiles2)
        if not last:
            ffn1_w = (*wg1, *wu1, *wd1)
    return h
```

```python
import functools
from typing import NamedTuple

import numpy as np
import jax
import jax.numpy as jnp
from jax import lax
from jax.experimental import pallas as pl
from jax.experimental.pallas import tpu as pltpu

D_MODEL = 2048
BATCH = 2
SEQ = 4096
DEPTH = 2
N_META = 16
CHUNK = 64
PAD = CHUNK - N_META
D_FF = 5504
FFN_RES = 0.5
EPS = 1e-6
N_BRANCH = 3
BRANCH_WIDTH = 1024

GLA_HEADS, GLA_DK, GLA_DV = 4, 128, 256
GLA_GATE_RANK = 16
GLA_GATE_TAU = 16.0
RET_HEADS, RET_DK, RET_DV = 4, 256, 256
ROPE_BASE = 10000.0
HGRN_HEADS, HGRN_DK, HGRN_DV = 8, 128, 128
FORGET_FLOOR = 1e-20

LANE = 128
L_PAD = PAD + N_META + SEQ
N_CHUNKS = L_PAD // CHUNK
ROWS = BATCH * L_PAD
N_LEVELS = 6
assert 1 << N_LEVELS == CHUNK

FFN1_TILES = (640, 512)
FFN2_TILES = (640, 512)
FINAL_TILES = (832, 512)
PROJ_ROW_TILE = 832
PROJ_TILE = 1792
MERGE_TILE = 320
CAST_TILE = 512
SIDE_BLOCK = LANE
SIDE_ROWS = 32
W_IN_BLOCK = 256
LR_PAD = CAST_TILE
W_IN_SHIFT = GLA_GATE_RANK

OFF_MG = 0
OFF_GLA = OFF_MG + N_BRANCH * D_MODEL
OFF_RET = OFF_GLA + 2 * GLA_HEADS * GLA_DK + 2 * GLA_HEADS * GLA_DV
OFF_HGRN = OFF_RET + 2 * RET_HEADS * RET_DK + 2 * RET_HEADS * RET_DV
OFF_LR = OFF_HGRN + 2 * HGRN_HEADS * HGRN_DK + 2 * HGRN_HEADS * HGRN_DV
W_PROJ = OFF_LR + LR_PAD

VMEM_LIMIT = 56 * 1024 * 1024

F32 = jnp.float32
BF16 = jnp.bfloat16


def _params(*sem):
    return pltpu.CompilerParams(dimension_semantics=sem, vmem_limit_bytes=VMEM_LIMIT)


def _rms_rows(x, gain):
    ms = jnp.mean(x * x, axis=-1, keepdims=True)
    return x * lax.rsqrt(ms + EPS) * gain


def _sigmoid(x):
    return 1.0 / (1.0 + jnp.exp(-x))


def _silu(x):
    return x * _sigmoid(x)


def _dot(a, b):
    return jnp.dot(a, b, preferred_element_type=F32)


def _dot_nt(a, b):
    return lax.dot_general(a, b, (((1,), (1,)), ((), ())), preferred_element_type=F32)


def _dot_tn(a, b):
    return lax.dot_general(a, b, (((0,), (0,)), ((), ())), preferred_element_type=F32)


def _cast_kernel(w_ref, o_ref, *, axis, valid):
    x = w_ref[...]
    idx = pl.program_id(0) * x.shape[axis] + lax.broadcasted_iota(jnp.int32, x.shape, axis)
    o_ref[...] = jnp.where(idx < valid, x, 0.0).astype(BF16)


def _cast(w, lead, axis, padded):
    rows, cols = w.shape[-2:]
    nl = len(lead)
    if axis == 0:
        block, out_shape, grid = (CAST_TILE, cols), (padded, cols), padded // CAST_TILE
        imap = lambda j: (*lead, j, 0)
        omap = lambda j: (j, 0)
    else:
        block, out_shape, grid = (rows, CAST_TILE), (padded // CAST_TILE, rows, CAST_TILE), padded // CAST_TILE
        imap = lambda j: (*lead, 0, j)
        omap = lambda j: (j, 0, 0)
    out_block = block if axis == 0 else (None,) + block
    return pl.pallas_call(
        functools.partial(_cast_kernel, axis=axis, valid=w.shape[-2 + axis]),
        grid=(grid,),
        in_specs=[pl.BlockSpec((None,) * nl + block, imap)],
        out_specs=pl.BlockSpec(out_block, omap),
        out_shape=jax.ShapeDtypeStruct(out_shape, BF16),
        compiler_params=_params("parallel"),
        name="cast",
    )(w)


SIDE_IDLE, SIDE_COPY, SIDE_ZERO, SIDE_SHIFTED, SIDE_LOW_RANK = 0, 1, 2, 3, 4


class _SideCast(NamedTuple):
    src: jax.Array
    lead: tuple
    axis: int
    padded: int
    start: int
    col_tile: int = 0


class _SideWIn(NamedTuple):
    src_t: jax.Array
    layer: int
    start: int


class _SidePlan(NamedTuple):
    table: np.ndarray
    in_specs: list
    operands: list
    out_specs: list
    out_shapes: list
    runners: list
    grid: tuple


def _w_in_blocks():
    per = lambda width: width // W_IN_BLOCK
    src = lambda col: col // W_IN_BLOCK
    gla_w = 2 * GLA_HEADS * GLA_DK + 2 * GLA_HEADS * GLA_DV
    ret_w = 2 * RET_HEADS * RET_DK + 2 * RET_HEADS * RET_DV
    hgrn_w = 2 * HGRN_HEADS * HGRN_DK + 2 * HGRN_HEADS * HGRN_DV
    lr_col = gla_w
    blocks = []
    for t in range(per(N_BRANCH * D_MODEL)):
        blocks.append((src(lr_col + ret_w + hgrn_w) + t, SIDE_SHIFTED))
    for t in range(per(gla_w)):
        blocks.append((t, SIDE_COPY))
    for t in range(per(ret_w + hgrn_w)):
        blocks.append((src(lr_col) + t, SIDE_SHIFTED))
    blocks.append((src(lr_col), SIDE_LOW_RANK))
    blocks += [(src(lr_col), SIDE_ZERO)] * (per(LR_PAD) - 1)
    assert len(blocks) == per(W_PROJ)
    return [(s, t, mode) for t, (s, mode) in enumerate(blocks)]


def _side_plan(jobs, grid):
    n_steps = int(np.prod(grid))
    rows, in_specs, operands, out_specs, out_shapes, runners = [], [], [], [], [], []

    def add_rows(blocks, start):
        n = len(blocks)
        assert start + n <= n_steps
        arr = np.asarray(blocks, np.int32).T
        tab = np.zeros((3, n_steps), np.int32)
        tab[:2, :start] = arr[:2, :1]
        tab[:, start:start + n] = arr
        tab[:2, start + n:] = arr[:2, -1:]
        base = len(rows) * n_steps
        rows.extend(tab)
        return base

    def entry(args, offset):
        *idx, tbl = args
        return tbl[offset + _linear_step(idx, grid)]

    for job in jobs:
        if isinstance(job, _SideCast):
            r, c = job.src.shape[-2:]
            lead, nl = job.lead, len(job.lead)
            if job.axis == 0:
                assert r % SIDE_BLOCK == 0 and job.padded % SIDE_BLOCK == 0
                n_valid, n_all = r // SIDE_BLOCK, job.padded // SIDE_BLOCK
                base = add_rows([(min(t, n_valid - 1), t, SIDE_COPY if t < n_valid else SIDE_ZERO)
                                 for t in range(n_all)], job.start)
                in_block = block = (SIDE_BLOCK, c)
                out_shape = (job.padded, c)
                omap = lambda *a, base=base: (entry(a, base + n_steps), 0)

                def run(tbl_ref, step, ins, out, base=base):
                    mode = tbl_ref[base + 2 * n_steps + step]

                    @pl.when(mode != SIDE_IDLE)
                    def _():
                        out[...] = jnp.where(mode == SIDE_ZERO, 0.0, ins[0][...]).astype(BF16)
            else:
                assert r % SIDE_ROWS == 0 and c % LANE == 0 and job.padded % job.col_tile == 0
                n_tiles, tile = job.padded // job.col_tile, job.col_tile
                base = add_rows([(t, t, SIDE_COPY) for t in range(r // SIDE_ROWS)], job.start)
                in_block = (SIDE_ROWS, c)
                block, out_shape = (n_tiles, SIDE_ROWS, tile), (n_tiles, r, tile)
                omap = lambda *a, base=base: (0, entry(a, base + n_steps), 0)

                def run(tbl_ref, step, ins, out, base=base, c=c, n_tiles=n_tiles, tile=tile):
                    mode = tbl_ref[base + 2 * n_steps + step]

                    @pl.when(mode != SIDE_IDLE)
                    def _():
                        for kk in range(n_tiles):
                            width = min(tile, c - kk * tile)
                            assert width > 0
                            piece = ins[0][:, kk * tile:kk * tile + width].astype(BF16)
                            if width < tile:
                                piece = jnp.concatenate(
                                    [piece, jnp.zeros((SIDE_ROWS, tile - width), BF16)], axis=1)
                            out[kk] = piece

            imap = lambda *a, base=base, lead=lead: (*lead, entry(a, base), 0)
            in_specs.append(pl.BlockSpec((None,) * nl + in_block, imap))
            operands.append(job.src)
            runners.append((1, run))
        else:
            base = add_rows(_w_in_blocks(), job.start)
            layer = job.layer
            shifts_per_block = W_IN_BLOCK // W_IN_SHIFT
            in_specs.append(pl.BlockSpec(
                (None, W_IN_BLOCK, D_MODEL),
                lambda *a, base=base, layer=layer: (layer, entry(a, base), 0)))
            in_specs.append(pl.BlockSpec(
                (None, W_IN_SHIFT, D_MODEL),
                lambda *a, base=base, layer=layer: (layer, (entry(a, base) + 1) * shifts_per_block, 0)))
            operands += [job.src_t, job.src_t]
            per_tile = PROJ_TILE // W_IN_BLOCK
            block, out_shape = (None, D_MODEL, W_IN_BLOCK), (W_PROJ // PROJ_TILE, D_MODEL, PROJ_TILE)

            def omap(*a, base=base, per_tile=per_tile):
                t = entry(a, base + n_steps)
                return (t // per_tile, 0, t % per_tile)

            def run(tbl_ref, step, ins, out, base=base):
                a_ref, b_ref = ins
                mode = tbl_ref[base + 2 * n_steps + step]

                @pl.when(mode == SIDE_SHIFTED)
                def _():
                    x = jnp.concatenate([a_ref[W_IN_SHIFT:, :], b_ref[...]], axis=0)
                    out[...] = x.astype(BF16).T

                @pl.when(mode == SIDE_COPY)
                def _():
                    out[...] = a_ref[...].astype(BF16).T

                @pl.when(mode == SIDE_LOW_RANK)
                def _():
                    row = lax.broadcasted_iota(jnp.int32, a_ref.shape, 0)
                    out[...] = jnp.where(row < GLA_GATE_RANK, a_ref[...], 0.0).astype(BF16).T

                @pl.when(mode == SIDE_ZERO)
                def _():
                    out[...] = jnp.zeros_like(out)

            runners.append((2, run))
        out_specs.append(pl.BlockSpec(block, omap))
        out_shapes.append(jax.ShapeDtypeStruct(out_shape, BF16))

    table = np.concatenate(rows) if rows else np.zeros((1,), np.int32)
    return _SidePlan(table, in_specs, operands, out_specs, out_shapes, runners, tuple(grid))


def _linear_step(idx, grid):
    step = idx[0]
    for i, n in zip(idx[1:], grid[1:]):
        step = step * n + i
    return step


def _run_sides(plan, tbl_ref, side_ins, side_outs):
    step = _linear_step([pl.program_id(d) for d in range(len(plan.grid))], plan.grid)
    k = 0
    for (n_in, run), out in zip(plan.runners, side_outs):
        run(tbl_ref, step, side_ins[k:k + n_in], out)
        k += n_in


def _hosted_call(kernel, plan, grid, in_specs, out_spec, out_shape, scratch_shapes, name, operands):
    n_side_in = len(plan.in_specs)
    n_main_in = len(in_specs)

    def body(tbl_ref, *refs):
        ins = refs[:n_main_in]
        side_ins = refs[n_main_in:n_main_in + n_side_in]
        out = refs[n_main_in + n_side_in]
        side_outs = refs[n_main_in + n_side_in + 1:n_main_in + n_side_in + 1 + len(plan.out_specs)]
        scratch = refs[n_main_in + n_side_in + 1 + len(plan.out_specs):]
        kernel(*ins, out, *scratch)
        _run_sides(plan, tbl_ref, side_ins, side_outs)

    res = pl.pallas_call(
        body,
        grid_spec=pltpu.PrefetchScalarGridSpec(
            num_scalar_prefetch=1,
            grid=grid,
            in_specs=list(in_specs) + plan.in_specs,
            out_specs=[out_spec] + plan.out_specs,
            scratch_shapes=scratch_shapes,
        ),
        out_shape=[out_shape] + plan.out_shapes,
        compiler_params=_params(*["arbitrary"] * len(grid)),
        name=name,
    )(jnp.asarray(plan.table), *operands, *plan.operands)
    return res[0], res[1:]


def _ffn_kernel(h_ref, g_ref, wg_ref, wu_ref, wd_ref, fg_ref, o_ref, u_scr, *, final, tail):
    j = pl.program_id(1)
    last = pl.num_programs(1) - 1
    tile = wg_ref.shape[1]
    half = h_ref.shape[0] // 2
    halves = (slice(0, half), slice(half, 2 * half))

    def partial_sum(u, width):
        a = _dot(u, wg_ref[:, :width])
        b = _dot(u, wu_ref[:, :width])
        act = (_silu(a) * b).astype(BF16)
        return _dot(act, wd_ref[:width, :])

    @pl.when(j == 0)
    def _():
        for r in halves:
            u = _rms_rows(h_ref[r, :], g_ref[...]).astype(BF16)
            u_scr[r, :] = u
            o_ref[r, :] = partial_sum(u, tile)

    @pl.when((j > 0) & (j < last))
    def _():
        o_ref[...] += partial_sum(u_scr[...], tile)

    @pl.when(j == last)
    def _():
        for r in (halves if final else (slice(None),)):
            y = h_ref[r, :] + FFN_RES * (o_ref[r, :] + partial_sum(u_scr[r, :], tail))
            if final:
                y = _rms_rows(y, fg_ref[...])
            o_ref[r, :] = y


def _ff_padded(tf):
    return pl.cdiv(D_FF, tf) * tf


def _ffn(h, gain, wg, wu, wd, final_gain, final, tiles, sides=()):
    tm, tf = tiles
    assert ROWS % tm == 0 and wg.shape == (_ff_padded(tf) // tf, D_MODEL, tf) and wd.shape[0] == _ff_padded(tf)
    grid = (ROWS // tm, wg.shape[0])
    tail = D_FF - (grid[1] - 1) * tf
    plan = _side_plan(sides, grid)
    if final:
        assert L_PAD % tm == 0
        per_seq = L_PAD // tm
        h = h.reshape(BATCH, L_PAD, D_MODEL)
        rows_spec = pl.BlockSpec((None, tm, D_MODEL), lambda i, j, tbl: (i // per_seq, i % per_seq, 0))
        out_shape = jax.ShapeDtypeStruct((BATCH, SEQ, D_MODEL), F32)
    else:
        rows_spec = pl.BlockSpec((tm, D_MODEL), lambda i, j, tbl: (i, 0))
        out_shape = jax.ShapeDtypeStruct((ROWS, D_MODEL), F32)
    return _hosted_call(
        functools.partial(_ffn_kernel, final=final, tail=tail), plan, grid,
        in_specs=[
            rows_spec,
            pl.BlockSpec((1, D_MODEL), lambda i, j, tbl: (0, 0)),
            pl.BlockSpec((None, D_MODEL, tf), lambda i, j, tbl: (j, 0, 0)),
            pl.BlockSpec((None, D_MODEL, tf), lambda i, j, tbl: (j, 0, 0)),
            pl.BlockSpec((tf, D_MODEL), lambda i, j, tbl: (j, 0)),
            pl.BlockSpec((1, D_MODEL), lambda i, j, tbl: (0, 0)),
        ],
        out_spec=rows_spec,
        out_shape=out_shape,
        scratch_shapes=[pltpu.VMEM((tm, D_MODEL), BF16)],
        name="ffn",
        operands=(h, gain, wg, wu, wd, final_gain),
    )


def _proj_kernel(h_ref, g_ref, w_ref, o_ref, u_scr):
    j = pl.program_id(1)

    @pl.when(j == 0)
    def _():
        half = h_ref.shape[0] // 2
        for r in (slice(0, half), slice(half, 2 * half)):
            u = _rms_rows(h_ref[r, :], g_ref[...]).astype(BF16)
            u_scr[r, :] = u
            o_ref[r, :] = _dot(u, w_ref[...]).astype(o_ref.dtype)

    @pl.when(j > 0)
    def _():
        o_ref[...] = _dot(u_scr[...], w_ref[...]).astype(o_ref.dtype)


def _proj(h, gain, w, sides=()):
    grid = (ROWS // PROJ_ROW_TILE, W_PROJ // PROJ_TILE)
    plan = _side_plan(sides, grid)
    return _hosted_call(
        _proj_kernel, plan, grid,
        in_specs=[
            pl.BlockSpec((PROJ_ROW_TILE, D_MODEL), lambda i, j, tbl: (i, 0)),
            pl.BlockSpec((1, D_MODEL), lambda i, j, tbl: (0, 0)),
            pl.BlockSpec((None, D_MODEL, PROJ_TILE), lambda i, j, tbl: (j, 0, 0)),
        ],
        out_spec=pl.BlockSpec((PROJ_ROW_TILE, PROJ_TILE), lambda i, j, tbl: (i, j)),
        out_shape=jax.ShapeDtypeStruct((ROWS, W_PROJ), BF16),
        scratch_shapes=[pltpu.VMEM((PROJ_ROW_TILE, D_MODEL), BF16)],
        name="in_proj",
        operands=(h, gain, w),
    )


def _decay_tables():
    c = CHUNK
    blocks = []
    r = np.arange(c)
    for l in range(1, N_LEVELS):
        b = 1 << l
        m = np.zeros((c, c), np.float32)
        for i in range(c):
            p = i % (2 * b)
            s = i - p + b
            if p >= b:
                m[i, s:i + 1] = 1.0
            else:
                m[i, i + 1:s] = 1.0
        blocks.append(m)
    blocks.append((r[None, :] <= r[:, None]).astype(np.float32))
    wc = np.concatenate(blocks, axis=0)
    wc = np.concatenate([wc] * 3, axis=1)

    masks = np.zeros((N_LEVELS + 1, c, c), np.float32)
    for l in range(N_LEVELS):
        b = 1 << l
        same = (r[:, None] // (2 * b)) == (r[None, :] // (2 * b))
        up = (r[:, None] & b) != 0
        lo = (r[None, :] & b) == 0
        masks[l] = (same & up & lo).astype(np.float32)
    masks[N_LEVELS] = np.eye(c, dtype=np.float32)
    return wc, masks


def _gla_core(qs, ks, vs, gs, wc_ref, mk_ref, s_ref, slot_of, n_heads, dk, dv):
    c = CHUNK
    seqs = range(len(qs))
    heads = range(n_heads)
    base = (N_LEVELS - 1) * c

    def split3(g):
        g_hi = g.astype(BF16)
        r1 = g - g_hi.astype(F32)
        g_mid = r1.astype(BF16)
        g_lo = (r1 - g_mid.astype(F32)).astype(BF16)
        return jnp.concatenate([g_hi, g_mid, g_lo], axis=0)

    sums = [_dot(wc_ref[...], split3(gs[b])) for b in seqs]
    ex = [jnp.exp(sums[b][:base]) for b in seqs]
    cum = [sums[b][base:base + c] for b in seqs]
    total = [cum[b][c - 1:c] for b in seqs]

    on_diag = [qs[b] * ks[b] for b in seqs]
    below = [qs[b] * jnp.exp(gs[b]) * pltpu.roll(ks[b], 1, axis=0) for b in seqs]
    scores = [[None] * n_heads for _ in seqs]
    for h in heads:
        lanes = slice(h * dk, (h + 1) * dk)
        for b in seqs:
            scores[b][h] = (mk_ref[N_LEVELS] * jnp.sum(on_diag[b][:, lanes], axis=-1, keepdims=True)
                            + mk_ref[0] * jnp.sum(below[b][:, lanes], axis=-1, keepdims=True))
    rows = lax.broadcasted_iota(jnp.int32, (c, 1), 0)
    for l in range(1, N_LEVELS):
        upper = (rows & (1 << l)) != 0
        x = [(jnp.where(upper, qs[b], ks[b]) * ex[b][(l - 1) * c:l * c]).astype(BF16) for b in seqs]
        for h in heads:
            for b in seqs:
                xh = x[b][:, h * dk:(h + 1) * dk]
                scores[b][h] = scores[b][h] + mk_ref[l] * _dot_nt(xh, xh)

    vb = [vs[b].astype(BF16) for b in seqs]
    qt = [(qs[b] * jnp.exp(cum[b])).astype(BF16) for b in seqs]
    kt = [(ks[b] * jnp.exp(total[b] - cum[b])).astype(BF16) for b in seqs]
    tot = [jnp.exp(total[b]) for b in seqs]
    outs = [[None] * n_heads for _ in seqs]
    for h in heads:
        for b in seqs:
            slot = slot_of(b, h)
            st = s_ref[slot]
            vh = vb[b][:, h * dv:(h + 1) * dv]
            outs[b][h] = (_dot(scores[b][h].astype(BF16), vh)
                          + _dot_nt(qt[b][:, h * dk:(h + 1) * dk], st.astype(BF16)))
            s_ref[slot] = st * tot[b][:, h * dk:(h + 1) * dk] + _dot_tn(vh, kt[b][:, h * dk:(h + 1) * dk])
    return outs


def _head_rms(outs, gain, gate):
    dv = outs[0].shape[-1]
    ys = []
    for h, o in enumerate(outs):
        ms = jnp.mean(o * o, axis=-1, keepdims=True)
        ys.append(o * lax.rsqrt(ms + EPS))
    y = jnp.concatenate(ys, axis=-1)
    return y * gain * _silu(gate)


def _log_sigmoid(z):
    return jnp.minimum(z, 0.0) - jnp.log(1.0 + jnp.exp(-jnp.abs(z)))


def _gla_kernel(q_ref, k_ref, v_ref, gate_ref, lr_ref, w2_ref, b_ref, norm_ref, wc_ref, mk_ref,
                o_ref, s_ref):
    @pl.when(pl.program_id(0) == 0)
    def _():
        s_ref[...] = jnp.zeros_like(s_ref)

    seqs = range(BATCH)
    z = [_dot(lr_ref[b], w2_ref[...]) + b_ref[...] for b in seqs]
    g = [_log_sigmoid(z[b]) * (1.0 / GLA_GATE_TAU) for b in seqs]
    q = [q_ref[b].astype(F32) * (GLA_DK ** -0.5) for b in seqs]
    k = [k_ref[b].astype(F32) for b in seqs]
    v = [v_ref[b].astype(F32) for b in seqs]
    outs = _gla_core(q, k, v, g, wc_ref, mk_ref, s_ref, lambda b, h: b * GLA_HEADS + h,
                     GLA_HEADS, GLA_DK, GLA_DV)
    for b in seqs:
        o_ref[b] = _head_rms(outs[b], norm_ref[...], gate_ref[b].astype(F32)).astype(o_ref.dtype)


def _hgrn_kernel(q_ref, f_ref, i_ref, gate_ref, lb_ref, norm_ref, wc_ref, mk_ref, o_ref, s_ref, *, layer):
    @pl.when(pl.program_id(0) == 0)
    def _():
        s_ref[...] = jnp.zeros_like(s_ref)

    logits = lb_ref[...]
    e = jnp.exp(logits - jnp.max(logits, axis=0, keepdims=True))
    soft = e / jnp.sum(e, axis=0, keepdims=True)
    lb = jnp.zeros((1, logits.shape[1]), F32)
    for l in range(1, layer + 1):
        lb = lb + soft[l:l + 1]

    seqs = range(BATCH)
    group = HGRN_HEADS // 2
    for first in range(0, HGRN_HEADS, group):
        qk = slice(first * HGRN_DK, (first + group) * HGRN_DK)
        vv = slice(first * HGRN_DV, (first + group) * HGRN_DV)
        lbg = lb[:, qk]
        hf = [f_ref[b, :, qk].astype(F32) for b in seqs]
        forget = [lbg + (1.0 - lbg) * _sigmoid(hf[b]) for b in seqs]
        g = [jnp.log(jnp.maximum(forget[b], FORGET_FLOOR)) for b in seqs]
        k = [1.0 - forget[b] for b in seqs]
        v = [_silu(i_ref[b, :, vv].astype(F32)) for b in seqs]
        q = [q_ref[b, :, qk].astype(F32) for b in seqs]
        outs = _gla_core(q, k, v, g, wc_ref, mk_ref, s_ref,
                         lambda b, h, first=first: b * HGRN_HEADS + first + h, group, HGRN_DK, HGRN_DV)
        for b in seqs:
            y = _head_rms(outs[b], norm_ref[:, vv], gate_ref[b, :, vv].astype(F32))
            o_ref[b, :, vv] = y.astype(o_ref.dtype)


def _chunk_block(c):
    return (c + N_CHUNKS - 1) % N_CHUNKS


def _col_spec(width, offset):
    assert offset % width == 0
    blk = offset // width
    return pl.BlockSpec((BATCH, CHUNK, width), lambda c, tbl: (0, _chunk_block(c), blk))


def _const_spec(shape):
    nd = len(shape)
    return pl.BlockSpec(shape, lambda c, tbl: (0,) * nd)


def _mixer_call(kernel, name, width, n_heads, dv, dk, in_specs, operands, sides):
    grid = (N_CHUNKS,)
    return _hosted_call(
        kernel, _side_plan(sides, grid), grid,
        in_specs=in_specs,
        out_spec=pl.BlockSpec((BATCH, CHUNK, width), lambda c, tbl: (0, _chunk_block(c), 0)),
        out_shape=jax.ShapeDtypeStruct((BATCH, L_PAD, width), BF16),
        scratch_shapes=[pltpu.VMEM((BATCH * n_heads, dv, dk), F32)],
        name=name,
        operands=operands,
    )


def _gla(proj, w2, bias, norm, wc, masks, sides=()):
    qk = GLA_HEADS * GLA_DK
    vw = GLA_HEADS * GLA_DV
    return _mixer_call(
        _gla_kernel, "gla", vw, GLA_HEADS, GLA_DV, GLA_DK,
        in_specs=[
            _col_spec(qk, OFF_GLA),
            _col_spec(qk, OFF_GLA + qk),
            _col_spec(vw, OFF_GLA + 2 * qk),
            _col_spec(vw, OFF_GLA + 2 * qk + vw),
            _col_spec(LANE, OFF_LR),
            _const_spec(w2.shape),
            _const_spec(bias.shape),
            _const_spec(norm.shape),
            _const_spec(wc.shape),
            _const_spec(masks.shape),
        ],
        operands=(proj, proj, proj, proj, proj, w2, bias, norm, wc, masks),
        sides=sides,
    )


def _hgrn(proj, lb_logits, norm, wc, masks, layer, sides=()):
    w = HGRN_HEADS * HGRN_DK
    return _mixer_call(
        functools.partial(_hgrn_kernel, layer=layer), "hgrn", w, HGRN_HEADS, HGRN_DV, HGRN_DK,
        in_specs=[
            _col_spec(w, OFF_HGRN),
            _col_spec(w, OFF_HGRN + w),
            _col_spec(w, OFF_HGRN + 2 * w),
            _col_spec(w, OFF_HGRN + 3 * w),
            _const_spec(lb_logits.shape),
            _const_spec(norm.shape),
            _const_spec(wc.shape),
            _const_spec(masks.shape),
        ],
        operands=(proj, proj, proj, proj, lb_logits, norm, wc, masks),
        sides=sides,
    )


def _ret_kernel(q_ref, k_ref, v_ref, gate_ref, cos_ref, sin_ref, qd_ref, kd_ref, cd_ref, im_ref, norm_ref,
                o_ref, s_ref):
    @pl.when(pl.program_id(0) == 0)
    def _():
        s_ref[...] = jnp.zeros_like(s_ref)

    half = RET_DK // 2
    cos = cos_ref[...]
    sin = sin_ref[...]
    seqs = range(BATCH)
    pairs = [(b, h) for h in range(RET_HEADS) for b in seqs]

    def rotate(ref, b, h):
        lo = h * RET_DK
        t1 = ref[b, :, lo:lo + half].astype(F32)
        t2 = ref[b, :, lo + half:lo + RET_DK].astype(F32)
        return jnp.concatenate([t1 * cos - t2 * sin, t1 * sin + t2 * cos], axis=-1)

    qr = {p: rotate(q_ref, *p) * (RET_DK ** -0.5) for p in pairs}
    kr = {p: rotate(k_ref, *p) for p in pairs}
    scores = {p: _dot_nt(qr[p].astype(BF16), kr[p].astype(BF16)) * im_ref[p[1]] for p in pairs}
    outs = {}
    for b, h in pairs:
        p = (b, h)
        vh = v_ref[b, :, h * RET_DV:(h + 1) * RET_DV]
        st = s_ref[b * RET_HEADS + h]
        outs[p] = (_dot(scores[p].astype(BF16), vh)
                   + _dot_nt((qr[p] * qd_ref[h]).astype(BF16), st.astype(BF16)))
        s_ref[b * RET_HEADS + h] = st * cd_ref[h] + _dot_tn(vh, (kr[p] * kd_ref[h]).astype(BF16))
    for b in seqs:
        ys = []
        for h in range(RET_HEADS):
            o = outs[(b, h)]
            o = o - jnp.mean(o, axis=-1, keepdims=True)
            ms = jnp.mean(o * o, axis=-1, keepdims=True)
            ys.append(o * lax.rsqrt(ms + EPS))
        y = jnp.concatenate(ys, axis=-1)
        o_ref[b] = (y * norm_ref[...] * _silu(gate_ref[b].astype(F32))).astype(o_ref.dtype)


def _ret(proj, cos, sin, qd, kd, cd, im, norm, sides=()):
    w = RET_HEADS * RET_DK
    half = RET_DK // 2
    return _mixer_call(
        _ret_kernel, "retention", w, RET_HEADS, RET_DV, RET_DK,
        in_specs=[
            _col_spec(w, OFF_RET),
            _col_spec(w, OFF_RET + w),
            _col_spec(w, OFF_RET + 2 * w),
            _col_spec(w, OFF_RET + 3 * w),
            pl.BlockSpec((CHUNK, half), lambda c, tbl: (c, 0)),
            pl.BlockSpec((CHUNK, half), lambda c, tbl: (c, 0)),
            _const_spec(qd.shape),
            _const_spec(kd.shape),
            _const_spec(cd.shape),
            _const_spec(im.shape),
            _const_spec(norm.shape),
        ],
        operands=(proj, proj, proj, proj, cos, sin, qd, kd, cd, im, norm),
        sides=sides,
    )


def _ret_tables():
    f32 = jnp.float32
    half = RET_DK // 2
    pos = jnp.arange(L_PAD, dtype=f32) - PAD
    inv_freq = ROPE_BASE ** (-jnp.arange(half, dtype=f32) / half)
    ang = pos[:, None] * inv_freq[None, :]
    log_gamma = jnp.log(1.0 - 2.0 ** (-5.0 - jnp.arange(RET_HEADS, dtype=f32)))
    idx = jnp.arange(CHUNK, dtype=f32)
    rel = idx[:, None] - idx[None, :]
    causal = (rel >= 0)[None]
    intra = jnp.where(causal, jnp.exp(jnp.where(causal, rel[None], 0.0) * log_gamma[:, None, None]), 0.0)
    q_decay = jnp.exp((idx[None, :] + 1.0) * log_gamma[:, None])[..., None]
    k_decay = jnp.exp((CHUNK - 1.0 - idx[None, :]) * log_gamma[:, None])[..., None]
    chunk_decay = jnp.exp(CHUNK * log_gamma)[:, None, None]
    return jnp.cos(ang), jnp.sin(ang), q_decay, k_decay, chunk_decay, intra


def _merge_kernel(y0_ref, y1_ref, y2_ref, mg_ref, wb_ref, wo_ref, h_ref, o_ref):
    merged = None
    for n, y_ref in enumerate((y0_ref, y1_ref, y2_ref)):
        gate = _sigmoid(mg_ref[:, n * D_MODEL:(n + 1) * D_MODEL].astype(F32))
        t = _dot(y_ref[...], wb_ref[n * BRANCH_WIDTH:(n + 1) * BRANCH_WIDTH, :]) * gate
        merged = t if merged is None else merged + t
    o_ref[...] = h_ref[...] + _dot(merged.astype(BF16), wo_ref[...])


def _merge(ys, proj, wb, wo, h):
    assert OFF_MG == 0
    rows = lambda width: pl.BlockSpec((MERGE_TILE, width), lambda i: (i, 0))
    whole = lambda a: pl.BlockSpec(a.shape, lambda i: (0, 0))
    return pl.pallas_call(
        _merge_kernel,
        grid=(ROWS // MERGE_TILE,),
        in_specs=[rows(BRANCH_WIDTH)] * N_BRANCH + [rows(N_BRANCH * D_MODEL)]
        + [whole(wb), whole(wo), rows(D_MODEL)],
        out_specs=rows(D_MODEL),
        out_shape=jax.ShapeDtypeStruct((ROWS, D_MODEL), F32),
        compiler_params=_params("parallel"),
        name="merge",
    )(*ys, proj, wb, wo, h)


def kernel(x, meta_tokens, ffn1_norm, ffn1_w_gate, ffn1_w_up, ffn1_w_down, mix_norm, w_in, gla_w_gate2, gla_b_gate, gla_norm, ret_norm, hgrn_lb_logits, hgrn_norm, w_branch, w_out, ffn2_norm, ffn2_w_gate, ffn2_w_up, ffn2_w_down, final_norm):
    b = x.shape[0]
    meta = jnp.broadcast_to(meta_tokens[None].astype(x.dtype), (b, N_META, D_MODEL))
    h = jnp.concatenate([x, jnp.zeros((b, PAD, D_MODEL), x.dtype), meta], axis=1).reshape(ROWS, D_MODEL)

    wc_np, masks_np = _decay_tables()
    wc = jnp.asarray(wc_np, BF16)
    masks = jnp.asarray(masks_np, F32)
    cos, sin, qd, kd, cd, im = _ret_tables()

    def ffn_casts(wg, wu, wd, layer, tiles):
        padded = _ff_padded(tiles[1])
        return ([_SideCast(wg, (layer,), 1, padded, 0, tiles[1]),
                 _SideCast(wu, (layer,), 1, padded, D_MODEL // SIDE_ROWS, tiles[1])],
                _SideCast(wd, (layer,), 0, padded, 0))

    row = lambda v: v.reshape(1, -1).astype(F32)
    final_gain = row(final_norm)
    w_in_t = jnp.swapaxes(w_in, 1, 2)
    wb_rows = w_branch.reshape(DEPTH, N_BRANCH * BRANCH_WIDTH, D_MODEL)
    padded = _ff_padded(FFN1_TILES[1])
    ffn1_w = (_cast(ffn1_w_gate, (0,), 1, padded), _cast(ffn1_w_up, (0,), 1, padded),
              _cast(ffn1_w_down, (0,), 0, padded))
    for layer in range(DEPTH):
        last = layer == DEPTH - 1
        tiles2 = FINAL_TILES if last else FFN2_TILES
        gate_up2, down2 = ffn_casts(ffn2_w_gate, ffn2_w_up, ffn2_w_down, layer, tiles2)
        h, (w_proj, wg2, wu2) = _ffn(h, row(ffn1_norm[layer]), *ffn1_w, final_gain, False, FFN1_TILES,
                                     sides=[_SideWIn(w_in_t, layer, 0)] + gate_up2)
        proj, _ = _proj(h, row(mix_norm[layer]), w_proj)
        hgrn_jobs = [down2]
        if not last:
            gate_up1, down1 = ffn_casts(ffn1_w_gate, ffn1_w_up, ffn1_w_down, layer + 1, FFN1_TILES)
            hgrn_jobs.append(down1)
        w2 = jnp.pad(gla_w_gate2[layer], ((0, LANE - GLA_GATE_RANK), (0, 0))).astype(BF16)
        proj_seq = proj.reshape(b, L_PAD, W_PROJ)
        y_gla, (wb,) = _gla(proj_seq, w2, row(gla_b_gate[layer]), row(gla_norm[layer]), wc, masks,
                            sides=[_SideCast(wb_rows, (layer,), 0, N_BRANCH * BRANCH_WIDTH, 0)])
        y_ret, (wo,) = _ret(proj_seq, cos, sin, qd, kd, cd, im, row(ret_norm[layer]),
                            sides=[_SideCast(w_out, (layer,), 0, D_MODEL, 0)])
        y_hgrn, (wd2, *wd1) = _hgrn(proj_seq, hgrn_lb_logits.astype(F32), row(hgrn_norm[layer]), wc, masks,
                                    layer, sides=hgrn_jobs)
        ys = [y.reshape(ROWS, BRANCH_WIDTH) for y in (y_gla, y_ret, y_hgrn)]
        h = _merge(ys, proj, wb, wo, h)
        h, gate_up = _ffn(h, row(ffn2_norm[layer]), wg2, wu2, wd2, final_gain, last, tiles2,
                          sides=[] if last else gate_up1)
        if not last:
            ffn1_w = (*gate_up, *wd1)
    return h
```

```python
import functools
from typing import NamedTuple

import numpy as np
import jax
import jax.numpy as jnp
from jax import lax
from jax.experimental import pallas as pl
from jax.experimental.pallas import tpu as pltpu

D_MODEL = 2048
BATCH = 2
SEQ = 4096
DEPTH = 2
N_META = 16
CHUNK = 64
PAD = CHUNK - N_META
D_FF = 5504
FFN_RES = 0.5
EPS = 1e-6
N_BRANCH = 3
BRANCH_WIDTH = 1024

GLA_HEADS, GLA_DK, GLA_DV = 4, 128, 256
GLA_GATE_RANK = 16
GLA_GATE_TAU = 16.0
RET_HEADS, RET_DK, RET_DV = 4, 256, 256
ROPE_BASE = 10000.0
HGRN_HEADS, HGRN_DK, HGRN_DV = 8, 128, 128
FORGET_FLOOR = 1e-20

LANE = 128
BF16_ROWS = 16
L_PAD = PAD + N_META + SEQ
N_CHUNKS = L_PAD // CHUNK
ROWS = BATCH * L_PAD
N_LEVELS = 6
assert 1 << N_LEVELS == CHUNK

FFN1_TILES = (640, 512)
FFN2_TILES = (640, 512)
FINAL_TILES = (832, 512)
PROJ_ROW_TILE = 1040
PROJ_TILE = 1792
MERGE_TILE = 320
CAST_TILE = 512
SIDE_BLOCK = LANE
SIDE_ROWS = 32
W_IN_BLOCK = 256
LR_PAD = CAST_TILE
W_IN_SHIFT = GLA_GATE_RANK

OFF_MG = 0
OFF_GLA = OFF_MG + N_BRANCH * D_MODEL
OFF_RET = OFF_GLA + 2 * GLA_HEADS * GLA_DK + 2 * GLA_HEADS * GLA_DV
OFF_HGRN = OFF_RET + 2 * RET_HEADS * RET_DK + 2 * RET_HEADS * RET_DV
OFF_LR = OFF_HGRN + 2 * HGRN_HEADS * HGRN_DK + 2 * HGRN_HEADS * HGRN_DV
W_PROJ = OFF_LR + LR_PAD

VMEM_LIMIT = 56 * 1024 * 1024

F32 = jnp.float32
BF16 = jnp.bfloat16


def _params(*sem):
    return pltpu.CompilerParams(dimension_semantics=sem, vmem_limit_bytes=VMEM_LIMIT)


def _rms_rows(x, gain):
    ms = jnp.mean(x * x, axis=-1, keepdims=True)
    return x * lax.rsqrt(ms + EPS) * gain


def _sigmoid(x):
    return 1.0 / (1.0 + jnp.exp(-x))


def _silu(x):
    return x * _sigmoid(x)


def _dot(a, b):
    return jnp.dot(a, b, preferred_element_type=F32)


def _dot_nt(a, b):
    return lax.dot_general(a, b, (((1,), (1,)), ((), ())), preferred_element_type=F32)


def _dot_tn(a, b):
    return lax.dot_general(a, b, (((0,), (0,)), ((), ())), preferred_element_type=F32)


def _cast_kernel(w_ref, o_ref, *, axis, valid):
    x = w_ref[...]
    idx = pl.program_id(0) * x.shape[axis] + lax.broadcasted_iota(jnp.int32, x.shape, axis)
    o_ref[...] = jnp.where(idx < valid, x, 0.0).astype(BF16)


def _cast(w, lead, axis, padded):
    rows, cols = w.shape[-2:]
    nl = len(lead)
    if axis == 0:
        block, out_shape, grid = (CAST_TILE, cols), (padded, cols), padded // CAST_TILE
        imap = lambda j: (*lead, j, 0)
        omap = lambda j: (j, 0)
    else:
        block, out_shape, grid = (rows, CAST_TILE), (padded // CAST_TILE, rows, CAST_TILE), padded // CAST_TILE
        imap = lambda j: (*lead, 0, j)
        omap = lambda j: (j, 0, 0)
    out_block = block if axis == 0 else (None,) + block
    return pl.pallas_call(
        functools.partial(_cast_kernel, axis=axis, valid=w.shape[-2 + axis]),
        grid=(grid,),
        in_specs=[pl.BlockSpec((None,) * nl + block, imap)],
        out_specs=pl.BlockSpec(out_block, omap),
        out_shape=jax.ShapeDtypeStruct(out_shape, BF16),
        compiler_params=_params("parallel"),
        name="cast",
    )(w)


SIDE_IDLE, SIDE_COPY, SIDE_ZERO, SIDE_SHIFTED, SIDE_LOW_RANK = 0, 1, 2, 3, 4


class _SideCast(NamedTuple):
    src: jax.Array
    lead: tuple
    axis: int
    padded: int
    start: int
    col_tile: int = 0


class _SideWIn(NamedTuple):
    src_t: jax.Array
    layer: int
    start: int


class _SidePlan(NamedTuple):
    table: np.ndarray
    in_specs: list
    operands: list
    out_specs: list
    out_shapes: list
    runners: list
    grid: tuple


def _w_in_blocks():
    per = lambda width: width // W_IN_BLOCK
    src = lambda col: col // W_IN_BLOCK
    gla_w = 2 * GLA_HEADS * GLA_DK + 2 * GLA_HEADS * GLA_DV
    ret_w = 2 * RET_HEADS * RET_DK + 2 * RET_HEADS * RET_DV
    hgrn_w = 2 * HGRN_HEADS * HGRN_DK + 2 * HGRN_HEADS * HGRN_DV
    lr_col = gla_w
    blocks = []
    for t in range(per(N_BRANCH * D_MODEL)):
        blocks.append((src(lr_col + ret_w + hgrn_w) + t, SIDE_SHIFTED))
    for t in range(per(gla_w)):
        blocks.append((t, SIDE_COPY))
    for t in range(per(ret_w + hgrn_w)):
        blocks.append((src(lr_col) + t, SIDE_SHIFTED))
    blocks.append((src(lr_col), SIDE_LOW_RANK))
    blocks += [(src(lr_col), SIDE_ZERO)] * (per(LR_PAD) - 1)
    assert len(blocks) == per(W_PROJ)
    return [(s, t, mode) for t, (s, mode) in enumerate(blocks)]


def _side_plan(jobs, grid):
    n_steps = int(np.prod(grid))
    rows, in_specs, operands, out_specs, out_shapes, runners = [], [], [], [], [], []

    def add_rows(blocks, start):
        n = len(blocks)
        assert start + n <= n_steps
        arr = np.asarray(blocks, np.int32).T
        tab = np.zeros((3, n_steps), np.int32)
        tab[:2, :start] = arr[:2, :1]
        tab[:, start:start + n] = arr
        tab[:2, start + n:] = arr[:2, -1:]
        base = len(rows) * n_steps
        rows.extend(tab)
        return base

    def entry(args, offset):
        *idx, tbl = args
        return tbl[offset + _linear_step(idx, grid)]

    for job in jobs:
        if isinstance(job, _SideCast):
            r, c = job.src.shape[-2:]
            lead, nl = job.lead, len(job.lead)
            if job.axis == 0:
                assert r % SIDE_BLOCK == 0 and job.padded % SIDE_BLOCK == 0
                n_valid, n_all = r // SIDE_BLOCK, job.padded // SIDE_BLOCK
                base = add_rows([(min(t, n_valid - 1), t, SIDE_COPY if t < n_valid else SIDE_ZERO)
                                 for t in range(n_all)], job.start)
                in_block = block = (SIDE_BLOCK, c)
                out_shape = (job.padded, c)
                omap = lambda *a, base=base: (entry(a, base + n_steps), 0)

                def run(tbl_ref, step, ins, out, base=base):
                    mode = tbl_ref[base + 2 * n_steps + step]

                    @pl.when(mode != SIDE_IDLE)
                    def _():
                        out[...] = jnp.where(mode == SIDE_ZERO, 0.0, ins[0][...]).astype(BF16)
            else:
                assert r % SIDE_ROWS == 0 and c % LANE == 0 and job.padded % job.col_tile == 0
                n_tiles, tile = job.padded // job.col_tile, job.col_tile
                base = add_rows([(t, t, SIDE_COPY) for t in range(r // SIDE_ROWS)], job.start)
                in_block = (SIDE_ROWS, c)
                block, out_shape = (n_tiles, SIDE_ROWS, tile), (n_tiles, r, tile)
                omap = lambda *a, base=base: (0, entry(a, base + n_steps), 0)

                def run(tbl_ref, step, ins, out, base=base, c=c, n_tiles=n_tiles, tile=tile):
                    mode = tbl_ref[base + 2 * n_steps + step]

                    @pl.when(mode != SIDE_IDLE)
                    def _():
                        for kk in range(n_tiles):
                            width = min(tile, c - kk * tile)
                            assert width > 0
                            piece = ins[0][:, kk * tile:kk * tile + width].astype(BF16)
                            if width < tile:
                                piece = jnp.concatenate(
                                    [piece, jnp.zeros((SIDE_ROWS, tile - width), BF16)], axis=1)
                            out[kk] = piece

            imap = lambda *a, base=base, lead=lead: (*lead, entry(a, base), 0)
            in_specs.append(pl.BlockSpec((None,) * nl + in_block, imap))
            operands.append(job.src)
            runners.append((1, run))
        else:
            base = add_rows(_w_in_blocks(), job.start)
            layer = job.layer
            shifts_per_block = W_IN_BLOCK // W_IN_SHIFT
            in_specs.append(pl.BlockSpec(
                (None, W_IN_BLOCK, D_MODEL),
                lambda *a, base=base, layer=layer: (layer, entry(a, base), 0)))
            in_specs.append(pl.BlockSpec(
                (None, W_IN_SHIFT, D_MODEL),
                lambda *a, base=base, layer=layer: (layer, (entry(a, base) + 1) * shifts_per_block, 0)))
            operands += [job.src_t, job.src_t]
            per_tile = PROJ_TILE // W_IN_BLOCK
            block, out_shape = (None, D_MODEL, W_IN_BLOCK), (W_PROJ // PROJ_TILE, D_MODEL, PROJ_TILE)

            def omap(*a, base=base, per_tile=per_tile):
                t = entry(a, base + n_steps)
                return (t // per_tile, 0, t % per_tile)

            def run(tbl_ref, step, ins, out, base=base):
                a_ref, b_ref = ins
                mode = tbl_ref[base + 2 * n_steps + step]

                @pl.when(mode == SIDE_SHIFTED)
                def _():
                    x = jnp.concatenate([a_ref[W_IN_SHIFT:, :], b_ref[...]], axis=0)
                    out[...] = x.astype(BF16).T

                @pl.when(mode == SIDE_COPY)
                def _():
                    out[...] = a_ref[...].astype(BF16).T

                @pl.when(mode == SIDE_LOW_RANK)
                def _():
                    row = lax.broadcasted_iota(jnp.int32, a_ref.shape, 0)
                    out[...] = jnp.where(row < GLA_GATE_RANK, a_ref[...], 0.0).astype(BF16).T

                @pl.when(mode == SIDE_ZERO)
                def _():
                    out[...] = jnp.zeros_like(out)

            runners.append((2, run))
        out_specs.append(pl.BlockSpec(block, omap))
        out_shapes.append(jax.ShapeDtypeStruct(out_shape, BF16))

    table = np.concatenate(rows) if rows else np.zeros((1,), np.int32)
    return _SidePlan(table, in_specs, operands, out_specs, out_shapes, runners, tuple(grid))


def _linear_step(idx, grid):
    step = idx[0]
    for i, n in zip(idx[1:], grid[1:]):
        step = step * n + i
    return step


def _run_sides(plan, tbl_ref, side_ins, side_outs):
    step = _linear_step([pl.program_id(d) for d in range(len(plan.grid))], plan.grid)
    k = 0
    for (n_in, run), out in zip(plan.runners, side_outs):
        run(tbl_ref, step, side_ins[k:k + n_in], out)
        k += n_in


def _hosted_call(kernel, plan, grid, in_specs, out_spec, out_shape, scratch_shapes, name, operands):
    n_side_in = len(plan.in_specs)
    n_main_in = len(in_specs)

    def body(tbl_ref, *refs):
        ins = refs[:n_main_in]
        side_ins = refs[n_main_in:n_main_in + n_side_in]
        out = refs[n_main_in + n_side_in]
        side_outs = refs[n_main_in + n_side_in + 1:n_main_in + n_side_in + 1 + len(plan.out_specs)]
        scratch = refs[n_main_in + n_side_in + 1 + len(plan.out_specs):]
        kernel(*ins, out, *scratch)
        _run_sides(plan, tbl_ref, side_ins, side_outs)

    res = pl.pallas_call(
        body,
        grid_spec=pltpu.PrefetchScalarGridSpec(
            num_scalar_prefetch=1,
            grid=grid,
            in_specs=list(in_specs) + plan.in_specs,
            out_specs=[out_spec] + plan.out_specs,
            scratch_shapes=scratch_shapes,
        ),
        out_shape=[out_shape] + plan.out_shapes,
        compiler_params=_params(*["arbitrary"] * len(grid)),
        name=name,
    )(jnp.asarray(plan.table), *operands, *plan.operands)
    return res[0], res[1:]


def _ffn_kernel(h_ref, g_ref, wg_ref, wu_ref, wd_ref, fg_ref, o_ref, u_scr, *, final, tail):
    j = pl.program_id(1)
    last = pl.num_programs(1) - 1
    tile = wg_ref.shape[1]
    half = h_ref.shape[0] // 2
    halves = (slice(0, half), slice(half, 2 * half))

    def partial_sum(u, width):
        a = _dot(u, wg_ref[:, :width])
        b = _dot(u, wu_ref[:, :width])
        act = (_silu(a) * b).astype(BF16)
        return _dot(act, wd_ref[:width, :])

    @pl.when(j == 0)
    def _():
        for r in halves:
            u = _rms_rows(h_ref[r, :], g_ref[...]).astype(BF16)
            u_scr[r, :] = u
            o_ref[r, :] = partial_sum(u, tile)

    @pl.when((j > 0) & (j < last))
    def _():
        o_ref[...] += partial_sum(u_scr[...], tile)

    @pl.when(j == last)
    def _():
        for r in (halves if final else (slice(None),)):
            y = h_ref[r, :] + FFN_RES * (o_ref[r, :] + partial_sum(u_scr[r, :], tail))
            if final:
                y = _rms_rows(y, fg_ref[...])
            o_ref[r, :] = y


def _ff_padded(tf):
    return pl.cdiv(D_FF, tf) * tf


def _ffn(h, gain, wg, wu, wd, final_gain, final, tiles, sides=()):
    tm, tf = tiles
    assert ROWS % tm == 0 and wg.shape == (_ff_padded(tf) // tf, D_MODEL, tf) and wd.shape[0] == _ff_padded(tf)
    grid = (ROWS // tm, wg.shape[0])
    tail = D_FF - (grid[1] - 1) * tf
    plan = _side_plan(sides, grid)
    if final:
        assert L_PAD % tm == 0
        per_seq = L_PAD // tm
        h = h.reshape(BATCH, L_PAD, D_MODEL)
        rows_spec = pl.BlockSpec((None, tm, D_MODEL), lambda i, j, tbl: (i // per_seq, i % per_seq, 0))
        out_shape = jax.ShapeDtypeStruct((BATCH, SEQ, D_MODEL), F32)
    else:
        rows_spec = pl.BlockSpec((tm, D_MODEL), lambda i, j, tbl: (i, 0))
        out_shape = jax.ShapeDtypeStruct((ROWS, D_MODEL), F32)
    return _hosted_call(
        functools.partial(_ffn_kernel, final=final, tail=tail), plan, grid,
        in_specs=[
            rows_spec,
            pl.BlockSpec((1, D_MODEL), lambda i, j, tbl: (0, 0)),
            pl.BlockSpec((None, D_MODEL, tf), lambda i, j, tbl: (j, 0, 0)),
            pl.BlockSpec((None, D_MODEL, tf), lambda i, j, tbl: (j, 0, 0)),
            pl.BlockSpec((tf, D_MODEL), lambda i, j, tbl: (j, 0)),
            pl.BlockSpec((1, D_MODEL), lambda i, j, tbl: (0, 0)),
        ],
        out_spec=rows_spec,
        out_shape=out_shape,
        scratch_shapes=[pltpu.VMEM((tm, D_MODEL), BF16)],
        name="ffn",
        operands=(h, gain, wg, wu, wd, final_gain),
    )


def _proj_kernel(h_ref, g_ref, w_ref, o_ref, u_scr):
    j = pl.program_id(1)

    @pl.when(j == 0)
    def _():
        rows = h_ref.shape[0]
        half = -(-rows // 2 // BF16_ROWS) * BF16_ROWS
        for r in (slice(0, half), slice(half, rows)):
            u = _rms_rows(h_ref[r, :], g_ref[...]).astype(BF16)
            u_scr[r, :] = u
            o_ref[r, :] = _dot(u, w_ref[...]).astype(o_ref.dtype)

    @pl.when(j > 0)
    def _():
        o_ref[...] = _dot(u_scr[...], w_ref[...]).astype(o_ref.dtype)


def _proj(h, gain, w, sides=()):
    grid = (ROWS // PROJ_ROW_TILE, W_PROJ // PROJ_TILE)
    plan = _side_plan(sides, grid)
    return _hosted_call(
        _proj_kernel, plan, grid,
        in_specs=[
            pl.BlockSpec((PROJ_ROW_TILE, D_MODEL), lambda i, j, tbl: (i, 0)),
            pl.BlockSpec((1, D_MODEL), lambda i, j, tbl: (0, 0)),
            pl.BlockSpec((None, D_MODEL, PROJ_TILE), lambda i, j, tbl: (j, 0, 0)),
        ],
        out_spec=pl.BlockSpec((PROJ_ROW_TILE, PROJ_TILE), lambda i, j, tbl: (i, j)),
        out_shape=jax.ShapeDtypeStruct((ROWS, W_PROJ), BF16),
        scratch_shapes=[pltpu.VMEM((PROJ_ROW_TILE, D_MODEL), BF16)],
        name="in_proj",
        operands=(h, gain, w),
    )


def _decay_tables():
    c = CHUNK
    blocks = []
    r = np.arange(c)
    for l in range(1, N_LEVELS):
        b = 1 << l
        m = np.zeros((c, c), np.float32)
        for i in range(c):
            p = i % (2 * b)
            s = i - p + b
            if p >= b:
                m[i, s:i + 1] = 1.0
            else:
                m[i, i + 1:s] = 1.0
        blocks.append(m)
    blocks.append((r[None, :] <= r[:, None]).astype(np.float32))
    wc = np.concatenate(blocks, axis=0)
    wc = np.concatenate([wc] * 3, axis=1)

    masks = np.zeros((N_LEVELS + 1, c, c), np.float32)
    for l in range(N_LEVELS):
        b = 1 << l
        same = (r[:, None] // (2 * b)) == (r[None, :] // (2 * b))
        up = (r[:, None] & b) != 0
        lo = (r[None, :] & b) == 0
        masks[l] = (same & up & lo).astype(np.float32)
    masks[N_LEVELS] = np.eye(c, dtype=np.float32)
    return wc, masks


def _gla_core(qs, ks, vs, gs, wc_ref, mk_ref, s_ref, slot_of, n_heads, dk, dv):
    c = CHUNK
    seqs = range(len(qs))
    heads = range(n_heads)
    base = (N_LEVELS - 1) * c

    def split3(g):
        g_hi = g.astype(BF16)
        r1 = g - g_hi.astype(F32)
        g_mid = r1.astype(BF16)
        g_lo = (r1 - g_mid.astype(F32)).astype(BF16)
        return jnp.concatenate([g_hi, g_mid, g_lo], axis=0)

    sums = [_dot(wc_ref[...], split3(gs[b])) for b in seqs]
    ex = [jnp.exp(sums[b][:base]) for b in seqs]
    cum = [sums[b][base:base + c] for b in seqs]
    total = [cum[b][c - 1:c] for b in seqs]

    on_diag = [qs[b] * ks[b] for b in seqs]
    below = [qs[b] * jnp.exp(gs[b]) * pltpu.roll(ks[b], 1, axis=0) for b in seqs]
    scores = [[None] * n_heads for _ in seqs]
    for h in heads:
        lanes = slice(h * dk, (h + 1) * dk)
        for b in seqs:
            scores[b][h] = (mk_ref[N_LEVELS] * jnp.sum(on_diag[b][:, lanes], axis=-1, keepdims=True)
                            + mk_ref[0] * jnp.sum(below[b][:, lanes], axis=-1, keepdims=True))
    rows = lax.broadcasted_iota(jnp.int32, (c, 1), 0)
    for l in range(1, N_LEVELS):
        upper = (rows & (1 << l)) != 0
        x = [(jnp.where(upper, qs[b], ks[b]) * ex[b][(l - 1) * c:l * c]).astype(BF16) for b in seqs]
        for h in heads:
            for b in seqs:
                xh = x[b][:, h * dk:(h + 1) * dk]
                scores[b][h] = scores[b][h] + mk_ref[l] * _dot_nt(xh, xh)

    vb = [vs[b].astype(BF16) for b in seqs]
    qt = [(qs[b] * jnp.exp(cum[b])).astype(BF16) for b in seqs]
    kt = [(ks[b] * jnp.exp(total[b] - cum[b])).astype(BF16) for b in seqs]
    tot = [jnp.exp(total[b]) for b in seqs]
    outs = [[None] * n_heads for _ in seqs]
    for h in heads:
        for b in seqs:
            slot = slot_of(b, h)
            st = s_ref[slot]
            vh = vb[b][:, h * dv:(h + 1) * dv]
            outs[b][h] = (_dot(scores[b][h].astype(BF16), vh)
                          + _dot_nt(qt[b][:, h * dk:(h + 1) * dk], st.astype(BF16)))
            s_ref[slot] = st * tot[b][:, h * dk:(h + 1) * dk] + _dot_tn(vh, kt[b][:, h * dk:(h + 1) * dk])
    return outs


def _head_rms(outs, gain, gate):
    dv = outs[0].shape[-1]
    ys = []
    for h, o in enumerate(outs):
        ms = jnp.mean(o * o, axis=-1, keepdims=True)
        ys.append(o * lax.rsqrt(ms + EPS))
    y = jnp.concatenate(ys, axis=-1)
    return y * gain * _silu(gate)


def _log_sigmoid(z):
    return jnp.minimum(z, 0.0) - jnp.log(1.0 + jnp.exp(-jnp.abs(z)))


def _gla_kernel(q_ref, k_ref, v_ref, gate_ref, lr_ref, w2_ref, b_ref, norm_ref, wc_ref, mk_ref,
                o_ref, s_ref):
    @pl.when(pl.program_id(0) == 0)
    def _():
        s_ref[...] = jnp.zeros_like(s_ref)

    seqs = range(BATCH)
    z = [_dot(lr_ref[b], w2_ref[...]) + b_ref[...] for b in seqs]
    g = [_log_sigmoid(z[b]) * (1.0 / GLA_GATE_TAU) for b in seqs]
    q = [q_ref[b].astype(F32) * (GLA_DK ** -0.5) for b in seqs]
    k = [k_ref[b].astype(F32) for b in seqs]
    v = [v_ref[b].astype(F32) for b in seqs]
    outs = _gla_core(q, k, v, g, wc_ref, mk_ref, s_ref, lambda b, h: b * GLA_HEADS + h,
                     GLA_HEADS, GLA_DK, GLA_DV)
    for b in seqs:
        o_ref[b] = _head_rms(outs[b], norm_ref[...], gate_ref[b].astype(F32)).astype(o_ref.dtype)


def _hgrn_kernel(q_ref, f_ref, i_ref, gate_ref, lb_ref, norm_ref, wc_ref, mk_ref, o_ref, s_ref, *, layer):
    @pl.when(pl.program_id(0) == 0)
    def _():
        s_ref[...] = jnp.zeros_like(s_ref)

    logits = lb_ref[...]
    e = jnp.exp(logits - jnp.max(logits, axis=0, keepdims=True))
    soft = e / jnp.sum(e, axis=0, keepdims=True)
    lb = jnp.zeros((1, logits.shape[1]), F32)
    for l in range(1, layer + 1):
        lb = lb + soft[l:l + 1]

    seqs = range(BATCH)
    group = HGRN_HEADS // 2
    for first in range(0, HGRN_HEADS, group):
        qk = slice(first * HGRN_DK, (first + group) * HGRN_DK)
        vv = slice(first * HGRN_DV, (first + group) * HGRN_DV)
        lbg = lb[:, qk]
        hf = [f_ref[b, :, qk].astype(F32) for b in seqs]
        forget = [lbg + (1.0 - lbg) * _sigmoid(hf[b]) for b in seqs]
        g = [jnp.log(jnp.maximum(forget[b], FORGET_FLOOR)) for b in seqs]
        k = [1.0 - forget[b] for b in seqs]
        v = [_silu(i_ref[b, :, vv].astype(F32)) for b in seqs]
        q = [q_ref[b, :, qk].astype(F32) for b in seqs]
        outs = _gla_core(q, k, v, g, wc_ref, mk_ref, s_ref,
                         lambda b, h, first=first: b * HGRN_HEADS + first + h, group, HGRN_DK, HGRN_DV)
        for b in seqs:
            y = _head_rms(outs[b], norm_ref[:, vv], gate_ref[b, :, vv].astype(F32))
            o_ref[b, :, vv] = y.astype(o_ref.dtype)


def _chunk_block(c):
    return (c + N_CHUNKS - 1) % N_CHUNKS


def _col_spec(width, offset):
    assert offset % width == 0
    blk = offset // width
    return pl.BlockSpec((BATCH, CHUNK, width), lambda c, tbl: (0, _chunk_block(c), blk))


def _const_spec(shape):
    nd = len(shape)
    return pl.BlockSpec(shape, lambda c, tbl: (0,) * nd)


def _mixer_call(kernel, name, width, n_heads, dv, dk, in_specs, operands, sides):
    grid = (N_CHUNKS,)
    return _hosted_call(
        kernel, _side_plan(sides, grid), grid,
        in_specs=in_specs,
        out_spec=pl.BlockSpec((BATCH, CHUNK, width), lambda c, tbl: (0, _chunk_block(c), 0)),
        out_shape=jax.ShapeDtypeStruct((BATCH, L_PAD, width), BF16),
        scratch_shapes=[pltpu.VMEM((BATCH * n_heads, dv, dk), F32)],
        name=name,
        operands=operands,
    )


def _gla(proj, w2, bias, norm, wc, masks, sides=()):
    qk = GLA_HEADS * GLA_DK
    vw = GLA_HEADS * GLA_DV
    return _mixer_call(
        _gla_kernel, "gla", vw, GLA_HEADS, GLA_DV, GLA_DK,
        in_specs=[
            _col_spec(qk, OFF_GLA),
            _col_spec(qk, OFF_GLA + qk),
            _col_spec(vw, OFF_GLA + 2 * qk),
            _col_spec(vw, OFF_GLA + 2 * qk + vw),
            _col_spec(LANE, OFF_LR),
            _const_spec(w2.shape),
            _const_spec(bias.shape),
            _const_spec(norm.shape),
            _const_spec(wc.shape),
            _const_spec(masks.shape),
        ],
        operands=(proj, proj, proj, proj, proj, w2, bias, norm, wc, masks),
        sides=sides,
    )


def _hgrn(proj, lb_logits, norm, wc, masks, layer, sides=()):
    w = HGRN_HEADS * HGRN_DK
    return _mixer_call(
        functools.partial(_hgrn_kernel, layer=layer), "hgrn", w, HGRN_HEADS, HGRN_DV, HGRN_DK,
        in_specs=[
            _col_spec(w, OFF_HGRN),
            _col_spec(w, OFF_HGRN + w),
            _col_spec(w, OFF_HGRN + 2 * w),
            _col_spec(w, OFF_HGRN + 3 * w),
            _const_spec(lb_logits.shape),
            _const_spec(norm.shape),
            _const_spec(wc.shape),
            _const_spec(masks.shape),
        ],
        operands=(proj, proj, proj, proj, lb_logits, norm, wc, masks),
        sides=sides,
    )


def _ret_kernel(q_ref, k_ref, v_ref, gate_ref, cos_ref, sin_ref, qd_ref, kd_ref, cd_ref, im_ref, norm_ref,
                o_ref, s_ref):
    @pl.when(pl.program_id(0) == 0)
    def _():
        s_ref[...] = jnp.zeros_like(s_ref)

    half = RET_DK // 2
    cos = cos_ref[...]
    sin = sin_ref[...]
    seqs = range(BATCH)
    pairs = [(b, h) for h in range(RET_HEADS) for b in seqs]

    def rotate(ref, b, h):
        lo = h * RET_DK
        t1 = ref[b, :, lo:lo + half].astype(F32)
        t2 = ref[b, :, lo + half:lo + RET_DK].astype(F32)
        return jnp.concatenate([t1 * cos - t2 * sin, t1 * sin + t2 * cos], axis=-1)

    qr = {p: rotate(q_ref, *p) * (RET_DK ** -0.5) for p in pairs}
    kr = {p: rotate(k_ref, *p) for p in pairs}
    scores = {p: _dot_nt(qr[p].astype(BF16), kr[p].astype(BF16)) * im_ref[p[1]] for p in pairs}
    outs = {}
    for b, h in pairs:
        p = (b, h)
        vh = v_ref[b, :, h * RET_DV:(h + 1) * RET_DV]
        st = s_ref[b * RET_HEADS + h]
        outs[p] = (_dot(scores[p].astype(BF16), vh)
                   + _dot_nt((qr[p] * qd_ref[h]).astype(BF16), st.astype(BF16)))
        s_ref[b * RET_HEADS + h] = st * cd_ref[h] + _dot_tn(vh, (kr[p] * kd_ref[h]).astype(BF16))
    for b in seqs:
        ys = []
        for h in range(RET_HEADS):
            o = outs[(b, h)]
            o = o - jnp.mean(o, axis=-1, keepdims=True)
            ms = jnp.mean(o * o, axis=-1, keepdims=True)
            ys.append(o * lax.rsqrt(ms + EPS))
        y = jnp.concatenate(ys, axis=-1)
        o_ref[b] = (y * norm_ref[...] * _silu(gate_ref[b].astype(F32))).astype(o_ref.dtype)


def _ret(proj, cos, sin, qd, kd, cd, im, norm, sides=()):
    w = RET_HEADS * RET_DK
    half = RET_DK // 2
    return _mixer_call(
        _ret_kernel, "retention", w, RET_HEADS, RET_DV, RET_DK,
        in_specs=[
            _col_spec(w, OFF_RET),
            _col_spec(w, OFF_RET + w),
            _col_spec(w, OFF_RET + 2 * w),
            _col_spec(w, OFF_RET + 3 * w),
            pl.BlockSpec((CHUNK, half), lambda c, tbl: (c, 0)),
            pl.BlockSpec((CHUNK, half), lambda c, tbl: (c, 0)),
            _const_spec(qd.shape),
            _const_spec(kd.shape),
            _const_spec(cd.shape),
            _const_spec(im.shape),
            _const_spec(norm.shape),
        ],
        operands=(proj, proj, proj, proj, cos, sin, qd, kd, cd, im, norm),
        sides=sides,
    )


def _ret_tables():
    f32 = jnp.float32
    half = RET_DK // 2
    pos = jnp.arange(L_PAD, dtype=f32) - PAD
    inv_freq = ROPE_BASE ** (-jnp.arange(half, dtype=f32) / half)
    ang = pos[:, None] * inv_freq[None, :]
    log_gamma = jnp.log(1.0 - 2.0 ** (-5.0 - jnp.arange(RET_HEADS, dtype=f32)))
    idx = jnp.arange(CHUNK, dtype=f32)
    rel = idx[:, None] - idx[None, :]
    causal = (rel >= 0)[None]
    intra = jnp.where(causal, jnp.exp(jnp.where(causal, rel[None], 0.0) * log_gamma[:, None, None]), 0.0)
    q_decay = jnp.exp((idx[None, :] + 1.0) * log_gamma[:, None])[..., None]
    k_decay = jnp.exp((CHUNK - 1.0 - idx[None, :]) * log_gamma[:, None])[..., None]
    chunk_decay = jnp.exp(CHUNK * log_gamma)[:, None, None]
    return jnp.cos(ang), jnp.sin(ang), q_decay, k_decay, chunk_decay, intra


def _merge_kernel(y0_ref, y1_ref, y2_ref, mg_ref, wb_ref, wo_ref, h_ref, o_ref):
    merged = None
    for n, y_ref in enumerate((y0_ref, y1_ref, y2_ref)):
        gate = _sigmoid(mg_ref[:, n * D_MODEL:(n + 1) * D_MODEL].astype(F32))
        t = _dot(y_ref[...], wb_ref[n * BRANCH_WIDTH:(n + 1) * BRANCH_WIDTH, :]) * gate
        merged = t if merged is None else merged + t
    o_ref[...] = h_ref[...] + _dot(merged.astype(BF16), wo_ref[...])


def _merge(ys, proj, wb, wo, h):
    assert OFF_MG == 0
    rows = lambda width: pl.BlockSpec((MERGE_TILE, width), lambda i: (i, 0))
    whole = lambda a: pl.BlockSpec(a.shape, lambda i: (0, 0))
    return pl.pallas_call(
        _merge_kernel,
        grid=(ROWS // MERGE_TILE,),
        in_specs=[rows(BRANCH_WIDTH)] * N_BRANCH + [rows(N_BRANCH * D_MODEL)]
        + [whole(wb), whole(wo), rows(D_MODEL)],
        out_specs=rows(D_MODEL),
        out_shape=jax.ShapeDtypeStruct((ROWS, D_MODEL), F32),
        compiler_params=_params("parallel"),
        name="merge",
    )(*ys, proj, wb, wo, h)


def kernel(x, meta_tokens, ffn1_norm, ffn1_w_gate, ffn1_w_up, ffn1_w_down, mix_norm, w_in, gla_w_gate2, gla_b_gate, gla_norm, ret_norm, hgrn_lb_logits, hgrn_norm, w_branch, w_out, ffn2_norm, ffn2_w_gate, ffn2_w_up, ffn2_w_down, final_norm):
    b = x.shape[0]
    meta = jnp.broadcast_to(meta_tokens[None].astype(x.dtype), (b, N_META, D_MODEL))
    h = jnp.concatenate([x, jnp.zeros((b, PAD, D_MODEL), x.dtype), meta], axis=1).reshape(ROWS, D_MODEL)

    wc_np, masks_np = _decay_tables()
    wc = jnp.asarray(wc_np, BF16)
    masks = jnp.asarray(masks_np, F32)
    cos, sin, qd, kd, cd, im = _ret_tables()

    def ffn_casts(wg, wu, wd, layer, tiles):
        padded = _ff_padded(tiles[1])
        return (_SideCast(wg, (layer,), 1, padded, 0, tiles[1]), _SideCast(wu, (layer,), 1, padded, 0, tiles[1]),
                _SideCast(wd, (layer,), 0, padded, 0))

    row = lambda v: v.reshape(1, -1).astype(F32)
    final_gain = row(final_norm)
    w_in_t = jnp.swapaxes(w_in, 1, 2)
    wb_rows = w_branch.reshape(DEPTH, N_BRANCH * BRANCH_WIDTH, D_MODEL)
    padded = _ff_padded(FFN1_TILES[1])
    ffn1_w = (_cast(ffn1_w_gate, (0,), 1, padded), _cast(ffn1_w_up, (0,), 1, padded),
              _cast(ffn1_w_down, (0,), 0, padded))
    for layer in range(DEPTH):
        last = layer == DEPTH - 1
        tiles2 = FINAL_TILES if last else FFN2_TILES
        h, (w_proj,) = _ffn(h, row(ffn1_norm[layer]), *ffn1_w, final_gain, False, FFN1_TILES,
                            sides=[_SideWIn(w_in_t, layer, 0)])
        proj, _ = _proj(h, row(mix_norm[layer]), w_proj)
        jobs = [[j] for j in ffn_casts(ffn2_w_gate, ffn2_w_up, ffn2_w_down, layer, tiles2)]
        jobs[0].append(_SideCast(wb_rows, (layer,), 0, N_BRANCH * BRANCH_WIDTH, 0))
        jobs[1].append(_SideCast(w_out, (layer,), 0, D_MODEL, 0))
        if not last:
            for mine, j in zip(jobs, ffn_casts(ffn1_w_gate, ffn1_w_up, ffn1_w_down, layer + 1, FFN1_TILES)):
                mine.append(j)
        w2 = jnp.pad(gla_w_gate2[layer], ((0, LANE - GLA_GATE_RANK), (0, 0))).astype(BF16)
        proj_seq = proj.reshape(b, L_PAD, W_PROJ)
        y_gla, (wg2, wb, *wg1) = _gla(proj_seq, w2, row(gla_b_gate[layer]), row(gla_norm[layer]), wc, masks,
                                      sides=jobs[0])
        y_ret, (wu2, wo, *wu1) = _ret(proj_seq, cos, sin, qd, kd, cd, im, row(ret_norm[layer]), sides=jobs[1])
        y_hgrn, (wd2, *wd1) = _hgrn(proj_seq, hgrn_lb_logits.astype(F32), row(hgrn_norm[layer]), wc, masks,
                                    layer, sides=jobs[2])
        ys = [y.reshape(ROWS, BRANCH_WIDTH) for y in (y_gla, y_ret, y_hgrn)]
        h = _merge(ys, proj, wb, wo, h)
        h, _ = _ffn(h, row(ffn2_norm[layer]), wg2, wu2, wd2, final_gain, last, tiles2)
        if not last:
            ffn1_w = (*wg1, *wu1, *wd1)
    return h
```

```python
import functools
from typing import NamedTuple

import numpy as np
import jax
import jax.numpy as jnp
from jax import lax
from jax.experimental import pallas as pl
from jax.experimental.pallas import tpu as pltpu

D_MODEL = 2048
BATCH = 2
SEQ = 4096
DEPTH = 2
N_META = 16
CHUNK = 64
PAD = CHUNK - N_META
D_FF = 5504
FFN_RES = 0.5
EPS = 1e-6
N_BRANCH = 3
BRANCH_WIDTH = 1024

GLA_HEADS, GLA_DK, GLA_DV = 4, 128, 256
GLA_GATE_RANK = 16
GLA_GATE_TAU = 16.0
RET_HEADS, RET_DK, RET_DV = 4, 256, 256
ROPE_BASE = 10000.0
HGRN_HEADS, HGRN_DK, HGRN_DV = 8, 128, 128
FORGET_FLOOR = 1e-20

LANE = 128
L_PAD = PAD + N_META + SEQ
N_CHUNKS = L_PAD // CHUNK
ROWS = BATCH * L_PAD
N_LEVELS = 6
assert 1 << N_LEVELS == CHUNK

FFN1_TILES = (640, 512)
FFN2_TILES = (640, 512)
FINAL_TILES = (832, 512)
PROJ_ROW_TILE = 832
PROJ_TILE = 1792
MERGE_TILE = 320
CAST_TILE = 512
SIDE_BLOCK = LANE
SIDE_ROWS = 32
W_IN_BLOCK = 256
LR_PAD = CAST_TILE
W_IN_SHIFT = GLA_GATE_RANK

OFF_MG = 0
OFF_GLA = OFF_MG + N_BRANCH * D_MODEL
OFF_RET = OFF_GLA + 2 * GLA_HEADS * GLA_DK + 2 * GLA_HEADS * GLA_DV
OFF_HGRN = OFF_RET + 2 * RET_HEADS * RET_DK + 2 * RET_HEADS * RET_DV
OFF_LR = OFF_HGRN + 2 * HGRN_HEADS * HGRN_DK + 2 * HGRN_HEADS * HGRN_DV
W_PROJ = OFF_LR + LR_PAD

VMEM_LIMIT = 56 * 1024 * 1024

F32 = jnp.float32
BF16 = jnp.bfloat16


def _params(*sem):
    return pltpu.CompilerParams(dimension_semantics=sem, vmem_limit_bytes=VMEM_LIMIT)


def _rms_rows(x, gain):
    ms = jnp.mean(x * x, axis=-1, keepdims=True)
    return x * lax.rsqrt(ms + EPS) * gain


def _sigmoid(x):
    return 1.0 / (1.0 + jnp.exp(-x))


def _silu(x):
    return x * _sigmoid(x)


def _dot(a, b):
    return jnp.dot(a, b, preferred_element_type=F32)


def _dot_nt(a, b):
    return lax.dot_general(a, b, (((1,), (1,)), ((), ())), preferred_element_type=F32)


def _dot_tn(a, b):
    return lax.dot_general(a, b, (((0,), (0,)), ((), ())), preferred_element_type=F32)


def _cast_kernel(w_ref, o_ref, *, axis, valid):
    x = w_ref[...]
    idx = pl.program_id(0) * x.shape[axis] + lax.broadcasted_iota(jnp.int32, x.shape, axis)
    o_ref[...] = jnp.where(idx < valid, x, 0.0).astype(BF16)


def _cast(w, lead, axis, padded):
    rows, cols = w.shape[-2:]
    nl = len(lead)
    if axis == 0:
        block, out_shape, grid = (CAST_TILE, cols), (padded, cols), padded // CAST_TILE
        imap = lambda j: (*lead, j, 0)
        omap = lambda j: (j, 0)
    else:
        block, out_shape, grid = (rows, CAST_TILE), (padded // CAST_TILE, rows, CAST_TILE), padded // CAST_TILE
        imap = lambda j: (*lead, 0, j)
        omap = lambda j: (j, 0, 0)
    out_block = block if axis == 0 else (None,) + block
    return pl.pallas_call(
        functools.partial(_cast_kernel, axis=axis, valid=w.shape[-2 + axis]),
        grid=(grid,),
        in_specs=[pl.BlockSpec((None,) * nl + block, imap)],
        out_specs=pl.BlockSpec(out_block, omap),
        out_shape=jax.ShapeDtypeStruct(out_shape, BF16),
        compiler_params=_params("parallel"),
        name="cast",
    )(w)


SIDE_IDLE, SIDE_COPY, SIDE_ZERO, SIDE_SHIFTED, SIDE_LOW_RANK = 0, 1, 2, 3, 4


class _SideCast(NamedTuple):
    src: jax.Array
    lead: tuple
    axis: int
    padded: int
    start: int
    col_tile: int = 0


class _SideWIn(NamedTuple):
    src_t: jax.Array
    layer: int
    start: int


class _SidePlan(NamedTuple):
    table: np.ndarray
    in_specs: list
    operands: list
    out_specs: list
    out_shapes: list
    runners: list
    grid: tuple


def _w_in_blocks():
    per = lambda width: width // W_IN_BLOCK
    src = lambda col: col // W_IN_BLOCK
    gla_w = 2 * GLA_HEADS * GLA_DK + 2 * GLA_HEADS * GLA_DV
    ret_w = 2 * RET_HEADS * RET_DK + 2 * RET_HEADS * RET_DV
    hgrn_w = 2 * HGRN_HEADS * HGRN_DK + 2 * HGRN_HEADS * HGRN_DV
    lr_col = gla_w
    blocks = []
    for t in range(per(N_BRANCH * D_MODEL)):
        blocks.append((src(lr_col + ret_w + hgrn_w) + t, SIDE_SHIFTED))
    for t in range(per(gla_w)):
        blocks.append((t, SIDE_COPY))
    for t in range(per(ret_w + hgrn_w)):
        blocks.append((src(lr_col) + t, SIDE_SHIFTED))
    blocks.append((src(lr_col), SIDE_LOW_RANK))
    blocks += [(src(lr_col), SIDE_ZERO)] * (per(LR_PAD) - 1)
    assert len(blocks) == per(W_PROJ)
    return [(s, t, mode) for t, (s, mode) in enumerate(blocks)]


def _side_plan(jobs, grid):
    n_steps = int(np.prod(grid))
    rows, in_specs, operands, out_specs, out_shapes, runners = [], [], [], [], [], []

    def add_rows(blocks, start):
        n = len(blocks)
        assert start + n <= n_steps
        arr = np.asarray(blocks, np.int32).T
        tab = np.zeros((3, n_steps), np.int32)
        tab[:2, :start] = arr[:2, :1]
        tab[:, start:start + n] = arr
        tab[:2, start + n:] = arr[:2, -1:]
        base = len(rows) * n_steps
        rows.extend(tab)
        return base

    def entry(args, offset):
        *idx, tbl = args
        return tbl[offset + _linear_step(idx, grid)]

    for job in jobs:
        if isinstance(job, _SideCast):
            r, c = job.src.shape[-2:]
            lead, nl = job.lead, len(job.lead)
            if job.axis == 0:
                assert r % SIDE_BLOCK == 0 and job.padded % SIDE_BLOCK == 0
                n_valid, n_all = r // SIDE_BLOCK, job.padded // SIDE_BLOCK
                base = add_rows([(min(t, n_valid - 1), t, SIDE_COPY if t < n_valid else SIDE_ZERO)
                                 for t in range(n_all)], job.start)
                in_block = block = (SIDE_BLOCK, c)
                out_shape = (job.padded, c)
                omap = lambda *a, base=base: (entry(a, base + n_steps), 0)

                def run(tbl_ref, step, ins, out, base=base):
                    mode = tbl_ref[base + 2 * n_steps + step]

                    @pl.when(mode != SIDE_IDLE)
                    def _():
                        out[...] = jnp.where(mode == SIDE_ZERO, 0.0, ins[0][...]).astype(BF16)
            else:
                assert r % SIDE_ROWS == 0 and c % LANE == 0 and job.padded % job.col_tile == 0
                n_tiles, tile = job.padded // job.col_tile, job.col_tile
                base = add_rows([(t, t, SIDE_COPY) for t in range(r // SIDE_ROWS)], job.start)
                in_block = (SIDE_ROWS, c)
                block, out_shape = (n_tiles, SIDE_ROWS, tile), (n_tiles, r, tile)
                omap = lambda *a, base=base: (0, entry(a, base + n_steps), 0)

                def run(tbl_ref, step, ins, out, base=base, c=c, n_tiles=n_tiles, tile=tile):
                    mode = tbl_ref[base + 2 * n_steps + step]

                    @pl.when(mode != SIDE_IDLE)
                    def _():
                        for kk in range(n_tiles):
                            width = min(tile, c - kk * tile)
                            assert width > 0
                            piece = ins[0][:, kk * tile:kk * tile + width].astype(BF16)
                            if width < tile:
                                piece = jnp.concatenate(
                                    [piece, jnp.zeros((SIDE_ROWS, tile - width), BF16)], axis=1)
                            out[kk] = piece

            imap = lambda *a, base=base, lead=lead: (*lead, entry(a, base), 0)
            in_specs.append(pl.BlockSpec((None,) * nl + in_block, imap))
            operands.append(job.src)
            runners.append((1, run))
        else:
            base = add_rows(_w_in_blocks(), job.start)
            layer = job.layer
            shifts_per_block = W_IN_BLOCK // W_IN_SHIFT
            in_specs.append(pl.BlockSpec(
                (None, W_IN_BLOCK, D_MODEL),
                lambda *a, base=base, layer=layer: (layer, entry(a, base), 0)))
            in_specs.append(pl.BlockSpec(
                (None, W_IN_SHIFT, D_MODEL),
                lambda *a, base=base, layer=layer: (layer, (entry(a, base) + 1) * shifts_per_block, 0)))
            operands += [job.src_t, job.src_t]
            per_tile = PROJ_TILE // W_IN_BLOCK
            block, out_shape = (None, D_MODEL, W_IN_BLOCK), (W_PROJ // PROJ_TILE, D_MODEL, PROJ_TILE)

            def omap(*a, base=base, per_tile=per_tile):
                t = entry(a, base + n_steps)
                return (t // per_tile, 0, t % per_tile)

            def run(tbl_ref, step, ins, out, base=base):
                a_ref, b_ref = ins
                mode = tbl_ref[base + 2 * n_steps + step]

                @pl.when(mode == SIDE_SHIFTED)
                def _():
                    x = jnp.concatenate([a_ref[W_IN_SHIFT:, :], b_ref[...]], axis=0)
                    out[...] = x.astype(BF16).T

                @pl.when(mode == SIDE_COPY)
                def _():
                    out[...] = a_ref[...].astype(BF16).T

                @pl.when(mode == SIDE_LOW_RANK)
                def _():
                    row = lax.broadcasted_iota(jnp.int32, a_ref.shape, 0)
                    out[...] = jnp.where(row < GLA_GATE_RANK, a_ref[...], 0.0).astype(BF16).T

                @pl.when(mode == SIDE_ZERO)
                def _():
                    out[...] = jnp.zeros_like(out)

            runners.append((2, run))
        out_specs.append(pl.BlockSpec(block, omap))
        out_shapes.append(jax.ShapeDtypeStruct(out_shape, BF16))

    table = np.concatenate(rows) if rows else np.zeros((1,), np.int32)
    return _SidePlan(table, in_specs, operands, out_specs, out_shapes, runners, tuple(grid))


def _linear_step(idx, grid):
    step = idx[0]
    for i, n in zip(idx[1:], grid[1:]):
        step = step * n + i
    return step


def _run_sides(plan, tbl_ref, side_ins, side_outs):
    step = _linear_step([pl.program_id(d) for d in range(len(plan.grid))], plan.grid)
    k = 0
    for (n_in, run), out in zip(plan.runners, side_outs):
        run(tbl_ref, step, side_ins[k:k + n_in], out)
        k += n_in


def _hosted_call(kernel, plan, grid, in_specs, out_spec, out_shape, scratch_shapes, name, operands):
    n_side_in = len(plan.in_specs)
    n_main_in = len(in_specs)

    def body(tbl_ref, *refs):
        ins = refs[:n_main_in]
        side_ins = refs[n_main_in:n_main_in + n_side_in]
        out = refs[n_main_in + n_side_in]
        side_outs = refs[n_main_in + n_side_in + 1:n_main_in + n_side_in + 1 + len(plan.out_specs)]
        scratch = refs[n_main_in + n_side_in + 1 + len(plan.out_specs):]
        kernel(*ins, out, *scratch)
        _run_sides(plan, tbl_ref, side_ins, side_outs)

    res = pl.pallas_call(
        body,
        grid_spec=pltpu.PrefetchScalarGridSpec(
            num_scalar_prefetch=1,
            grid=grid,
            in_specs=list(in_specs) + plan.in_specs,
            out_specs=[out_spec] + plan.out_specs,
            scratch_shapes=scratch_shapes,
        ),
        out_shape=[out_shape] + plan.out_shapes,
        compiler_params=_params(*["arbitrary"] * len(grid)),
        name=name,
    )(jnp.asarray(plan.table), *operands, *plan.operands)
    return res[0], res[1:]


def _ffn_kernel(h_ref, g_ref, wg_ref, wu_ref, wd_ref, fg_ref, o_ref, u_scr, *, final, tail):
    j = pl.program_id(1)
    last = pl.num_programs(1) - 1
    tile = wg_ref.shape[1]
    half = h_ref.shape[0] // 2
    halves = (slice(0, half), slice(half, 2 * half))

    def partial_sum(u, width):
        a = _dot(u, wg_ref[:, :width])
        b = _dot(u, wu_ref[:, :width])
        act = (_silu(a) * b).astype(BF16)
        return _dot(act, wd_ref[:width, :])

    @pl.when(j == 0)
    def _():
        for r in halves:
            u = _rms_rows(h_ref[r, :], g_ref[...]).astype(BF16)
            u_scr[r, :] = u
            o_ref[r, :] = partial_sum(u, tile)

    @pl.when((j > 0) & (j < last))
    def _():
        o_ref[...] += partial_sum(u_scr[...], tile)

    @pl.when(j == last)
    def _():
        for r in (halves if final else (slice(None),)):
            y = h_ref[r, :] + FFN_RES * (o_ref[r, :] + partial_sum(u_scr[r, :], tail))
            if final:
                y = _rms_rows(y, fg_ref[...])
            o_ref[r, :] = y


def _ff_padded(tf):
    return pl.cdiv(D_FF, tf) * tf


def _ffn(h, gain, wg, wu, wd, final_gain, final, tiles, sides=()):
    tm, tf = tiles
    assert ROWS % tm == 0 and wg.shape == (_ff_padded(tf) // tf, D_MODEL, tf) and wd.shape[0] == _ff_padded(tf)
    grid = (ROWS // tm, wg.shape[0])
    tail = D_FF - (grid[1] - 1) * tf
    plan = _side_plan(sides, grid)
    if final:
        assert L_PAD % tm == 0
        per_seq = L_PAD // tm
        h = h.reshape(BATCH, L_PAD, D_MODEL)
        rows_spec = pl.BlockSpec((None, tm, D_MODEL), lambda i, j, tbl: (i // per_seq, i % per_seq, 0))
        out_shape = jax.ShapeDtypeStruct((BATCH, SEQ, D_MODEL), F32)
    else:
        rows_spec = pl.BlockSpec((tm, D_MODEL), lambda i, j, tbl: (i, 0))
        out_shape = jax.ShapeDtypeStruct((ROWS, D_MODEL), F32)
    return _hosted_call(
        functools.partial(_ffn_kernel, final=final, tail=tail), plan, grid,
        in_specs=[
            rows_spec,
            pl.BlockSpec((1, D_MODEL), lambda i, j, tbl: (0, 0)),
            pl.BlockSpec((None, D_MODEL, tf), lambda i, j, tbl: (j, 0, 0)),
            pl.BlockSpec((None, D_MODEL, tf), lambda i, j, tbl: (j, 0, 0)),
            pl.BlockSpec((tf, D_MODEL), lambda i, j, tbl: (j, 0)),
            pl.BlockSpec((1, D_MODEL), lambda i, j, tbl: (0, 0)),
        ],
        out_spec=rows_spec,
        out_shape=out_shape,
        scratch_shapes=[pltpu.VMEM((tm, D_MODEL), BF16)],
        name="ffn",
        operands=(h, gain, wg, wu, wd, final_gain),
    )


def _proj_kernel(h_ref, g_ref, w_ref, o_ref, u_scr):
    j = pl.program_id(1)

    @pl.when(j == 0)
    def _():
        half = h_ref.shape[0] // 2
        for r in (slice(0, half), slice(half, 2 * half)):
            u = _rms_rows(h_ref[r, :], g_ref[...]).astype(BF16)
            u_scr[r, :] = u
            o_ref[r, :] = _dot(u, w_ref[...]).astype(o_ref.dtype)

    @pl.when(j > 0)
    def _():
        o_ref[...] = _dot(u_scr[...], w_ref[...]).astype(o_ref.dtype)


def _proj(h, gain, w, sides=()):
    grid = (ROWS // PROJ_ROW_TILE, W_PROJ // PROJ_TILE)
    plan = _side_plan(sides, grid)
    return _hosted_call(
        _proj_kernel, plan, grid,
        in_specs=[
            pl.BlockSpec((PROJ_ROW_TILE, D_MODEL), lambda i, j, tbl: (i, 0)),
            pl.BlockSpec((1, D_MODEL), lambda i, j, tbl: (0, 0)),
            pl.BlockSpec((None, D_MODEL, PROJ_TILE), lambda i, j, tbl: (j, 0, 0)),
        ],
        out_spec=pl.BlockSpec((PROJ_ROW_TILE, PROJ_TILE), lambda i, j, tbl: (i, j)),
        out_shape=jax.ShapeDtypeStruct((ROWS, W_PROJ), BF16),
        scratch_shapes=[pltpu.VMEM((PROJ_ROW_TILE, D_MODEL), BF16)],
        name="in_proj",
        operands=(h, gain, w),
    )


def _decay_tables():
    c = CHUNK
    blocks = []
    r = np.arange(c)
    for l in range(1, N_LEVELS):
        b = 1 << l
        m = np.zeros((c, c), np.float32)
        for i in range(c):
            p = i % (2 * b)
            s = i - p + b
            if p >= b:
                m[i, s:i + 1] = 1.0
            else:
                m[i, i + 1:s] = 1.0
        blocks.append(m)
    blocks.append((r[None, :] <= r[:, None]).astype(np.float32))
    wc = np.concatenate(blocks, axis=0)
    wc = np.concatenate([wc] * 3, axis=1)

    masks = np.zeros((N_LEVELS + 1, c, c), np.float32)
    for l in range(N_LEVELS):
        b = 1 << l
        same = (r[:, None] // (2 * b)) == (r[None, :] // (2 * b))
        up = (r[:, None] & b) != 0
        lo = (r[None, :] & b) == 0
        masks[l] = (same & up & lo).astype(np.float32)
    masks[N_LEVELS] = np.eye(c, dtype=np.float32)
    return wc, masks


def _gla_core(qs, ks, vs, gs, wc_ref, mk_ref, s_ref, slot_of, n_heads, dk, dv):
    c = CHUNK
    seqs = range(len(qs))
    heads = range(n_heads)
    base = (N_LEVELS - 1) * c

    def split3(g):
        g_hi = g.astype(BF16)
        r1 = g - g_hi.astype(F32)
        g_mid = r1.astype(BF16)
        g_lo = (r1 - g_mid.astype(F32)).astype(BF16)
        return jnp.concatenate([g_hi, g_mid, g_lo], axis=0)

    sums = [_dot(wc_ref[...], split3(gs[b])) for b in seqs]
    ex = [jnp.exp(sums[b][:base]) for b in seqs]
    cum = [sums[b][base:base + c] for b in seqs]
    total = [cum[b][c - 1:c] for b in seqs]

    on_diag = [qs[b] * ks[b] for b in seqs]
    below = [qs[b] * jnp.exp(gs[b]) * pltpu.roll(ks[b], 1, axis=0) for b in seqs]
    scores = [[None] * n_heads for _ in seqs]
    for h in heads:
        lanes = slice(h * dk, (h + 1) * dk)
        for b in seqs:
            scores[b][h] = (mk_ref[N_LEVELS] * jnp.sum(on_diag[b][:, lanes], axis=-1, keepdims=True)
                            + mk_ref[0] * jnp.sum(below[b][:, lanes], axis=-1, keepdims=True))
    rows = lax.broadcasted_iota(jnp.int32, (c, 1), 0)
    for l in range(1, N_LEVELS):
        upper = (rows & (1 << l)) != 0
        x = [(jnp.where(upper, qs[b], ks[b]) * ex[b][(l - 1) * c:l * c]).astype(BF16) for b in seqs]
        for h in heads:
            for b in seqs:
                xh = x[b][:, h * dk:(h + 1) * dk]
                scores[b][h] = scores[b][h] + mk_ref[l] * _dot_nt(xh, xh)

    vb = [vs[b].astype(BF16) for b in seqs]
    qt = [(qs[b] * jnp.exp(cum[b])).astype(BF16) for b in seqs]
    kt = [(ks[b] * jnp.exp(total[b] - cum[b])).astype(BF16) for b in seqs]
    tot = [jnp.exp(total[b]) for b in seqs]
    outs = [[None] * n_heads for _ in seqs]
    for h in heads:
        for b in seqs:
            slot = slot_of(b, h)
            st = s_ref[slot]
            vh = vb[b][:, h * dv:(h + 1) * dv]
            outs[b][h] = (_dot(scores[b][h].astype(BF16), vh)
                          + _dot_nt(qt[b][:, h * dk:(h + 1) * dk], st.astype(BF16)))
            s_ref[slot] = st * tot[b][:, h * dk:(h + 1) * dk] + _dot_tn(vh, kt[b][:, h * dk:(h + 1) * dk])
    return outs


def _head_rms(outs, gain, gate):
    dv = outs[0].shape[-1]
    ys = []
    for h, o in enumerate(outs):
        ms = jnp.mean(o * o, axis=-1, keepdims=True)
        ys.append(o * lax.rsqrt(ms + EPS))
    y = jnp.concatenate(ys, axis=-1)
    return y * gain * _silu(gate)


def _log_sigmoid(z):
    return jnp.minimum(z, 0.0) - jnp.log(1.0 + jnp.exp(-jnp.abs(z)))


def _gla_kernel(q_ref, k_ref, v_ref, gate_ref, lr_ref, w2_ref, b_ref, norm_ref, wc_ref, mk_ref,
                o_ref, s_ref):
    @pl.when(pl.program_id(0) == 0)
    def _():
        s_ref[...] = jnp.zeros_like(s_ref)

    seqs = range(BATCH)
    z = [_dot(lr_ref[b], w2_ref[...]) + b_ref[...] for b in seqs]
    g = [_log_sigmoid(z[b]) * (1.0 / GLA_GATE_TAU) for b in seqs]
    q = [q_ref[b].astype(F32) * (GLA_DK ** -0.5) for b in seqs]
    k = [k_ref[b].astype(F32) for b in seqs]
    v = [v_ref[b].astype(F32) for b in seqs]
    outs = _gla_core(q, k, v, g, wc_ref, mk_ref, s_ref, lambda b, h: b * GLA_HEADS + h,
                     GLA_HEADS, GLA_DK, GLA_DV)
    for b in seqs:
        o_ref[b] = _head_rms(outs[b], norm_ref[...], gate_ref[b].astype(F32)).astype(o_ref.dtype)


def _hgrn_kernel(q_ref, f_ref, i_ref, gate_ref, lb_ref, norm_ref, wc_ref, mk_ref, o_ref, s_ref, *, layer):
    @pl.when(pl.program_id(0) == 0)
    def _():
        s_ref[...] = jnp.zeros_like(s_ref)

    logits = lb_ref[...]
    e = jnp.exp(logits - jnp.max(logits, axis=0, keepdims=True))
    soft = e / jnp.sum(e, axis=0, keepdims=True)
    lb = jnp.zeros((1, logits.shape[1]), F32)
    for l in range(1, layer + 1):
        lb = lb + soft[l:l + 1]

    seqs = range(BATCH)
    group = HGRN_HEADS // 2
    for first in range(0, HGRN_HEADS, group):
        qk = slice(first * HGRN_DK, (first + group) * HGRN_DK)
        vv = slice(first * HGRN_DV, (first + group) * HGRN_DV)
        lbg = lb[:, qk]
        hf = [f_ref[b, :, qk].astype(F32) for b in seqs]
        forget = [lbg + (1.0 - lbg) * _sigmoid(hf[b]) for b in seqs]
        g = [jnp.log(jnp.maximum(forget[b], FORGET_FLOOR)) for b in seqs]
        k = [1.0 - forget[b] for b in seqs]
        v = [_silu(i_ref[b, :, vv].astype(F32)) for b in seqs]
        q = [q_ref[b, :, qk].astype(F32) for b in seqs]
        outs = _gla_core(q, k, v, g, wc_ref, mk_ref, s_ref,
                         lambda b, h, first=first: b * HGRN_HEADS + first + h, group, HGRN_DK, HGRN_DV)
        for b in seqs:
            y = _head_rms(outs[b], norm_ref[:, vv], gate_ref[b, :, vv].astype(F32))
            o_ref[b, :, vv] = y.astype(o_ref.dtype)


def _chunk_block(c):
    return (c + N_CHUNKS - 1) % N_CHUNKS


def _col_spec(width, offset):
    assert offset % width == 0
    blk = offset // width
    return pl.BlockSpec((BATCH, CHUNK, width), lambda c, tbl: (0, _chunk_block(c), blk))


def _const_spec(shape):
    nd = len(shape)
    return pl.BlockSpec(shape, lambda c, tbl: (0,) * nd)


class _MixerPart(NamedTuple):
    kernel: object
    width: int
    state: tuple
    in_specs: list
    operands: tuple


def _mixer_call(kernel, name, width, n_heads, dv, dk, in_specs, operands, sides):
    del name, sides
    return _MixerPart(kernel, width, (BATCH * n_heads, dv, dk), in_specs, operands)


def _fused_mixers(parts, sides):
    grid = (N_CHUNKS,)
    plan = _side_plan(sides, grid)
    n_parts, n_side_in, n_side_out = len(parts), len(plan.in_specs), len(plan.out_specs)

    def body(tbl_ref, *refs):
        pos = 0
        ins = []
        for p in parts:
            ins.append(refs[pos:pos + len(p.in_specs)])
            pos += len(p.in_specs)
        side_ins = refs[pos:pos + n_side_in]
        pos += n_side_in
        outs = refs[pos:pos + n_parts]
        pos += n_parts
        side_outs = refs[pos:pos + n_side_out]
        states = refs[pos + n_side_out:]
        for p, i, o, s in zip(parts, ins, outs, states):
            p.kernel(*i, o, s)
        _run_sides(plan, tbl_ref, side_ins, side_outs)

    res = pl.pallas_call(
        body,
        grid_spec=pltpu.PrefetchScalarGridSpec(
            num_scalar_prefetch=1,
            grid=grid,
            in_specs=[s for p in parts for s in p.in_specs] + plan.in_specs,
            out_specs=[pl.BlockSpec((BATCH, CHUNK, p.width), lambda c, tbl: (0, _chunk_block(c), 0))
                       for p in parts] + plan.out_specs,
            scratch_shapes=[pltpu.VMEM(p.state, F32) for p in parts],
        ),
        out_shape=[jax.ShapeDtypeStruct((BATCH, L_PAD, p.width), BF16) for p in parts] + plan.out_shapes,
        compiler_params=_params("arbitrary"),
        name="mixers",
    )(jnp.asarray(plan.table), *[o for p in parts for o in p.operands], *plan.operands)
    return res[:n_parts], res[n_parts:]


def _gla(proj, w2, bias, norm, wc, masks, sides=()):
    qk = GLA_HEADS * GLA_DK
    vw = GLA_HEADS * GLA_DV
    return _mixer_call(
        _gla_kernel, "gla", vw, GLA_HEADS, GLA_DV, GLA_DK,
        in_specs=[
            _col_spec(qk, OFF_GLA),
            _col_spec(qk, OFF_GLA + qk),
            _col_spec(vw, OFF_GLA + 2 * qk),
            _col_spec(vw, OFF_GLA + 2 * qk + vw),
            _col_spec(LANE, OFF_LR),
            _const_spec(w2.shape),
            _const_spec(bias.shape),
            _const_spec(norm.shape),
            _const_spec(wc.shape),
            _const_spec(masks.shape),
        ],
        operands=(proj, proj, proj, proj, proj, w2, bias, norm, wc, masks),
        sides=sides,
    )


def _hgrn(proj, lb_logits, norm, wc, masks, layer, sides=()):
    w = HGRN_HEADS * HGRN_DK
    return _mixer_call(
        functools.partial(_hgrn_kernel, layer=layer), "hgrn", w, HGRN_HEADS, HGRN_DV, HGRN_DK,
        in_specs=[
            _col_spec(w, OFF_HGRN),
            _col_spec(w, OFF_HGRN + w),
            _col_spec(w, OFF_HGRN + 2 * w),
            _col_spec(w, OFF_HGRN + 3 * w),
            _const_spec(lb_logits.shape),
            _const_spec(norm.shape),
            _const_spec(wc.shape),
            _const_spec(masks.shape),
        ],
        operands=(proj, proj, proj, proj, lb_logits, norm, wc, masks),
        sides=sides,
    )


def _ret_kernel(q_ref, k_ref, v_ref, gate_ref, cos_ref, sin_ref, qd_ref, kd_ref, cd_ref, im_ref, norm_ref,
                o_ref, s_ref):
    @pl.when(pl.program_id(0) == 0)
    def _():
        s_ref[...] = jnp.zeros_like(s_ref)

    half = RET_DK // 2
    cos = cos_ref[...]
    sin = sin_ref[...]
    seqs = range(BATCH)
    pairs = [(b, h) for h in range(RET_HEADS) for b in seqs]

    def rotate(ref, b, h):
        lo = h * RET_DK
        t1 = ref[b, :, lo:lo + half].astype(F32)
        t2 = ref[b, :, lo + half:lo + RET_DK].astype(F32)
        return jnp.concatenate([t1 * cos - t2 * sin, t1 * sin + t2 * cos], axis=-1)

    qr = {p: rotate(q_ref, *p) * (RET_DK ** -0.5) for p in pairs}
    kr = {p: rotate(k_ref, *p) for p in pairs}
    scores = {p: _dot_nt(qr[p].astype(BF16), kr[p].astype(BF16)) * im_ref[p[1]] for p in pairs}
    outs = {}
    for b, h in pairs:
        p = (b, h)
        vh = v_ref[b, :, h * RET_DV:(h + 1) * RET_DV]
        st = s_ref[b * RET_HEADS + h]
        outs[p] = (_dot(scores[p].astype(BF16), vh)
                   + _dot_nt((qr[p] * qd_ref[h]).astype(BF16), st.astype(BF16)))
        s_ref[b * RET_HEADS + h] = st * cd_ref[h] + _dot_tn(vh, (kr[p] * kd_ref[h]).astype(BF16))
    for b in seqs:
        ys = []
        for h in range(RET_HEADS):
            o = outs[(b, h)]
            o = o - jnp.mean(o, axis=-1, keepdims=True)
            ms = jnp.mean(o * o, axis=-1, keepdims=True)
            ys.append(o * lax.rsqrt(ms + EPS))
        y = jnp.concatenate(ys, axis=-1)
        o_ref[b] = (y * norm_ref[...] * _silu(gate_ref[b].astype(F32))).astype(o_ref.dtype)


def _ret(proj, cos, sin, qd, kd, cd, im, norm, sides=()):
    w = RET_HEADS * RET_DK
    half = RET_DK // 2
    return _mixer_call(
        _ret_kernel, "retention", w, RET_HEADS, RET_DV, RET_DK,
        in_specs=[
            _col_spec(w, OFF_RET),
            _col_spec(w, OFF_RET + w),
            _col_spec(w, OFF_RET + 2 * w),
            _col_spec(w, OFF_RET + 3 * w),
            pl.BlockSpec((CHUNK, half), lambda c, tbl: (c, 0)),
            pl.BlockSpec((CHUNK, half), lambda c, tbl: (c, 0)),
            _const_spec(qd.shape),
            _const_spec(kd.shape),
            _const_spec(cd.shape),
            _const_spec(im.shape),
            _const_spec(norm.shape),
        ],
        operands=(proj, proj, proj, proj, cos, sin, qd, kd, cd, im, norm),
        sides=sides,
    )


def _ret_tables():
    f32 = jnp.float32
    half = RET_DK // 2
    pos = jnp.arange(L_PAD, dtype=f32) - PAD
    inv_freq = ROPE_BASE ** (-jnp.arange(half, dtype=f32) / half)
    ang = pos[:, None] * inv_freq[None, :]
    log_gamma = jnp.log(1.0 - 2.0 ** (-5.0 - jnp.arange(RET_HEADS, dtype=f32)))
    idx = jnp.arange(CHUNK, dtype=f32)
    rel = idx[:, None] - idx[None, :]
    causal = (rel >= 0)[None]
    intra = jnp.where(causal, jnp.exp(jnp.where(causal, rel[None], 0.0) * log_gamma[:, None, None]), 0.0)
    q_decay = jnp.exp((idx[None, :] + 1.0) * log_gamma[:, None])[..., None]
    k_decay = jnp.exp((CHUNK - 1.0 - idx[None, :]) * log_gamma[:, None])[..., None]
    chunk_decay = jnp.exp(CHUNK * log_gamma)[:, None, None]
    return jnp.cos(ang), jnp.sin(ang), q_decay, k_decay, chunk_decay, intra


def _merge_kernel(y0_ref, y1_ref, y2_ref, mg_ref, wb_ref, wo_ref, h_ref, o_ref):
    merged = None
    for n, y_ref in enumerate((y0_ref, y1_ref, y2_ref)):
        gate = _sigmoid(mg_ref[:, n * D_MODEL:(n + 1) * D_MODEL].astype(F32))
        t = _dot(y_ref[...], wb_ref[n * BRANCH_WIDTH:(n + 1) * BRANCH_WIDTH, :]) * gate
        merged = t if merged is None else merged + t
    o_ref[...] = h_ref[...] + _dot(merged.astype(BF16), wo_ref[...])


def _merge(ys, proj, wb, wo, h):
    assert OFF_MG == 0
    rows = lambda width: pl.BlockSpec((MERGE_TILE, width), lambda i: (i, 0))
    whole = lambda a: pl.BlockSpec(a.shape, lambda i: (0, 0))
    return pl.pallas_call(
        _merge_kernel,
        grid=(ROWS // MERGE_TILE,),
        in_specs=[rows(BRANCH_WIDTH)] * N_BRANCH + [rows(N_BRANCH * D_MODEL)]
        + [whole(wb), whole(wo), rows(D_MODEL)],
        out_specs=rows(D_MODEL),
        out_shape=jax.ShapeDtypeStruct((ROWS, D_MODEL), F32),
        compiler_params=_params("parallel"),
        name="merge",
    )(*ys, proj, wb, wo, h)


def kernel(x, meta_tokens, ffn1_norm, ffn1_w_gate, ffn1_w_up, ffn1_w_down, mix_norm, w_in, gla_w_gate2, gla_b_gate, gla_norm, ret_norm, hgrn_lb_logits, hgrn_norm, w_branch, w_out, ffn2_norm, ffn2_w_gate, ffn2_w_up, ffn2_w_down, final_norm):
    b = x.shape[0]
    meta = jnp.broadcast_to(meta_tokens[None].astype(x.dtype), (b, N_META, D_MODEL))
    h = jnp.concatenate([x, jnp.zeros((b, PAD, D_MODEL), x.dtype), meta], axis=1).reshape(ROWS, D_MODEL)

    wc_np, masks_np = _decay_tables()
    wc = jnp.asarray(wc_np, BF16)
    masks = jnp.asarray(masks_np, F32)
    cos, sin, qd, kd, cd, im = _ret_tables()

    def ffn_casts(wg, wu, wd, layer, tiles):
        padded = _ff_padded(tiles[1])
        return (_SideCast(wg, (layer,), 1, padded, 0, tiles[1]), _SideCast(wu, (layer,), 1, padded, 0, tiles[1]),
                _SideCast(wd, (layer,), 0, padded, 0))

    row = lambda v: v.reshape(1, -1).astype(F32)
    final_gain = row(final_norm)
    w_in_t = jnp.swapaxes(w_in, 1, 2)
    wb_rows = w_branch.reshape(DEPTH, N_BRANCH * BRANCH_WIDTH, D_MODEL)
    padded = _ff_padded(FFN1_TILES[1])
    ffn1_w = (_cast(ffn1_w_gate, (0,), 1, padded), _cast(ffn1_w_up, (0,), 1, padded),
              _cast(ffn1_w_down, (0,), 0, padded))
    for layer in range(DEPTH):
        last = layer == DEPTH - 1
        tiles2 = FINAL_TILES if last else FFN2_TILES
        h, (w_proj,) = _ffn(h, row(ffn1_norm[layer]), *ffn1_w, final_gain, False, FFN1_TILES,
                            sides=[_SideWIn(w_in_t, layer, 0)])
        proj, _ = _proj(h, row(mix_norm[layer]), w_proj)
        jobs = [[j] for j in ffn_casts(ffn2_w_gate, ffn2_w_up, ffn2_w_down, layer, tiles2)]
        jobs[0].append(_SideCast(wb_rows, (layer,), 0, N_BRANCH * BRANCH_WIDTH, 0))
        jobs[1].append(_SideCast(w_out, (layer,), 0, D_MODEL, 0))
        if not last:
            for mine, j in zip(jobs, ffn_casts(ffn1_w_gate, ffn1_w_up, ffn1_w_down, layer + 1, FFN1_TILES)):
                mine.append(j)
        w2 = jnp.pad(gla_w_gate2[layer], ((0, LANE - GLA_GATE_RANK), (0, 0))).astype(BF16)
        proj_seq = proj.reshape(b, L_PAD, W_PROJ)
        parts = [_gla(proj_seq, w2, row(gla_b_gate[layer]), row(gla_norm[layer]), wc, masks),
                 _ret(proj_seq, cos, sin, qd, kd, cd, im, row(ret_norm[layer])),
                 _hgrn(proj_seq, hgrn_lb_logits.astype(F32), row(hgrn_norm[layer]), wc, masks, layer)]
        n_jobs = [len(j) for j in jobs]
        ys, copies = _fused_mixers(parts, [j for mine in jobs for j in mine])
        (wg2, wb, *wg1) = copies[:n_jobs[0]]
        (wu2, wo, *wu1) = copies[n_jobs[0]:n_jobs[0] + n_jobs[1]]
        (wd2, *wd1) = copies[n_jobs[0] + n_jobs[1]:]
        ys = [y.reshape(ROWS, BRANCH_WIDTH) for y in ys]
        h = _merge(ys, proj, wb, wo, h)
        h, _ = _ffn(h, row(ffn2_norm[layer]), wg2, wu2, wd2, final_gain, last, tiles2)
        if not last:
            ffn1_w = (*wg1, *wu1, *wd1)
    return h
```

```python
import functools
from typing import NamedTuple

import numpy as np
import jax
import jax.numpy as jnp
from jax import lax
from jax.experimental import pallas as pl
from jax.experimental.pallas import tpu as pltpu

D_MODEL = 2048
BATCH = 2
SEQ = 4096
DEPTH = 2
N_META = 16
CHUNK = 64
PAD = CHUNK - N_META
D_FF = 5504
FFN_RES = 0.5
EPS = 1e-6
N_BRANCH = 3
BRANCH_WIDTH = 1024

GLA_HEADS, GLA_DK, GLA_DV = 4, 128, 256
GLA_GATE_RANK = 16
GLA_GATE_TAU = 16.0
RET_HEADS, RET_DK, RET_DV = 4, 256, 256
ROPE_BASE = 10000.0
HGRN_HEADS, HGRN_DK, HGRN_DV = 8, 128, 128
FORGET_FLOOR = 1e-20

LANE = 128
L_PAD = PAD + N_META + SEQ
N_CHUNKS = L_PAD // CHUNK
ROWS = BATCH * L_PAD
N_LEVELS = 6
assert 1 << N_LEVELS == CHUNK

FFN1_TILES = (640, 512)
FFN2_TILES = (640, 512)
FINAL_TILES = (832, 512)
PROJ_ROW_TILE = 832
PROJ_TILE = 1792
MERGE_TILE = 320
CAST_TILE = 512
SIDE_BLOCK = LANE
SIDE_ROWS = 32
W_IN_BLOCK = 256
LR_PAD = CAST_TILE
W_IN_SHIFT = GLA_GATE_RANK

OFF_MG = 0
OFF_GLA = OFF_MG + N_BRANCH * D_MODEL
OFF_RET = OFF_GLA + 2 * GLA_HEADS * GLA_DK + 2 * GLA_HEADS * GLA_DV
OFF_HGRN = OFF_RET + 2 * RET_HEADS * RET_DK + 2 * RET_HEADS * RET_DV
OFF_LR = OFF_HGRN + 2 * HGRN_HEADS * HGRN_DK + 2 * HGRN_HEADS * HGRN_DV
W_PROJ = OFF_LR + LR_PAD

VMEM_LIMIT = 56 * 1024 * 1024

F32 = jnp.float32
BF16 = jnp.bfloat16


def _params(*sem):
    return pltpu.CompilerParams(dimension_semantics=sem, vmem_limit_bytes=VMEM_LIMIT)


def _rms_rows(x, gain):
    ms = jnp.mean(x * x, axis=-1, keepdims=True)
    return x * lax.rsqrt(ms + EPS) * gain


def _sigmoid(x):
    return 1.0 / (1.0 + jnp.exp(-x))


def _silu(x):
    return x * _sigmoid(x)


def _dot(a, b):
    return jnp.dot(a, b, preferred_element_type=F32)


def _dot_nt(a, b):
    return lax.dot_general(a, b, (((1,), (1,)), ((), ())), preferred_element_type=F32)


def _dot_tn(a, b):
    return lax.dot_general(a, b, (((0,), (0,)), ((), ())), preferred_element_type=F32)


def _cast_kernel(w_ref, o_ref, *, axis, valid):
    x = w_ref[...]
    idx = pl.program_id(0) * x.shape[axis] + lax.broadcasted_iota(jnp.int32, x.shape, axis)
    o_ref[...] = jnp.where(idx < valid, x, 0.0).astype(BF16)


def _cast_ffn_kernel(wg_ref, wu_ref, wd_ref, og_ref, ou_ref, od_ref, *, valid):
    _cast_kernel(wg_ref, og_ref, axis=1, valid=valid)
    _cast_kernel(wu_ref, ou_ref, axis=1, valid=valid)
    _cast_kernel(wd_ref, od_ref, axis=0, valid=valid)


def _cast_ffn(wg, wu, wd, layer, padded):
    n = padded // CAST_TILE
    cols_in = pl.BlockSpec((None, D_MODEL, CAST_TILE), lambda j: (layer, 0, j))
    cols_out = pl.BlockSpec((None, D_MODEL, CAST_TILE), lambda j: (j, 0, 0))
    tiled = jax.ShapeDtypeStruct((n, D_MODEL, CAST_TILE), BF16)
    return pl.pallas_call(
        functools.partial(_cast_ffn_kernel, valid=wg.shape[-1]),
        grid=(n,),
        in_specs=[cols_in, cols_in, pl.BlockSpec((None, CAST_TILE, D_MODEL), lambda j: (layer, j, 0))],
        out_specs=[cols_out, cols_out, pl.BlockSpec((CAST_TILE, D_MODEL), lambda j: (j, 0))],
        out_shape=[tiled, tiled, jax.ShapeDtypeStruct((padded, D_MODEL), BF16)],
        compiler_params=_params("parallel"),
        name="cast",
    )(wg, wu, wd)


SIDE_IDLE, SIDE_COPY, SIDE_ZERO, SIDE_SHIFTED, SIDE_LOW_RANK = 0, 1, 2, 3, 4


class _SideCast(NamedTuple):
    src: jax.Array
    lead: tuple
    axis: int
    padded: int
    start: int
    col_tile: int = 0


class _SideWIn(NamedTuple):
    src_t: jax.Array
    layer: int
    start: int


class _SidePlan(NamedTuple):
    table: np.ndarray
    in_specs: list
    operands: list
    out_specs: list
    out_shapes: list
    runners: list
    grid: tuple


def _w_in_blocks():
    per = lambda width: width // W_IN_BLOCK
    src = lambda col: col // W_IN_BLOCK
    gla_w = 2 * GLA_HEADS * GLA_DK + 2 * GLA_HEADS * GLA_DV
    ret_w = 2 * RET_HEADS * RET_DK + 2 * RET_HEADS * RET_DV
    hgrn_w = 2 * HGRN_HEADS * HGRN_DK + 2 * HGRN_HEADS * HGRN_DV
    lr_col = gla_w
    blocks = []
    for t in range(per(N_BRANCH * D_MODEL)):
        blocks.append((src(lr_col + ret_w + hgrn_w) + t, SIDE_SHIFTED))
    for t in range(per(gla_w)):
        blocks.append((t, SIDE_COPY))
    for t in range(per(ret_w + hgrn_w)):
        blocks.append((src(lr_col) + t, SIDE_SHIFTED))
    blocks.append((src(lr_col), SIDE_LOW_RANK))
    blocks += [(src(lr_col), SIDE_ZERO)] * (per(LR_PAD) - 1)
    assert len(blocks) == per(W_PROJ)
    return [(s, t, mode) for t, (s, mode) in enumerate(blocks)]


def _side_plan(jobs, grid):
    n_steps = int(np.prod(grid))
    rows, in_specs, operands, out_specs, out_shapes, runners = [], [], [], [], [], []

    def add_rows(blocks, start):
        n = len(blocks)
        assert start + n <= n_steps
        arr = np.asarray(blocks, np.int32).T
        tab = np.zeros((3, n_steps), np.int32)
        tab[:2, :start] = arr[:2, :1]
        tab[:, start:start + n] = arr
        tab[:2, start + n:] = arr[:2, -1:]
        base = len(rows) * n_steps
        rows.extend(tab)
        return base

    def entry(args, offset):
        *idx, tbl = args
        return tbl[offset + _linear_step(idx, grid)]

    for job in jobs:
        if isinstance(job, _SideCast):
            r, c = job.src.shape[-2:]
            lead, nl = job.lead, len(job.lead)
            if job.axis == 0:
                assert r % SIDE_BLOCK == 0 and job.padded % SIDE_BLOCK == 0
                n_valid, n_all = r // SIDE_BLOCK, job.padded // SIDE_BLOCK
                base = add_rows([(min(t, n_valid - 1), t, SIDE_COPY if t < n_valid else SIDE_ZERO)
                                 for t in range(n_all)], job.start)
                in_block = block = (SIDE_BLOCK, c)
                out_shape = (job.padded, c)
                omap = lambda *a, base=base: (entry(a, base + n_steps), 0)

                def run(tbl_ref, step, ins, out, base=base):
                    mode = tbl_ref[base + 2 * n_steps + step]

                    @pl.when(mode != SIDE_IDLE)
                    def _():
                        out[...] = jnp.where(mode == SIDE_ZERO, 0.0, ins[0][...]).astype(BF16)
            else:
                assert r % SIDE_ROWS == 0 and c % LANE == 0 and job.padded % job.col_tile == 0
                n_tiles, tile = job.padded // job.col_tile, job.col_tile
                base = add_rows([(t, t, SIDE_COPY) for t in range(r // SIDE_ROWS)], job.start)
                in_block = (SIDE_ROWS, c)
                block, out_shape = (n_tiles, SIDE_ROWS, tile), (n_tiles, r, tile)
                omap = lambda *a, base=base: (0, entry(a, base + n_steps), 0)

                def run(tbl_ref, step, ins, out, base=base, c=c, n_tiles=n_tiles, tile=tile):
                    mode = tbl_ref[base + 2 * n_steps + step]

                    @pl.when(mode != SIDE_IDLE)
                    def _():
                        for kk in range(n_tiles):
                            width = min(tile, c - kk * tile)
                            assert width > 0
                            piece = ins[0][:, kk * tile:kk * tile + width].astype(BF16)
                            if width < tile:
                                piece = jnp.concatenate(
                                    [piece, jnp.zeros((SIDE_ROWS, tile - width), BF16)], axis=1)
                            out[kk] = piece

            imap = lambda *a, base=base, lead=lead: (*lead, entry(a, base), 0)
            in_specs.append(pl.BlockSpec((None,) * nl + in_block, imap))
            operands.append(job.src)
            runners.append((1, run))
        else:
            base = add_rows(_w_in_blocks(), job.start)
            layer = job.layer
            shifts_per_block = W_IN_BLOCK // W_IN_SHIFT
            in_specs.append(pl.BlockSpec(
                (None, W_IN_BLOCK, D_MODEL),
                lambda *a, base=base, layer=layer: (layer, entry(a, base), 0)))
            in_specs.append(pl.BlockSpec(
                (None, W_IN_SHIFT, D_MODEL),
                lambda *a, base=base, layer=layer: (layer, (entry(a, base) + 1) * shifts_per_block, 0)))
            operands += [job.src_t, job.src_t]
            per_tile = PROJ_TILE // W_IN_BLOCK
            block, out_shape = (None, D_MODEL, W_IN_BLOCK), (W_PROJ // PROJ_TILE, D_MODEL, PROJ_TILE)

            def omap(*a, base=base, per_tile=per_tile):
                t = entry(a, base + n_steps)
                return (t // per_tile, 0, t % per_tile)

            def run(tbl_ref, step, ins, out, base=base):
                a_ref, b_ref = ins
                mode = tbl_ref[base + 2 * n_steps + step]

                @pl.when(mode == SIDE_SHIFTED)
                def _():
                    x = jnp.concatenate([a_ref[W_IN_SHIFT:, :], b_ref[...]], axis=0)
                    out[...] = x.astype(BF16).T

                @pl.when(mode == SIDE_COPY)
                def _():
                    out[...] = a_ref[...].astype(BF16).T

                @pl.when(mode == SIDE_LOW_RANK)
                def _():
                    row = lax.broadcasted_iota(jnp.int32, a_ref.shape, 0)
                    out[...] = jnp.where(row < GLA_GATE_RANK, a_ref[...], 0.0).astype(BF16).T

                @pl.when(mode == SIDE_ZERO)
                def _():
                    out[...] = jnp.zeros_like(out)

            runners.append((2, run))
        out_specs.append(pl.BlockSpec(block, omap))
        out_shapes.append(jax.ShapeDtypeStruct(out_shape, BF16))

    table = np.concatenate(rows) if rows else np.zeros((1,), np.int32)
    return _SidePlan(table, in_specs, operands, out_specs, out_shapes, runners, tuple(grid))


def _linear_step(idx, grid):
    step = idx[0]
    for i, n in zip(idx[1:], grid[1:]):
        step = step * n + i
    return step


def _run_sides(plan, tbl_ref, side_ins, side_outs):
    step = _linear_step([pl.program_id(d) for d in range(len(plan.grid))], plan.grid)
    k = 0
    for (n_in, run), out in zip(plan.runners, side_outs):
        run(tbl_ref, step, side_ins[k:k + n_in], out)
        k += n_in


def _hosted_call(kernel, plan, grid, in_specs, out_spec, out_shape, scratch_shapes, name, operands):
    n_side_in = len(plan.in_specs)
    n_main_in = len(in_specs)

    def body(tbl_ref, *refs):
        ins = refs[:n_main_in]
        side_ins = refs[n_main_in:n_main_in + n_side_in]
        out = refs[n_main_in + n_side_in]
        side_outs = refs[n_main_in + n_side_in + 1:n_main_in + n_side_in + 1 + len(plan.out_specs)]
        scratch = refs[n_main_in + n_side_in + 1 + len(plan.out_specs):]
        kernel(*ins, out, *scratch)
        _run_sides(plan, tbl_ref, side_ins, side_outs)

    res = pl.pallas_call(
        body,
        grid_spec=pltpu.PrefetchScalarGridSpec(
            num_scalar_prefetch=1,
            grid=grid,
            in_specs=list(in_specs) + plan.in_specs,
            out_specs=[out_spec] + plan.out_specs,
            scratch_shapes=scratch_shapes,
        ),
        out_shape=[out_shape] + plan.out_shapes,
        compiler_params=_params(*["arbitrary"] * len(grid)),
        name=name,
    )(jnp.asarray(plan.table), *operands, *plan.operands)
    return res[0], res[1:]


def _ffn_kernel(h_ref, g_ref, wg_ref, wu_ref, wd_ref, fg_ref, o_ref, u_scr, *, final, tail):
    j = pl.program_id(1)
    last = pl.num_programs(1) - 1
    tile = wg_ref.shape[1]
    half = h_ref.shape[0] // 2
    halves = (slice(0, half), slice(half, 2 * half))

    def partial_sum(u, width):
        a = _dot(u, wg_ref[:, :width])
        b = _dot(u, wu_ref[:, :width])
        act = (_silu(a) * b).astype(BF16)
        return _dot(act, wd_ref[:width, :])

    @pl.when(j == 0)
    def _():
        for r in halves:
            u = _rms_rows(h_ref[r, :], g_ref[...]).astype(BF16)
            u_scr[r, :] = u
            o_ref[r, :] = partial_sum(u, tile)

    @pl.when((j > 0) & (j < last))
    def _():
        o_ref[...] += partial_sum(u_scr[...], tile)

    @pl.when(j == last)
    def _():
        for r in (halves if final else (slice(None),)):
            y = h_ref[r, :] + FFN_RES * (o_ref[r, :] + partial_sum(u_scr[r, :], tail))
            if final:
                y = _rms_rows(y, fg_ref[...])
            o_ref[r, :] = y


def _ff_padded(tf):
    return pl.cdiv(D_FF, tf) * tf


def _ffn(h, gain, wg, wu, wd, final_gain, final, tiles, sides=()):
    tm, tf = tiles
    assert ROWS % tm == 0 and wg.shape == (_ff_padded(tf) // tf, D_MODEL, tf) and wd.shape[0] == _ff_padded(tf)
    grid = (ROWS // tm, wg.shape[0])
    tail = D_FF - (grid[1] - 1) * tf
    plan = _side_plan(sides, grid)
    if final:
        assert L_PAD % tm == 0
        per_seq = L_PAD // tm
        h = h.reshape(BATCH, L_PAD, D_MODEL)
        rows_spec = pl.BlockSpec((None, tm, D_MODEL), lambda i, j, tbl: (i // per_seq, i % per_seq, 0))
        out_shape = jax.ShapeDtypeStruct((BATCH, SEQ, D_MODEL), F32)
    else:
        rows_spec = pl.BlockSpec((tm, D_MODEL), lambda i, j, tbl: (i, 0))
        out_shape = jax.ShapeDtypeStruct((ROWS, D_MODEL), F32)
    return _hosted_call(
        functools.partial(_ffn_kernel, final=final, tail=tail), plan, grid,
        in_specs=[
            rows_spec,
            pl.BlockSpec((1, D_MODEL), lambda i, j, tbl: (0, 0)),
            pl.BlockSpec((None, D_MODEL, tf), lambda i, j, tbl: (j, 0, 0)),
            pl.BlockSpec((None, D_MODEL, tf), lambda i, j, tbl: (j, 0, 0)),
            pl.BlockSpec((tf, D_MODEL), lambda i, j, tbl: (j, 0)),
            pl.BlockSpec((1, D_MODEL), lambda i, j, tbl: (0, 0)),
        ],
        out_spec=rows_spec,
        out_shape=out_shape,
        scratch_shapes=[pltpu.VMEM((tm, D_MODEL), BF16)],
        name="ffn",
        operands=(h, gain, wg, wu, wd, final_gain),
    )


def _proj_kernel(h_ref, g_ref, w_ref, o_ref, u_scr):
    j = pl.program_id(1)

    @pl.when(j == 0)
    def _():
        half = h_ref.shape[0] // 2
        for r in (slice(0, half), slice(half, 2 * half)):
            u = _rms_rows(h_ref[r, :], g_ref[...]).astype(BF16)
            u_scr[r, :] = u
            o_ref[r, :] = _dot(u, w_ref[...]).astype(o_ref.dtype)

    @pl.when(j > 0)
    def _():
        o_ref[...] = _dot(u_scr[...], w_ref[...]).astype(o_ref.dtype)


def _proj(h, gain, w, sides=()):
    grid = (ROWS // PROJ_ROW_TILE, W_PROJ // PROJ_TILE)
    plan = _side_plan(sides, grid)
    return _hosted_call(
        _proj_kernel, plan, grid,
        in_specs=[
            pl.BlockSpec((PROJ_ROW_TILE, D_MODEL), lambda i, j, tbl: (i, 0)),
            pl.BlockSpec((1, D_MODEL), lambda i, j, tbl: (0, 0)),
            pl.BlockSpec((None, D_MODEL, PROJ_TILE), lambda i, j, tbl: (j, 0, 0)),
        ],
        out_spec=pl.BlockSpec((PROJ_ROW_TILE, PROJ_TILE), lambda i, j, tbl: (i, j)),
        out_shape=jax.ShapeDtypeStruct((ROWS, W_PROJ), BF16),
        scratch_shapes=[pltpu.VMEM((PROJ_ROW_TILE, D_MODEL), BF16)],
        name="in_proj",
        operands=(h, gain, w),
    )


def _decay_tables():
    c = CHUNK
    blocks = []
    r = np.arange(c)
    for l in range(1, N_LEVELS):
        b = 1 << l
        m = np.zeros((c, c), np.float32)
        for i in range(c):
            p = i % (2 * b)
            s = i - p + b
            if p >= b:
                m[i, s:i + 1] = 1.0
            else:
                m[i, i + 1:s] = 1.0
        blocks.append(m)
    blocks.append((r[None, :] <= r[:, None]).astype(np.float32))
    wc = np.concatenate(blocks, axis=0)
    wc = np.concatenate([wc] * 3, axis=1)

    masks = np.zeros((N_LEVELS + 1, c, c), np.float32)
    for l in range(N_LEVELS):
        b = 1 << l
        same = (r[:, None] // (2 * b)) == (r[None, :] // (2 * b))
        up = (r[:, None] & b) != 0
        lo = (r[None, :] & b) == 0
        masks[l] = (same & up & lo).astype(np.float32)
    masks[N_LEVELS] = np.eye(c, dtype=np.float32)
    return wc, masks


def _gla_core(qs, ks, vs, gs, wc_ref, mk_ref, s_ref, slot_of, n_heads, dk, dv):
    c = CHUNK
    seqs = range(len(qs))
    heads = range(n_heads)
    base = (N_LEVELS - 1) * c

    def split3(g):
        g_hi = g.astype(BF16)
        r1 = g - g_hi.astype(F32)
        g_mid = r1.astype(BF16)
        g_lo = (r1 - g_mid.astype(F32)).astype(BF16)
        return jnp.concatenate([g_hi, g_mid, g_lo], axis=0)

    sums = [_dot(wc_ref[...], split3(gs[b])) for b in seqs]
    ex = [jnp.exp(sums[b][:base]) for b in seqs]
    cum = [sums[b][base:base + c] for b in seqs]
    total = [cum[b][c - 1:c] for b in seqs]

    on_diag = [qs[b] * ks[b] for b in seqs]
    below = [qs[b] * jnp.exp(gs[b]) * pltpu.roll(ks[b], 1, axis=0) for b in seqs]
    scores = [[None] * n_heads for _ in seqs]
    for h in heads:
        lanes = slice(h * dk, (h + 1) * dk)
        for b in seqs:
            scores[b][h] = (mk_ref[N_LEVELS] * jnp.sum(on_diag[b][:, lanes], axis=-1, keepdims=True)
                            + mk_ref[0] * jnp.sum(below[b][:, lanes], axis=-1, keepdims=True))
    rows = lax.broadcasted_iota(jnp.int32, (c, 1), 0)
    for l in range(1, N_LEVELS):
        upper = (rows & (1 << l)) != 0
        x = [(jnp.where(upper, qs[b], ks[b]) * ex[b][(l - 1) * c:l * c]).astype(BF16) for b in seqs]
        for h in heads:
            for b in seqs:
                xh = x[b][:, h * dk:(h + 1) * dk]
                scores[b][h] = scores[b][h] + mk_ref[l] * _dot_nt(xh, xh)

    vb = [vs[b].astype(BF16) for b in seqs]
    qt = [(qs[b] * jnp.exp(cum[b])).astype(BF16) for b in seqs]
    kt = [(ks[b] * jnp.exp(total[b] - cum[b])).astype(BF16) for b in seqs]
    tot = [jnp.exp(total[b]) for b in seqs]
    outs = [[None] * n_heads for _ in seqs]
    for h in heads:
        for b in seqs:
            slot = slot_of(b, h)
            st = s_ref[slot]
            vh = vb[b][:, h * dv:(h + 1) * dv]
            outs[b][h] = (_dot(scores[b][h].astype(BF16), vh)
                          + _dot_nt(qt[b][:, h * dk:(h + 1) * dk], st.astype(BF16)))
            s_ref[slot] = st * tot[b][:, h * dk:(h + 1) * dk] + _dot_tn(vh, kt[b][:, h * dk:(h + 1) * dk])
    return outs


def _head_rms(outs, gain, gate):
    dv = outs[0].shape[-1]
    ys = []
    for h, o in enumerate(outs):
        ms = jnp.mean(o * o, axis=-1, keepdims=True)
        ys.append(o * lax.rsqrt(ms + EPS))
    y = jnp.concatenate(ys, axis=-1)
    return y * gain * _silu(gate)


def _log_sigmoid(z):
    return jnp.minimum(z, 0.0) - jnp.log(1.0 + jnp.exp(-jnp.abs(z)))


def _gla_kernel(q_ref, k_ref, v_ref, gate_ref, lr_ref, w2_ref, b_ref, norm_ref, wc_ref, mk_ref,
                o_ref, s_ref):
    @pl.when(pl.program_id(0) == 0)
    def _():
        s_ref[...] = jnp.zeros_like(s_ref)

    seqs = range(BATCH)
    z = [_dot(lr_ref[b], w2_ref[...]) + b_ref[...] for b in seqs]
    g = [_log_sigmoid(z[b]) * (1.0 / GLA_GATE_TAU) for b in seqs]
    q = [q_ref[b].astype(F32) * (GLA_DK ** -0.5) for b in seqs]
    k = [k_ref[b].astype(F32) for b in seqs]
    v = [v_ref[b].astype(F32) for b in seqs]
    outs = _gla_core(q, k, v, g, wc_ref, mk_ref, s_ref, lambda b, h: b * GLA_HEADS + h,
                     GLA_HEADS, GLA_DK, GLA_DV)
    for b in seqs:
        o_ref[b] = _head_rms(outs[b], norm_ref[...], gate_ref[b].astype(F32)).astype(o_ref.dtype)


def _hgrn_kernel(q_ref, f_ref, i_ref, gate_ref, lb_ref, norm_ref, wc_ref, mk_ref, o_ref, s_ref, *, layer):
    @pl.when(pl.program_id(0) == 0)
    def _():
        s_ref[...] = jnp.zeros_like(s_ref)

    logits = lb_ref[...]
    e = jnp.exp(logits - jnp.max(logits, axis=0, keepdims=True))
    soft = e / jnp.sum(e, axis=0, keepdims=True)
    lb = jnp.zeros((1, logits.shape[1]), F32)
    for l in range(1, layer + 1):
        lb = lb + soft[l:l + 1]

    seqs = range(BATCH)
    group = HGRN_HEADS // 2
    for first in range(0, HGRN_HEADS, group):
        qk = slice(first * HGRN_DK, (first + group) * HGRN_DK)
        vv = slice(first * HGRN_DV, (first + group) * HGRN_DV)
        lbg = lb[:, qk]
        hf = [f_ref[b, :, qk].astype(F32) for b in seqs]
        forget = [lbg + (1.0 - lbg) * _sigmoid(hf[b]) for b in seqs]
        g = [jnp.log(jnp.maximum(forget[b], FORGET_FLOOR)) for b in seqs]
        k = [1.0 - forget[b] for b in seqs]
        v = [_silu(i_ref[b, :, vv].astype(F32)) for b in seqs]
        q = [q_ref[b, :, qk].astype(F32) for b in seqs]
        outs = _gla_core(q, k, v, g, wc_ref, mk_ref, s_ref,
                         lambda b, h, first=first: b * HGRN_HEADS + first + h, group, HGRN_DK, HGRN_DV)
        for b in seqs:
            y = _head_rms(outs[b], norm_ref[:, vv], gate_ref[b, :, vv].astype(F32))
            o_ref[b, :, vv] = y.astype(o_ref.dtype)


def _chunk_block(c):
    return (c + N_CHUNKS - 1) % N_CHUNKS


def _col_spec(width, offset):
    assert offset % width == 0
    blk = offset // width
    return pl.BlockSpec((BATCH, CHUNK, width), lambda c, tbl: (0, _chunk_block(c), blk))


def _const_spec(shape):
    nd = len(shape)
    return pl.BlockSpec(shape, lambda c, tbl: (0,) * nd)


class _MixerPart(NamedTuple):
    kernel: object
    width: int
    state: tuple
    in_specs: list
    operands: tuple


def _mixer_call(kernel, name, width, n_heads, dv, dk, in_specs, operands, sides):
    del name, sides
    return _MixerPart(kernel, width, (BATCH * n_heads, dv, dk), in_specs, operands)


def _fused_mixers(parts, sides):
    grid = (N_CHUNKS,)
    plan = _side_plan(sides, grid)
    n_parts, n_side_in, n_side_out = len(parts), len(plan.in_specs), len(plan.out_specs)

    def body(tbl_ref, *refs):
        pos = 0
        ins = []
        for p in parts:
            ins.append(refs[pos:pos + len(p.in_specs)])
            pos += len(p.in_specs)
        side_ins = refs[pos:pos + n_side_in]
        pos += n_side_in
        outs = refs[pos:pos + n_parts]
        pos += n_parts
        side_outs = refs[pos:pos + n_side_out]
        states = refs[pos + n_side_out:]
        for p, i, o, s in zip(parts, ins, outs, states):
            p.kernel(*i, o, s)
        _run_sides(plan, tbl_ref, side_ins, side_outs)

    res = pl.pallas_call(
        body,
        grid_spec=pltpu.PrefetchScalarGridSpec(
            num_scalar_prefetch=1,
            grid=grid,
            in_specs=[s for p in parts for s in p.in_specs] + plan.in_specs,
            out_specs=[pl.BlockSpec((BATCH, CHUNK, p.width), lambda c, tbl: (0, _chunk_block(c), 0))
                       for p in parts] + plan.out_specs,
            scratch_shapes=[pltpu.VMEM(p.state, F32) for p in parts],
        ),
        out_shape=[jax.ShapeDtypeStruct((BATCH, L_PAD, p.width), BF16) for p in parts] + plan.out_shapes,
        compiler_params=_params("arbitrary"),
        name="mixers",
    )(jnp.asarray(plan.table), *[o for p in parts for o in p.operands], *plan.operands)
    return res[:n_parts], res[n_parts:]


def _gla(proj, w2, bias, norm, wc, masks, sides=()):
    qk = GLA_HEADS * GLA_DK
    vw = GLA_HEADS * GLA_DV
    return _mixer_call(
        _gla_kernel, "gla", vw, GLA_HEADS, GLA_DV, GLA_DK,
        in_specs=[
            _col_spec(qk, OFF_GLA),
            _col_spec(qk, OFF_GLA + qk),
            _col_spec(vw, OFF_GLA + 2 * qk),
            _col_spec(vw, OFF_GLA + 2 * qk + vw),
            _col_spec(LANE, OFF_LR),
            _const_spec(w2.shape),
            _const_spec(bias.shape),
            _const_spec(norm.shape),
            _const_spec(wc.shape),
            _const_spec(masks.shape),
        ],
        operands=(proj, proj, proj, proj, proj, w2, bias, norm, wc, masks),
        sides=sides,
    )


def _hgrn(proj, lb_logits, norm, wc, masks, layer, sides=()):
    w = HGRN_HEADS * HGRN_DK
    return _mixer_call(
        functools.partial(_hgrn_kernel, layer=layer), "hgrn", w, HGRN_HEADS, HGRN_DV, HGRN_DK,
        in_specs=[
            _col_spec(w, OFF_HGRN),
            _col_spec(w, OFF_HGRN + w),
            _col_spec(w, OFF_HGRN + 2 * w),
            _col_spec(w, OFF_HGRN + 3 * w),
            _const_spec(lb_logits.shape),
            _const_spec(norm.shape),
            _const_spec(wc.shape),
            _const_spec(masks.shape),
        ],
        operands=(proj, proj, proj, proj, lb_logits, norm, wc, masks),
        sides=sides,
    )


def _ret_kernel(q_ref, k_ref, v_ref, gate_ref, cos_ref, sin_ref, qd_ref, kd_ref, cd_ref, im_ref, norm_ref,
                o_ref, s_ref):
    @pl.when(pl.program_id(0) == 0)
    def _():
        s_ref[...] = jnp.zeros_like(s_ref)

    half = RET_DK // 2
    cos = cos_ref[...]
    sin = sin_ref[...]
    seqs = range(BATCH)
    pairs = [(b, h) for h in range(RET_HEADS) for b in seqs]

    def rotate(ref, b, h):
        lo = h * RET_DK
        t1 = ref[b, :, lo:lo + half].astype(F32)
        t2 = ref[b, :, lo + half:lo + RET_DK].astype(F32)
        return jnp.concatenate([t1 * cos - t2 * sin, t1 * sin + t2 * cos], axis=-1)

    qr = {p: rotate(q_ref, *p) * (RET_DK ** -0.5) for p in pairs}
    kr = {p: rotate(k_ref, *p) for p in pairs}
    scores = {p: _dot_nt(qr[p].astype(BF16), kr[p].astype(BF16)) * im_ref[p[1]] for p in pairs}
    outs = {}
    for b, h in pairs:
        p = (b, h)
        vh = v_ref[b, :, h * RET_DV:(h + 1) * RET_DV]
        st = s_ref[b * RET_HEADS + h]
        outs[p] = (_dot(scores[p].astype(BF16), vh)
                   + _dot_nt((qr[p] * qd_ref[h]).astype(BF16), st.astype(BF16)))
        s_ref[b * RET_HEADS + h] = st * cd_ref[h] + _dot_tn(vh, (kr[p] * kd_ref[h]).astype(BF16))
    for b in seqs:
        ys = []
        for h in range(RET_HEADS):
            o = outs[(b, h)]
            o = o - jnp.mean(o, axis=-1, keepdims=True)
            ms = jnp.mean(o * o, axis=-1, keepdims=True)
            ys.append(o * lax.rsqrt(ms + EPS))
        y = jnp.concatenate(ys, axis=-1)
        o_ref[b] = (y * norm_ref[...] * _silu(gate_ref[b].astype(F32))).astype(o_ref.dtype)


def _ret(proj, cos, sin, qd, kd, cd, im, norm, sides=()):
    w = RET_HEADS * RET_DK
    half = RET_DK // 2
    return _mixer_call(
        _ret_kernel, "retention", w, RET_HEADS, RET_DV, RET_DK,
        in_specs=[
            _col_spec(w, OFF_RET),
            _col_spec(w, OFF_RET + w),
            _col_spec(w, OFF_RET + 2 * w),
            _col_spec(w, OFF_RET + 3 * w),
            pl.BlockSpec((CHUNK, half), lambda c, tbl: (c, 0)),
            pl.BlockSpec((CHUNK, half), lambda c, tbl: (c, 0)),
            _const_spec(qd.shape),
            _const_spec(kd.shape),
            _const_spec(cd.shape),
            _const_spec(im.shape),
            _const_spec(norm.shape),
        ],
        operands=(proj, proj, proj, proj, cos, sin, qd, kd, cd, im, norm),
        sides=sides,
    )


def _ret_tables():
    f32 = jnp.float32
    half = RET_DK // 2
    pos = jnp.arange(L_PAD, dtype=f32) - PAD
    inv_freq = ROPE_BASE ** (-jnp.arange(half, dtype=f32) / half)
    ang = pos[:, None] * inv_freq[None, :]
    log_gamma = jnp.log(1.0 - 2.0 ** (-5.0 - jnp.arange(RET_HEADS, dtype=f32)))
    idx = jnp.arange(CHUNK, dtype=f32)
    rel = idx[:, None] - idx[None, :]
    causal = (rel >= 0)[None]
    intra = jnp.where(causal, jnp.exp(jnp.where(causal, rel[None], 0.0) * log_gamma[:, None, None]), 0.0)
    q_decay = jnp.exp((idx[None, :] + 1.0) * log_gamma[:, None])[..., None]
    k_decay = jnp.exp((CHUNK - 1.0 - idx[None, :]) * log_gamma[:, None])[..., None]
    chunk_decay = jnp.exp(CHUNK * log_gamma)[:, None, None]
    return jnp.cos(ang), jnp.sin(ang), q_decay, k_decay, chunk_decay, intra


def _merge_kernel(y0_ref, y1_ref, y2_ref, mg_ref, wb_ref, wo_ref, h_ref, o_ref):
    merged = None
    for n, y_ref in enumerate((y0_ref, y1_ref, y2_ref)):
        gate = _sigmoid(mg_ref[:, n * D_MODEL:(n + 1) * D_MODEL].astype(F32))
        t = _dot(y_ref[...], wb_ref[n * BRANCH_WIDTH:(n + 1) * BRANCH_WIDTH, :]) * gate
        merged = t if merged is None else merged + t
    o_ref[...] = h_ref[...] + _dot(merged.astype(BF16), wo_ref[...])


def _merge(ys, proj, wb, wo, h):
    assert OFF_MG == 0
    rows = lambda width: pl.BlockSpec((MERGE_TILE, width), lambda i: (i, 0))
    whole = lambda a: pl.BlockSpec(a.shape, lambda i: (0, 0))
    return pl.pallas_call(
        _merge_kernel,
        grid=(ROWS // MERGE_TILE,),
        in_specs=[rows(BRANCH_WIDTH)] * N_BRANCH + [rows(N_BRANCH * D_MODEL)]
        + [whole(wb), whole(wo), rows(D_MODEL)],
        out_specs=rows(D_MODEL),
        out_shape=jax.ShapeDtypeStruct((ROWS, D_MODEL), F32),
        compiler_params=_params("parallel"),
        name="merge",
    )(*ys, proj, wb, wo, h)


def kernel(x, meta_tokens, ffn1_norm, ffn1_w_gate, ffn1_w_up, ffn1_w_down, mix_norm, w_in, gla_w_gate2, gla_b_gate, gla_norm, ret_norm, hgrn_lb_logits, hgrn_norm, w_branch, w_out, ffn2_norm, ffn2_w_gate, ffn2_w_up, ffn2_w_down, final_norm):
    b = x.shape[0]
    meta = jnp.broadcast_to(meta_tokens[None].astype(x.dtype), (b, N_META, D_MODEL))
    h = jnp.concatenate([x, jnp.zeros((b, PAD, D_MODEL), x.dtype), meta], axis=1).reshape(ROWS, D_MODEL)

    wc_np, masks_np = _decay_tables()
    wc = jnp.asarray(wc_np, BF16)
    masks = jnp.asarray(masks_np, F32)
    cos, sin, qd, kd, cd, im = _ret_tables()

    def ffn_casts(wg, wu, wd, layer, tiles):
        padded = _ff_padded(tiles[1])
        return (_SideCast(wg, (layer,), 1, padded, 0, tiles[1]), _SideCast(wu, (layer,), 1, padded, 0, tiles[1]),
                _SideCast(wd, (layer,), 0, padded, 0))

    row = lambda v: v.reshape(1, -1).astype(F32)
    final_gain = row(final_norm)
    w_in_t = jnp.swapaxes(w_in, 1, 2)
    wb_rows = w_branch.reshape(DEPTH, N_BRANCH * BRANCH_WIDTH, D_MODEL)
    padded = _ff_padded(FFN1_TILES[1])
    ffn1_w = _cast_ffn(ffn1_w_gate, ffn1_w_up, ffn1_w_down, 0, padded)
    for layer in range(DEPTH):
        last = layer == DEPTH - 1
        tiles2 = FINAL_TILES if last else FFN2_TILES
        h, (w_proj,) = _ffn(h, row(ffn1_norm[layer]), *ffn1_w, final_gain, False, FFN1_TILES,
                            sides=[_SideWIn(w_in_t, layer, 0)])
        proj, _ = _proj(h, row(mix_norm[layer]), w_proj)
        jobs = [[j] for j in ffn_casts(ffn2_w_gate, ffn2_w_up, ffn2_w_down, layer, tiles2)]
        jobs[0].append(_SideCast(wb_rows, (layer,), 0, N_BRANCH * BRANCH_WIDTH, 0))
        jobs[1].append(_SideCast(w_out, (layer,), 0, D_MODEL, 0))
        if not last:
            for mine, j in zip(jobs, ffn_casts(ffn1_w_gate, ffn1_w_up, ffn1_w_down, layer + 1, FFN1_TILES)):
                mine.append(j)
        w2 = jnp.pad(gla_w_gate2[layer], ((0, LANE - GLA_GATE_RANK), (0, 0))).astype(BF16)
        proj_seq = proj.reshape(b, L_PAD, W_PROJ)
        parts = [_gla(proj_seq, w2, row(gla_b_gate[layer]), row(gla_norm[layer]), wc, masks),
                 _ret(proj_seq, cos, sin, qd, kd, cd, im, row(ret_norm[layer])),
                 _hgrn(proj_seq, hgrn_lb_logits.astype(F32), row(hgrn_norm[layer]), wc, masks, layer)]
        n_jobs = [len(j) for j in jobs]
        ys, copies = _fused_mixers(parts, [j for mine in jobs for j in mine])
        (wg2, wb, *wg1) = copies[:n_jobs[0]]
        (wu2, wo, *wu1) = copies[n_jobs[0]:n_jobs[0] + n_jobs[1]]
        (wd2, *wd1) = copies[n_jobs[0] + n_jobs[1]:]
        ys = [y.reshape(ROWS, BRANCH_WIDTH) for y in ys]
        h = _merge(ys, proj, wb, wo, h)
        h, _ = _ffn(h, row(ffn2_norm[layer]), wg2, wu2, wd2, final_gain, last, tiles2)
        if not last:
            ffn1_w = (*wg1, *wu1, *wd1)
    return h
```
